```python
import math
import jax, jax.numpy as jnp
from jax import lax
import numpy as np

D_MODEL = 1024
BATCH = 8
SEQ = 2048
DEPTH = 2

RET_HEADS = 4
RET_WIDTH = D_MODEL // 2
RET_HEAD_DIM = RET_WIDTH // RET_HEADS
RET_CHUNK = 128
S5_WIDTH = D_MODEL // 2
S5_GROUP_CH = 16
S5_GROUPS = S5_WIDTH // S5_GROUP_CH
S5_STATE = 64
IN_WIDTH = 4 * RET_WIDTH + S5_WIDTH
SWA_HEAD_DIM = 64
SWA_HEADS = D_MODEL // SWA_HEAD_DIM
SWA_KV_HEADS = SWA_HEADS // 4
SWA_Q_PER_KV = SWA_HEADS // SWA_KV_HEADS
SWA_WINDOW = 128
SWA_BLOCK = 128
REL_BUCKETS = 32
REL_MAX_DIST = 128
MOE_GROUPS = 4
MOE_EXPERTS_PER_GROUP = 8
MOE_EXPERTS = MOE_GROUPS * MOE_EXPERTS_PER_GROUP
MOE_TOP_K = 2
MOE_HIDDEN = D_MODEL // 2
MOE_BLOCK = 128
ROPE_BASE = 10000.0
RMS_EPS = 1e-6
GN_EPS = 1e-5
NEG_INF = -1e30

kernel_name = 'hybrid_retention_s5_swa_hmoe_encoder'


def rms_norm(x, g):
    xf = x.astype(jnp.float32)
    y = xf * lax.rsqrt(jnp.mean(xf * xf, axis=-1, keepdims=True) + RMS_EPS)
    return (y * g.astype(jnp.float32)).astype(x.dtype)


def rotary(t):
    seq, dim = t.shape[1], t.shape[-1]
    inv_freq = ROPE_BASE ** (-jnp.arange(0, dim, 2, dtype=jnp.float32) / dim)
    ang = jnp.arange(seq, dtype=jnp.float32)[:, None] * inv_freq[None, :]
    cos = jnp.cos(ang)[None, :, None, :]
    sin = jnp.sin(ang)[None, :, None, :]
    t1, t2 = t[..., : dim // 2], t[..., dim // 2:]
    return jnp.concatenate([t1 * cos - t2 * sin, t1 * sin + t2 * cos], axis=-1)


def chunk_state_scan(inc, chunk_decay, reverse):
    def step(state, inc_c):
        return state * chunk_decay + inc_c, state
    _, states = lax.scan(step, jnp.zeros_like(inc[0]), inc, reverse=reverse)
    return states


def bidirectional_retention(q, k, v, decay_logit):
    bsz, seq, nh, dk = q.shape
    dv = v.shape[-1]
    nc = seq // RET_CHUNK
    log_gamma = jax.nn.log_sigmoid(decay_logit.astype(jnp.float32))
    lg_f, lg_b = log_gamma[0], log_gamma[1]
    pos = jnp.arange(RET_CHUNK, dtype=jnp.float32)
    rel = pos[:, None] - pos[None, :]
    lg_sel = jnp.where(rel[None] >= 0, lg_f[:, None, None], lg_b[:, None, None])
    decay_mask = jnp.exp(jnp.abs(rel)[None] * lg_sel)
    qc = q.reshape(bsz, nc, RET_CHUNK, nh, dk)
    kc = k.reshape(bsz, nc, RET_CHUNK, nh, dk)
    vc = v.reshape(bsz, nc, RET_CHUNK, nh, dv)
    scores = jnp.einsum('bnihd,bnjhd->bnhij', qc, kc) * decay_mask
    out = jnp.einsum('bnhij,bnjhe->bnihe', scores, vc)
    k_f = kc * jnp.exp((RET_CHUNK - 1.0 - pos)[:, None] * lg_f[None, :])[:, :, None]
    inc_f = jnp.einsum('bnjhd,bnjhe->nbhde', k_f, vc)
    s_prev = chunk_state_scan(inc_f, jnp.exp(RET_CHUNK * lg_f)[:, None, None], reverse=False)
    q_f = qc * jnp.exp((pos + 1.0)[:, None] * lg_f[None, :])[:, :, None]
    out = out + jnp.einsum('bnihd,nbhde->bnihe', q_f, s_prev)
    k_b = kc * jnp.exp(pos[:, None] * lg_b[None, :])[:, :, None]
    inc_b = jnp.einsum('bnjhd,bnjhe->nbhde', k_b, vc)
    s_next = chunk_state_scan(inc_b, jnp.exp(RET_CHUNK * lg_b)[:, None, None], reverse=True)
    q_b = qc * jnp.exp((RET_CHUNK - pos)[:, None] * lg_b[None, :])[:, :, None]
    out = out + jnp.einsum('bnihd,nbhde->bnihe', q_b, s_next)
    return out.reshape(bsz, seq, nh, dv)


def complex_affine_combine(e1, e2):
    a1r, a1i, b1r, b1i = e1
    a2r, a2i, b2r, b2i = e2
    return (a1r * a2r - a1i * a2i,
            a1r * a2i + a1i * a2r,
            a2r * b1r - a2i * b1i + b2r,
            a2r * b1i + a2i * b1r + b2i)


def s5_glu(u, a_re, a_im, log_step, b_re, b_im, c_re, c_im, d_skip, w_glu, b_glu):
    bsz, seq, _ = u.shape
    ug = u.reshape(bsz, seq, S5_GROUPS, S5_GROUP_CH)
    bre, bim = b_re.astype(jnp.float32), b_im.astype(jnp.float32)
    y = ug * d_skip.astype(jnp.float32)
    for direction in range(2):
        ar = a_re[direction].astype(jnp.float32)
        ai = a_im[direction].astype(jnp.float32)
        dt = jnp.exp(log_step[direction].astype(jnp.float32))[:, None]
        mag = jnp.exp(ar * dt)
        lam_re, lam_im = mag * jnp.cos(ai * dt), mag * jnp.sin(ai * dt)
        nr, ni = lam_re - 1.0, lam_im
        den = ar * ar + ai * ai
        coef_re = (nr * ar + ni * ai) / den
        coef_im = (ni * ar - nr * ai) / den
        bbar_re = coef_re[..., None] * bre - coef_im[..., None] * bim
        bbar_im = coef_re[..., None] * bim + coef_im[..., None] * bre
        bu_re = jnp.einsum('blgc,gpc->blgp', ug, bbar_re)
        bu_im = jnp.einsum('blgc,gpc->blgp', ug, bbar_im)
        lam_re_b = jnp.broadcast_to(lam_re, bu_re.shape)
        lam_im_b = jnp.broadcast_to(lam_im, bu_re.shape)
        _, _, x_re, x_im = lax.associative_scan(
            complex_affine_combine, (lam_re_b, lam_im_b, bu_re, bu_im), axis=1, reverse=(direction == 1))
        y = (y + jnp.einsum('blgp,gcp->blgc', x_re, c_re[direction].astype(jnp.float32))
             - jnp.einsum('blgp,gcp->blgc', x_im, c_im[direction].astype(jnp.float32)))
    y = jax.nn.gelu(y.reshape(bsz, seq, S5_WIDTH))
    return y * jax.nn.sigmoid(y @ w_glu.astype(jnp.float32) + b_glu.astype(jnp.float32))


def hybrid_retention_s5(h, w_in, decay_logit, a_re, a_im, log_step, b_re, b_im, c_re, c_im,
                        d_skip, w_glu, b_glu, w_out):
    bsz, seq, _ = h.shape
    z = (h @ w_in).astype(jnp.float32)
    q, k, v, g, u = jnp.split(z, [RET_WIDTH, 2 * RET_WIDTH, 3 * RET_WIDTH, 4 * RET_WIDTH], axis=-1)
    q = rotary(q.reshape(bsz, seq, RET_HEADS, RET_HEAD_DIM))
    k = rotary(k.reshape(bsz, seq, RET_HEADS, RET_HEAD_DIM)) * (RET_HEAD_DIM ** -0.5)
    v = v.reshape(bsz, seq, RET_HEADS, RET_HEAD_DIM)
    r = bidirectional_retention(q, k, v, decay_logit)
    mu = jnp.mean(r, axis=-1, keepdims=True)
    var = jnp.mean(jnp.square(r - mu), axis=-1, keepdims=True)
    r = ((r - mu) * lax.rsqrt(var + GN_EPS)).reshape(bsz, seq, RET_WIDTH)
    y_ret = jax.nn.silu(g) * r
    y_s5 = s5_glu(u, a_re, a_im, log_step, b_re, b_im, c_re, c_im, d_skip, w_glu, b_glu)
    return jnp.concatenate([y_ret, y_s5], axis=-1).astype(h.dtype) @ w_out


def t5_bucket(rel):
    half = REL_BUCKETS // 2
    max_exact = half // 2
    n = jnp.abs(rel)
    large = max_exact + (jnp.log(jnp.maximum(n, 1).astype(jnp.float32) / max_exact)
                         / math.log(REL_MAX_DIST / max_exact) * (half - max_exact)).astype(jnp.int32)
    large = jnp.minimum(large, half - 1)
    return jnp.where(rel > 0, half, 0) + jnp.where(n < max_exact, n, large)


def windowed_gqa(h, w_qkv, sink, w_o, rel_bias):
    bsz, seq, _ = h.shape
    nb = seq // SWA_BLOCK
    qd = SWA_HEADS * SWA_HEAD_DIM
    kvd = SWA_KV_HEADS * SWA_HEAD_DIM
    qkv = h @ w_qkv
    q = qkv[..., :qd].reshape(bsz, nb, SWA_BLOCK, SWA_KV_HEADS, SWA_Q_PER_KV, SWA_HEAD_DIM)
    k = qkv[..., qd:qd + kvd].reshape(bsz, seq, SWA_KV_HEADS, SWA_HEAD_DIM)
    v = qkv[..., qd + kvd:].reshape(bsz, seq, SWA_KV_HEADS, SWA_HEAD_DIM)

    def band(t):
        tp = jnp.pad(t, ((0, 0), (SWA_BLOCK, SWA_BLOCK), (0, 0), (0, 0)))
        tp = tp.reshape(bsz, nb + 2, SWA_BLOCK, SWA_KV_HEADS, SWA_HEAD_DIM)
        return jnp.concatenate([tp[:, :-2], tp[:, 1:-1], tp[:, 2:]], axis=2)

    kb, vb = band(k), band(v)
    s = jnp.einsum('bnqkgd,bnskd->bnkgqs', q, kb).astype(jnp.float32) * (SWA_HEAD_DIM ** -0.5)
    rel = jnp.arange(3 * SWA_BLOCK)[None, :] - SWA_BLOCK - jnp.arange(SWA_BLOCK)[:, None]
    bias = rel_bias.astype(jnp.float32)[t5_bucket(rel)]
    bias = bias.transpose(2, 0, 1).reshape(SWA_KV_HEADS, SWA_Q_PER_KV, SWA_BLOCK, 3 * SWA_BLOCK)
    key_pos = (jnp.arange(nb)[:, None] - 1) * SWA_BLOCK + jnp.arange(3 * SWA_BLOCK)[None, :]
    mask = (jnp.abs(rel) <= SWA_WINDOW)[None] & ((key_pos >= 0) & (key_pos < seq))[:, None, :]
    s = jnp.where(mask[None, :, None, None], s + bias, NEG_INF)
    sink_col = jnp.broadcast_to(sink.astype(jnp.float32).reshape(SWA_KV_HEADS, SWA_Q_PER_KV, 1, 1),
                                s.shape[:-1] + (1,))
    p = jax.nn.softmax(jnp.concatenate([s, sink_col], axis=-1), axis=-1)[..., :-1]
    o = jnp.einsum('bnkgqs,bnskd->bnqkgd', p.astype(vb.dtype), vb).reshape(bsz, seq, qd)
    return o @ w_o


def hier_moe(h, w_group, b_group, w_er, b_er, w_gate, w_up, w_down):
    bsz, seq, d = h.shape
    t = h.reshape(-1, d)
    n_tok = t.shape[0]
    g_logit = (t @ w_group).astype(jnp.float32) + b_group.astype(jnp.float32)
    g_prob = jax.nn.softmax(g_logit, axis=-1)
    g_sel = jnp.argmax(g_logit, axis=-1)
    p_g = jnp.take_along_axis(g_prob, g_sel[:, None], axis=-1)[:, 0]
    e_logit = jnp.einsum('td,gde->tge', t, w_er).astype(jnp.float32) + b_er.astype(jnp.float32)
    e_logit = jnp.take_along_axis(e_logit, g_sel[:, None, None], axis=1)[:, 0]
    top_v, top_i = lax.top_k(e_logit, MOE_TOP_K)
    gate = jax.nn.softmax(top_v, axis=-1) * p_g[:, None]
    expert = g_sel[:, None] * MOE_EXPERTS_PER_GROUP + top_i
    n_asg = n_tok * MOE_TOP_K
    e_flat = expert.reshape(-1)
    tok_flat = jnp.arange(n_asg) // MOE_TOP_K
    order = jnp.argsort(e_flat)
    e_sorted = e_flat[order]
    tok_sorted = tok_flat[order]
    gate_sorted = gate.reshape(-1)[order].astype(t.dtype)
    counts = jnp.bincount(e_flat, length=MOE_EXPERTS)
    padded = ((counts + MOE_BLOCK - 1) // MOE_BLOCK) * MOE_BLOCK
    start = jnp.cumsum(counts) - counts
    pend = jnp.cumsum(padded)
    pstart = pend - padded
    dest = pstart[e_sorted] + jnp.arange(n_asg) - start[e_sorted]
    n_blocks = -(-n_asg // MOE_BLOCK) + MOE_EXPERTS
    xbuf = jnp.zeros((n_blocks * MOE_BLOCK, d), t.dtype).at[dest].set(t[tok_sorted])
    block_expert = jnp.minimum(
        jnp.searchsorted(pend, jnp.arange(n_blocks) * MOE_BLOCK, side='right'), MOE_EXPERTS - 1)

    def expert_block(args):
        xb, e = args
        return (jax.nn.silu(xb @ w_gate[e]) * (xb @ w_up[e])) @ w_down[e]

    ybuf = lax.map(expert_block, (xbuf.reshape(n_blocks, MOE_BLOCK, d), block_expert)).reshape(-1, d)
    y = jnp.zeros_like(t).at[tok_sorted].add(ybuf[dest] * gate_sorted[:, None])
    return y.reshape(bsz, seq, d)


def setup_inputs(seed: int = 0) -> dict:
    key = jax.random.key(seed)
    ks = iter(jax.random.split(key, 40))
    f32 = jnp.float32
    n_even = (DEPTH + 1) // 2
    n_odd = DEPTH // 2

    def nrm(shape, scale):
        return jax.random.normal(next(ks), shape, f32) * scale

    x = nrm((BATCH, SEQ, D_MODEL), 1.0)
    norm_mix_g = 1.0 + nrm((DEPTH, D_MODEL), 0.02)
    norm_ffn_g = 1.0 + nrm((DEPTH, D_MODEL), 0.02)
    norm_final_g = 1.0 + nrm((D_MODEL,), 0.02)
    hyb_w_in = nrm((n_even, D_MODEL, IN_WIDTH), D_MODEL ** -0.5)
    heads = jnp.arange(RET_HEADS, dtype=f32)
    ret_logit0 = jnp.log(2.0 ** (5.0 + heads) - 1.0)
    ret_decay_logit = ret_logit0[None, None, :] + nrm((n_even, 2, RET_HEADS), 0.01)
    s5_a_re = -0.5 + nrm((n_even, 2, S5_GROUPS, S5_STATE), 0.01)
    s5_a_im = jnp.pi * jnp.arange(S5_STATE, dtype=f32) + nrm((n_even, 2, S5_GROUPS, S5_STATE), 0.01)
    s5_log_step = jax.random.uniform(next(ks), (n_even, 2, S5_GROUPS), f32,
                                     minval=math.log(1e-3), maxval=math.log(1e-1))
    s5_b_re = nrm((n_even, S5_GROUPS, S5_STATE, S5_GROUP_CH), (2 * S5_GROUP_CH) ** -0.5)
    s5_b_im = nrm((n_even, S5_GROUPS, S5_STATE, S5_GROUP_CH), (2 * S5_GROUP_CH) ** -0.5)
    s5_c_re = nrm((n_even, 2, S5_GROUPS, S5_GROUP_CH, S5_STATE), S5_STATE ** -0.5)
    s5_c_im = nrm((n_even, 2, S5_GROUPS, S5_GROUP_CH, S5_STATE), S5_STATE ** -0.5)
    s5_d = nrm((n_even, S5_GROUPS, S5_GROUP_CH), 0.5)
    s5_w_glu = nrm((n_even, S5_WIDTH, S5_WIDTH), S5_WIDTH ** -0.5)
    s5_b_glu = nrm((n_even, S5_WIDTH), 0.02)
    hyb_w_out = nrm((n_even, RET_WIDTH + S5_WIDTH, D_MODEL), (RET_WIDTH + S5_WIDTH) ** -0.5)
    qkv_width = (SWA_HEADS + 2 * SWA_KV_HEADS) * SWA_HEAD_DIM
    swa_w_qkv = nrm((n_odd, D_MODEL, qkv_width), D_MODEL ** -0.5)
    swa_sink = nrm((n_odd, SWA_HEADS), 0.5)
    swa_w_o = nrm((n_odd, SWA_HEADS * SWA_HEAD_DIM, D_MODEL), (SWA_HEADS * SWA_HEAD_DIM) ** -0.5)
    rel_bias = nrm((REL_BUCKETS, SWA_HEADS), 0.5)
    moe_w_group = nrm((DEPTH, D_MODEL, MOE_GROUPS), D_MODEL ** -0.5)
    moe_b_group = nrm((DEPTH, MOE_GROUPS), 0.01)
    moe_w_expert_router = nrm((DEPTH, MOE_GROUPS, D_MODEL, MOE_EXPERTS_PER_GROUP), D_MODEL ** -0.5)
    moe_b_expert_router = nrm((DEPTH, MOE_GROUPS, MOE_EXPERTS_PER_GROUP), 0.01)
    moe_w_gate = nrm((DEPTH, MOE_EXPERTS, D_MODEL, MOE_HIDDEN), D_MODEL ** -0.5)
    moe_w_up = nrm((DEPTH, MOE_EXPERTS, D_MODEL, MOE_HIDDEN), D_MODEL ** -0.5)
    moe_w_down = nrm((DEPTH, MOE_EXPERTS, MOE_HIDDEN, D_MODEL), MOE_HIDDEN ** -0.5)
    return {'x': x, 'norm_mix_g': norm_mix_g, 'norm_ffn_g': norm_ffn_g, 'norm_final_g': norm_final_g,
            'hyb_w_in': hyb_w_in, 'ret_decay_logit': ret_decay_logit, 's5_a_re': s5_a_re,
            's5_a_im': s5_a_im, 's5_log_step': s5_log_step, 's5_b_re': s5_b_re, 's5_b_im': s5_b_im,
            's5_c_re': s5_c_re, 's5_c_im': s5_c_im, 's5_d': s5_d, 's5_w_glu': s5_w_glu,
            's5_b_glu': s5_b_glu, 'hyb_w_out': hyb_w_out, 'swa_w_qkv': swa_w_qkv, 'swa_sink': swa_sink,
            'swa_w_o': swa_w_o, 'rel_bias': rel_bias, 'moe_w_group': moe_w_group,
            'moe_b_group': moe_b_group, 'moe_w_expert_router': moe_w_expert_router,
            'moe_b_expert_router': moe_b_expert_router, 'moe_w_gate': moe_w_gate,
            'moe_w_up': moe_w_up, 'moe_w_down': moe_w_down}


def reference(x, norm_mix_g, norm_ffn_g, norm_final_g, hyb_w_in, ret_decay_logit, s5_a_re, s5_a_im,
              s5_log_step, s5_b_re, s5_b_im, s5_c_re, s5_c_im, s5_d, s5_w_glu, s5_b_glu, hyb_w_out,
              swa_w_qkv, swa_sink, swa_w_o, rel_bias, moe_w_group, moe_b_group, moe_w_expert_router,
              moe_b_expert_router, moe_w_gate, moe_w_up, moe_w_down):
    for layer in range(DEPTH):
        i = layer // 2
        h = rms_norm(x, norm_mix_g[layer])
        if layer % 2 == 0:
            mix = hybrid_retention_s5(h, hyb_w_in[i], ret_decay_logit[i], s5_a_re[i], s5_a_im[i],
                                      s5_log_step[i], s5_b_re[i], s5_b_im[i], s5_c_re[i], s5_c_im[i],
                                      s5_d[i], s5_w_glu[i], s5_b_glu[i], hyb_w_out[i])
        else:
            mix = windowed_gqa(h, swa_w_qkv[i], swa_sink[i], swa_w_o[i], rel_bias)
        x = x + mix.astype(x.dtype)
        h = rms_norm(x, norm_ffn_g[layer])
        x = x + hier_moe(h, moe_w_group[layer], moe_b_group[layer], moe_w_expert_router[layer],
                         moe_b_expert_router[layer], moe_w_gate[layer], moe_w_up[layer],
                         moe_w_down[layer]).astype(x.dtype)
    return rms_norm(x, norm_final_g)
```

```python
import functools
import math

import jax
import jax.numpy as jnp
from jax import lax
from jax.experimental import pallas as pl
from jax.experimental.pallas import tpu as pltpu

F32 = jnp.float32
BF16 = jnp.bfloat16
I32 = jnp.int32

RET_HEADS = 4
RET_CHUNK = 128
S5_GROUP_CH = 16
S5_STATE = 64
SWA_HEAD_DIM = 64
SWA_Q_PER_KV = 4
SWA_WINDOW = 128
SWA_BLOCK = 128
REL_BUCKETS = 32
REL_MAX_DIST = 128
MOE_GROUPS = 4
MOE_EXPERTS_PER_GROUP = 8
MOE_EXPERTS = MOE_GROUPS * MOE_EXPERTS_PER_GROUP
ROPE_BASE = 10000.0
RMS_EPS = 1e-6
GN_EPS = 1e-5
NEG_INF = -1e30

LANES = 128
SUBLANES = 8
V7X_VMEM_BYTES = 64 * 1024 * 1024
VMEM_LIMIT = V7X_VMEM_BYTES - 8 * 1024 * 1024

MOE_ROWS = 256
S5_CHUNK = 128
S5_COLS = 512
S5_KBLK = 128
ROUTER_ROWS = 128

NT_DIMS = (((1,), (1,)), ((), ()))
TN_DIMS = (((0,), (0,)), ((), ()))


def _params(semantics):
    return pltpu.CompilerParams(dimension_semantics=semantics, vmem_limit_bytes=VMEM_LIMIT)


def _rms(x, g):
    ms = jnp.mean(x * x, axis=-1, keepdims=True)
    return (x * lax.rsqrt(ms + RMS_EPS)) * g


def _norm_matmul_kernel(x_ref, g_ref, w_ref, o_ref):
    h = _rms(x_ref[...], g_ref[...])
    o_ref[...] = jnp.dot(h.astype(BF16), w_ref[...], preferred_element_type=F32)


def norm_matmul(x, g, w, tm=512):
    t, d = x.shape
    n = w.shape[1]
    return pl.pallas_call(
        _norm_matmul_kernel,
        grid=(t // tm,),
        in_specs=[
            pl.BlockSpec((tm, d), lambda i: (i, 0)),
            pl.BlockSpec((1, d), lambda i: (0, 0)),
            pl.BlockSpec((d, n), lambda i: (0, 0)),
        ],
        out_specs=pl.BlockSpec((tm, n), lambda i: (i, 0)),
        out_shape=jax.ShapeDtypeStruct((t, n), F32),
        compiler_params=_params(("parallel",)),
        name="norm_matmul",
    )(x, g.reshape(1, d), w.astype(BF16))


def _retention_kernel(lg_ref, q_ref, k_ref, v_ref, g_ref, cos_ref, sin_ref, o_ref,
                      qr_ref, kr_ref, acc_ref):
    h = pl.program_id(1)
    lg_f = lg_ref[0, h]
    lg_b = lg_ref[1, h]
    seq, dk = q_ref.shape
    c = RET_CHUNK
    nc = seq // c

    cos = cos_ref[...]
    sin = sin_ref[...]
    q = q_ref[...]
    qr_ref[...] = q * cos + pltpu.roll(q, dk // 2, 1) * sin
    k = k_ref[...]
    kr_ref[...] = (k * cos + pltpu.roll(k, dk // 2, 1) * sin) * (dk ** -0.5)

    pos = lax.broadcasted_iota(I32, (c, dk), 0).astype(F32)
    kf_scale = jnp.exp((c - 1.0 - pos) * lg_f)
    qf_scale = jnp.exp((pos + 1.0) * lg_f)
    kb_scale = jnp.exp(pos * lg_b)
    qb_scale = jnp.exp((c - pos) * lg_b)
    rel = (lax.broadcasted_iota(I32, (c, c), 0) - lax.broadcasted_iota(I32, (c, c), 1)).astype(F32)
    mask = jnp.exp(jnp.abs(rel) * jnp.where(rel >= 0, lg_f, lg_b))
    dec_f = jnp.exp(jnp.full((dk, dk), c * lg_f, F32))
    dec_b = jnp.exp(jnp.full((dk, dk), c * lg_b, F32))

    def fwd(n, state):
        rows = pl.ds(pl.multiple_of(n * c, c), c)
        qc = qr_ref[rows, :]
        kc = kr_ref[rows, :]
        vb = v_ref[rows, :].astype(BF16)
        s = lax.dot_general(qc.astype(BF16), kc.astype(BF16), NT_DIMS,
                            preferred_element_type=F32) * mask
        out = jnp.dot(s.astype(BF16), vb, preferred_element_type=F32)
        out = out + jnp.dot((qc * qf_scale).astype(BF16), state.astype(BF16),
                            preferred_element_type=F32)
        acc_ref[rows, :] = out
        inc = lax.dot_general((kc * kf_scale).astype(BF16), vb, TN_DIMS,
                              preferred_element_type=F32)
        return state * dec_f + inc

    lax.fori_loop(0, nc, fwd, jnp.zeros((dk, dk), F32))

    def bwd(i, state):
        n = nc - 1 - i
        rows = pl.ds(pl.multiple_of(n * c, c), c)
        qc = qr_ref[rows, :]
        kc = kr_ref[rows, :]
        vb = v_ref[rows, :].astype(BF16)
        out = acc_ref[rows, :] + jnp.dot((qc * qb_scale).astype(BF16), state.astype(BF16),
                                         preferred_element_type=F32)
        mu = jnp.mean(out, axis=-1, keepdims=True)
        cen = out - mu
        var = jnp.mean(cen * cen, axis=-1, keepdims=True)
        r = cen * lax.rsqrt(var + GN_EPS)
        g = g_ref[rows, :]
        o_ref[rows, :] = (g * jax.nn.sigmoid(g)) * r
        inc = lax.dot_general((kc * kb_scale).astype(BF16), vb, TN_DIMS,
                              preferred_element_type=F32)
        return state * dec_b + inc

    lax.fori_loop(0, nc, bwd, jnp.zeros((dk, dk), F32))


def retention(z3, log_gamma, cos, sin):
    b, seq, _ = z3.shape
    nh = RET_HEADS
    dk = cos.shape[1]

    def col(off):
        return pl.BlockSpec((None, seq, dk), lambda bi, hi: (bi, 0, off + hi))

    return pl.pallas_call(
        _retention_kernel,
        grid=(b, nh),
        in_specs=[
            pl.BlockSpec(memory_space=pltpu.SMEM),
            col(0), col(nh), col(2 * nh), col(3 * nh),
            pl.BlockSpec((seq, dk), lambda bi, hi: (0, 0)),
            pl.BlockSpec((seq, dk), lambda bi, hi: (0, 0)),
        ],
        out_specs=pl.BlockSpec((None, seq, dk), lambda bi, hi: (bi, 0, hi)),
        out_shape=jax.ShapeDtypeStruct((b, seq, nh * dk), F32),
        scratch_shapes=[pltpu.VMEM((seq, dk), F32)] * 3,
        compiler_params=_params(("parallel", "parallel")),
        name="retention",
    )(log_gamma, z3, z3, z3, z3, cos, sin)


def _s5_kernel(u_ref, lam_ref, b_ref, c_ref, y_ref, xre_ref, xim_ref, sre_ref, sim_ref):
    d = pl.program_id(0)
    n = pl.program_id(1)
    cn, nb, width = u_ref.shape
    nk = width // S5_KBLK
    ncols = sre_ref.shape[1] // S5_COLS

    @pl.when(n == 0)
    def _():
        sre_ref[...] = jnp.zeros_like(sre_ref)
        sim_ref[...] = jnp.zeros_like(sim_ref)

    u = u_ref[...].reshape(cn * nb, width).astype(BF16)
    for kb in range(nk):
        bu = jnp.dot(u[:, kb * S5_KBLK:(kb + 1) * S5_KBLK], b_ref[kb], preferred_element_type=F32)
        xre_ref[:, kb * S5_COLS:(kb + 1) * S5_COLS] = bu[:, :S5_COLS]
        xim_ref[:, kb * S5_COLS:(kb + 1) * S5_COLS] = bu[:, S5_COLS:]

    for cb in range(ncols):
        cols = slice(cb * S5_COLS, (cb + 1) * S5_COLS)
        lr = jnp.broadcast_to(lam_ref[0:1, cols], (nb, S5_COLS))
        li = jnp.broadcast_to(lam_ref[1:2, cols], (nb, S5_COLS))

        def step(i, carry):
            xr, xi = carry
            t = i + d * (cn - 1 - 2 * i)
            rows = pl.ds(pl.multiple_of(t * nb, nb), nb)
            nxr = lr * xr - li * xi + xre_ref[rows, cols]
            nxi = lr * xi + li * xr + xim_ref[rows, cols]
            xre_ref[rows, cols] = nxr
            xim_ref[rows, cols] = nxi
            return nxr, nxi

        xr, xi = lax.fori_loop(0, cn, step, (sre_ref[:, cols], sim_ref[:, cols]), unroll=8)
        sre_ref[:, cols] = xr
        sim_ref[:, cols] = xi

    for kb in range(nk):
        cols = slice(kb * S5_COLS, (kb + 1) * S5_COLS)
        y = jnp.dot(xre_ref[:, cols].astype(BF16), c_ref[kb, :S5_COLS, :], preferred_element_type=F32)
        y = y + jnp.dot(xim_ref[:, cols].astype(BF16), c_ref[kb, S5_COLS:, :],
                        preferred_element_type=F32)
        y_ref[:, :, kb * S5_KBLK:(kb + 1) * S5_KBLK] = y.reshape(cn, nb, S5_KBLK)


def s5_scan(u_t, lam, bmat, cmat):
    seq, nb, width = u_t.shape
    nstate = lam.shape[2]
    cn = min(S5_CHUNK, seq)
    nch = seq // cn

    def chunk(d, n):
        return n + d * (nch - 1 - 2 * n)

    return pl.pallas_call(
        _s5_kernel,
        grid=(2, nch),
        in_specs=[
            pl.BlockSpec((cn, nb, width), lambda d, n: (chunk(d, n), 0, 0)),
            pl.BlockSpec((None, 2, nstate), lambda d, n: (d, 0, 0)),
            pl.BlockSpec((None,) + bmat.shape[1:], lambda d, n: (d, 0, 0, 0)),
            pl.BlockSpec((None,) + cmat.shape[1:], lambda d, n: (d, 0, 0, 0)),
        ],
        out_specs=pl.BlockSpec((None, cn, nb, width), lambda d, n: (d, chunk(d, n), 0, 0)),
        out_shape=jax.ShapeDtypeStruct((2, seq, nb, width), F32),
        scratch_shapes=[
            pltpu.VMEM((cn * nb, nstate), F32),
            pltpu.VMEM((cn * nb, nstate), F32),
            pltpu.VMEM((nb, nstate), F32),
            pltpu.VMEM((nb, nstate), F32),
        ],
        compiler_params=_params(("arbitrary", "arbitrary")),
        name="s5_scan",
    )(u_t, lam, bmat, cmat)


def _s5_discretize(a_re, a_im, log_step, b_re, b_im, c_re, c_im):
    ng, npst = a_re.shape[1], a_re.shape[2]
    gpb = S5_KBLK // S5_GROUP_CH
    nk = ng // gpb
    eye = jnp.eye(gpb, dtype=F32)
    bre, bim = b_re.astype(F32), b_im.astype(F32)
    lams, bmats, cmats = [], [], []
    for direction in range(2):
        ar = a_re[direction].astype(F32)
        ai = a_im[direction].astype(F32)
        dt = jnp.exp(log_step[direction].astype(F32))[:, None]
        mag = jnp.exp(ar * dt)
        lam_re, lam_im = mag * jnp.cos(ai * dt), mag * jnp.sin(ai * dt)
        nr, ni = lam_re - 1.0, lam_im
        den = ar * ar + ai * ai
        coef_re = (nr * ar + ni * ai) / den
        coef_im = (ni * ar - nr * ai) / den
        bbar_re = coef_re[..., None] * bre - coef_im[..., None] * bim
        bbar_im = coef_re[..., None] * bim + coef_im[..., None] * bre

        def in_blocks(m):
            m4 = m.reshape(nk, gpb, npst, S5_GROUP_CH)
            return jnp.einsum('kgpc,gh->kgchp', m4, eye).reshape(nk, S5_KBLK, gpb * npst)

        def out_blocks(m):
            m4 = m.reshape(nk, gpb, S5_GROUP_CH, npst)
            return jnp.einsum('kgcp,gh->kgphc', m4, eye).reshape(nk, gpb * npst, S5_KBLK)

        lams.append(jnp.stack([lam_re.reshape(-1), lam_im.reshape(-1)]))
        bmats.append(jnp.concatenate([in_blocks(bbar_re), in_blocks(bbar_im)], axis=2))
        cmats.append(jnp.concatenate([out_blocks(c_re[direction].astype(F32)),
                                      -out_blocks(c_im[direction].astype(F32))], axis=1))
    return jnp.stack(lams), jnp.stack(bmats).astype(BF16), jnp.stack(cmats).astype(BF16)


def _glu_kernel(yr_ref, yf_ref, yb_ref, u_ref, d_ref, w_ref, b_ref, o_ref):
    w = yr_ref.shape[1]
    y = u_ref[...] * d_ref[...] + yf_ref[...] + yb_ref[...]
    y = jax.nn.gelu(y)
    gate = jax.nn.sigmoid(jnp.dot(y.astype(BF16), w_ref[...], preferred_element_type=F32) + b_ref[...])
    o_ref[:, :w] = yr_ref[...].astype(BF16)
    o_ref[:, w:] = (y * gate).astype(BF16)


def glu_concat(y_ret, ys5, z, d_skip, w_glu, b_glu, nb, tm=512):
    t, w = y_ret.shape
    seq = t // nb
    tm = min(tm, seq)
    nl = seq // tm
    ucol = z.shape[1] // w - 1
    return pl.pallas_call(
        _glu_kernel,
        grid=(nb, nl),
        in_specs=[
            pl.BlockSpec((tm, w), lambda b, i: (b * nl + i, 0)),
            pl.BlockSpec((None, tm, w), lambda b, i: (0, i, b)),
            pl.BlockSpec((None, tm, w), lambda b, i: (1, i, b)),
            pl.BlockSpec((tm, w), lambda b, i: (b * nl + i, ucol)),
            pl.BlockSpec((1, w), lambda b, i: (0, 0)),
            pl.BlockSpec((w, w), lambda b, i: (0, 0)),
            pl.BlockSpec((1, w), lambda b, i: (0, 0)),
        ],
        out_specs=pl.BlockSpec((tm, 2 * w), lambda b, i: (b * nl + i, 0)),
        out_shape=jax.ShapeDtypeStruct((t, 2 * w), BF16),
        compiler_params=_params(("parallel", "parallel")),
        name="glu_concat",
    )(y_ret, ys5, ys5, z, d_skip.reshape(1, w), w_glu.astype(BF16), b_glu.reshape(1, w))


def _proj_kernel(a_ref, w_ref, x_ref, g_ref, rh_ref, rl_ref, rb_ref, x1_ref, h_ref, lt_ref):
    x1 = x_ref[...] + jnp.dot(a_ref[...], w_ref[...], preferred_element_type=F32)
    x1_ref[...] = x1
    h = _rms(x1, g_ref[...])
    h_ref[...] = h
    h_hi = h.astype(BF16)
    h_lo = (h - h_hi.astype(F32)).astype(BF16)
    lt = lax.dot_general(rh_ref[...], h_hi, NT_DIMS, preferred_element_type=F32)
    lt = lt + lax.dot_general(rh_ref[...], h_lo, NT_DIMS, preferred_element_type=F32)
    lt = lt + lax.dot_general(rl_ref[...], h_hi, NT_DIMS, preferred_element_type=F32)
    lt_ref[...] = lt + rb_ref[...]


def proj_norm_router(a, w, x, g, r_hi, r_lo, r_bias, tm=256):
    t, d = x.shape
    k = a.shape[1]
    nr = r_hi.shape[0]
    return pl.pallas_call(
        _proj_kernel,
        grid=(t // tm,),
        in_specs=[
            pl.BlockSpec((tm, k), lambda i: (i, 0)),
            pl.BlockSpec((k, d), lambda i: (0, 0)),
            pl.BlockSpec((tm, d), lambda i: (i, 0)),
            pl.BlockSpec((1, d), lambda i: (0, 0)),
            pl.BlockSpec((nr, d), lambda i: (0, 0)),
            pl.BlockSpec((nr, d), lambda i: (0, 0)),
            pl.BlockSpec((nr, 1), lambda i: (0, 0)),
        ],
        out_specs=[
            pl.BlockSpec((tm, d), lambda i: (i, 0)),
            pl.BlockSpec((tm, d), lambda i: (i, 0)),
            pl.BlockSpec((nr, tm), lambda i: (0, i)),
        ],
        out_shape=[
            jax.ShapeDtypeStruct((t, d), F32),
            jax.ShapeDtypeStruct((t, d), F32),
            jax.ShapeDtypeStruct((nr, t), F32),
        ],
        compiler_params=_params(("parallel",)),
        name="proj_norm_router",
    )(a, w.astype(BF16), x, g.reshape(1, d), r_hi, r_lo, r_bias)


def _router_operands(w_group, b_group, w_er, b_er):
    d = w_group.shape[0]
    wt = jnp.concatenate([
        jnp.transpose(w_er.astype(F32), (0, 2, 1)).reshape(MOE_EXPERTS, d),
        jnp.transpose(w_group.astype(F32)),
        jnp.zeros((ROUTER_ROWS - MOE_EXPERTS - MOE_GROUPS, d), F32)], axis=0)
    bias = jnp.concatenate([
        b_er.astype(F32).reshape(-1), b_group.astype(F32),
        jnp.zeros((ROUTER_ROWS - MOE_EXPERTS - MOE_GROUPS,), F32)]).reshape(ROUTER_ROWS, 1)
    hi = wt.astype(BF16)
    lo = (wt - hi.astype(F32)).astype(BF16)
    return hi, lo, bias


def _route_kernel(lt_ref, eid_ref, gate_ref, rank_ref, cnt_ref, run_ref):
    i = pl.program_id(0)
    tm = lt_ref.shape[1]
    ne, npg, ng = MOE_EXPERTS, MOE_EXPERTS_PER_GROUP, MOE_GROUPS

    @pl.when(i == 0)
    def _():
        run_ref[...] = jnp.zeros_like(run_ref)

    gl = lt_ref[ne:ne + ng, :]
    gmax = jnp.max(gl, axis=0, keepdims=True)
    gidx = lax.broadcasted_iota(I32, (ng, tm), 0)
    gsel = jnp.min(jnp.where(gl == gmax, gidx, ng), axis=0, keepdims=True)
    p_g = 1.0 / jnp.sum(jnp.exp(gl - gmax), axis=0, keepdims=True)

    e8 = lt_ref[(ng - 1) * npg:ng * npg, :]
    for g in range(ng - 2, -1, -1):
        e8 = jnp.where(gsel == g, lt_ref[g * npg:(g + 1) * npg, :], e8)
    eidx = lax.broadcasted_iota(I32, (npg, tm), 0)
    m1 = jnp.max(e8, axis=0, keepdims=True)
    i1 = jnp.min(jnp.where(e8 == m1, eidx, npg), axis=0, keepdims=True)
    e8b = jnp.where(eidx == i1, -jnp.inf, e8)
    m2 = jnp.max(e8b, axis=0, keepdims=True)
    i2 = jnp.min(jnp.where(e8b == m2, eidx, npg), axis=0, keepdims=True)
    t2 = jnp.exp(m2 - m1)
    den = 1.0 + t2
    gate_ref[0:1, :] = (1.0 / den) * p_g
    gate_ref[1:2, :] = (t2 / den) * p_g
    id1 = gsel * npg + i1
    id2 = gsel * npg + i2
    eid_ref[0:1, :] = id1
    eid_ref[1:2, :] = id2

    rows = lax.broadcasted_iota(I32, (ne, tm), 0)
    oh1 = rows == id1
    oh2 = rows == id2
    both = jnp.where(oh1, 1.0, 0.0) + jnp.where(oh2, 1.0, 0.0)
    earlier = (lax.broadcasted_iota(I32, (tm, tm), 0) < lax.broadcasted_iota(I32, (tm, tm), 1))
    prefix = jnp.dot(both.astype(BF16), jnp.where(earlier, 1.0, 0.0).astype(BF16),
                     preferred_element_type=F32)
    base = prefix + run_ref[:, 0:1]
    rank_ref[0:1, :] = jnp.sum(jnp.where(oh1, base, 0.0), axis=0, keepdims=True).astype(I32)
    rank_ref[1:2, :] = jnp.sum(jnp.where(oh2, base, 0.0), axis=0, keepdims=True).astype(I32)
    run = run_ref[...] + jnp.sum(both, axis=1, keepdims=True)
    run_ref[...] = run
    cnt_ref[...] = run.astype(I32)


def route(lt, tm=512):
    nr, t = lt.shape
    two = lambda dt: jax.ShapeDtypeStruct((2, t), dt)
    return pl.pallas_call(
        _route_kernel,
        grid=(t // tm,),
        in_specs=[pl.BlockSpec((nr, tm), lambda i: (0, i))],
        out_specs=[
            pl.BlockSpec((2, tm), lambda i: (0, i)),
            pl.BlockSpec((2, tm), lambda i: (0, i)),
            pl.BlockSpec((2, tm), lambda i: (0, i)),
            pl.BlockSpec((MOE_EXPERTS, LANES), lambda i: (0, 0)),
        ],
        out_shape=[two(I32), two(F32), two(I32),
                   jax.ShapeDtypeStruct((MOE_EXPERTS, LANES), I32)],
        scratch_shapes=[pltpu.VMEM((MOE_EXPERTS, LANES), F32)],
        compiler_params=_params(("arbitrary",)),
        name="route",
    )(lt)


def _dispatch_kernel(dest_ref, h_ref, xin_ref, xbuf_ref, sem):
    del xin_ref
    tm = h_ref.shape[0]

    def copy(r, k):
        return pltpu.make_async_copy(h_ref.at[pl.ds(r, 1), :],
                                     xbuf_ref.at[pl.ds(dest_ref[k, r], 1), :], sem)

    def start(r, carry):
        copy(r, 0).start()
        copy(r, 1).start()
        return carry

    lax.fori_loop(0, tm, start, 0, unroll=8)

    def wait(r, carry):
        copy(r, 0).wait()
        copy(r, 1).wait()
        return carry

    lax.fori_loop(0, tm, wait, 0, unroll=8)


def dispatch(dest3, h, n_rows):
    t, d = h.shape
    nt, _, tm = dest3.shape
    return pl.pallas_call(
        _dispatch_kernel,
        grid=(nt,),
        in_specs=[
            pl.BlockSpec((None, 2, tm), lambda i: (i, 0, 0), memory_space=pltpu.SMEM),
            pl.BlockSpec((tm, d), lambda i: (i, 0)),
            pl.BlockSpec(memory_space=pl.ANY),
        ],
        out_specs=pl.BlockSpec(memory_space=pl.ANY),
        out_shape=jax.ShapeDtypeStruct((n_rows, d), h.dtype),
        scratch_shapes=[pltpu.SemaphoreType.DMA(())],
        input_output_aliases={2: 0},
        compiler_params=_params(("arbitrary",)),
        name="moe_dispatch",
    )(dest3, h, jnp.zeros((n_rows, d), h.dtype))


def _experts_kernel(be_ref, nu_ref, x_ref, wg_ref, wu_ref, wd_ref, o_ref, wgb, wub, wdb):
    i = pl.program_id(0)
    changed = jnp.logical_or(i == 0, be_ref[i] != be_ref[jnp.maximum(i - 1, 0)])

    @pl.when(changed)
    def _():
        wgb[...] = wg_ref[...].astype(BF16)
        wub[...] = wu_ref[...].astype(BF16)
        wdb[...] = wd_ref[...].astype(BF16)

    @pl.when(i < nu_ref[0])
    def _():
        x = x_ref[...].astype(BF16)
        g = jnp.dot(x, wgb[...], preferred_element_type=F32)
        u = jnp.dot(x, wub[...], preferred_element_type=F32)
        a = ((g * jax.nn.sigmoid(g)) * u).astype(BF16)
        o_ref[...] = jnp.dot(a, wdb[...], preferred_element_type=F32)

    @pl.when(i >= nu_ref[0])
    def _():
        o_ref[...] = jnp.zeros_like(o_ref)


def experts(block_expert, n_used, xbuf, w_gate, w_up, w_down):
    n_rows, d = xbuf.shape
    hid = w_gate.shape[2]
    bm = MOE_ROWS
    nblk = n_rows // bm
    grid_spec = pltpu.PrefetchScalarGridSpec(
        num_scalar_prefetch=2,
        grid=(nblk,),
        in_specs=[
            pl.BlockSpec((bm, d), lambda i, be, nu: (jnp.minimum(i, nu[0] - 1), 0)),
            pl.BlockSpec((None, d, hid), lambda i, be, nu: (be[i], 0, 0)),
            pl.BlockSpec((None, d, hid), lambda i, be, nu: (be[i], 0, 0)),
            pl.BlockSpec((None, hid, d), lambda i, be, nu: (be[i], 0, 0)),
        ],
        out_specs=pl.BlockSpec((bm, d), lambda i, be, nu: (i, 0)),
        scratch_shapes=[pltpu.VMEM((d, hid), BF16), pltpu.VMEM((d, hid), BF16),
                        pltpu.VMEM((hid, d), BF16)],
    )
    return pl.pallas_call(
        _experts_kernel,
        grid_spec=grid_spec,
        out_shape=jax.ShapeDtypeStruct((n_rows, d), F32),
        compiler_params=_params(("arbitrary",)),
        name="moe_experts",
    )(block_expert, n_used, xbuf, w_gate, w_up, w_down)


def _combine_kernel(dest_ref, gate_ref, x_ref, g_ref, ybuf_ref, o_ref, buf, sem, *, final_norm):
    tm = x_ref.shape[0]

    def copy(r, k):
        return pltpu.make_async_copy(ybuf_ref.at[pl.ds(dest_ref[k, r], 1), :],
                                     buf.at[k, pl.ds(r, 1), :], sem)

    def start(r, carry):
        copy(r, 0).start()
        copy(r, 1).start()
        return carry

    lax.fori_loop(0, tm, start, 0, unroll=8)

    def wait(r, carry):
        copy(r, 0).wait()
        copy(r, 1).wait()
        return carry

    lax.fori_loop(0, tm, wait, 0, unroll=8)

    gates = gate_ref[...]
    y = gates[:, 0:1] * buf[0] + gates[:, 1:2] * buf[1]
    out = x_ref[...] + y
    if final_norm:
        out = _rms(out, g_ref[...])
    o_ref[...] = out


def combine(dest3, gates_t, x, ybuf, g_final, final_norm):
    t, d = x.shape
    nt, _, tm = dest3.shape
    return pl.pallas_call(
        functools.partial(_combine_kernel, final_norm=final_norm),
        grid=(nt,),
        in_specs=[
            pl.BlockSpec((None, 2, tm), lambda i: (i, 0, 0), memory_space=pltpu.SMEM),
            pl.BlockSpec((tm, 2), lambda i: (i, 0)),
            pl.BlockSpec((tm, d), lambda i: (i, 0)),
            pl.BlockSpec((1, d), lambda i: (0, 0)),
            pl.BlockSpec(memory_space=pl.ANY),
        ],
        out_specs=pl.BlockSpec((tm, d), lambda i: (i, 0)),
        out_shape=jax.ShapeDtypeStruct((t, d), F32),
        scratch_shapes=[pltpu.VMEM((2, tm, d), F32), pltpu.SemaphoreType.DMA(())],
        compiler_params=_params(("arbitrary",)),
        name="moe_combine",
    )(dest3, gates_t, x, g_final.reshape(1, d), ybuf)


def hier_moe_block(x1, h, lt, w_gate, w_up, w_down, g_final, final_norm, tm=256):
    t, d = x1.shape
    bm = MOE_ROWS
    eid, gate, rank, cnt = route(lt)
    counts = cnt[:, 0]
    padded = ((counts + bm - 1) // bm) * bm
    pend = jnp.cumsum(padded)
    pstart = pend - padded
    dest = pstart[eid] + rank
    n_rows = 2 * t + MOE_EXPERTS * bm
    nblk = n_rows // bm
    n_used = (pend[-1] // bm).astype(I32)
    blk = jnp.arange(nblk, dtype=I32)
    block_expert = jnp.searchsorted(pend, jnp.minimum(blk, n_used - 1) * bm, side='right').astype(I32)
    block_expert = jnp.minimum(block_expert, MOE_EXPERTS - 1)
    dest3 = dest.reshape(2, t // tm, tm).transpose(1, 0, 2)
    xbuf = dispatch(dest3, h, n_rows)
    ybuf = experts(block_expert, n_used.reshape(1), xbuf, w_gate, w_up, w_down)
    return combine(dest3, jnp.transpose(gate), x1, ybuf, g_final, final_norm)


def _attn_kernel(sink_ref, q_ref, kp_ref, kc_ref, kn_ref, vp_ref, vc_ref, vn_ref, bias_ref, o_ref):
    n = pl.program_id(1)
    nblk = pl.num_programs(1)
    blk = q_ref.shape[0]
    dh = SWA_HEAD_DIM
    qpk = SWA_Q_PER_KV
    nkv = kc_ref.shape[1] // dh
    pen_prev = jnp.where(n > 0, 0.0, NEG_INF)
    pen_next = jnp.where(n < nblk - 1, 0.0, NEG_INF)
    scale = dh ** -0.5
    for kv in range(nkv):
        heads = range(kv * qpk, (kv + 1) * qpk)
        q = jnp.concatenate([q_ref[:, hh * dh:(hh + 1) * dh] for hh in heads], axis=0).astype(BF16)
        ksl = slice(kv * dh, (kv + 1) * dh)
        bias = bias_ref[kv * qpk:(kv + 1) * qpk].reshape(qpk * blk, 3 * blk)
        sink = jnp.concatenate([jnp.full((blk, 1), sink_ref[0, hh], F32) for hh in heads], axis=0)

        def scores(k_ref, part):
            s = lax.dot_general(q, k_ref[:, ksl].astype(BF16), NT_DIMS, preferred_element_type=F32)
            return s * scale + bias[:, part * blk:(part + 1) * blk]

        s_p = scores(kp_ref, 0) + pen_prev
        s_c = scores(kc_ref, 1)
        s_n = scores(kn_ref, 2) + pen_next
        m = jnp.maximum(jnp.maximum(jnp.max(s_p, axis=-1, keepdims=True),
                                    jnp.max(s_c, axis=-1, keepdims=True)),
                        jnp.maximum(jnp.max(s_n, axis=-1, keepdims=True), sink))
        e_p = jnp.exp(s_p - m)
        e_c = jnp.exp(s_c - m)
        e_n = jnp.exp(s_n - m)
        den = (jnp.sum(e_p, axis=-1, keepdims=True) + jnp.sum(e_c, axis=-1, keepdims=True)
               + jnp.sum(e_n, axis=-1, keepdims=True) + jnp.exp(sink - m))
        o = jnp.dot(e_p.astype(BF16), vp_ref[:, ksl].astype(BF16), preferred_element_type=F32)
        o = o + jnp.dot(e_c.astype(BF16), vc_ref[:, ksl].astype(BF16), preferred_element_type=F32)
        o = o + jnp.dot(e_n.astype(BF16), vn_ref[:, ksl].astype(BF16), preferred_element_type=F32)
        o = o / den
        for j, hh in enumerate(heads):
            o_ref[:, hh * dh:(hh + 1) * dh] = o[j * blk:(j + 1) * blk, :].astype(BF16)


def banded_attention(qkv3, sink, bias):
    b, seq, width = qkv3.shape
    nh = sink.shape[1]
    qd = nh * SWA_HEAD_DIM
    kvd = (width - qd) // 2
    blk = SWA_BLOCK
    nblk = seq // blk
    kcol, vcol = qd // kvd, qd // kvd + 1

    def band(col, off):
        return pl.BlockSpec((None, blk, kvd),
                            lambda bi, n: (bi, jnp.clip(n + off, 0, nblk - 1), col))

    return pl.pallas_call(
        _attn_kernel,
        grid=(b, nblk),
        in_specs=[
            pl.BlockSpec(memory_space=pltpu.SMEM),
            pl.BlockSpec((None, blk, qd), lambda bi, n: (bi, n, 0)),
            band(kcol, -1), band(kcol, 0), band(kcol, 1),
            band(vcol, -1), band(vcol, 0), band(vcol, 1),
            pl.BlockSpec(bias.shape, lambda bi, n: (0, 0, 0)),
        ],
        out_specs=pl.BlockSpec((None, blk, qd), lambda bi, n: (bi, n, 0)),
        out_shape=jax.ShapeDtypeStruct((b, seq, qd), BF16),
        compiler_params=_params(("parallel", "parallel")),
        name="banded_attention",
    )(sink, qkv3, qkv3, qkv3, qkv3, qkv3, qkv3, qkv3, bias)


def _t5_bucket(rel):
    half = REL_BUCKETS // 2
    max_exact = half // 2
    n = jnp.abs(rel)
    large = max_exact + (jnp.log(jnp.maximum(n, 1).astype(F32) / max_exact)
                         / math.log(REL_MAX_DIST / max_exact) * (half - max_exact)).astype(I32)
    large = jnp.minimum(large, half - 1)
    return jnp.where(rel > 0, half, 0) + jnp.where(n < max_exact, n, large)


def _attention_bias(rel_bias):
    blk = SWA_BLOCK
    rel = jnp.arange(3 * blk)[None, :] - blk - jnp.arange(blk)[:, None]
    bias = rel_bias.astype(F32)[_t5_bucket(rel)].transpose(2, 0, 1)
    return jnp.where((jnp.abs(rel) <= SWA_WINDOW)[None], bias, NEG_INF)


def _rotary_tables(seq, dim):
    inv_freq = ROPE_BASE ** (-jnp.arange(0, dim, 2, dtype=F32) / dim)
    ang = jnp.arange(seq, dtype=F32)[:, None] * inv_freq[None, :]
    cos, sin = jnp.cos(ang), jnp.sin(ang)
    return jnp.concatenate([cos, cos], axis=1), jnp.concatenate([-sin, sin], axis=1)


def kernel(x, norm_mix_g, norm_ffn_g, norm_final_g, hyb_w_in, ret_decay_logit, s5_a_re, s5_a_im, s5_log_step, s5_b_re, s5_b_im, s5_c_re, s5_c_im, s5_d, s5_w_glu, s5_b_glu, hyb_w_out, swa_w_qkv, swa_sink, swa_w_o, rel_bias, moe_w_group, moe_b_group, moe_w_expert_router, moe_b_expert_router, moe_w_gate, moe_w_up, moe_w_down):
    nb, seq, d = x.shape
    t = nb * seq
    depth = norm_mix_g.shape[0]
    xt = x.reshape(t, d)
    for layer in range(depth):
        i = layer // 2
        if layer % 2 == 0:
            w = hyb_w_out.shape[1] // 2
            z = norm_matmul(xt, norm_mix_g[layer], hyb_w_in[i])
            z3 = z.reshape(nb, seq, z.shape[1])
            cos, sin = _rotary_tables(seq, w // RET_HEADS)
            log_gamma = jax.nn.log_sigmoid(ret_decay_logit[i].astype(F32))
            y_ret = retention(z3, log_gamma, cos, sin).reshape(t, w)
            lam, bmat, cmat = _s5_discretize(s5_a_re[i], s5_a_im[i], s5_log_step[i], s5_b_re[i],
                                             s5_b_im[i], s5_c_re[i], s5_c_im[i])
            u_t = jnp.transpose(z3[:, :, 4 * w:], (1, 0, 2))
            ys5 = s5_scan(u_t, lam, bmat, cmat).reshape(2, seq, nb * w)
            mix_in = glu_concat(y_ret, ys5, z, s5_d[i].astype(F32).reshape(-1), s5_w_glu[i],
                                s5_b_glu[i].astype(F32), nb)
            w_mix = hyb_w_out[i]
        else:
            qkv = norm_matmul(xt, norm_mix_g[layer], swa_w_qkv[i])
            mix_in = banded_attention(qkv.reshape(nb, seq, qkv.shape[1]),
                                      swa_sink[i].astype(F32).reshape(1, -1),
                                      _attention_bias(rel_bias)).reshape(t, -1)
            w_mix = swa_w_o[i]
        r_hi, r_lo, r_bias = _router_operands(moe_w_group[layer], moe_b_group[layer],
                                              moe_w_expert_router[layer], moe_b_expert_router[layer])
        x1, h, lt = proj_norm_router(mix_in, w_mix, xt, norm_ffn_g[layer], r_hi, r_lo, r_bias)
        last = layer == depth - 1
        xt = hier_moe_block(x1, h, lt, moe_w_gate[layer], moe_w_up[layer], moe_w_down[layer],
                            norm_final_g, final_norm=last)
    return xt.reshape(nb, seq, d)
```

```python
import functools
import math

import jax
import jax.numpy as jnp
from jax import lax
from jax.experimental import pallas as pl
from jax.experimental.pallas import tpu as pltpu

F32 = jnp.float32
BF16 = jnp.bfloat16
I32 = jnp.int32

RET_HEADS = 4
RET_CHUNK = 128
S5_GROUP_CH = 16
S5_STATE = 64
SWA_HEAD_DIM = 64
SWA_Q_PER_KV = 4
SWA_WINDOW = 128
SWA_BLOCK = 128
REL_BUCKETS = 32
REL_MAX_DIST = 128
MOE_GROUPS = 4
MOE_EXPERTS_PER_GROUP = 8
MOE_EXPERTS = MOE_GROUPS * MOE_EXPERTS_PER_GROUP
ROPE_BASE = 10000.0
RMS_EPS = 1e-6
GN_EPS = 1e-5
NEG_INF = -1e30
LOG2E = 1.4426950408889634

LANES = 128
SUBLANES = 8
V7X_VMEM_BYTES = 64 * 1024 * 1024
VMEM_LIMIT = V7X_VMEM_BYTES - 8 * 1024 * 1024

MOE_ROWS = 256
S5_CHUNK = 128
S5_COLS = 512
S5_KBLK = 128
ROUTER_ROWS = 128

NT_DIMS = (((1,), (1,)), ((), ()))
TN_DIMS = (((0,), (0,)), ((), ()))


def _params(semantics):
    return pltpu.CompilerParams(dimension_semantics=semantics, vmem_limit_bytes=VMEM_LIMIT)


def _rms(x, g):
    ms = jnp.mean(x * x, axis=-1, keepdims=True)
    return (x * lax.rsqrt(ms + RMS_EPS)) * g


def _norm_matmul_kernel(x_ref, g_ref, w_ref, o_ref):
    h = _rms(x_ref[...], g_ref[...])
    o_ref[...] = jnp.dot(h.astype(BF16), w_ref[...], preferred_element_type=F32)


def norm_matmul(x, g, w, tm=512):
    t, d = x.shape
    n = w.shape[1]
    return pl.pallas_call(
        _norm_matmul_kernel,
        grid=(t // tm,),
        in_specs=[
            pl.BlockSpec((tm, d), lambda i: (i, 0)),
            pl.BlockSpec((1, d), lambda i: (0, 0)),
            pl.BlockSpec((d, n), lambda i: (0, 0)),
        ],
        out_specs=pl.BlockSpec((tm, n), lambda i: (i, 0)),
        out_shape=jax.ShapeDtypeStruct((t, n), F32),
        compiler_params=_params(("parallel",)),
        name="norm_matmul",
    )(x, g.reshape(1, d), w.astype(BF16))


def _retention_kernel(lg_ref, q_ref, k_ref, v_ref, g_ref, cos_ref, sin_ref, o_ref,
                      qr_ref, kr_ref, inc_ref, st_ref):
    h = pl.program_id(1)
    lg_f = lg_ref[0, h]
    lg_b = lg_ref[1, h]
    seq, dk = q_ref.shape
    c = RET_CHUNK
    nc = seq // c

    cos = cos_ref[...]
    sin = sin_ref[...]
    q = q_ref[...]
    qr_ref[...] = q * cos + pltpu.roll(q, dk // 2, 1) * sin
    k = k_ref[...]
    kr_ref[...] = (k * cos + pltpu.roll(k, dk // 2, 1) * sin) * (dk ** -0.5)

    pos = lax.broadcasted_iota(I32, (c, dk), 0).astype(F32)
    kf_scale = jnp.exp((c - 1.0 - pos) * lg_f)
    qf_scale = jnp.exp((pos + 1.0) * lg_f)
    kb_scale = jnp.exp(pos * lg_b)
    qb_scale = jnp.exp((c - pos) * lg_b)
    rel = (lax.broadcasted_iota(I32, (c, c), 0) - lax.broadcasted_iota(I32, (c, c), 1)).astype(F32)
    mask = jnp.exp(jnp.abs(rel) * jnp.where(rel >= 0, lg_f, lg_b))
    dec_f = jnp.exp(jnp.full((dk, dk), c * lg_f, F32))
    dec_b = jnp.exp(jnp.full((dk, dk), c * lg_b, F32))

    def increments(n, carry):
        rows = pl.ds(pl.multiple_of(n * c, c), c)
        kc = kr_ref[rows, :]
        kk = jnp.concatenate([kc * kf_scale, kc * kb_scale], axis=1).astype(BF16)
        inc_ref[n] = lax.dot_general(kk, v_ref[rows, :].astype(BF16), TN_DIMS,
                                     preferred_element_type=F32)
        return carry

    lax.fori_loop(0, nc, increments, 0, unroll=2)

    def fwd(n, state):
        st_ref[n, :dk, :] = state.astype(BF16)
        return state * dec_f + inc_ref[n, :dk, :]

    lax.fori_loop(0, nc, fwd, jnp.zeros((dk, dk), F32))

    def bwd(i, state):
        n = nc - 1 - i
        st_ref[n, dk:, :] = state.astype(BF16)
        return state * dec_b + inc_ref[n, dk:, :]

    lax.fori_loop(0, nc, bwd, jnp.zeros((dk, dk), F32))

    def outputs(n, carry):
        rows = pl.ds(pl.multiple_of(n * c, c), c)
        qc = qr_ref[rows, :]
        s = lax.dot_general(qc.astype(BF16), kr_ref[rows, :].astype(BF16), NT_DIMS,
                            preferred_element_type=F32) * mask
        lhs = jnp.concatenate([s, qc * qf_scale, qc * qb_scale], axis=1).astype(BF16)
        rhs = jnp.concatenate([v_ref[rows, :].astype(BF16), st_ref[n]], axis=0)
        out = jnp.dot(lhs, rhs, preferred_element_type=F32)
        mu = jnp.mean(out, axis=-1, keepdims=True)
        cen = out - mu
        var = jnp.mean(cen * cen, axis=-1, keepdims=True)
        g = g_ref[rows, :]
        o_ref[rows, :] = (g * jax.nn.sigmoid(g)) * (cen * lax.rsqrt(var + GN_EPS))
        return carry

    lax.fori_loop(0, nc, outputs, 0, unroll=2)


def retention(z3, log_gamma, cos, sin):
    b, seq, _ = z3.shape
    nh = RET_HEADS
    dk = cos.shape[1]

    def col(off):
        return pl.BlockSpec((None, seq, dk), lambda bi, hi: (bi, 0, off + hi))

    return pl.pallas_call(
        _retention_kernel,
        grid=(b, nh),
        in_specs=[
            pl.BlockSpec(memory_space=pltpu.SMEM),
            col(0), col(nh), col(2 * nh), col(3 * nh),
            pl.BlockSpec((seq, dk), lambda bi, hi: (0, 0)),
            pl.BlockSpec((seq, dk), lambda bi, hi: (0, 0)),
        ],
        out_specs=pl.BlockSpec((None, seq, dk), lambda bi, hi: (bi, 0, hi)),
        out_shape=jax.ShapeDtypeStruct((b, seq, nh * dk), F32),
        scratch_shapes=[pltpu.VMEM((seq, dk), F32), pltpu.VMEM((seq, dk), F32),
                        pltpu.VMEM((seq // RET_CHUNK, 2 * dk, dk), F32),
                        pltpu.VMEM((seq // RET_CHUNK, 2 * dk, dk), BF16)],
        compiler_params=_params(("parallel", "parallel")),
        name="retention",
    )(log_gamma, z3, z3, z3, z3, cos, sin)


def _s5_kernel(u_ref, lam_ref, b_ref, c_ref, y_ref, up_ref, xre_ref, xim_ref, sre_ref, sim_ref):
    d = pl.program_id(0)
    n = pl.program_id(1)
    nb, cn, width = u_ref.shape
    srows = up_ref.shape[1]
    nk = width // S5_KBLK
    ncols = sre_ref.shape[1] // S5_COLS

    @pl.when(jnp.logical_and(d == 0, n == 0))
    def _():
        up_ref[...] = jnp.zeros_like(up_ref)

    @pl.when(n == 0)
    def _():
        sre_ref[...] = jnp.zeros_like(sre_ref)
        sim_ref[...] = jnp.zeros_like(sim_ref)

    for b in range(nb):
        up_ref[b, b:b + cn, :] = u_ref[b]
    u = up_ref[...].reshape(nb * srows, width).astype(BF16)
    ppc = S5_COLS // LANES
    for kb in range(nk):
        bu = jnp.dot(u[:, kb * S5_KBLK:(kb + 1) * S5_KBLK], b_ref[kb], preferred_element_type=F32)
        for j in range(ppc):
            xre_ref[kb * ppc + j] = bu[:, j * LANES:(j + 1) * LANES]
            xim_ref[kb * ppc + j] = bu[:, S5_COLS + j * LANES:S5_COLS + (j + 1) * LANES]

    for cb in range(ncols):
        cols = slice(cb * S5_COLS, (cb + 1) * S5_COLS)
        lr = jnp.broadcast_to(lam_ref[0:1, cols], (nb, S5_COLS))
        li = jnp.broadcast_to(lam_ref[1:2, cols], (nb, S5_COLS))

        def step(i, carry):
            xr, xi = carry
            t = i + d * (cn - 1 - 2 * i)
            rows = pl.ds(t, nb, stride=srows + 1)
            bur = jnp.concatenate([xre_ref[cb * ppc + j, rows, :] for j in range(ppc)], axis=1)
            bui = jnp.concatenate([xim_ref[cb * ppc + j, rows, :] for j in range(ppc)], axis=1)
            nxr = lr * xr - li * xi + bur
            nxi = lr * xi + li * xr + bui
            for j in range(ppc):
                xre_ref[cb * ppc + j, rows, :] = nxr[:, j * LANES:(j + 1) * LANES]
                xim_ref[cb * ppc + j, rows, :] = nxi[:, j * LANES:(j + 1) * LANES]
            return nxr, nxi

        xr, xi = lax.fori_loop(0, cn, step, (sre_ref[:, cols], sim_ref[:, cols]), unroll=8)
        sre_ref[:, cols] = xr
        sim_ref[:, cols] = xi

    for kb in range(nk):
        xr = jnp.concatenate([xre_ref[kb * ppc + j] for j in range(ppc)], axis=1).astype(BF16)
        xi = jnp.concatenate([xim_ref[kb * ppc + j] for j in range(ppc)], axis=1).astype(BF16)
        y = jnp.dot(xr, c_ref[kb, :S5_COLS, :], preferred_element_type=F32)
        y = y + jnp.dot(xi, c_ref[kb, S5_COLS:, :], preferred_element_type=F32)
        y = y.reshape(nb, srows, S5_KBLK)
        for b in range(nb):
            y_ref[b, :, kb * S5_KBLK:(kb + 1) * S5_KBLK] = y[b, b:b + cn, :]


def s5_scan(z3, lam, bmat, cmat):
    nb, seq, zw = z3.shape
    width = bmat.shape[1] * bmat.shape[2]
    nstate = lam.shape[2]
    cn = min(S5_CHUNK, seq)
    nch = seq // cn
    srows = cn + SUBLANES
    ucol = zw // width - 1

    def chunk(d, n):
        return n + d * (nch - 1 - 2 * n)

    return pl.pallas_call(
        _s5_kernel,
        grid=(2, nch),
        in_specs=[
            pl.BlockSpec((nb, cn, width), lambda d, n: (0, chunk(d, n), ucol)),
            pl.BlockSpec((None, 2, nstate), lambda d, n: (d, 0, 0)),
            pl.BlockSpec((None,) + bmat.shape[1:], lambda d, n: (d, 0, 0, 0)),
            pl.BlockSpec((None,) + cmat.shape[1:], lambda d, n: (d, 0, 0, 0)),
        ],
        out_specs=pl.BlockSpec((None, nb, cn, width), lambda d, n: (d, 0, chunk(d, n), 0)),
        out_shape=jax.ShapeDtypeStruct((2, nb, seq, width), F32),
        scratch_shapes=[
            pltpu.VMEM((nb, srows, width), F32),
            pltpu.VMEM((nstate // LANES, nb * srows, LANES), F32),
            pltpu.VMEM((nstate // LANES, nb * srows, LANES), F32),
            pltpu.VMEM((nb, nstate), F32),
            pltpu.VMEM((nb, nstate), F32),
        ],
        compiler_params=_params(("arbitrary", "arbitrary")),
        name="s5_scan",
    )(z3, lam, bmat, cmat)


def _s5_discretize(a_re, a_im, log_step, b_re, b_im, c_re, c_im):
    ng, npst = a_re.shape[1], a_re.shape[2]
    gpb = S5_KBLK // S5_GROUP_CH
    nk = ng // gpb
    eye = jnp.eye(gpb, dtype=F32)
    bre, bim = b_re.astype(F32), b_im.astype(F32)
    lams, bmats, cmats = [], [], []
    for direction in range(2):
        ar = a_re[direction].astype(F32)
        ai = a_im[direction].astype(F32)
        dt = jnp.exp(log_step[direction].astype(F32))[:, None]
        mag = jnp.exp(ar * dt)
        lam_re, lam_im = mag * jnp.cos(ai * dt), mag * jnp.sin(ai * dt)
        nr, ni = lam_re - 1.0, lam_im
        den = ar * ar + ai * ai
        coef_re = (nr * ar + ni * ai) / den
        coef_im = (ni * ar - nr * ai) / den
        bbar_re = coef_re[..., None] * bre - coef_im[..., None] * bim
        bbar_im = coef_re[..., None] * bim + coef_im[..., None] * bre

        def in_blocks(m):
            m4 = m.reshape(nk, gpb, npst, S5_GROUP_CH)
            return jnp.einsum('kgpc,gh->kgchp', m4, eye).reshape(nk, S5_KBLK, gpb * npst)

        def out_blocks(m):
            m4 = m.reshape(nk, gpb, S5_GROUP_CH, npst)
            return jnp.einsum('kgcp,gh->kgphc', m4, eye).reshape(nk, gpb * npst, S5_KBLK)

        lams.append(jnp.stack([lam_re.reshape(-1), lam_im.reshape(-1)]))
        bmats.append(jnp.concatenate([in_blocks(bbar_re), in_blocks(bbar_im)], axis=2))
        cmats.append(jnp.concatenate([out_blocks(c_re[direction].astype(F32)),
                                      -out_blocks(c_im[direction].astype(F32))], axis=1))
    return jnp.stack(lams), jnp.stack(bmats).astype(BF16), jnp.stack(cmats).astype(BF16)


def _glu_kernel(yr_ref, yf_ref, yb_ref, u_ref, d_ref, w_ref, b_ref, o_ref):
    w = yr_ref.shape[1]
    y = u_ref[...] * d_ref[...] + yf_ref[...] + yb_ref[...]
    y = jax.nn.gelu(y)
    gate = jax.nn.sigmoid(jnp.dot(y.astype(BF16), w_ref[...], preferred_element_type=F32) + b_ref[...])
    o_ref[:, :w] = yr_ref[...].astype(BF16)
    o_ref[:, w:] = (y * gate).astype(BF16)


def glu_concat(y_ret, ys5, z, d_skip, w_glu, b_glu, nb, tm=512):
    t, w = y_ret.shape
    seq = t // nb
    tm = min(tm, seq)
    nl = seq // tm
    ucol = z.shape[1] // w - 1
    return pl.pallas_call(
        _glu_kernel,
        grid=(nb, nl),
        in_specs=[
            pl.BlockSpec((tm, w), lambda b, i: (b * nl + i, 0)),
            pl.BlockSpec((None, tm, w), lambda b, i: (0, b * nl + i, 0)),
            pl.BlockSpec((None, tm, w), lambda b, i: (1, b * nl + i, 0)),
            pl.BlockSpec((tm, w), lambda b, i: (b * nl + i, ucol)),
            pl.BlockSpec((1, w), lambda b, i: (0, 0)),
            pl.BlockSpec((w, w), lambda b, i: (0, 0)),
            pl.BlockSpec((1, w), lambda b, i: (0, 0)),
        ],
        out_specs=pl.BlockSpec((tm, 2 * w), lambda b, i: (b * nl + i, 0)),
        out_shape=jax.ShapeDtypeStruct((t, 2 * w), BF16),
        compiler_params=_params(("parallel", "parallel")),
        name="glu_concat",
    )(y_ret, ys5, ys5, z, d_skip.reshape(1, w), w_glu.astype(BF16), b_glu.reshape(1, w))


def _proj_kernel(a_ref, w_ref, x_ref, g_ref, rh_ref, rl_ref, rb_ref, x1_ref, h_ref, lt_ref):
    x1 = x_ref[...] + jnp.dot(a_ref[...], w_ref[...], preferred_element_type=F32)
    x1_ref[...] = x1
    h = _rms(x1, g_ref[...])
    h_ref[...] = h
    h_hi = h.astype(BF16)
    h_lo = (h - h_hi.astype(F32)).astype(BF16)
    lt = lax.dot_general(rh_ref[...], h_hi, NT_DIMS, preferred_element_type=F32)
    lt = lt + lax.dot_general(rh_ref[...], h_lo, NT_DIMS, preferred_element_type=F32)
    lt = lt + lax.dot_general(rl_ref[...], h_hi, NT_DIMS, preferred_element_type=F32)
    lt_ref[...] = lt + rb_ref[...]


def proj_norm_router(a, w, x, g, r_hi, r_lo, r_bias, tm=512):
    t, d = x.shape
    k = a.shape[1]
    nr = r_hi.shape[0]
    return pl.pallas_call(
        _proj_kernel,
        grid=(t // tm,),
        in_specs=[
            pl.BlockSpec((tm, k), lambda i: (i, 0)),
            pl.BlockSpec((k, d), lambda i: (0, 0)),
            pl.BlockSpec((tm, d), lambda i: (i, 0)),
            pl.BlockSpec((1, d), lambda i: (0, 0)),
            pl.BlockSpec((nr, d), lambda i: (0, 0)),
            pl.BlockSpec((nr, d), lambda i: (0, 0)),
            pl.BlockSpec((nr, 1), lambda i: (0, 0)),
        ],
        out_specs=[
            pl.BlockSpec((tm, d), lambda i: (i, 0)),
            pl.BlockSpec((tm, d), lambda i: (i, 0)),
            pl.BlockSpec((nr, tm), lambda i: (0, i)),
        ],
        out_shape=[
            jax.ShapeDtypeStruct((t, d), F32),
            jax.ShapeDtypeStruct((t, d), F32),
            jax.ShapeDtypeStruct((nr, t), F32),
        ],
        compiler_params=_params(("parallel",)),
        name="proj_norm_router",
    )(a, w.astype(BF16), x, g.reshape(1, d), r_hi, r_lo, r_bias)


def _router_operands(w_group, b_group, w_er, b_er):
    d = w_group.shape[0]
    wt = jnp.concatenate([
        jnp.transpose(w_er.astype(F32), (0, 2, 1)).reshape(MOE_EXPERTS, d),
        jnp.transpose(w_group.astype(F32)),
        jnp.zeros((ROUTER_ROWS - MOE_EXPERTS - MOE_GROUPS, d), F32)], axis=0)
    bias = jnp.concatenate([
        b_er.astype(F32).reshape(-1), b_group.astype(F32),
        jnp.zeros((ROUTER_ROWS - MOE_EXPERTS - MOE_GROUPS,), F32)]).reshape(ROUTER_ROWS, 1)
    hi = wt.astype(BF16)
    lo = (wt - hi.astype(F32)).astype(BF16)
    return hi, lo, bias


def _route_kernel(lt_ref, eid_ref, gate_ref, rank_ref, cnt_ref, run_ref):
    i = pl.program_id(0)
    tm = lt_ref.shape[1]
    ne, npg, ng = MOE_EXPERTS, MOE_EXPERTS_PER_GROUP, MOE_GROUPS

    @pl.when(i == 0)
    def _():
        run_ref[...] = jnp.zeros_like(run_ref)

    gl = lt_ref[ne:ne + ng, :]
    gmax = jnp.max(gl, axis=0, keepdims=True)
    gidx = lax.broadcasted_iota(I32, (ng, tm), 0)
    gsel = jnp.min(jnp.where(gl == gmax, gidx, ng), axis=0, keepdims=True)
    p_g = 1.0 / jnp.sum(jnp.exp(gl - gmax), axis=0, keepdims=True)

    e8 = lt_ref[(ng - 1) * npg:ng * npg, :]
    for g in range(ng - 2, -1, -1):
        e8 = jnp.where(gsel == g, lt_ref[g * npg:(g + 1) * npg, :], e8)
    eidx = lax.broadcasted_iota(I32, (npg, tm), 0)
    m1 = jnp.max(e8, axis=0, keepdims=True)
    i1 = jnp.min(jnp.where(e8 == m1, eidx, npg), axis=0, keepdims=True)
    e8b = jnp.where(eidx == i1, -jnp.inf, e8)
    m2 = jnp.max(e8b, axis=0, keepdims=True)
    i2 = jnp.min(jnp.where(e8b == m2, eidx, npg), axis=0, keepdims=True)
    t2 = jnp.exp(m2 - m1)
    den = 1.0 + t2
    gate_ref[0:1, :] = (1.0 / den) * p_g
    gate_ref[1:2, :] = (t2 / den) * p_g
    id1 = gsel * npg + i1
    id2 = gsel * npg + i2
    eid_ref[0:1, :] = id1
    eid_ref[1:2, :] = id2

    rows = lax.broadcasted_iota(I32, (ne, tm), 0)
    oh1 = rows == id1
    oh2 = rows == id2
    both = jnp.where(oh1, 1.0, 0.0) + jnp.where(oh2, 1.0, 0.0)
    earlier = (lax.broadcasted_iota(I32, (tm, tm), 0) < lax.broadcasted_iota(I32, (tm, tm), 1))
    prefix = jnp.dot(both.astype(BF16), jnp.where(earlier, 1.0, 0.0).astype(BF16),
                     preferred_element_type=F32)
    base = prefix + run_ref[:, 0:1]
    rank_ref[0:1, :] = jnp.sum(jnp.where(oh1, base, 0.0), axis=0, keepdims=True).astype(I32)
    rank_ref[1:2, :] = jnp.sum(jnp.where(oh2, base, 0.0), axis=0, keepdims=True).astype(I32)
    run = run_ref[...] + jnp.sum(both, axis=1, keepdims=True)
    run_ref[...] = run
    cnt_ref[...] = run.astype(I32)


def route(lt, tm=512):
    nr, t = lt.shape
    two = lambda dt: jax.ShapeDtypeStruct((2, t), dt)
    return pl.pallas_call(
        _route_kernel,
        grid=(t // tm,),
        in_specs=[pl.BlockSpec((nr, tm), lambda i: (0, i))],
        out_specs=[
            pl.BlockSpec((2, tm), lambda i: (0, i)),
            pl.BlockSpec((2, tm), lambda i: (0, i)),
            pl.BlockSpec((2, tm), lambda i: (0, i)),
            pl.BlockSpec((MOE_EXPERTS, LANES), lambda i: (0, 0)),
        ],
        out_shape=[two(I32), two(F32), two(I32),
                   jax.ShapeDtypeStruct((MOE_EXPERTS, LANES), I32)],
        scratch_shapes=[pltpu.VMEM((MOE_EXPERTS, LANES), F32)],
        compiler_params=_params(("arbitrary",)),
        name="route",
    )(lt)


def _dispatch_kernel(dest_ref, h_ref, xin_ref, xbuf_ref, sem):
    del xin_ref
    tm = h_ref.shape[0]

    def copy(r, k):
        return pltpu.make_async_copy(h_ref.at[pl.ds(r, 1), :],
                                     xbuf_ref.at[pl.ds(dest_ref[k, r], 1), :], sem)

    def start(r, carry):
        copy(r, 0).start(priority=0)
        copy(r, 1).start(priority=1)
        return carry

    lax.fori_loop(0, tm, start, 0, unroll=8)

    def wait(r, carry):
        copy(r, 0).wait()
        copy(r, 1).wait()
        return carry

    lax.fori_loop(0, tm, wait, 0, unroll=8)


def dispatch(dest3, h, n_rows):
    t, d = h.shape
    nt, _, tm = dest3.shape
    return pl.pallas_call(
        _dispatch_kernel,
        grid=(nt,),
        in_specs=[
            pl.BlockSpec((None, 2, tm), lambda i: (i, 0, 0), memory_space=pltpu.SMEM),
            pl.BlockSpec((tm, d), lambda i: (i, 0)),
            pl.BlockSpec(memory_space=pl.ANY),
        ],
        out_specs=pl.BlockSpec(memory_space=pl.ANY),
        out_shape=jax.ShapeDtypeStruct((n_rows, d), h.dtype),
        scratch_shapes=[pltpu.SemaphoreType.DMA(())],
        input_output_aliases={2: 0},
        compiler_params=_params(("arbitrary",)),
        name="moe_dispatch",
    )(dest3, h, jnp.zeros((n_rows, d), h.dtype))


def _experts_kernel(be_ref, nu_ref, x_ref, wg_ref, wu_ref, wd_ref, o_ref, wgb, wub, wdb):
    i = pl.program_id(0)
    changed = jnp.logical_or(i == 0, be_ref[i] != be_ref[jnp.maximum(i - 1, 0)])

    @pl.when(changed)
    def _():
        wgb[...] = wg_ref[...].astype(BF16)
        wub[...] = wu_ref[...].astype(BF16)
        wdb[...] = wd_ref[...].astype(BF16)

    @pl.when(i < nu_ref[0])
    def _():
        x = x_ref[...].astype(BF16)
        g = jnp.dot(x, wgb[...], preferred_element_type=F32)
        u = jnp.dot(x, wub[...], preferred_element_type=F32)
        a = ((g * jax.nn.sigmoid(g)) * u).astype(BF16)
        o_ref[...] = jnp.dot(a, wdb[...], preferred_element_type=F32)

    @pl.when(i >= nu_ref[0])
    def _():
        o_ref[...] = jnp.zeros_like(o_ref)


def experts(block_expert, n_used, xbuf, w_gate, w_up, w_down):
    n_rows, d = xbuf.shape
    hid = w_gate.shape[2]
    bm = MOE_ROWS
    nblk = n_rows // bm
    grid_spec = pltpu.PrefetchScalarGridSpec(
        num_scalar_prefetch=2,
        grid=(nblk,),
        in_specs=[
            pl.BlockSpec((bm, d), lambda i, be, nu: (jnp.minimum(i, nu[0] - 1), 0)),
            pl.BlockSpec((None, d, hid), lambda i, be, nu: (be[i], 0, 0)),
            pl.BlockSpec((None, d, hid), lambda i, be, nu: (be[i], 0, 0)),
            pl.BlockSpec((None, hid, d), lambda i, be, nu: (be[i], 0, 0)),
        ],
        out_specs=pl.BlockSpec((bm, d), lambda i, be, nu: (i, 0)),
        scratch_shapes=[pltpu.VMEM((d, hid), BF16), pltpu.VMEM((d, hid), BF16),
                        pltpu.VMEM((hid, d), BF16)],
    )
    return pl.pallas_call(
        _experts_kernel,
        grid_spec=grid_spec,
        out_shape=jax.ShapeDtypeStruct((n_rows, d), F32),
        compiler_params=_params(("arbitrary",)),
        name="moe_experts",
    )(block_expert, n_used, xbuf, w_gate, w_up, w_down)


def _combine_kernel(dest_ref, gate_ref, x_ref, g_ref, ybuf_ref, o_ref, buf, sem, *, final_norm):
    tm = x_ref.shape[0]

    def copy(r, k):
        return pltpu.make_async_copy(ybuf_ref.at[pl.ds(dest_ref[k, r], 1), :],
                                     buf.at[k, pl.ds(r, 1), :], sem)

    def start(r, carry):
        copy(r, 0).start(priority=0)
        copy(r, 1).start(priority=1)
        return carry

    lax.fori_loop(0, tm, start, 0, unroll=8)

    def wait(r, carry):
        copy(r, 0).wait()
        copy(r, 1).wait()
        return carry

    lax.fori_loop(0, tm, wait, 0, unroll=8)

    gates = gate_ref[...]
    y = gates[:, 0:1] * buf[0] + gates[:, 1:2] * buf[1]
    out = x_ref[...] + y
    if final_norm:
        out = _rms(out, g_ref[...])
    o_ref[...] = out


def combine(dest3, gates_t, x, ybuf, g_final, final_norm):
    t, d = x.shape
    nt, _, tm = dest3.shape
    return pl.pallas_call(
        functools.partial(_combine_kernel, final_norm=final_norm),
        grid=(nt,),
        in_specs=[
            pl.BlockSpec((None, 2, tm), lambda i: (i, 0, 0), memory_space=pltpu.SMEM),
            pl.BlockSpec((tm, 2), lambda i: (i, 0)),
            pl.BlockSpec((tm, d), lambda i: (i, 0)),
            pl.BlockSpec((1, d), lambda i: (0, 0)),
            pl.BlockSpec(memory_space=pl.ANY),
        ],
        out_specs=pl.BlockSpec((tm, d), lambda i: (i, 0)),
        out_shape=jax.ShapeDtypeStruct((t, d), F32),
        scratch_shapes=[pltpu.VMEM((2, tm, d), F32), pltpu.SemaphoreType.DMA(())],
        compiler_params=_params(("arbitrary",)),
        name="moe_combine",
    )(dest3, gates_t, x, g_final.reshape(1, d), ybuf)


def hier_moe_block(x1, h, lt, w_gate, w_up, w_down, g_final, final_norm, tm=256):
    t, d = x1.shape
    bm = MOE_ROWS
    eid, gate, rank, cnt = route(lt)
    counts = cnt[:, 0]
    padded = ((counts + bm - 1) // bm) * bm
    pend = jnp.cumsum(padded)
    pstart = pend - padded
    experts_col = jnp.arange(MOE_EXPERTS, dtype=I32)[:, None, None]
    dest = rank + jnp.sum(jnp.where(eid[None] == experts_col, pstart[:, None, None], 0), axis=0)
    n_rows = 2 * t + MOE_EXPERTS * bm
    nblk = n_rows // bm
    n_used = (pend[-1] // bm).astype(I32)
    first_row = jnp.minimum(jnp.arange(nblk, dtype=I32), n_used - 1) * bm
    block_expert = jnp.sum(pend[None, :] <= first_row[:, None], axis=1).astype(I32)
    block_expert = jnp.minimum(block_expert, MOE_EXPERTS - 1)
    dest3 = dest.reshape(2, t // tm, tm).transpose(1, 0, 2)
    xbuf = dispatch(dest3, h, n_rows)
    ybuf = experts(block_expert, n_used.reshape(1), xbuf, w_gate, w_up, w_down)
    return combine(dest3, jnp.transpose(gate), x1, ybuf, g_final, final_norm)


def _attn_kernel(sink_ref, q_ref, kp_ref, kc_ref, kn_ref, vp_ref, vc_ref, vn_ref, bias_ref, o_ref):
    n = pl.program_id(1)
    nblk = pl.num_programs(1)
    blk = q_ref.shape[0]
    dh = SWA_HEAD_DIM
    nkv = kc_ref.shape[1] // dh
    masked = bias_ref.shape[1] - 1
    part_prev = jnp.where(n > 0, 0, masked)
    part_next = jnp.where(n < nblk - 1, 2, masked)
    lo = lax.broadcasted_iota(I32, (blk, 2 * dh), 1) < dh
    top = lax.broadcasted_iota(I32, (2 * blk, 1), 0) < blk
    ones_ext = jnp.concatenate([jnp.where(lo, 1.0, 0.0), jnp.where(lo, 0.0, 1.0)], axis=0).astype(BF16)
    qscale = (dh ** -0.5) * LOG2E

    for kv in range(nkv):
        col = slice((kv // 2) * 2 * dh, (kv // 2 + 1) * 2 * dh)

        def extend(ref):
            x = ref[:, col]
            r = pltpu.roll(x, dh, 1)
            x_lo, x_hi = (x, r) if kv % 2 == 0 else (r, x)
            return jnp.concatenate([jnp.where(lo, x_lo, 0.0), jnp.where(lo, 0.0, x_hi)],
                                   axis=0).astype(BF16)

        q2 = jnp.concatenate([q_ref[:, (2 * kv) * 2 * dh:(2 * kv + 1) * 2 * dh],
                              q_ref[:, (2 * kv + 1) * 2 * dh:(2 * kv + 2) * 2 * dh]], axis=0)
        q2 = (q2 * qscale).astype(BF16)

        def scores(k_ref, part):
            return lax.dot_general(q2, extend(k_ref), NT_DIMS,
                                   preferred_element_type=F32) + bias_ref[kv, part]

        s = [scores(kp_ref, part_prev), scores(kc_ref, 1), scores(kn_ref, part_next)]
        mx = jnp.maximum(jnp.maximum(s[0], s[1]), s[2])
        sk = [jnp.where(top, sink_ref[0, 4 * kv + par], sink_ref[0, 4 * kv + 2 + par])
              for par in range(2)]
        m = [jnp.maximum(jnp.max(mx[:, par * blk:(par + 1) * blk], axis=-1, keepdims=True), sk[par])
             for par in range(2)]
        acc = jnp.zeros((2 * blk, 4 * dh), F32)
        for sp, v_ref in zip(s, (vp_ref, vc_ref, vn_ref)):
            e = jnp.concatenate([jnp.exp2(sp[:, :blk] - m[0]), jnp.exp2(sp[:, blk:] - m[1])],
                                axis=1).astype(BF16)
            rhs = jnp.concatenate([extend(v_ref), ones_ext], axis=1)
            acc = acc + jnp.dot(e, rhs, preferred_element_type=F32)
        den = acc[:, 2 * dh:] + jnp.where(lo[:1], jnp.exp2(sk[0] - m[0]), jnp.exp2(sk[1] - m[1]))
        o = (acc[:, :2 * dh] / den).astype(BF16)
        o_ref[:, (2 * kv) * 2 * dh:(2 * kv + 1) * 2 * dh] = o[:blk]
        o_ref[:, (2 * kv + 1) * 2 * dh:(2 * kv + 2) * 2 * dh] = o[blk:]


def banded_attention(qkv3, sink, bias):
    b, seq, width = qkv3.shape
    nh = sink.shape[1]
    qd = nh * SWA_HEAD_DIM
    kvd = (width - qd) // 2
    blk = SWA_BLOCK
    nblk = seq // blk
    kcol, vcol = qd // kvd, qd // kvd + 1

    def band(col, off):
        return pl.BlockSpec((None, blk, kvd),
                            lambda bi, n: (bi, jnp.clip(n + off, 0, nblk - 1), col))

    return pl.pallas_call(
        _attn_kernel,
        grid=(b, nblk),
        in_specs=[
            pl.BlockSpec(memory_space=pltpu.SMEM),
            pl.BlockSpec((None, blk, qd), lambda bi, n: (bi, n, 0)),
            band(kcol, -1), band(kcol, 0), band(kcol, 1),
            band(vcol, -1), band(vcol, 0), band(vcol, 1),
            pl.BlockSpec(bias.shape, lambda bi, n: (0, 0, 0, 0)),
        ],
        out_specs=pl.BlockSpec((None, blk, qd), lambda bi, n: (bi, n, 0)),
        out_shape=jax.ShapeDtypeStruct((b, seq, qd), BF16),
        compiler_params=_params(("parallel", "parallel")),
        name="banded_attention",
    )(sink, qkv3, qkv3, qkv3, qkv3, qkv3, qkv3, qkv3, bias)


def _t5_bucket(rel):
    half = REL_BUCKETS // 2
    max_exact = half // 2
    n = jnp.abs(rel)
    large = max_exact + (jnp.log(jnp.maximum(n, 1).astype(F32) / max_exact)
                         / math.log(REL_MAX_DIST / max_exact) * (half - max_exact)).astype(I32)
    large = jnp.minimum(large, half - 1)
    return jnp.where(rel > 0, half, 0) + jnp.where(n < max_exact, n, large)


def _attention_bias(rel_bias):
    blk = SWA_BLOCK
    nh = rel_bias.shape[1]
    nkv = nh // SWA_Q_PER_KV
    rel = jnp.arange(3 * blk)[None, :] - blk - jnp.arange(blk)[:, None]
    onehot = (_t5_bucket(rel)[..., None] == jnp.arange(REL_BUCKETS)).astype(F32)
    bias = jnp.einsum('ijb,bh->hij', onehot, rel_bias.astype(F32), precision=lax.Precision.HIGHEST)
    bias = jnp.where((jnp.abs(rel) <= SWA_WINDOW)[None], bias, NEG_INF) * LOG2E
    tiles = bias.reshape(nkv, 2, 2, blk, 3, blk).transpose(0, 4, 1, 3, 2, 5)
    tiles = tiles.reshape(nkv, 3, 2 * blk, 2 * blk)
    masked = jnp.full((nkv, 1, 2 * blk, 2 * blk), NEG_INF * LOG2E, F32)
    return jnp.concatenate([tiles, masked], axis=1)


def _rotary_tables(seq, dim):
    inv_freq = ROPE_BASE ** (-jnp.arange(0, dim, 2, dtype=F32) / dim)
    ang = jnp.arange(seq, dtype=F32)[:, None] * inv_freq[None, :]
    cos, sin = jnp.cos(ang), jnp.sin(ang)
    return jnp.concatenate([cos, cos], axis=1), jnp.concatenate([-sin, sin], axis=1)


def kernel(x, norm_mix_g, norm_ffn_g, norm_final_g, hyb_w_in, ret_decay_logit, s5_a_re, s5_a_im, s5_log_step, s5_b_re, s5_b_im, s5_c_re, s5_c_im, s5_d, s5_w_glu, s5_b_glu, hyb_w_out, swa_w_qkv, swa_sink, swa_w_o, rel_bias, moe_w_group, moe_b_group, moe_w_expert_router, moe_b_expert_router, moe_w_gate, moe_w_up, moe_w_down):
    nb, seq, d = x.shape
    t = nb * seq
    depth = norm_mix_g.shape[0]
    xt = x.reshape(t, d)
    for layer in range(depth):
        i = layer // 2
        if layer % 2 == 0:
            w = hyb_w_out.shape[1] // 2
            z = norm_matmul(xt, norm_mix_g[layer], hyb_w_in[i])
            z3 = z.reshape(nb, seq, z.shape[1])
            cos, sin = _rotary_tables(seq, w // RET_HEADS)
            log_gamma = jax.nn.log_sigmoid(ret_decay_logit[i].astype(F32))
            y_ret = retention(z3, log_gamma, cos, sin).reshape(t, w)
            lam, bmat, cmat = _s5_discretize(s5_a_re[i], s5_a_im[i], s5_log_step[i], s5_b_re[i],
                                             s5_b_im[i], s5_c_re[i], s5_c_im[i])
            ys5 = s5_scan(z3, lam, bmat, cmat).reshape(2, t, w)
            mix_in = glu_concat(y_ret, ys5, z, s5_d[i].astype(F32).reshape(-1), s5_w_glu[i],
                                s5_b_glu[i].astype(F32), nb)
            w_mix = hyb_w_out[i]
        else:
            qkv = norm_matmul(xt, norm_mix_g[layer], swa_w_qkv[i])
            mix_in = banded_attention(qkv.reshape(nb, seq, qkv.shape[1]),
                                      swa_sink[i].astype(F32).reshape(1, -1) * LOG2E,
                                      _attention_bias(rel_bias)).reshape(t, -1)
            w_mix = swa_w_o[i]
        r_hi, r_lo, r_bias = _router_operands(moe_w_group[layer], moe_b_group[layer],
                                              moe_w_expert_router[layer], moe_b_expert_router[layer])
        x1, h, lt = proj_norm_router(mix_in, w_mix, xt, norm_ffn_g[layer], r_hi, r_lo, r_bias)
        last = layer == depth - 1
        xt = hier_moe_block(x1, h, lt, moe_w_gate[layer], moe_w_up[layer], moe_w_down[layer],
                            norm_final_g, final_norm=last)
    return xt.reshape(nb, seq, d)
```

```python
import functools
import math

import jax
import jax.numpy as jnp
from jax import lax
from jax.experimental import pallas as pl
from jax.experimental.pallas import tpu as pltpu

F32 = jnp.float32
BF16 = jnp.bfloat16
I32 = jnp.int32

RET_HEADS = 4
RET_CHUNK = 128
S5_GROUP_CH = 16
S5_STATE = 64
SWA_HEAD_DIM = 64
SWA_Q_PER_KV = 4
SWA_WINDOW = 128
SWA_BLOCK = 128
REL_BUCKETS = 32
REL_MAX_DIST = 128
MOE_GROUPS = 4
MOE_EXPERTS_PER_GROUP = 8
MOE_EXPERTS = MOE_GROUPS * MOE_EXPERTS_PER_GROUP
ROPE_BASE = 10000.0
RMS_EPS = 1e-6
GN_EPS = 1e-5
NEG_INF = -1e30
LOG2E = 1.4426950408889634

LANES = 128
SUBLANES = 8
V7X_VMEM_BYTES = 64 * 1024 * 1024
VMEM_LIMIT = V7X_VMEM_BYTES - 8 * 1024 * 1024

MOE_ROWS = 512
S5_CHUNK = 128
S5_COLS = 512
S5_KBLK = 128
ROUTER_ROWS = 128
RET_UNROLL = 8

NT_DIMS = (((1,), (1,)), ((), ()))
TN_DIMS = (((0,), (0,)), ((), ()))


def _params(semantics):
    return pltpu.CompilerParams(dimension_semantics=semantics, vmem_limit_bytes=VMEM_LIMIT)


def _rms(x, g):
    ms = jnp.mean(x * x, axis=-1, keepdims=True)
    return (x * lax.rsqrt(ms + RMS_EPS)) * g


def _norm_matmul_kernel(x_ref, g_ref, w_ref, o_ref):
    h = _rms(x_ref[...], g_ref[...])
    o_ref[...] = jnp.dot(h.astype(BF16), w_ref[...],
                         preferred_element_type=F32).astype(o_ref.dtype)


def norm_matmul(x, g, w, tm=512):
    t, d = x.shape
    n = w.shape[1]
    return pl.pallas_call(
        _norm_matmul_kernel,
        grid=(t // tm,),
        in_specs=[
            pl.BlockSpec((tm, d), lambda i: (i, 0)),
            pl.BlockSpec((1, d), lambda i: (0, 0)),
            pl.BlockSpec((d, n), lambda i: (0, 0)),
        ],
        out_specs=pl.BlockSpec((tm, n), lambda i: (i, 0)),
        out_shape=jax.ShapeDtypeStruct((t, n), BF16),
        compiler_params=_params(("parallel",)),
        name="norm_matmul",
    )(x, g.reshape(1, d), w.astype(BF16))


def _retention_kernel(lg_ref, q_ref, k_ref, v_ref, g_ref, cos_ref, sin_ref, o_ref,
                      qr_ref, kr_ref, inc_ref, st_ref):
    h = pl.program_id(1)
    lg_f = lg_ref[0, h]
    lg_b = lg_ref[1, h]
    seq, dk = q_ref.shape
    c = RET_CHUNK
    nc = seq // c

    cos = cos_ref[...]
    sin = sin_ref[...]
    q = q_ref[...].astype(F32)
    qr_ref[...] = q * cos + pltpu.roll(q, dk // 2, 1) * sin
    k = k_ref[...].astype(F32)
    kr_ref[...] = (k * cos + pltpu.roll(k, dk // 2, 1) * sin) * (dk ** -0.5)

    pos = lax.broadcasted_iota(I32, (c, dk), 0).astype(F32)
    kf_scale = jnp.exp((c - 1.0 - pos) * lg_f)
    qf_scale = jnp.exp((pos + 1.0) * lg_f)
    kb_scale = jnp.exp(pos * lg_b)
    qb_scale = jnp.exp((c - pos) * lg_b)
    rel = (lax.broadcasted_iota(I32, (c, c), 0) - lax.broadcasted_iota(I32, (c, c), 1)).astype(F32)
    mask = jnp.exp(jnp.abs(rel) * jnp.where(rel >= 0, lg_f, lg_b))
    dec_f = jnp.exp(jnp.full((dk, dk), c * lg_f, F32))
    dec_b = jnp.exp(jnp.full((dk, dk), c * lg_b, F32))

    def increments(n, carry):
        rows = pl.ds(pl.multiple_of(n * c, c), c)
        kc = kr_ref[rows, :]
        kk = jnp.concatenate([kc * kf_scale, kc * kb_scale], axis=1).astype(BF16)
        inc_ref[n] = lax.dot_general(kk, v_ref[rows, :], TN_DIMS,
                                     preferred_element_type=F32)
        return carry

    lax.fori_loop(0, nc, increments, 0, unroll=min(RET_UNROLL, nc))

    def fwd(n, state):
        st_ref[n, :dk, :] = state.astype(BF16)
        return state * dec_f + inc_ref[n, :dk, :]

    lax.fori_loop(0, nc, fwd, jnp.zeros((dk, dk), F32))

    def bwd(i, state):
        n = nc - 1 - i
        st_ref[n, dk:, :] = state.astype(BF16)
        return state * dec_b + inc_ref[n, dk:, :]

    lax.fori_loop(0, nc, bwd, jnp.zeros((dk, dk), F32))

    def outputs(n, carry):
        rows = pl.ds(pl.multiple_of(n * c, c), c)
        qc = qr_ref[rows, :]
        s = lax.dot_general(qc.astype(BF16), kr_ref[rows, :].astype(BF16), NT_DIMS,
                            preferred_element_type=F32) * mask
        lhs = jnp.concatenate([s, qc * qf_scale, qc * qb_scale], axis=1).astype(BF16)
        rhs = jnp.concatenate([v_ref[rows, :], st_ref[n]], axis=0)
        out = jnp.dot(lhs, rhs, preferred_element_type=F32)
        mu = jnp.mean(out, axis=-1, keepdims=True)
        cen = out - mu
        var = jnp.mean(cen * cen, axis=-1, keepdims=True)
        g = g_ref[rows, :].astype(F32)
        o_ref[rows, :] = ((g * jax.nn.sigmoid(g)) * (cen * lax.rsqrt(var + GN_EPS))).astype(BF16)
        return carry

    lax.fori_loop(0, nc, outputs, 0, unroll=min(RET_UNROLL, nc))


def retention(z3, log_gamma, cos, sin):
    b, seq, _ = z3.shape
    nh = RET_HEADS
    dk = cos.shape[1]

    def col(off):
        return pl.BlockSpec((None, seq, dk), lambda bi, hi: (bi, 0, off + hi))

    return pl.pallas_call(
        _retention_kernel,
        grid=(b, nh),
        in_specs=[
            pl.BlockSpec(memory_space=pltpu.SMEM),
            col(0), col(nh), col(2 * nh), col(3 * nh),
            pl.BlockSpec((seq, dk), lambda bi, hi: (0, 0)),
            pl.BlockSpec((seq, dk), lambda bi, hi: (0, 0)),
        ],
        out_specs=pl.BlockSpec((None, seq, dk), lambda bi, hi: (bi, 0, hi)),
        out_shape=jax.ShapeDtypeStruct((b, seq, nh * dk), BF16),
        scratch_shapes=[pltpu.VMEM((seq, dk), F32), pltpu.VMEM((seq, dk), F32),
                        pltpu.VMEM((seq // RET_CHUNK, 2 * dk, dk), F32),
                        pltpu.VMEM((seq // RET_CHUNK, 2 * dk, dk), BF16)],
        compiler_params=_params(("parallel", "parallel")),
        name="retention",
    )(log_gamma, z3, z3, z3, z3, cos, sin)


def _s5_kernel(u_ref, lam_ref, b_ref, c_ref, y_ref, up_ref, xre_ref, xim_ref, sre_ref, sim_ref):
    d = pl.program_id(0)
    n = pl.program_id(1)
    nb, cn, width = u_ref.shape
    srows = up_ref.shape[1]
    nk = width // S5_KBLK
    ncols = sre_ref.shape[1] // S5_COLS

    @pl.when(jnp.logical_and(d == 0, n == 0))
    def _():
        up_ref[...] = jnp.zeros_like(up_ref)

    @pl.when(n == 0)
    def _():
        sre_ref[...] = jnp.zeros_like(sre_ref)
        sim_ref[...] = jnp.zeros_like(sim_ref)

    for b in range(nb):
        up_ref[b, b:b + cn, :] = u_ref[b].astype(F32)
    u = up_ref[...].reshape(nb * srows, width).astype(BF16)
    ppc = S5_COLS // LANES
    for kb in range(nk):
        bu = jnp.dot(u[:, kb * S5_KBLK:(kb + 1) * S5_KBLK], b_ref[kb], preferred_element_type=F32)
        for j in range(ppc):
            xre_ref[kb * ppc + j] = bu[:, j * LANES:(j + 1) * LANES]
            xim_ref[kb * ppc + j] = bu[:, S5_COLS + j * LANES:S5_COLS + (j + 1) * LANES]

    for cb in range(ncols):
        cols = slice(cb * S5_COLS, (cb + 1) * S5_COLS)
        lr = jnp.broadcast_to(lam_ref[0:1, cols], (nb, S5_COLS))
        li = jnp.broadcast_to(lam_ref[1:2, cols], (nb, S5_COLS))

        def step(i, carry):
            xr, xi = carry
            t = i + d * (cn - 1 - 2 * i)
            rows = pl.ds(t, nb, stride=srows + 1)
            bur = jnp.concatenate([xre_ref[cb * ppc + j, rows, :] for j in range(ppc)], axis=1)
            bui = jnp.concatenate([xim_ref[cb * ppc + j, rows, :] for j in range(ppc)], axis=1)
            nxr = lr * xr - li * xi + bur
            nxi = lr * xi + li * xr + bui
            for j in range(ppc):
                xre_ref[cb * ppc + j, rows, :] = nxr[:, j * LANES:(j + 1) * LANES]
                xim_ref[cb * ppc + j, rows, :] = nxi[:, j * LANES:(j + 1) * LANES]
            return nxr, nxi

        xr, xi = lax.fori_loop(0, cn, step, (sre_ref[:, cols], sim_ref[:, cols]), unroll=8)
        sre_ref[:, cols] = xr
        sim_ref[:, cols] = xi

    for kb in range(nk):
        xr = jnp.concatenate([xre_ref[kb * ppc + j] for j in range(ppc)], axis=1).astype(BF16)
        xi = jnp.concatenate([xim_ref[kb * ppc + j] for j in range(ppc)], axis=1).astype(BF16)
        y = jnp.dot(xr, c_ref[kb, :S5_COLS, :], preferred_element_type=F32)
        y = y + jnp.dot(xi, c_ref[kb, S5_COLS:, :], preferred_element_type=F32)
        y = y.reshape(nb, srows, S5_KBLK)
        for b in range(nb):
            y_ref[b, :, kb * S5_KBLK:(kb + 1) * S5_KBLK] = y[b, b:b + cn, :]


def s5_scan(z3, lam, bmat, cmat):
    nb, seq, zw = z3.shape
    width = bmat.shape[1] * bmat.shape[2]
    nstate = lam.shape[2]
    cn = min(S5_CHUNK, seq)
    nch = seq // cn
    srows = cn + SUBLANES
    ucol = zw // width - 1

    def chunk(d, n):
        return n + d * (nch - 1 - 2 * n)

    return pl.pallas_call(
        _s5_kernel,
        grid=(2, nch),
        in_specs=[
            pl.BlockSpec((nb, cn, width), lambda d, n: (0, chunk(d, n), ucol)),
            pl.BlockSpec((None, 2, nstate), lambda d, n: (d, 0, 0)),
            pl.BlockSpec((None,) + bmat.shape[1:], lambda d, n: (d, 0, 0, 0)),
            pl.BlockSpec((None,) + cmat.shape[1:], lambda d, n: (d, 0, 0, 0)),
        ],
        out_specs=pl.BlockSpec((None, nb, cn, width), lambda d, n: (d, 0, chunk(d, n), 0)),
        out_shape=jax.ShapeDtypeStruct((2, nb, seq, width), F32),
        scratch_shapes=[
            pltpu.VMEM((nb, srows, width), F32),
            pltpu.VMEM((nstate // LANES, nb * srows, LANES), F32),
            pltpu.VMEM((nstate // LANES, nb * srows, LANES), F32),
            pltpu.VMEM((nb, nstate), F32),
            pltpu.VMEM((nb, nstate), F32),
        ],
        compiler_params=_params(("arbitrary", "arbitrary")),
        name="s5_scan",
    )(z3, lam, bmat, cmat)


def _s5_discretize(a_re, a_im, log_step, b_re, b_im, c_re, c_im):
    ng, npst = a_re.shape[1], a_re.shape[2]
    gpb = S5_KBLK // S5_GROUP_CH
    nk = ng // gpb
    eye = jnp.eye(gpb, dtype=F32)
    bre, bim = b_re.astype(F32), b_im.astype(F32)
    lams, bmats, cmats = [], [], []
    for direction in range(2):
        ar = a_re[direction].astype(F32)
        ai = a_im[direction].astype(F32)
        dt = jnp.exp(log_step[direction].astype(F32))[:, None]
        mag = jnp.exp(ar * dt)
        lam_re, lam_im = mag * jnp.cos(ai * dt), mag * jnp.sin(ai * dt)
        nr, ni = lam_re - 1.0, lam_im
        den = ar * ar + ai * ai
        coef_re = (nr * ar + ni * ai) / den
        coef_im = (ni * ar - nr * ai) / den
        bbar_re = coef_re[..., None] * bre - coef_im[..., None] * bim
        bbar_im = coef_re[..., None] * bim + coef_im[..., None] * bre

        def in_blocks(m):
            m4 = m.reshape(nk, gpb, npst, S5_GROUP_CH)
            return jnp.einsum('kgpc,gh->kgchp', m4, eye).reshape(nk, S5_KBLK, gpb * npst)

        def out_blocks(m):
            m4 = m.reshape(nk, gpb, S5_GROUP_CH, npst)
            return jnp.einsum('kgcp,gh->kgphc', m4, eye).reshape(nk, gpb * npst, S5_KBLK)

        lams.append(jnp.stack([lam_re.reshape(-1), lam_im.reshape(-1)]))
        bmats.append(jnp.concatenate([in_blocks(bbar_re), in_blocks(bbar_im)], axis=2))
        cmats.append(jnp.concatenate([out_blocks(c_re[direction].astype(F32)),
                                      -out_blocks(c_im[direction].astype(F32))], axis=1))
    return jnp.stack(lams), jnp.stack(bmats).astype(BF16), jnp.stack(cmats).astype(BF16)


def _glu_kernel(yr_ref, yf_ref, yb_ref, u_ref, d_ref, w_ref, b_ref, o_ref):
    w = yr_ref.shape[1]
    y = u_ref[...].astype(F32) * d_ref[...] + yf_ref[...] + yb_ref[...]
    y = jax.nn.gelu(y)
    gate = jax.nn.sigmoid(jnp.dot(y.astype(BF16), w_ref[...], preferred_element_type=F32) + b_ref[...])
    o_ref[:, :w] = yr_ref[...]
    o_ref[:, w:] = (y * gate).astype(BF16)


def glu_concat(y_ret, ys5, z, d_skip, w_glu, b_glu, nb, tm=512):
    t, w = y_ret.shape
    seq = t // nb
    tm = min(tm, seq)
    nl = seq // tm
    ucol = z.shape[1] // w - 1
    return pl.pallas_call(
        _glu_kernel,
        grid=(nb, nl),
        in_specs=[
            pl.BlockSpec((tm, w), lambda b, i: (b * nl + i, 0)),
            pl.BlockSpec((None, tm, w), lambda b, i: (0, b * nl + i, 0)),
            pl.BlockSpec((None, tm, w), lambda b, i: (1, b * nl + i, 0)),
            pl.BlockSpec((tm, w), lambda b, i: (b * nl + i, ucol)),
            pl.BlockSpec((1, w), lambda b, i: (0, 0)),
            pl.BlockSpec((w, w), lambda b, i: (0, 0)),
            pl.BlockSpec((1, w), lambda b, i: (0, 0)),
        ],
        out_specs=pl.BlockSpec((tm, 2 * w), lambda b, i: (b * nl + i, 0)),
        out_shape=jax.ShapeDtypeStruct((t, 2 * w), BF16),
        compiler_params=_params(("parallel", "parallel")),
        name="glu_concat",
    )(y_ret, ys5, ys5, z, d_skip.reshape(1, w), w_glu.astype(BF16), b_glu.reshape(1, w))


def _proj_kernel(a_ref, w_ref, x_ref, g_ref, rh_ref, rl_ref, rb_ref, x1_ref, h_ref, lt_ref):
    x1 = x_ref[...] + jnp.dot(a_ref[...], w_ref[...], preferred_element_type=F32)
    x1_ref[...] = x1
    h = _rms(x1, g_ref[...])
    h_ref[...] = h
    h_hi = h.astype(BF16)
    h_lo = (h - h_hi.astype(F32)).astype(BF16)
    lt = lax.dot_general(rh_ref[...], h_hi, NT_DIMS, preferred_element_type=F32)
    lt = lt + lax.dot_general(rh_ref[...], h_lo, NT_DIMS, preferred_element_type=F32)
    lt = lt + lax.dot_general(rl_ref[...], h_hi, NT_DIMS, preferred_element_type=F32)
    lt_ref[...] = lt + rb_ref[...]


def proj_norm_router(a, w, x, g, r_hi, r_lo, r_bias, tm=512):
    t, d = x.shape
    k = a.shape[1]
    nr = r_hi.shape[0]
    return pl.pallas_call(
        _proj_kernel,
        grid=(t // tm,),
        in_specs=[
            pl.BlockSpec((tm, k), lambda i: (i, 0)),
            pl.BlockSpec((k, d), lambda i: (0, 0)),
            pl.BlockSpec((tm, d), lambda i: (i, 0)),
            pl.BlockSpec((1, d), lambda i: (0, 0)),
            pl.BlockSpec((nr, d), lambda i: (0, 0)),
            pl.BlockSpec((nr, d), lambda i: (0, 0)),
            pl.BlockSpec((nr, 1), lambda i: (0, 0)),
        ],
        out_specs=[
            pl.BlockSpec((tm, d), lambda i: (i, 0)),
            pl.BlockSpec((tm, d), lambda i: (i, 0)),
            pl.BlockSpec((nr, tm), lambda i: (0, i)),
        ],
        out_shape=[
            jax.ShapeDtypeStruct((t, d), F32),
            jax.ShapeDtypeStruct((t, d), F32),
            jax.ShapeDtypeStruct((nr, t), F32),
        ],
        compiler_params=_params(("parallel",)),
        name="proj_norm_router",
    )(a, w.astype(BF16), x, g.reshape(1, d), r_hi, r_lo, r_bias)


def _router_operands(w_group, b_group, w_er, b_er):
    d = w_group.shape[0]
    wt = jnp.concatenate([
        jnp.transpose(w_er.astype(F32), (0, 2, 1)).reshape(MOE_EXPERTS, d),
        jnp.transpose(w_group.astype(F32)),
        jnp.zeros((ROUTER_ROWS - MOE_EXPERTS - MOE_GROUPS, d), F32)], axis=0)
    bias = jnp.concatenate([
        b_er.astype(F32).reshape(-1), b_group.astype(F32),
        jnp.zeros((ROUTER_ROWS - MOE_EXPERTS - MOE_GROUPS,), F32)]).reshape(ROUTER_ROWS, 1)
    hi = wt.astype(BF16)
    lo = (wt - hi.astype(F32)).astype(BF16)
    return hi, lo, bias


def _route_kernel(lt_ref, eid_ref, gate_ref, rank_ref, cnt_ref, run_ref):
    i = pl.program_id(0)
    tm = lt_ref.shape[1]
    ne, npg, ng = MOE_EXPERTS, MOE_EXPERTS_PER_GROUP, MOE_GROUPS

    @pl.when(i == 0)
    def _():
        run_ref[...] = jnp.zeros_like(run_ref)

    gl = lt_ref[ne:ne + ng, :]
    gmax = jnp.max(gl, axis=0, keepdims=True)
    gidx = lax.broadcasted_iota(I32, (ng, tm), 0)
    gsel = jnp.min(jnp.where(gl == gmax, gidx, ng), axis=0, keepdims=True)
    p_g = 1.0 / jnp.sum(jnp.exp(gl - gmax), axis=0, keepdims=True)

    e8 = lt_ref[(ng - 1) * npg:ng * npg, :]
    for g in range(ng - 2, -1, -1):
        e8 = jnp.where(gsel == g, lt_ref[g * npg:(g + 1) * npg, :], e8)
    eidx = lax.broadcasted_iota(I32, (npg, tm), 0)
    m1 = jnp.max(e8, axis=0, keepdims=True)
    i1 = jnp.min(jnp.where(e8 == m1, eidx, npg), axis=0, keepdims=True)
    e8b = jnp.where(eidx == i1, -jnp.inf, e8)
    m2 = jnp.max(e8b, axis=0, keepdims=True)
    i2 = jnp.min(jnp.where(e8b == m2, eidx, npg), axis=0, keepdims=True)
    t2 = jnp.exp(m2 - m1)
    den = 1.0 + t2
    gate_ref[0:1, :] = (1.0 / den) * p_g
    gate_ref[1:2, :] = (t2 / den) * p_g
    id1 = gsel * npg + i1
    id2 = gsel * npg + i2
    eid_ref[0:1, :] = id1
    eid_ref[1:2, :] = id2

    rows = lax.broadcasted_iota(I32, (ne, tm), 0)
    oh1 = rows == id1
    oh2 = rows == id2
    both = jnp.where(oh1, 1.0, 0.0) + jnp.where(oh2, 1.0, 0.0)
    earlier = (lax.broadcasted_iota(I32, (tm, tm), 0) < lax.broadcasted_iota(I32, (tm, tm), 1))
    prefix = jnp.dot(both.astype(BF16), jnp.where(earlier, 1.0, 0.0).astype(BF16),
                     preferred_element_type=F32)
    base = prefix + run_ref[:, 0:1]
    rank_ref[0:1, :] = jnp.sum(jnp.where(oh1, base, 0.0), axis=0, keepdims=True).astype(I32)
    rank_ref[1:2, :] = jnp.sum(jnp.where(oh2, base, 0.0), axis=0, keepdims=True).astype(I32)
    run = run_ref[...] + jnp.sum(both, axis=1, keepdims=True)
    run_ref[...] = run
    cnt_ref[...] = run.astype(I32)


def route(lt, tm=512):
    nr, t = lt.shape
    two = lambda dt: jax.ShapeDtypeStruct((2, t), dt)
    return pl.pallas_call(
        _route_kernel,
        grid=(t // tm,),
        in_specs=[pl.BlockSpec((nr, tm), lambda i: (0, i))],
        out_specs=[
            pl.BlockSpec((2, tm), lambda i: (0, i)),
            pl.BlockSpec((2, tm), lambda i: (0, i)),
            pl.BlockSpec((2, tm), lambda i: (0, i)),
            pl.BlockSpec((MOE_EXPERTS, LANES), lambda i: (0, 0)),
        ],
        out_shape=[two(I32), two(F32), two(I32),
                   jax.ShapeDtypeStruct((MOE_EXPERTS, LANES), I32)],
        scratch_shapes=[pltpu.VMEM((MOE_EXPERTS, LANES), F32)],
        compiler_params=_params(("arbitrary",)),
        name="route",
    )(lt)


def _dispatch_kernel(dest_ref, h_ref, xin_ref, xbuf_ref, sem):
    del xin_ref
    tm = h_ref.shape[0]

    def copy(r, k):
        return pltpu.make_async_copy(h_ref.at[pl.ds(r, 1), :],
                                     xbuf_ref.at[pl.ds(dest_ref[k, r], 1), :], sem)

    def start(r, carry):
        copy(r, 0).start(priority=0)
        copy(r, 1).start(priority=1)
        return carry

    lax.fori_loop(0, tm, start, 0, unroll=8)

    def wait(r, carry):
        copy(r, 0).wait()
        copy(r, 1).wait()
        return carry

    lax.fori_loop(0, tm, wait, 0, unroll=8)


def dispatch(dest3, h, n_rows):
    t, d = h.shape
    nt, _, tm = dest3.shape
    return pl.pallas_call(
        _dispatch_kernel,
        grid=(nt,),
        in_specs=[
            pl.BlockSpec((None, 2, tm), lambda i: (i, 0, 0), memory_space=pltpu.SMEM),
            pl.BlockSpec((tm, d), lambda i: (i, 0)),
            pl.BlockSpec(memory_space=pl.ANY),
        ],
        out_specs=pl.BlockSpec(memory_space=pl.ANY),
        out_shape=jax.ShapeDtypeStruct((n_rows, d), h.dtype),
        scratch_shapes=[pltpu.SemaphoreType.DMA(())],
        input_output_aliases={2: 0},
        compiler_params=_params(("arbitrary",)),
        name="moe_dispatch",
    )(dest3, h, jnp.zeros((n_rows, d), h.dtype))


def _experts_kernel(be_ref, nu_ref, x_ref, wg_ref, wu_ref, wd_ref, o_ref, wgb, wub, wdb):
    i = pl.program_id(0)
    changed = jnp.logical_or(i == 0, be_ref[i] != be_ref[jnp.maximum(i - 1, 0)])

    @pl.when(changed)
    def _():
        wgb[...] = wg_ref[...].astype(BF16)
        wub[...] = wu_ref[...].astype(BF16)
        wdb[...] = wd_ref[...].astype(BF16)

    @pl.when(i < nu_ref[0])
    def _():
        x = x_ref[...].astype(BF16)
        g = jnp.dot(x, wgb[...], preferred_element_type=F32)
        u = jnp.dot(x, wub[...], preferred_element_type=F32)
        a = ((g * jax.nn.sigmoid(g)) * u).astype(BF16)
        o_ref[...] = jnp.dot(a, wdb[...], preferred_element_type=F32)

    @pl.when(i >= nu_ref[0])
    def _():
        o_ref[...] = jnp.zeros_like(o_ref)


def experts(layer, block_expert, n_used, xbuf, w_gate, w_up, w_down):
    n_rows, d = xbuf.shape
    hid = w_gate.shape[3]
    bm = MOE_ROWS
    nblk = n_rows // bm
    grid_spec = pltpu.PrefetchScalarGridSpec(
        num_scalar_prefetch=2,
        grid=(nblk,),
        in_specs=[
            pl.BlockSpec((bm, d), lambda i, be, nu: (jnp.minimum(i, nu[0] - 1), 0)),
            pl.BlockSpec((None, None, d, hid), lambda i, be, nu: (layer, be[i], 0, 0)),
            pl.BlockSpec((None, None, d, hid), lambda i, be, nu: (layer, be[i], 0, 0)),
            pl.BlockSpec((None, None, hid, d), lambda i, be, nu: (layer, be[i], 0, 0)),
        ],
        out_specs=pl.BlockSpec((bm, d), lambda i, be, nu: (i, 0)),
        scratch_shapes=[pltpu.VMEM((d, hid), BF16), pltpu.VMEM((d, hid), BF16),
                        pltpu.VMEM((hid, d), BF16)],
    )
    return pl.pallas_call(
        _experts_kernel,
        grid_spec=grid_spec,
        out_shape=jax.ShapeDtypeStruct((n_rows, d), F32),
        compiler_params=_params(("arbitrary",)),
        name="moe_experts",
    )(block_expert, n_used, xbuf, w_gate, w_up, w_down)


def _combine_kernel(dest_ref, gate_ref, x_ref, g_ref, ybuf_ref, o_ref, buf, sem, *, final_norm):
    tm = x_ref.shape[0]

    def copy(r, k):
        return pltpu.make_async_copy(ybuf_ref.at[pl.ds(dest_ref[k, r], 1), :],
                                     buf.at[k, pl.ds(r, 1), :], sem)

    def start(r, carry):
        copy(r, 0).start(priority=0)
        copy(r, 1).start(priority=1)
        return carry

    lax.fori_loop(0, tm, start, 0, unroll=8)

    def wait(r, carry):
        copy(r, 0).wait()
        copy(r, 1).wait()
        return carry

    lax.fori_loop(0, tm, wait, 0, unroll=8)

    gates = gate_ref[...]
    y = gates[:, 0:1] * buf[0] + gates[:, 1:2] * buf[1]
    out = x_ref[...] + y
    if final_norm:
        out = _rms(out, g_ref[...])
    o_ref[...] = out


def combine(dest3, gates_t, x, ybuf, g_final, final_norm):
    t, d = x.shape
    nt, _, tm = dest3.shape
    return pl.pallas_call(
        functools.partial(_combine_kernel, final_norm=final_norm),
        grid=(nt,),
        in_specs=[
            pl.BlockSpec((None, 2, tm), lambda i: (i, 0, 0), memory_space=pltpu.SMEM),
            pl.BlockSpec((tm, 2), lambda i: (i, 0)),
            pl.BlockSpec((tm, d), lambda i: (i, 0)),
            pl.BlockSpec((1, d), lambda i: (0, 0)),
            pl.BlockSpec(memory_space=pl.ANY),
        ],
        out_specs=pl.BlockSpec((tm, d), lambda i: (i, 0)),
        out_shape=jax.ShapeDtypeStruct((t, d), F32),
        scratch_shapes=[pltpu.VMEM((2, tm, d), F32), pltpu.SemaphoreType.DMA(())],
        compiler_params=_params(("arbitrary",)),
        name="moe_combine",
    )(dest3, gates_t, x, g_final.reshape(1, d), ybuf)


def hier_moe_block(layer, x1, h, lt, w_gate, w_up, w_down, g_final, final_norm, tm=256):
    t, d = x1.shape
    bm = MOE_ROWS
    eid, gate, rank, cnt = route(lt)
    counts = cnt[:, 0]
    padded = ((counts + bm - 1) // bm) * bm
    pend = jnp.cumsum(padded)
    pstart = pend - padded
    experts_col = jnp.arange(MOE_EXPERTS, dtype=I32)[:, None, None]
    dest = rank + jnp.sum(jnp.where(eid[None] == experts_col, pstart[:, None, None], 0), axis=0)
    n_rows = 2 * t + MOE_EXPERTS * bm
    nblk = n_rows // bm
    n_used = (pend[-1] // bm).astype(I32)
    first_row = jnp.minimum(jnp.arange(nblk, dtype=I32), n_used - 1) * bm
    block_expert = jnp.sum(pend[None, :] <= first_row[:, None], axis=1).astype(I32)
    block_expert = jnp.minimum(block_expert, MOE_EXPERTS - 1)
    dest3 = dest.reshape(2, t // tm, tm).transpose(1, 0, 2)
    xbuf = dispatch(dest3, h, n_rows)
    ybuf = experts(layer, block_expert, n_used.reshape(1), xbuf, w_gate, w_up, w_down)
    return combine(dest3, jnp.transpose(gate), x1, ybuf, g_final, final_norm)


def _attn_kernel(sink_ref, q_ref, kp_ref, kc_ref, kn_ref, vp_ref, vc_ref, vn_ref, bias_ref, o_ref):
    n = pl.program_id(1)
    nblk = pl.num_programs(1)
    blk = q_ref.shape[0]
    dh = SWA_HEAD_DIM
    nkv = kc_ref.shape[1] // dh
    masked = bias_ref.shape[1] - 1
    part_prev = jnp.where(n > 0, 0, masked)
    part_next = jnp.where(n < nblk - 1, 2, masked)
    lo = lax.broadcasted_iota(I32, (blk, 2 * dh), 1) < dh
    top = lax.broadcasted_iota(I32, (2 * blk, 2 * dh), 0) < blk
    lo2 = lax.broadcasted_iota(I32, (2 * blk, 2 * dh), 1) < dh
    ones_ext = jnp.concatenate([jnp.where(lo, 1.0, 0.0), jnp.where(lo, 0.0, 1.0)], axis=0).astype(BF16)
    qscale = (dh ** -0.5) * LOG2E

    for kv in range(nkv):
        col = slice((kv // 2) * 2 * dh, (kv // 2 + 1) * 2 * dh)

        def extend(ref):
            x = ref[:, col].astype(F32)
            r = pltpu.roll(x, dh, 1)
            x_lo, x_hi = (x, r) if kv % 2 == 0 else (r, x)
            return jnp.concatenate([jnp.where(lo, x_lo, 0.0), jnp.where(lo, 0.0, x_hi)],
                                   axis=0).astype(BF16)

        q2 = jnp.concatenate([q_ref[:, (2 * kv) * 2 * dh:(2 * kv + 1) * 2 * dh],
                              q_ref[:, (2 * kv + 1) * 2 * dh:(2 * kv + 2) * 2 * dh]], axis=0)
        q2 = (q2.astype(F32) * qscale).astype(BF16)

        def scores(k_ref, part):
            return lax.dot_general(q2, extend(k_ref), NT_DIMS,
                                   preferred_element_type=F32) + bias_ref[kv, part]

        s = [scores(kp_ref, part_prev), scores(kc_ref, 1), scores(kn_ref, part_next)]
        mx = jnp.maximum(jnp.maximum(s[0], s[1]), s[2])
        sk = [jnp.where(top, sink_ref[0, 4 * kv + par], sink_ref[0, 4 * kv + 2 + par])
              for par in range(2)]
        m = [jnp.maximum(jnp.broadcast_to(jnp.max(mx[:, par * blk:(par + 1) * blk], axis=-1,
                                                  keepdims=True), (2 * blk, 2 * dh)), sk[par])
             for par in range(2)]
        acc = jnp.zeros((2 * blk, 4 * dh), F32)
        for sp, v_ref in zip(s, (vp_ref, vc_ref, vn_ref)):
            e = jnp.concatenate([jnp.exp2(sp[:, :blk] - m[0]), jnp.exp2(sp[:, blk:] - m[1])],
                                axis=1).astype(BF16)
            rhs = jnp.concatenate([extend(v_ref), ones_ext], axis=1)
            acc = acc + jnp.dot(e, rhs, preferred_element_type=F32)
        den = acc[:, 2 * dh:] + jnp.exp2(jnp.where(lo2, sk[0] - m[0], sk[1] - m[1]))
        o = (acc[:, :2 * dh] / den).astype(BF16)
        o_ref[:, (2 * kv) * 2 * dh:(2 * kv + 1) * 2 * dh] = o[:blk]
        o_ref[:, (2 * kv + 1) * 2 * dh:(2 * kv + 2) * 2 * dh] = o[blk:]


def banded_attention(qkv3, sink, bias):
    b, seq, width = qkv3.shape
    nh = sink.shape[1]
    qd = nh * SWA_HEAD_DIM
    kvd = (width - qd) // 2
    blk = SWA_BLOCK
    nblk = seq // blk
    kcol, vcol = qd // kvd, qd // kvd + 1

    def band(col, off):
        return pl.BlockSpec((None, blk, kvd),
                            lambda bi, n: (bi, jnp.clip(n + off, 0, nblk - 1), col))

    return pl.pallas_call(
        _attn_kernel,
        grid=(b, nblk),
        in_specs=[
            pl.BlockSpec(memory_space=pltpu.SMEM),
            pl.BlockSpec((None, blk, qd), lambda bi, n: (bi, n, 0)),
            band(kcol, -1), band(kcol, 0), band(kcol, 1),
            band(vcol, -1), band(vcol, 0), band(vcol, 1),
            pl.BlockSpec(bias.shape, lambda bi, n: (0, 0, 0, 0)),
        ],
        out_specs=pl.BlockSpec((None, blk, qd), lambda bi, n: (bi, n, 0)),
        out_shape=jax.ShapeDtypeStruct((b, seq, qd), BF16),
        compiler_params=_params(("parallel", "parallel")),
        name="banded_attention",
    )(sink, qkv3, qkv3, qkv3, qkv3, qkv3, qkv3, qkv3, bias)


def _t5_bucket(rel):
    half = REL_BUCKETS // 2
    max_exact = half // 2
    n = jnp.abs(rel)
    large = max_exact + (jnp.log(jnp.maximum(n, 1).astype(F32) / max_exact)
                         / math.log(REL_MAX_DIST / max_exact) * (half - max_exact)).astype(I32)
    large = jnp.minimum(large, half - 1)
    return jnp.where(rel > 0, half, 0) + jnp.where(n < max_exact, n, large)


def _attention_bias(rel_bias):
    blk = SWA_BLOCK
    nh = rel_bias.shape[1]
    nkv = nh // SWA_Q_PER_KV
    rel = jnp.arange(3 * blk)[None, :] - blk - jnp.arange(blk)[:, None]
    onehot = (_t5_bucket(rel)[..., None] == jnp.arange(REL_BUCKETS)).astype(F32)
    bias = jnp.einsum('ijb,bh->hij', onehot, rel_bias.astype(F32), precision=lax.Precision.HIGHEST)
    bias = jnp.where((jnp.abs(rel) <= SWA_WINDOW)[None], bias, NEG_INF) * LOG2E
    tiles = bias.reshape(nkv, 2, 2, blk, 3, blk).transpose(0, 4, 1, 3, 2, 5)
    tiles = tiles.reshape(nkv, 3, 2 * blk, 2 * blk)
    masked = jnp.full((nkv, 1, 2 * blk, 2 * blk), NEG_INF * LOG2E, F32)
    return jnp.concatenate([tiles, masked], axis=1)


def _rotary_tables(seq, dim):
    inv_freq = ROPE_BASE ** (-jnp.arange(0, dim, 2, dtype=F32) / dim)
    ang = jnp.arange(seq, dtype=F32)[:, None] * inv_freq[None, :]
    cos, sin = jnp.cos(ang), jnp.sin(ang)
    return jnp.concatenate([cos, cos], axis=1), jnp.concatenate([-sin, sin], axis=1)


def kernel(x, norm_mix_g, norm_ffn_g, norm_final_g, hyb_w_in, ret_decay_logit, s5_a_re, s5_a_im, s5_log_step, s5_b_re, s5_b_im, s5_c_re, s5_c_im, s5_d, s5_w_glu, s5_b_glu, hyb_w_out, swa_w_qkv, swa_sink, swa_w_o, rel_bias, moe_w_group, moe_b_group, moe_w_expert_router, moe_b_expert_router, moe_w_gate, moe_w_up, moe_w_down):
    nb, seq, d = x.shape
    t = nb * seq
    depth = norm_mix_g.shape[0]
    xt = x.reshape(t, d)
    for layer in range(depth):
        i = layer // 2
        if layer % 2 == 0:
            w = hyb_w_out.shape[1] // 2
            z = norm_matmul(xt, norm_mix_g[layer], hyb_w_in[i])
            z3 = z.reshape(nb, seq, z.shape[1])
            cos, sin = _rotary_tables(seq, w // RET_HEADS)
            log_gamma = jax.nn.log_sigmoid(ret_decay_logit[i].astype(F32))
            y_ret = retention(z3, log_gamma, cos, sin).reshape(t, w)
            lam, bmat, cmat = _s5_discretize(s5_a_re[i], s5_a_im[i], s5_log_step[i], s5_b_re[i],
                                             s5_b_im[i], s5_c_re[i], s5_c_im[i])
            ys5 = s5_scan(z3, lam, bmat, cmat).reshape(2, t, w)
            mix_in = glu_concat(y_ret, ys5, z, s5_d[i].astype(F32).reshape(-1), s5_w_glu[i],
                                s5_b_glu[i].astype(F32), nb)
            w_mix = hyb_w_out[i]
        else:
            qkv = norm_matmul(xt, norm_mix_g[layer], swa_w_qkv[i])
            mix_in = banded_attention(qkv.reshape(nb, seq, qkv.shape[1]),
                                      swa_sink[i].astype(F32).reshape(1, -1) * LOG2E,
                                      _attention_bias(rel_bias)).reshape(t, -1)
            w_mix = swa_w_o[i]
        r_hi, r_lo, r_bias = _router_operands(moe_w_group[layer], moe_b_group[layer],
                                              moe_w_expert_router[layer], moe_b_expert_router[layer])
        x1, h, lt = proj_norm_router(mix_in, w_mix, xt, norm_ffn_g[layer], r_hi, r_lo, r_bias)
        last = layer == depth - 1
        xt = hier_moe_block(layer, x1, h, lt, moe_w_gate, moe_w_up, moe_w_down,
                            norm_final_g, final_norm=last)
    return xt.reshape(nb, seq, d)
```

```python
import functools
import math

import jax
import jax.numpy as jnp
from jax import lax
from jax.experimental import pallas as pl
from jax.experimental.pallas import tpu as pltpu

F32 = jnp.float32
BF16 = jnp.bfloat16
I32 = jnp.int32

RET_HEADS = 4
RET_CHUNK = 128
S5_GROUP_CH = 16
S5_STATE = 64
SWA_HEAD_DIM = 64
SWA_Q_PER_KV = 4
SWA_WINDOW = 128
SWA_BLOCK = 128
REL_BUCKETS = 32
REL_MAX_DIST = 128
MOE_GROUPS = 4
MOE_EXPERTS_PER_GROUP = 8
MOE_EXPERTS = MOE_GROUPS * MOE_EXPERTS_PER_GROUP
ROPE_BASE = 10000.0
RMS_EPS = 1e-6
GN_EPS = 1e-5
NEG_INF = -1e30
LOG2E = 1.4426950408889634

LANES = 128
SUBLANES = 8
V7X_VMEM_BYTES = 64 * 1024 * 1024
VMEM_LIMIT = V7X_VMEM_BYTES - 8 * 1024 * 1024

MOE_ROWS = 512
MOE_TOKENS = 512
S5_CHUNK = 128
S5_COLS = 512
S5_KBLK = 128
ROUTER_ROWS = 128
RET_UNROLL = 8

NT_DIMS = (((1,), (1,)), ((), ()))
TN_DIMS = (((0,), (0,)), ((), ()))


def _params(semantics):
    return pltpu.CompilerParams(dimension_semantics=semantics, vmem_limit_bytes=VMEM_LIMIT)


def _rms(x, g):
    ms = jnp.mean(x * x, axis=-1, keepdims=True)
    return (x * lax.rsqrt(ms + RMS_EPS)) * g


def _norm_matmul_kernel(x_ref, g_ref, w_ref, o_ref):
    h = _rms(x_ref[...], g_ref[...])
    o_ref[...] = jnp.dot(h.astype(BF16), w_ref[...],
                         preferred_element_type=F32).astype(o_ref.dtype)


def norm_matmul(x, g, w, tm=512):
    t, d = x.shape
    n = w.shape[1]
    return pl.pallas_call(
        _norm_matmul_kernel,
        grid=(t // tm,),
        in_specs=[
            pl.BlockSpec((tm, d), lambda i: (i, 0)),
            pl.BlockSpec((1, d), lambda i: (0, 0)),
            pl.BlockSpec((d, n), lambda i: (0, 0)),
        ],
        out_specs=pl.BlockSpec((tm, n), lambda i: (i, 0)),
        out_shape=jax.ShapeDtypeStruct((t, n), BF16),
        compiler_params=_params(("parallel",)),
        name="norm_matmul",
    )(x, g.reshape(1, d), w.astype(BF16))


def _retention_kernel(lg_ref, q_ref, k_ref, v_ref, g_ref, cos_ref, sin_ref, o_ref,
                      qr_ref, kr_ref, inc_ref, st_ref):
    h = pl.program_id(1)
    lg_f = lg_ref[0, h]
    lg_b = lg_ref[1, h]
    seq, dk = q_ref.shape
    c = RET_CHUNK
    nc = seq // c

    cos = cos_ref[...]
    sin = sin_ref[...]
    q = q_ref[...].astype(F32)
    qr_ref[...] = q * cos + pltpu.roll(q, dk // 2, 1) * sin
    k = k_ref[...].astype(F32)
    kr_ref[...] = (k * cos + pltpu.roll(k, dk // 2, 1) * sin) * (dk ** -0.5)

    pos = lax.broadcasted_iota(I32, (c, dk), 0).astype(F32)
    kf_scale = jnp.exp((c - 1.0 - pos) * lg_f)
    qf_scale = jnp.exp((pos + 1.0) * lg_f)
    kb_scale = jnp.exp(pos * lg_b)
    qb_scale = jnp.exp((c - pos) * lg_b)
    rel = (lax.broadcasted_iota(I32, (c, c), 0) - lax.broadcasted_iota(I32, (c, c), 1)).astype(F32)
    mask = jnp.exp(jnp.abs(rel) * jnp.where(rel >= 0, lg_f, lg_b))
    dec_f = jnp.exp(jnp.full((dk, dk), c * lg_f, F32))
    dec_b = jnp.exp(jnp.full((dk, dk), c * lg_b, F32))

    def increments(n, carry):
        rows = pl.ds(pl.multiple_of(n * c, c), c)
        kc = kr_ref[rows, :]
        kk = jnp.concatenate([kc * kf_scale, kc * kb_scale], axis=1).astype(BF16)
        inc_ref[n] = lax.dot_general(kk, v_ref[rows, :], TN_DIMS,
                                     preferred_element_type=F32)
        return carry

    lax.fori_loop(0, nc, increments, 0, unroll=min(RET_UNROLL, nc))

    def fwd(n, state):
        st_ref[n, :dk, :] = state.astype(BF16)
        return state * dec_f + inc_ref[n, :dk, :]

    lax.fori_loop(0, nc, fwd, jnp.zeros((dk, dk), F32))

    def bwd(i, state):
        n = nc - 1 - i
        st_ref[n, dk:, :] = state.astype(BF16)
        return state * dec_b + inc_ref[n, dk:, :]

    lax.fori_loop(0, nc, bwd, jnp.zeros((dk, dk), F32))

    def outputs(n, carry):
        rows = pl.ds(pl.multiple_of(n * c, c), c)
        qc = qr_ref[rows, :]
        s = lax.dot_general(qc.astype(BF16), kr_ref[rows, :].astype(BF16), NT_DIMS,
                            preferred_element_type=F32) * mask
        lhs = jnp.concatenate([s, qc * qf_scale, qc * qb_scale], axis=1).astype(BF16)
        rhs = jnp.concatenate([v_ref[rows, :], st_ref[n]], axis=0)
        out = jnp.dot(lhs, rhs, preferred_element_type=F32)
        mu = jnp.mean(out, axis=-1, keepdims=True)
        cen = out - mu
        var = jnp.mean(cen * cen, axis=-1, keepdims=True)
        g = g_ref[rows, :].astype(F32)
        o_ref[rows, :] = ((g * jax.nn.sigmoid(g)) * (cen * lax.rsqrt(var + GN_EPS))).astype(BF16)
        return carry

    lax.fori_loop(0, nc, outputs, 0, unroll=min(RET_UNROLL, nc))


def retention(z3, log_gamma, cos, sin):
    b, seq, _ = z3.shape
    nh = RET_HEADS
    dk = cos.shape[1]

    def col(off):
        return pl.BlockSpec((None, seq, dk), lambda bi, hi: (bi, 0, off + hi))

    return pl.pallas_call(
        _retention_kernel,
        grid=(b, nh),
        in_specs=[
            pl.BlockSpec(memory_space=pltpu.SMEM),
            col(0), col(nh), col(2 * nh), col(3 * nh),
            pl.BlockSpec((seq, dk), lambda bi, hi: (0, 0)),
            pl.BlockSpec((seq, dk), lambda bi, hi: (0, 0)),
        ],
        out_specs=pl.BlockSpec((None, seq, dk), lambda bi, hi: (bi, 0, hi)),
        out_shape=jax.ShapeDtypeStruct((b, seq, nh * dk), BF16),
        scratch_shapes=[pltpu.VMEM((seq, dk), F32), pltpu.VMEM((seq, dk), F32),
                        pltpu.VMEM((seq // RET_CHUNK, 2 * dk, dk), F32),
                        pltpu.VMEM((seq // RET_CHUNK, 2 * dk, dk), BF16)],
        compiler_params=_params(("parallel", "parallel")),
        name="retention",
    )(log_gamma, z3, z3, z3, z3, cos, sin)


def _s5_kernel(u_ref, lam_ref, b_ref, c_ref, y_ref, up_ref, xre_ref, xim_ref, sre_ref, sim_ref):
    d = pl.program_id(0)
    n = pl.program_id(1)
    nb, cn, width = u_ref.shape
    srows = up_ref.shape[1]
    nk = width // S5_KBLK
    ncols = sre_ref.shape[1] // S5_COLS

    @pl.when(jnp.logical_and(d == 0, n == 0))
    def _():
        up_ref[...] = jnp.zeros_like(up_ref)

    @pl.when(n == 0)
    def _():
        sre_ref[...] = jnp.zeros_like(sre_ref)
        sim_ref[...] = jnp.zeros_like(sim_ref)

    for b in range(nb):
        up_ref[b, b:b + cn, :] = u_ref[b].astype(F32)
    u = up_ref[...].reshape(nb * srows, width).astype(BF16)
    ppc = S5_COLS // LANES
    for kb in range(nk):
        bu = jnp.dot(u[:, kb * S5_KBLK:(kb + 1) * S5_KBLK], b_ref[kb], preferred_element_type=F32)
        for j in range(ppc):
            xre_ref[kb * ppc + j] = bu[:, j * LANES:(j + 1) * LANES]
            xim_ref[kb * ppc + j] = bu[:, S5_COLS + j * LANES:S5_COLS + (j + 1) * LANES]

    for cb in range(ncols):
        cols = slice(cb * S5_COLS, (cb + 1) * S5_COLS)
        lr = jnp.broadcast_to(lam_ref[0:1, cols], (nb, S5_COLS))
        li = jnp.broadcast_to(lam_ref[1:2, cols], (nb, S5_COLS))

        def step(i, carry):
            xr, xi = carry
            t = i + d * (cn - 1 - 2 * i)
            rows = pl.ds(t, nb, stride=srows + 1)
            bur = jnp.concatenate([xre_ref[cb * ppc + j, rows, :] for j in range(ppc)], axis=1)
            bui = jnp.concatenate([xim_ref[cb * ppc + j, rows, :] for j in range(ppc)], axis=1)
            nxr = lr * xr - li * xi + bur
            nxi = lr * xi + li * xr + bui
            for j in range(ppc):
                xre_ref[cb * ppc + j, rows, :] = nxr[:, j * LANES:(j + 1) * LANES]
                xim_ref[cb * ppc + j, rows, :] = nxi[:, j * LANES:(j + 1) * LANES]
            return nxr, nxi

        xr, xi = lax.fori_loop(0, cn, step, (sre_ref[:, cols], sim_ref[:, cols]), unroll=8)
        sre_ref[:, cols] = xr
        sim_ref[:, cols] = xi

    for kb in range(nk):
        xr = jnp.concatenate([xre_ref[kb * ppc + j] for j in range(ppc)], axis=1).astype(BF16)
        xi = jnp.concatenate([xim_ref[kb * ppc + j] for j in range(ppc)], axis=1).astype(BF16)
        y = jnp.dot(xr, c_ref[kb, :S5_COLS, :], preferred_element_type=F32)
        y = y + jnp.dot(xi, c_ref[kb, S5_COLS:, :], preferred_element_type=F32)
        y = y.reshape(nb, srows, S5_KBLK)
        for b in range(nb):
            y_ref[b, :, kb * S5_KBLK:(kb + 1) * S5_KBLK] = y[b, b:b + cn, :]


def s5_scan(z3, lam, bmat, cmat):
    nb, seq, zw = z3.shape
    width = bmat.shape[1] * bmat.shape[2]
    nstate = lam.shape[2]
    cn = min(S5_CHUNK, seq)
    nch = seq // cn
    srows = cn + SUBLANES
    ucol = zw // width - 1

    def chunk(d, n):
        return n + d * (nch - 1 - 2 * n)

    return pl.pallas_call(
        _s5_kernel,
        grid=(2, nch),
        in_specs=[
            pl.BlockSpec((nb, cn, width), lambda d, n: (0, chunk(d, n), ucol)),
            pl.BlockSpec((None, 2, nstate), lambda d, n: (d, 0, 0)),
            pl.BlockSpec((None,) + bmat.shape[1:], lambda d, n: (d, 0, 0, 0)),
            pl.BlockSpec((None,) + cmat.shape[1:], lambda d, n: (d, 0, 0, 0)),
        ],
        out_specs=pl.BlockSpec((None, nb, cn, width), lambda d, n: (d, 0, chunk(d, n), 0)),
        out_shape=jax.ShapeDtypeStruct((2, nb, seq, width), F32),
        scratch_shapes=[
            pltpu.VMEM((nb, srows, width), F32),
            pltpu.VMEM((nstate // LANES, nb * srows, LANES), F32),
            pltpu.VMEM((nstate // LANES, nb * srows, LANES), F32),
            pltpu.VMEM((nb, nstate), F32),
            pltpu.VMEM((nb, nstate), F32),
        ],
        compiler_params=_params(("arbitrary", "arbitrary")),
        name="s5_scan",
    )(z3, lam, bmat, cmat)


def _s5_discretize(a_re, a_im, log_step, b_re, b_im, c_re, c_im):
    ng, npst = a_re.shape[1], a_re.shape[2]
    gpb = S5_KBLK // S5_GROUP_CH
    nk = ng // gpb
    eye = jnp.eye(gpb, dtype=F32)
    bre, bim = b_re.astype(F32), b_im.astype(F32)
    lams, bmats, cmats = [], [], []
    for direction in range(2):
        ar = a_re[direction].astype(F32)
        ai = a_im[direction].astype(F32)
        dt = jnp.exp(log_step[direction].astype(F32))[:, None]
        mag = jnp.exp(ar * dt)
        lam_re, lam_im = mag * jnp.cos(ai * dt), mag * jnp.sin(ai * dt)
        nr, ni = lam_re - 1.0, lam_im
        den = ar * ar + ai * ai
        coef_re = (nr * ar + ni * ai) / den
        coef_im = (ni * ar - nr * ai) / den
        bbar_re = coef_re[..., None] * bre - coef_im[..., None] * bim
        bbar_im = coef_re[..., None] * bim + coef_im[..., None] * bre

        def in_blocks(m):
            m4 = m.reshape(nk, gpb, npst, S5_GROUP_CH)
            return jnp.einsum('kgpc,gh->kgchp', m4, eye).reshape(nk, S5_KBLK, gpb * npst)

        def out_blocks(m):
            m4 = m.reshape(nk, gpb, S5_GROUP_CH, npst)
            return jnp.einsum('kgcp,gh->kgphc', m4, eye).reshape(nk, gpb * npst, S5_KBLK)

        lams.append(jnp.stack([lam_re.reshape(-1), lam_im.reshape(-1)]))
        bmats.append(jnp.concatenate([in_blocks(bbar_re), in_blocks(bbar_im)], axis=2))
        cmats.append(jnp.concatenate([out_blocks(c_re[direction].astype(F32)),
                                      -out_blocks(c_im[direction].astype(F32))], axis=1))
    return jnp.stack(lams), jnp.stack(bmats).astype(BF16), jnp.stack(cmats).astype(BF16)


def _glu_kernel(yr_ref, yf_ref, yb_ref, u_ref, d_ref, w_ref, b_ref, o_ref):
    w = yr_ref.shape[1]
    y = u_ref[...].astype(F32) * d_ref[...] + yf_ref[...] + yb_ref[...]
    y = jax.nn.gelu(y)
    gate = jax.nn.sigmoid(jnp.dot(y.astype(BF16), w_ref[...], preferred_element_type=F32) + b_ref[...])
    o_ref[:, :w] = yr_ref[...]
    o_ref[:, w:] = (y * gate).astype(BF16)


def glu_concat(y_ret, ys5, z, d_skip, w_glu, b_glu, nb, tm=512):
    t, w = y_ret.shape
    seq = t // nb
    tm = min(tm, seq)
    nl = seq // tm
    ucol = z.shape[1] // w - 1
    return pl.pallas_call(
        _glu_kernel,
        grid=(nb, nl),
        in_specs=[
            pl.BlockSpec((tm, w), lambda b, i: (b * nl + i, 0)),
            pl.BlockSpec((None, tm, w), lambda b, i: (0, b * nl + i, 0)),
            pl.BlockSpec((None, tm, w), lambda b, i: (1, b * nl + i, 0)),
            pl.BlockSpec((tm, w), lambda b, i: (b * nl + i, ucol)),
            pl.BlockSpec((1, w), lambda b, i: (0, 0)),
            pl.BlockSpec((w, w), lambda b, i: (0, 0)),
            pl.BlockSpec((1, w), lambda b, i: (0, 0)),
        ],
        out_specs=pl.BlockSpec((tm, 2 * w), lambda b, i: (b * nl + i, 0)),
        out_shape=jax.ShapeDtypeStruct((t, 2 * w), BF16),
        compiler_params=_params(("parallel", "parallel")),
        name="glu_concat",
    )(y_ret, ys5, ys5, z, d_skip.reshape(1, w), w_glu.astype(BF16), b_glu.reshape(1, w))


def _proj_kernel(a_ref, w_ref, x_ref, g_ref, rh_ref, rl_ref, rb_ref, x1_ref, h_ref, lt_ref):
    x1 = x_ref[...] + jnp.dot(a_ref[...], w_ref[...], preferred_element_type=F32)
    x1_ref[...] = x1
    h = _rms(x1, g_ref[...])
    h_hi = h.astype(BF16)
    h_ref[...] = h_hi.reshape(h_ref.shape)
    h_lo = (h - h_hi.astype(F32)).astype(BF16)
    lt = lax.dot_general(rh_ref[...], h_hi, NT_DIMS, preferred_element_type=F32)
    lt = lt + lax.dot_general(rh_ref[...], h_lo, NT_DIMS, preferred_element_type=F32)
    lt = lt + lax.dot_general(rl_ref[...], h_hi, NT_DIMS, preferred_element_type=F32)
    lt_ref[...] = lt + rb_ref[...]


def proj_norm_router(a, w, x, g, r_hi, r_lo, r_bias, tm=512):
    t, d = x.shape
    k = a.shape[1]
    nr = r_hi.shape[0]
    return pl.pallas_call(
        _proj_kernel,
        grid=(t // tm,),
        in_specs=[
            pl.BlockSpec((tm, k), lambda i: (i, 0)),
            pl.BlockSpec((k, d), lambda i: (0, 0)),
            pl.BlockSpec((tm, d), lambda i: (i, 0)),
            pl.BlockSpec((1, d), lambda i: (0, 0)),
            pl.BlockSpec((nr, d), lambda i: (0, 0)),
            pl.BlockSpec((nr, d), lambda i: (0, 0)),
            pl.BlockSpec((nr, 1), lambda i: (0, 0)),
        ],
        out_specs=[
            pl.BlockSpec((tm, d), lambda i: (i, 0)),
            pl.BlockSpec((tm, d // LANES, LANES), lambda i: (i, 0, 0)),
            pl.BlockSpec((nr, tm), lambda i: (0, i)),
        ],
        out_shape=[
            jax.ShapeDtypeStruct((t, d), F32),
            jax.ShapeDtypeStruct((t, d // LANES, LANES), BF16),
            jax.ShapeDtypeStruct((nr, t), F32),
        ],
        compiler_params=_params(("parallel",)),
        name="proj_norm_router",
    )(a, w.astype(BF16), x, g.reshape(1, d), r_hi, r_lo, r_bias)


def _router_operands(w_group, b_group, w_er, b_er):
    d = w_group.shape[0]
    wt = jnp.concatenate([
        jnp.transpose(w_er.astype(F32), (0, 2, 1)).reshape(MOE_EXPERTS, d),
        jnp.transpose(w_group.astype(F32)),
        jnp.zeros((ROUTER_ROWS - MOE_EXPERTS - MOE_GROUPS, d), F32)], axis=0)
    bias = jnp.concatenate([
        b_er.astype(F32).reshape(-1), b_group.astype(F32),
        jnp.zeros((ROUTER_ROWS - MOE_EXPERTS - MOE_GROUPS,), F32)]).reshape(ROUTER_ROWS, 1)
    hi = wt.astype(BF16)
    lo = (wt - hi.astype(F32)).astype(BF16)
    return hi, lo, bias


def _route_kernel(lt_ref, eid_ref, gate_ref, rank_ref, cnt_ref, run_ref):
    i = pl.program_id(0)
    tm = lt_ref.shape[1]
    ne, npg, ng = MOE_EXPERTS, MOE_EXPERTS_PER_GROUP, MOE_GROUPS

    @pl.when(i == 0)
    def _():
        run_ref[...] = jnp.zeros_like(run_ref)

    gl = lt_ref[ne:ne + ng, :]
    gmax = jnp.max(gl, axis=0, keepdims=True)
    gidx = lax.broadcasted_iota(I32, (ng, tm), 0)
    gsel = jnp.min(jnp.where(gl == gmax, gidx, ng), axis=0, keepdims=True)
    p_g = 1.0 / jnp.sum(jnp.exp(gl - gmax), axis=0, keepdims=True)

    e8 = lt_ref[(ng - 1) * npg:ng * npg, :]
    for g in range(ng - 2, -1, -1):
        e8 = jnp.where(gsel == g, lt_ref[g * npg:(g + 1) * npg, :], e8)
    eidx = lax.broadcasted_iota(I32, (npg, tm), 0)
    m1 = jnp.max(e8, axis=0, keepdims=True)
    i1 = jnp.min(jnp.where(e8 == m1, eidx, npg), axis=0, keepdims=True)
    e8b = jnp.where(eidx == i1, -jnp.inf, e8)
    m2 = jnp.max(e8b, axis=0, keepdims=True)
    i2 = jnp.min(jnp.where(e8b == m2, eidx, npg), axis=0, keepdims=True)
    t2 = jnp.exp(m2 - m1)
    den = 1.0 + t2
    gate_ref[0:1, :] = (1.0 / den) * p_g
    gate_ref[1:2, :] = (t2 / den) * p_g
    id1 = gsel * npg + i1
    id2 = gsel * npg + i2
    eid_ref[0:1, :] = id1
    eid_ref[1:2, :] = id2

    rows = lax.broadcasted_iota(I32, (ne, tm), 0)
    oh1 = rows == id1
    oh2 = rows == id2
    both = jnp.where(oh1, 1.0, 0.0) + jnp.where(oh2, 1.0, 0.0)
    earlier = (lax.broadcasted_iota(I32, (tm, tm), 0) < lax.broadcasted_iota(I32, (tm, tm), 1))
    prefix = jnp.dot(both.astype(BF16), jnp.where(earlier, 1.0, 0.0).astype(BF16),
                     preferred_element_type=F32)
    base = prefix + run_ref[:, 0:1]
    rank_ref[0:1, :] = jnp.sum(jnp.where(oh1, base, 0.0), axis=0, keepdims=True).astype(I32)
    rank_ref[1:2, :] = jnp.sum(jnp.where(oh2, base, 0.0), axis=0, keepdims=True).astype(I32)
    run = run_ref[...] + jnp.sum(both, axis=1, keepdims=True)
    run_ref[...] = run
    cnt_ref[...] = run.astype(I32)


def route(lt, tm=512):
    nr, t = lt.shape
    two = lambda dt: jax.ShapeDtypeStruct((2, t), dt)
    return pl.pallas_call(
        _route_kernel,
        grid=(t // tm,),
        in_specs=[pl.BlockSpec((nr, tm), lambda i: (0, i))],
        out_specs=[
            pl.BlockSpec((2, tm), lambda i: (0, i)),
            pl.BlockSpec((2, tm), lambda i: (0, i)),
            pl.BlockSpec((2, tm), lambda i: (0, i)),
            pl.BlockSpec((MOE_EXPERTS, LANES), lambda i: (0, 0)),
        ],
        out_shape=[two(I32), two(F32), two(I32),
                   jax.ShapeDtypeStruct((MOE_EXPERTS, LANES), I32)],
        scratch_shapes=[pltpu.VMEM((MOE_EXPERTS, LANES), F32)],
        compiler_params=_params(("arbitrary",)),
        name="route",
    )(lt)


def _dispatch_kernel(d0_ref, d1_ref, h_ref, xin_ref, xbuf_ref, sem):
    del xin_ref
    tm = h_ref.shape[0]
    dests = (d0_ref, d1_ref)

    def copy(r, k):
        return pltpu.make_async_copy(h_ref.at[r], xbuf_ref.at[dests[k][0, r]], sem)

    def start(r, carry):
        copy(r, 0).start()
        copy(r, 1).start()
        return carry

    lax.fori_loop(0, tm, start, 0, unroll=8)

    def wait(r, carry):
        copy(r, 0).wait()
        copy(r, 1).wait()
        return carry

    lax.fori_loop(0, tm, wait, 0, unroll=8)


def _row_index_spec(tm):
    return pl.BlockSpec((None, 1, tm), lambda i: (i, 0, 0), memory_space=pltpu.SMEM)


def dispatch(dest0, dest1, h3, n_rows):
    t, s, lanes = h3.shape
    nt, _, tm = dest0.shape
    return pl.pallas_call(
        _dispatch_kernel,
        grid=(nt,),
        in_specs=[
            _row_index_spec(tm), _row_index_spec(tm),
            pl.BlockSpec((tm, s, lanes), lambda i: (i, 0, 0)),
            pl.BlockSpec(memory_space=pl.ANY),
        ],
        out_specs=pl.BlockSpec(memory_space=pl.ANY),
        out_shape=jax.ShapeDtypeStruct((n_rows, s, lanes), h3.dtype),
        scratch_shapes=[pltpu.SemaphoreType.DMA(())],
        input_output_aliases={3: 0},
        compiler_params=_params(("arbitrary",)),
        name="moe_dispatch",
    )(dest0, dest1, h3, jnp.zeros((n_rows, s, lanes), h3.dtype))


def _experts_kernel(be_ref, nu_ref, x_ref, wg_ref, wu_ref, wd_ref, o_ref, wgb, wub, wdb):
    i = pl.program_id(0)
    changed = jnp.logical_or(i == 0, be_ref[i] != be_ref[jnp.maximum(i - 1, 0)])

    @pl.when(changed)
    def _():
        wgb[...] = wg_ref[...].astype(BF16)
        wub[...] = wu_ref[...].astype(BF16)
        wdb[...] = wd_ref[...].astype(BF16)

    @pl.when(i < nu_ref[0])
    def _():
        bm, s, lanes = x_ref.shape
        x = x_ref[...].reshape(bm, s * lanes)
        g = jnp.dot(x, wgb[...], preferred_element_type=F32)
        u = jnp.dot(x, wub[...], preferred_element_type=F32)
        a = ((g * jax.nn.sigmoid(g)) * u).astype(BF16)
        y = jnp.dot(a, wdb[...], preferred_element_type=F32)
        o_ref[...] = y.astype(BF16).reshape(o_ref.shape)

    @pl.when(i >= nu_ref[0])
    def _():
        o_ref[...] = jnp.zeros_like(o_ref)


def experts(layer, block_expert, n_used, xbuf, w_gate, w_up, w_down):
    n_rows, s, lanes = xbuf.shape
    d = s * lanes
    hid = w_gate.shape[3]
    bm = MOE_ROWS
    nblk = n_rows // bm
    grid_spec = pltpu.PrefetchScalarGridSpec(
        num_scalar_prefetch=2,
        grid=(nblk,),
        in_specs=[
            pl.BlockSpec((bm, s, lanes), lambda i, be, nu: (jnp.minimum(i, nu[0] - 1), 0, 0)),
            pl.BlockSpec((None, None, d, hid), lambda i, be, nu: (layer, be[i], 0, 0)),
            pl.BlockSpec((None, None, d, hid), lambda i, be, nu: (layer, be[i], 0, 0)),
            pl.BlockSpec((None, None, hid, d), lambda i, be, nu: (layer, be[i], 0, 0)),
        ],
        out_specs=pl.BlockSpec((bm, s, lanes), lambda i, be, nu: (i, 0, 0)),
        scratch_shapes=[pltpu.VMEM((d, hid), BF16), pltpu.VMEM((d, hid), BF16),
                        pltpu.VMEM((hid, d), BF16)],
    )
    return pl.pallas_call(
        _experts_kernel,
        grid_spec=grid_spec,
        out_shape=jax.ShapeDtypeStruct((n_rows, s, lanes), BF16),
        compiler_params=_params(("arbitrary",)),
        name="moe_experts",
    )(block_expert, n_used, xbuf, w_gate, w_up, w_down)


def _combine_kernel(d0_ref, d1_ref, gate_ref, x_ref, g_ref, ybuf_ref, o_ref, buf, sem, *,
                    final_norm):
    tm, d = x_ref.shape
    dests = (d0_ref, d1_ref)

    def copy(r, k):
        return pltpu.make_async_copy(ybuf_ref.at[dests[k][0, r]], buf.at[k, r], sem)

    def start(r, carry):
        copy(r, 0).start()
        copy(r, 1).start()
        return carry

    lax.fori_loop(0, tm, start, 0, unroll=8)

    def wait(r, carry):
        copy(r, 0).wait()
        copy(r, 1).wait()
        return carry

    lax.fori_loop(0, tm, wait, 0, unroll=8)

    gates = gate_ref[...]
    y = (gates[:, 0:1] * buf[0].reshape(tm, d).astype(F32)
         + gates[:, 1:2] * buf[1].reshape(tm, d).astype(F32))
    out = x_ref[...] + y
    if final_norm:
        out = _rms(out, g_ref[...])
    o_ref[...] = out


def combine(dest0, dest1, gates_t, x, ybuf, g_final, final_norm):
    t, d = x.shape
    nt, _, tm = dest0.shape
    _, s, lanes = ybuf.shape
    return pl.pallas_call(
        functools.partial(_combine_kernel, final_norm=final_norm),
        grid=(nt,),
        in_specs=[
            _row_index_spec(tm), _row_index_spec(tm),
            pl.BlockSpec((tm, 2), lambda i: (i, 0)),
            pl.BlockSpec((tm, d), lambda i: (i, 0)),
            pl.BlockSpec((1, d), lambda i: (0, 0)),
            pl.BlockSpec(memory_space=pl.ANY),
        ],
        out_specs=pl.BlockSpec((tm, d), lambda i: (i, 0)),
        out_shape=jax.ShapeDtypeStruct((t, d), F32),
        scratch_shapes=[pltpu.VMEM((2, tm, s, lanes), ybuf.dtype), pltpu.SemaphoreType.DMA(())],
        compiler_params=_params(("arbitrary",)),
        name="moe_combine",
    )(dest0, dest1, gates_t, x, g_final.reshape(1, d), ybuf)


def hier_moe_block(layer, x1, h, lt, w_gate, w_up, w_down, g_final, final_norm, tm=MOE_TOKENS):
    t, d = x1.shape
    bm = MOE_ROWS
    eid, gate, rank, cnt = route(lt)
    counts = cnt[:, 0]
    padded = ((counts + bm - 1) // bm) * bm
    pend = jnp.cumsum(padded)
    pstart = pend - padded
    experts_col = jnp.arange(MOE_EXPERTS, dtype=I32)[:, None, None]
    dest = rank + jnp.sum(jnp.where(eid[None] == experts_col, pstart[:, None, None], 0), axis=0)
    n_rows = 2 * t + MOE_EXPERTS * bm
    nblk = n_rows // bm
    n_used = (pend[-1] // bm).astype(I32)
    first_row = jnp.minimum(jnp.arange(nblk, dtype=I32), n_used - 1) * bm
    block_expert = jnp.sum(pend[None, :] <= first_row[:, None], axis=1).astype(I32)
    block_expert = jnp.minimum(block_expert, MOE_EXPERTS - 1)
    dest0 = dest[0].reshape(t // tm, 1, tm)
    dest1 = dest[1].reshape(t // tm, 1, tm)
    xbuf = dispatch(dest0, dest1, h, n_rows)
    ybuf = experts(layer, block_expert, n_used.reshape(1), xbuf, w_gate, w_up, w_down)
    return combine(dest0, dest1, jnp.transpose(gate), x1, ybuf, g_final, final_norm)


def _attn_kernel(sink_ref, q_ref, kp_ref, kc_ref, kn_ref, vp_ref, vc_ref, vn_ref, bias_ref, o_ref):
    n = pl.program_id(1)
    nblk = pl.num_programs(1)
    blk = q_ref.shape[0]
    dh = SWA_HEAD_DIM
    nkv = kc_ref.shape[1] // dh
    masked = bias_ref.shape[1] - 1
    part_prev = jnp.where(n > 0, 0, masked)
    part_next = jnp.where(n < nblk - 1, 2, masked)
    lo = lax.broadcasted_iota(I32, (blk, 2 * dh), 1) < dh
    top = lax.broadcasted_iota(I32, (2 * blk, 2 * dh), 0) < blk
    lo2 = lax.broadcasted_iota(I32, (2 * blk, 2 * dh), 1) < dh
    ones_ext = jnp.concatenate([jnp.where(lo, 1.0, 0.0), jnp.where(lo, 0.0, 1.0)], axis=0).astype(BF16)
    qscale = (dh ** -0.5) * LOG2E

    for kv in range(nkv):
        col = slice((kv // 2) * 2 * dh, (kv // 2 + 1) * 2 * dh)

        def extend(ref):
            x = ref[:, col].astype(F32)
            r = pltpu.roll(x, dh, 1)
            x_lo, x_hi = (x, r) if kv % 2 == 0 else (r, x)
            return jnp.concatenate([jnp.where(lo, x_lo, 0.0), jnp.where(lo, 0.0, x_hi)],
                                   axis=0).astype(BF16)

        q2 = jnp.concatenate([q_ref[:, (2 * kv) * 2 * dh:(2 * kv + 1) * 2 * dh],
                              q_ref[:, (2 * kv + 1) * 2 * dh:(2 * kv + 2) * 2 * dh]], axis=0)
        q2 = (q2.astype(F32) * qscale).astype(BF16)

        def scores(k_ref, part):
            return lax.dot_general(q2, extend(k_ref), NT_DIMS,
                                   preferred_element_type=F32) + bias_ref[kv, part]

        s = [scores(kp_ref, part_prev), scores(kc_ref, 1), scores(kn_ref, part_next)]
        mx = jnp.maximum(jnp.maximum(s[0], s[1]), s[2])
        sk = [jnp.where(top, sink_ref[0, 4 * kv + par], sink_ref[0, 4 * kv + 2 + par])
              for par in range(2)]
        m = [jnp.maximum(jnp.broadcast_to(jnp.max(mx[:, par * blk:(par + 1) * blk], axis=-1,
                                                  keepdims=True), (2 * blk, 2 * dh)), sk[par])
             for par in range(2)]
        acc = jnp.zeros((2 * blk, 4 * dh), F32)
        for sp, v_ref in zip(s, (vp_ref, vc_ref, vn_ref)):
            e = jnp.concatenate([jnp.exp2(sp[:, :blk] - m[0]), jnp.exp2(sp[:, blk:] - m[1])],
                                axis=1).astype(BF16)
            rhs = jnp.concatenate([extend(v_ref), ones_ext], axis=1)
            acc = acc + jnp.dot(e, rhs, preferred_element_type=F32)
        den = acc[:, 2 * dh:] + jnp.exp2(jnp.where(lo2, sk[0] - m[0], sk[1] - m[1]))
        o = (acc[:, :2 * dh] / den).astype(BF16)
        o_ref[:, (2 * kv) * 2 * dh:(2 * kv + 1) * 2 * dh] = o[:blk]
        o_ref[:, (2 * kv + 1) * 2 * dh:(2 * kv + 2) * 2 * dh] = o[blk:]


def banded_attention(qkv3, sink, bias):
    b, seq, width = qkv3.shape
    nh = sink.shape[1]
    qd = nh * SWA_HEAD_DIM
    kvd = (width - qd) // 2
    blk = SWA_BLOCK
    nblk = seq // blk
    kcol, vcol = qd // kvd, qd // kvd + 1

    def band(col, off):
        return pl.BlockSpec((None, blk, kvd),
                            lambda bi, n: (bi, jnp.clip(n + off, 0, nblk - 1), col))

    return pl.pallas_call(
        _attn_kernel,
        grid=(b, nblk),
        in_specs=[
            pl.BlockSpec(memory_space=pltpu.SMEM),
            pl.BlockSpec((None, blk, qd), lambda bi, n: (bi, n, 0)),
            band(kcol, -1), band(kcol, 0), band(kcol, 1),
            band(vcol, -1), band(vcol, 0), band(vcol, 1),
            pl.BlockSpec(bias.shape, lambda bi, n: (0, 0, 0, 0)),
        ],
        out_specs=pl.BlockSpec((None, blk, qd), lambda bi, n: (bi, n, 0)),
        out_shape=jax.ShapeDtypeStruct((b, seq, qd), BF16),
        compiler_params=_params(("parallel", "parallel")),
        name="banded_attention",
    )(sink, qkv3, qkv3, qkv3, qkv3, qkv3, qkv3, qkv3, bias)


def _t5_bucket(rel):
    half = REL_BUCKETS // 2
    max_exact = half // 2
    n = jnp.abs(rel)
    large = max_exact + (jnp.log(jnp.maximum(n, 1).astype(F32) / max_exact)
                         / math.log(REL_MAX_DIST / max_exact) * (half - max_exact)).astype(I32)
    large = jnp.minimum(large, half - 1)
    return jnp.where(rel > 0, half, 0) + jnp.where(n < max_exact, n, large)


def _attention_bias(rel_bias):
    blk = SWA_BLOCK
    nh = rel_bias.shape[1]
    nkv = nh // SWA_Q_PER_KV
    rel = jnp.arange(3 * blk)[None, :] - blk - jnp.arange(blk)[:, None]
    onehot = (_t5_bucket(rel)[..., None] == jnp.arange(REL_BUCKETS)).astype(F32)
    bias = jnp.einsum('ijb,bh->hij', onehot, rel_bias.astype(F32), precision=lax.Precision.HIGHEST)
    bias = jnp.where((jnp.abs(rel) <= SWA_WINDOW)[None], bias, NEG_INF) * LOG2E
    tiles = bias.reshape(nkv, 2, 2, blk, 3, blk).transpose(0, 4, 1, 3, 2, 5)
    tiles = tiles.reshape(nkv, 3, 2 * blk, 2 * blk)
    masked = jnp.full((nkv, 1, 2 * blk, 2 * blk), NEG_INF * LOG2E, F32)
    return jnp.concatenate([tiles, masked], axis=1)


def _rotary_tables(seq, dim):
    inv_freq = ROPE_BASE ** (-jnp.arange(0, dim, 2, dtype=F32) / dim)
    ang = jnp.arange(seq, dtype=F32)[:, None] * inv_freq[None, :]
    cos, sin = jnp.cos(ang), jnp.sin(ang)
    return jnp.concatenate([cos, cos], axis=1), jnp.concatenate([-sin, sin], axis=1)


def kernel(x, norm_mix_g, norm_ffn_g, norm_final_g, hyb_w_in, ret_decay_logit, s5_a_re, s5_a_im, s5_log_step, s5_b_re, s5_b_im, s5_c_re, s5_c_im, s5_d, s5_w_glu, s5_b_glu, hyb_w_out, swa_w_qkv, swa_sink, swa_w_o, rel_bias, moe_w_group, moe_b_group, moe_w_expert_router, moe_b_expert_router, moe_w_gate, moe_w_up, moe_w_down):
    nb, seq, d = x.shape
    t = nb * seq
    depth = norm_mix_g.shape[0]
    xt = x.reshape(t, d)
    for layer in range(depth):
        i = layer // 2
        if layer % 2 == 0:
            w = hyb_w_out.shape[1] // 2
            z = norm_matmul(xt, norm_mix_g[layer], hyb_w_in[i])
            z3 = z.reshape(nb, seq, z.shape[1])
            cos, sin = _rotary_tables(seq, w // RET_HEADS)
            log_gamma = jax.nn.log_sigmoid(ret_decay_logit[i].astype(F32))
            y_ret = retention(z3, log_gamma, cos, sin).reshape(t, w)
            lam, bmat, cmat = _s5_discretize(s5_a_re[i], s5_a_im[i], s5_log_step[i], s5_b_re[i],
                                             s5_b_im[i], s5_c_re[i], s5_c_im[i])
            ys5 = s5_scan(z3, lam, bmat, cmat).reshape(2, t, w)
            mix_in = glu_concat(y_ret, ys5, z, s5_d[i].astype(F32).reshape(-1), s5_w_glu[i],
                                s5_b_glu[i].astype(F32), nb)
            w_mix = hyb_w_out[i]
        else:
            qkv = norm_matmul(xt, norm_mix_g[layer], swa_w_qkv[i])
            mix_in = banded_attention(qkv.reshape(nb, seq, qkv.shape[1]),
                                      swa_sink[i].astype(F32).reshape(1, -1) * LOG2E,
                                      _attention_bias(rel_bias)).reshape(t, -1)
            w_mix = swa_w_o[i]
        r_hi, r_lo, r_bias = _router_operands(moe_w_group[layer], moe_b_group[layer],
                                              moe_w_expert_router[layer], moe_b_expert_router[layer])
        x1, h, lt = proj_norm_router(mix_in, w_mix, xt, norm_ffn_g[layer], r_hi, r_lo, r_bias)
        last = layer == depth - 1
        xt = hier_moe_block(layer, x1, h, lt, moe_w_gate, moe_w_up, moe_w_down,
                            norm_final_g, final_norm=last)
    return xt.reshape(nb, seq, d)
```

```python
import functools
import math

import jax
import jax.numpy as jnp
from jax import lax
from jax.experimental import pallas as pl
from jax.experimental.pallas import tpu as pltpu

F32 = jnp.float32
BF16 = jnp.bfloat16
I32 = jnp.int32

RET_HEADS = 4
RET_CHUNK = 128
S5_GROUP_CH = 16
S5_STATE = 64
SWA_HEAD_DIM = 64
SWA_Q_PER_KV = 4
SWA_WINDOW = 128
SWA_BLOCK = 128
REL_BUCKETS = 32
REL_MAX_DIST = 128
MOE_GROUPS = 4
MOE_EXPERTS_PER_GROUP = 8
MOE_EXPERTS = MOE_GROUPS * MOE_EXPERTS_PER_GROUP
ROPE_BASE = 10000.0
RMS_EPS = 1e-6
GN_EPS = 1e-5
NEG_INF = -1e30
LOG2E = 1.4426950408889634

LANES = 128
SUBLANES = 8
V7X_VMEM_BYTES = 64 * 1024 * 1024
VMEM_LIMIT = V7X_VMEM_BYTES - 8 * 1024 * 1024

MOE_ROWS = 512
MOE_TOKENS = 512
S5_CHUNK = 128
S5_COLS = 512
S5_SCAN_COLS = 1024
S5_KBLK = 128
ROUTER_ROWS = 128
RET_UNROLL = 8

NT_DIMS = (((1,), (1,)), ((), ()))
TN_DIMS = (((0,), (0,)), ((), ()))


def _params(semantics):
    return pltpu.CompilerParams(dimension_semantics=semantics, vmem_limit_bytes=VMEM_LIMIT)


def _rms(x, g):
    ms = jnp.mean(x * x, axis=-1, keepdims=True)
    return (x * lax.rsqrt(ms + RMS_EPS)) * g


def _norm_matmul_kernel(x_ref, g_ref, w_ref, o_ref):
    h = _rms(x_ref[...], g_ref[...])
    o_ref[...] = jnp.dot(h.astype(BF16), w_ref[...],
                         preferred_element_type=F32).astype(o_ref.dtype)


def norm_matmul(x, g, w, tm=512):
    t, d = x.shape
    n = w.shape[1]
    return pl.pallas_call(
        _norm_matmul_kernel,
        grid=(t // tm,),
        in_specs=[
            pl.BlockSpec((tm, d), lambda i: (i, 0)),
            pl.BlockSpec((1, d), lambda i: (0, 0)),
            pl.BlockSpec((d, n), lambda i: (0, 0)),
        ],
        out_specs=pl.BlockSpec((tm, n), lambda i: (i, 0)),
        out_shape=jax.ShapeDtypeStruct((t, n), BF16),
        compiler_params=_params(("parallel",)),
        name="norm_matmul",
    )(x, g.reshape(1, d), w.astype(BF16))


def _retention_kernel(lg_ref, q_ref, k_ref, v_ref, g_ref, cos_ref, sin_ref, o_ref,
                      qr_ref, kr_ref, inc_ref, st_ref):
    h = pl.program_id(1)
    lg_f = lg_ref[0, h]
    lg_b = lg_ref[1, h]
    seq, dk = q_ref.shape
    c = RET_CHUNK
    nc = seq // c

    cos = cos_ref[...]
    sin = sin_ref[...]
    q = q_ref[...].astype(F32)
    qr_ref[...] = q * cos + pltpu.roll(q, dk // 2, 1) * sin
    k = k_ref[...].astype(F32)
    kr_ref[...] = (k * cos + pltpu.roll(k, dk // 2, 1) * sin) * (dk ** -0.5)

    pos = lax.broadcasted_iota(I32, (c, dk), 0).astype(F32)
    kf_scale = jnp.exp((c - 1.0 - pos) * lg_f)
    qf_scale = jnp.exp((pos + 1.0) * lg_f)
    kb_scale = jnp.exp(pos * lg_b)
    qb_scale = jnp.exp((c - pos) * lg_b)
    rel = (lax.broadcasted_iota(I32, (c, c), 0) - lax.broadcasted_iota(I32, (c, c), 1)).astype(F32)
    mask = jnp.exp(jnp.abs(rel) * jnp.where(rel >= 0, lg_f, lg_b))
    dec_f = jnp.exp(jnp.full((dk, dk), c * lg_f, F32))
    dec_b = jnp.exp(jnp.full((dk, dk), c * lg_b, F32))

    def increments(n, carry):
        rows = pl.ds(pl.multiple_of(n * c, c), c)
        kc = kr_ref[rows, :]
        kk = jnp.concatenate([kc * kf_scale, kc * kb_scale], axis=1).astype(BF16)
        inc_ref[n] = lax.dot_general(kk, v_ref[rows, :], TN_DIMS,
                                     preferred_element_type=F32)
        return carry

    lax.fori_loop(0, nc, increments, 0, unroll=min(RET_UNROLL, nc))

    def fwd(n, state):
        st_ref[n, :dk, :] = state.astype(BF16)
        return state * dec_f + inc_ref[n, :dk, :]

    lax.fori_loop(0, nc, fwd, jnp.zeros((dk, dk), F32))

    def bwd(i, state):
        n = nc - 1 - i
        st_ref[n, dk:, :] = state.astype(BF16)
        return state * dec_b + inc_ref[n, dk:, :]

    lax.fori_loop(0, nc, bwd, jnp.zeros((dk, dk), F32))

    def outputs(n, carry):
        rows = pl.ds(pl.multiple_of(n * c, c), c)
        qc = qr_ref[rows, :]
        s = lax.dot_general(qc.astype(BF16), kr_ref[rows, :].astype(BF16), NT_DIMS,
                            preferred_element_type=F32) * mask
        lhs = jnp.concatenate([s, qc * qf_scale, qc * qb_scale], axis=1).astype(BF16)
        rhs = jnp.concatenate([v_ref[rows, :], st_ref[n]], axis=0)
        out = jnp.dot(lhs, rhs, preferred_element_type=F32)
        mu = jnp.mean(out, axis=-1, keepdims=True)
        cen = out - mu
        var = jnp.mean(cen * cen, axis=-1, keepdims=True)
        g = g_ref[rows, :].astype(F32)
        o_ref[rows, :] = ((g * jax.nn.sigmoid(g)) * (cen * lax.rsqrt(var + GN_EPS))).astype(BF16)
        return carry

    lax.fori_loop(0, nc, outputs, 0, unroll=min(RET_UNROLL, nc))


def retention(z3, log_gamma, cos, sin):
    b, seq, _ = z3.shape
    nh = RET_HEADS
    dk = cos.shape[1]

    def col(off):
        return pl.BlockSpec((None, seq, dk), lambda bi, hi: (bi, 0, off + hi))

    return pl.pallas_call(
        _retention_kernel,
        grid=(b, nh),
        in_specs=[
            pl.BlockSpec(memory_space=pltpu.SMEM),
            col(0), col(nh), col(2 * nh), col(3 * nh),
            pl.BlockSpec((seq, dk), lambda bi, hi: (0, 0)),
            pl.BlockSpec((seq, dk), lambda bi, hi: (0, 0)),
        ],
        out_specs=pl.BlockSpec((None, seq, dk), lambda bi, hi: (bi, 0, hi)),
        out_shape=jax.ShapeDtypeStruct((b, seq, nh * dk), BF16),
        scratch_shapes=[pltpu.VMEM((seq, dk), F32), pltpu.VMEM((seq, dk), F32),
                        pltpu.VMEM((seq // RET_CHUNK, 2 * dk, dk), F32),
                        pltpu.VMEM((seq // RET_CHUNK, 2 * dk, dk), BF16)],
        compiler_params=_params(("parallel", "parallel")),
        name="retention",
    )(log_gamma, z3, z3, z3, z3, cos, sin)


def _s5_kernel(u_ref, lam_ref, b_ref, c_ref, y_ref, up_ref, xre_ref, xim_ref, sre_ref, sim_ref):
    d = pl.program_id(0)
    n = pl.program_id(1)
    nb, cn, width = u_ref.shape
    srows = up_ref.shape[1]
    nk = width // S5_KBLK

    @pl.when(jnp.logical_and(d == 0, n == 0))
    def _():
        up_ref[...] = jnp.zeros_like(up_ref)

    @pl.when(n == 0)
    def _():
        sre_ref[...] = jnp.zeros_like(sre_ref)
        sim_ref[...] = jnp.zeros_like(sim_ref)

    for b in range(nb):
        up_ref[b, b:b + cn, :] = u_ref[b].astype(F32)
    u = up_ref[...].reshape(nb * srows, width).astype(BF16)
    ppc = S5_COLS // LANES
    for kb in range(nk):
        bu = jnp.dot(u[:, kb * S5_KBLK:(kb + 1) * S5_KBLK], b_ref[kb], preferred_element_type=F32)
        for j in range(ppc):
            xre_ref[kb * ppc + j] = bu[:, j * LANES:(j + 1) * LANES]
            xim_ref[kb * ppc + j] = bu[:, S5_COLS + j * LANES:S5_COLS + (j + 1) * LANES]

    pps = S5_SCAN_COLS // LANES
    for cb in range(sre_ref.shape[1] // S5_SCAN_COLS):
        cols = slice(cb * S5_SCAN_COLS, (cb + 1) * S5_SCAN_COLS)
        lr = jnp.broadcast_to(lam_ref[0:1, cols], (nb, S5_SCAN_COLS))
        li = jnp.broadcast_to(lam_ref[1:2, cols], (nb, S5_SCAN_COLS))

        def step(i, carry):
            xr, xi = carry
            t = i + d * (cn - 1 - 2 * i)
            rows = pl.ds(t, nb, stride=srows + 1)
            bur = jnp.concatenate([xre_ref[cb * pps + j, rows, :] for j in range(pps)], axis=1)
            bui = jnp.concatenate([xim_ref[cb * pps + j, rows, :] for j in range(pps)], axis=1)
            nxr = lr * xr - li * xi + bur
            nxi = lr * xi + li * xr + bui
            for j in range(pps):
                xre_ref[cb * pps + j, rows, :] = nxr[:, j * LANES:(j + 1) * LANES]
                xim_ref[cb * pps + j, rows, :] = nxi[:, j * LANES:(j + 1) * LANES]
            return nxr, nxi

        xr, xi = lax.fori_loop(0, cn, step, (sre_ref[:, cols], sim_ref[:, cols]), unroll=8)
        sre_ref[:, cols] = xr
        sim_ref[:, cols] = xi

    for kb in range(nk):
        xr = jnp.concatenate([xre_ref[kb * ppc + j] for j in range(ppc)], axis=1).astype(BF16)
        xi = jnp.concatenate([xim_ref[kb * ppc + j] for j in range(ppc)], axis=1).astype(BF16)
        y = jnp.dot(xr, c_ref[kb, :S5_COLS, :], preferred_element_type=F32)
        y = y + jnp.dot(xi, c_ref[kb, S5_COLS:, :], preferred_element_type=F32)
        y = y.reshape(nb, srows, S5_KBLK)
        for b in range(nb):
            y_ref[b, :, kb * S5_KBLK:(kb + 1) * S5_KBLK] = y[b, b:b + cn, :]


def s5_scan(z3, lam, bmat, cmat):
    nb, seq, zw = z3.shape
    width = bmat.shape[1] * bmat.shape[2]
    nstate = lam.shape[2]
    cn = min(S5_CHUNK, seq)
    nch = seq // cn
    srows = cn + SUBLANES
    ucol = zw // width - 1

    def chunk(d, n):
        return n + d * (nch - 1 - 2 * n)

    return pl.pallas_call(
        _s5_kernel,
        grid=(2, nch),
        in_specs=[
            pl.BlockSpec((nb, cn, width), lambda d, n: (0, chunk(d, n), ucol)),
            pl.BlockSpec((None, 2, nstate), lambda d, n: (d, 0, 0)),
            pl.BlockSpec((None,) + bmat.shape[1:], lambda d, n: (d, 0, 0, 0)),
            pl.BlockSpec((None,) + cmat.shape[1:], lambda d, n: (d, 0, 0, 0)),
        ],
        out_specs=pl.BlockSpec((None, nb, cn, width), lambda d, n: (d, 0, chunk(d, n), 0)),
        out_shape=jax.ShapeDtypeStruct((2, nb, seq, width), F32),
        scratch_shapes=[
            pltpu.VMEM((nb, srows, width), F32),
            pltpu.VMEM((nstate // LANES, nb * srows, LANES), F32),
            pltpu.VMEM((nstate // LANES, nb * srows, LANES), F32),
            pltpu.VMEM((nb, nstate), F32),
            pltpu.VMEM((nb, nstate), F32),
        ],
        compiler_params=_params(("arbitrary", "arbitrary")),
        name="s5_scan",
    )(z3, lam, bmat, cmat)


def _s5_discretize(a_re, a_im, log_step, b_re, b_im, c_re, c_im):
    ng, npst = a_re.shape[1], a_re.shape[2]
    gpb = S5_KBLK // S5_GROUP_CH
    nk = ng // gpb
    eye = jnp.eye(gpb, dtype=F32)
    bre, bim = b_re.astype(F32), b_im.astype(F32)
    lams, bmats, cmats = [], [], []
    for direction in range(2):
        ar = a_re[direction].astype(F32)
        ai = a_im[direction].astype(F32)
        dt = jnp.exp(log_step[direction].astype(F32))[:, None]
        mag = jnp.exp(ar * dt)
        lam_re, lam_im = mag * jnp.cos(ai * dt), mag * jnp.sin(ai * dt)
        nr, ni = lam_re - 1.0, lam_im
        den = ar * ar + ai * ai
        coef_re = (nr * ar + ni * ai) / den
        coef_im = (ni * ar - nr * ai) / den
        bbar_re = coef_re[..., None] * bre - coef_im[..., None] * bim
        bbar_im = coef_re[..., None] * bim + coef_im[..., None] * bre

        def in_blocks(m):
            m4 = m.reshape(nk, gpb, npst, S5_GROUP_CH)
            return jnp.einsum('kgpc,gh->kgchp', m4, eye).reshape(nk, S5_KBLK, gpb * npst)

        def out_blocks(m):
            m4 = m.reshape(nk, gpb, S5_GROUP_CH, npst)
            return jnp.einsum('kgcp,gh->kgphc', m4, eye).reshape(nk, gpb * npst, S5_KBLK)

        lams.append(jnp.stack([lam_re.reshape(-1), lam_im.reshape(-1)]))
        bmats.append(jnp.concatenate([in_blocks(bbar_re), in_blocks(bbar_im)], axis=2))
        cmats.append(jnp.concatenate([out_blocks(c_re[direction].astype(F32)),
                                      -out_blocks(c_im[direction].astype(F32))], axis=1))
    return jnp.stack(lams), jnp.stack(bmats).astype(BF16), jnp.stack(cmats).astype(BF16)


def _glu_kernel(yr_ref, yf_ref, yb_ref, u_ref, d_ref, w_ref, b_ref, o_ref):
    w = yr_ref.shape[1]
    y = u_ref[...].astype(F32) * d_ref[...] + yf_ref[...] + yb_ref[...]
    y = jax.nn.gelu(y)
    gate = jax.nn.sigmoid(jnp.dot(y.astype(BF16), w_ref[...], preferred_element_type=F32) + b_ref[...])
    o_ref[:, :w] = yr_ref[...]
    o_ref[:, w:] = (y * gate).astype(BF16)


def glu_concat(y_ret, ys5, z, d_skip, w_glu, b_glu, nb, tm=512):
    t, w = y_ret.shape
    seq = t // nb
    tm = min(tm, seq)
    nl = seq // tm
    ucol = z.shape[1] // w - 1
    return pl.pallas_call(
        _glu_kernel,
        grid=(nb, nl),
        in_specs=[
            pl.BlockSpec((tm, w), lambda b, i: (b * nl + i, 0)),
            pl.BlockSpec((None, tm, w), lambda b, i: (0, b * nl + i, 0)),
            pl.BlockSpec((None, tm, w), lambda b, i: (1, b * nl + i, 0)),
            pl.BlockSpec((tm, w), lambda b, i: (b * nl + i, ucol)),
            pl.BlockSpec((1, w), lambda b, i: (0, 0)),
            pl.BlockSpec((w, w), lambda b, i: (0, 0)),
            pl.BlockSpec((1, w), lambda b, i: (0, 0)),
        ],
        out_specs=pl.BlockSpec((tm, 2 * w), lambda b, i: (b * nl + i, 0)),
        out_shape=jax.ShapeDtypeStruct((t, 2 * w), BF16),
        compiler_params=_params(("parallel", "parallel")),
        name="glu_concat",
    )(y_ret, ys5, ys5, z, d_skip.reshape(1, w), w_glu.astype(BF16), b_glu.reshape(1, w))


def _proj_kernel(a_ref, w_ref, x_ref, g_ref, rh_ref, rl_ref, rb_ref, x1_ref, h_ref, lt_ref):
    x1 = x_ref[...] + jnp.dot(a_ref[...], w_ref[...], preferred_element_type=F32)
    x1_ref[...] = x1
    h = _rms(x1, g_ref[...])
    h_hi = h.astype(BF16)
    h_ref[...] = h_hi.reshape(h_ref.shape)
    h_lo = (h - h_hi.astype(F32)).astype(BF16)
    lt = lax.dot_general(rh_ref[...], h_hi, NT_DIMS, preferred_element_type=F32)
    lt = lt + lax.dot_general(rh_ref[...], h_lo, NT_DIMS, preferred_element_type=F32)
    lt = lt + lax.dot_general(rl_ref[...], h_hi, NT_DIMS, preferred_element_type=F32)
    lt_ref[...] = lt + rb_ref[...]


def proj_norm_router(a, w, x, g, r_hi, r_lo, r_bias, tm=512):
    t, d = x.shape
    k = a.shape[1]
    nr = r_hi.shape[0]
    return pl.pallas_call(
        _proj_kernel,
        grid=(t // tm,),
        in_specs=[
            pl.BlockSpec((tm, k), lambda i: (i, 0)),
            pl.BlockSpec((k, d), lambda i: (0, 0)),
            pl.BlockSpec((tm, d), lambda i: (i, 0)),
            pl.BlockSpec((1, d), lambda i: (0, 0)),
            pl.BlockSpec((nr, d), lambda i: (0, 0)),
            pl.BlockSpec((nr, d), lambda i: (0, 0)),
            pl.BlockSpec((nr, 1), lambda i: (0, 0)),
        ],
        out_specs=[
            pl.BlockSpec((tm, d), lambda i: (i, 0)),
            pl.BlockSpec((tm, d // LANES, LANES), lambda i: (i, 0, 0)),
            pl.BlockSpec((nr, tm), lambda i: (0, i)),
        ],
        out_shape=[
            jax.ShapeDtypeStruct((t, d), F32),
            jax.ShapeDtypeStruct((t, d // LANES, LANES), BF16),
            jax.ShapeDtypeStruct((nr, t), F32),
        ],
        compiler_params=_params(("parallel",)),
        name="proj_norm_router",
    )(a, w.astype(BF16), x, g.reshape(1, d), r_hi, r_lo, r_bias)


def _router_operands(w_group, b_group, w_er, b_er):
    d = w_group.shape[0]
    wt = jnp.concatenate([
        jnp.transpose(w_er.astype(F32), (0, 2, 1)).reshape(MOE_EXPERTS, d),
        jnp.transpose(w_group.astype(F32)),
        jnp.zeros((ROUTER_ROWS - MOE_EXPERTS - MOE_GROUPS, d), F32)], axis=0)
    bias = jnp.concatenate([
        b_er.astype(F32).reshape(-1), b_group.astype(F32),
        jnp.zeros((ROUTER_ROWS - MOE_EXPERTS - MOE_GROUPS,), F32)]).reshape(ROUTER_ROWS, 1)
    hi = wt.astype(BF16)
    lo = (wt - hi.astype(F32)).astype(BF16)
    return hi, lo, bias


def _route_kernel(lt_ref, eid_ref, gate_ref, rank_ref, cnt_ref, run_ref):
    i = pl.program_id(0)
    tm = lt_ref.shape[1]
    ne, npg, ng = MOE_EXPERTS, MOE_EXPERTS_PER_GROUP, MOE_GROUPS

    @pl.when(i == 0)
    def _():
        run_ref[...] = jnp.zeros_like(run_ref)

    gl = lt_ref[ne:ne + ng, :]
    gmax = jnp.max(gl, axis=0, keepdims=True)
    gidx = lax.broadcasted_iota(I32, (ng, tm), 0)
    gsel = jnp.min(jnp.where(gl == gmax, gidx, ng), axis=0, keepdims=True)
    p_g = 1.0 / jnp.sum(jnp.exp(gl - gmax), axis=0, keepdims=True)

    e8 = lt_ref[(ng - 1) * npg:ng * npg, :]
    for g in range(ng - 2, -1, -1):
        e8 = jnp.where(gsel == g, lt_ref[g * npg:(g + 1) * npg, :], e8)
    eidx = lax.broadcasted_iota(I32, (npg, tm), 0)
    m1 = jnp.max(e8, axis=0, keepdims=True)
    i1 = jnp.min(jnp.where(e8 == m1, eidx, npg), axis=0, keepdims=True)
    e8b = jnp.where(eidx == i1, -jnp.inf, e8)
    m2 = jnp.max(e8b, axis=0, keepdims=True)
    i2 = jnp.min(jnp.where(e8b == m2, eidx, npg), axis=0, keepdims=True)
    t2 = jnp.exp(m2 - m1)
    den = 1.0 + t2
    gate_ref[0:1, :] = (1.0 / den) * p_g
    gate_ref[1:2, :] = (t2 / den) * p_g
    id1 = gsel * npg + i1
    id2 = gsel * npg + i2
    eid_ref[0:1, :] = id1
    eid_ref[1:2, :] = id2

    rows = lax.broadcasted_iota(I32, (ne, tm), 0)
    oh1 = rows == id1
    oh2 = rows == id2
    both = jnp.where(oh1, 1.0, 0.0) + jnp.where(oh2, 1.0, 0.0)
    earlier = (lax.broadcasted_iota(I32, (tm, tm), 0) < lax.broadcasted_iota(I32, (tm, tm), 1))
    prefix = jnp.dot(both.astype(BF16), jnp.where(earlier, 1.0, 0.0).astype(BF16),
                     preferred_element_type=F32)
    base = prefix + run_ref[:, 0:1]
    rank_ref[0:1, :] = jnp.sum(jnp.where(oh1, base, 0.0), axis=0, keepdims=True).astype(I32)
    rank_ref[1:2, :] = jnp.sum(jnp.where(oh2, base, 0.0), axis=0, keepdims=True).astype(I32)
    run = run_ref[...] + jnp.sum(both, axis=1, keepdims=True)
    run_ref[...] = run
    cnt_ref[...] = run.astype(I32)


def route(lt, tm=512):
    nr, t = lt.shape
    two = lambda dt: jax.ShapeDtypeStruct((2, t), dt)
    return pl.pallas_call(
        _route_kernel,
        grid=(t // tm,),
        in_specs=[pl.BlockSpec((nr, tm), lambda i: (0, i))],
        out_specs=[
            pl.BlockSpec((2, tm), lambda i: (0, i)),
            pl.BlockSpec((2, tm), lambda i: (0, i)),
            pl.BlockSpec((2, tm), lambda i: (0, i)),
            pl.BlockSpec((MOE_EXPERTS, LANES), lambda i: (0, 0)),
        ],
        out_shape=[two(I32), two(F32), two(I32),
                   jax.ShapeDtypeStruct((MOE_EXPERTS, LANES), I32)],
        scratch_shapes=[pltpu.VMEM((MOE_EXPERTS, LANES), F32)],
        compiler_params=_params(("arbitrary",)),
        name="route",
    )(lt)


def _dispatch_kernel(d0_ref, d1_ref, h_ref, xin_ref, xbuf_ref, sem):
    del xin_ref
    tm = h_ref.shape[0]
    dests = (d0_ref, d1_ref)

    def copy(r, k):
        return pltpu.make_async_copy(h_ref.at[r], xbuf_ref.at[dests[k][0, r]], sem)

    def start(r, carry):
        copy(r, 0).start(priority=0)
        copy(r, 1).start(priority=1)
        return carry

    lax.fori_loop(0, tm, start, 0, unroll=8)

    def wait(r, carry):
        copy(r, 0).wait()
        copy(r, 1).wait()
        return carry

    lax.fori_loop(0, tm, wait, 0, unroll=8)


def _row_index_spec(tm, ahead=0, last=None):
    def index(i):
        return (i if ahead == 0 else jnp.minimum(i + ahead, last), 0, 0)
    return pl.BlockSpec((None, 1, tm), index, memory_space=pltpu.SMEM)


def dispatch(dest0, dest1, h3, n_rows):
    t, s, lanes = h3.shape
    nt, _, tm = dest0.shape
    return pl.pallas_call(
        _dispatch_kernel,
        grid=(nt,),
        in_specs=[
            _row_index_spec(tm), _row_index_spec(tm),
            pl.BlockSpec((tm, s, lanes), lambda i: (i, 0, 0)),
            pl.BlockSpec(memory_space=pl.ANY),
        ],
        out_specs=pl.BlockSpec(memory_space=pl.ANY),
        out_shape=jax.ShapeDtypeStruct((n_rows, s, lanes), h3.dtype),
        scratch_shapes=[pltpu.SemaphoreType.DMA(())],
        input_output_aliases={3: 0},
        compiler_params=_params(("arbitrary",)),
        name="moe_dispatch",
    )(dest0, dest1, h3, jnp.zeros((n_rows, s, lanes), h3.dtype))


def _experts_kernel(be_ref, nu_ref, x_ref, wg_ref, wu_ref, wd_ref, o_ref, wgb, wub, wdb):
    i = pl.program_id(0)
    changed = jnp.logical_or(i == 0, be_ref[i] != be_ref[jnp.maximum(i - 1, 0)])

    @pl.when(changed)
    def _():
        wgb[...] = wg_ref[...].astype(BF16)
        wub[...] = wu_ref[...].astype(BF16)
        wdb[...] = wd_ref[...].astype(BF16)

    @pl.when(i < nu_ref[0])
    def _():
        bm, s, lanes = x_ref.shape
        x = x_ref[...].reshape(bm, s * lanes)
        g = jnp.dot(x, wgb[...], preferred_element_type=F32)
        u = jnp.dot(x, wub[...], preferred_element_type=F32)
        a = ((g * jax.nn.sigmoid(g)) * u).astype(BF16)
        y = jnp.dot(a, wdb[...], preferred_element_type=F32)
        o_ref[...] = y.astype(BF16).reshape(o_ref.shape)

    @pl.when(i >= nu_ref[0])
    def _():
        o_ref[...] = jnp.zeros_like(o_ref)


def experts(layer, block_expert, n_used, xbuf, w_gate, w_up, w_down):
    n_rows, s, lanes = xbuf.shape
    d = s * lanes
    hid = w_gate.shape[3]
    bm = MOE_ROWS
    nblk = n_rows // bm
    grid_spec = pltpu.PrefetchScalarGridSpec(
        num_scalar_prefetch=2,
        grid=(nblk,),
        in_specs=[
            pl.BlockSpec((bm, s, lanes), lambda i, be, nu: (jnp.minimum(i, nu[0] - 1), 0, 0)),
            pl.BlockSpec((None, None, d, hid), lambda i, be, nu: (layer, be[i], 0, 0)),
            pl.BlockSpec((None, None, d, hid), lambda i, be, nu: (layer, be[i], 0, 0)),
            pl.BlockSpec((None, None, hid, d), lambda i, be, nu: (layer, be[i], 0, 0)),
        ],
        out_specs=pl.BlockSpec((bm, s, lanes), lambda i, be, nu: (i, 0, 0)),
        scratch_shapes=[pltpu.VMEM((d, hid), BF16), pltpu.VMEM((d, hid), BF16),
                        pltpu.VMEM((hid, d), BF16)],
    )
    return pl.pallas_call(
        _experts_kernel,
        grid_spec=grid_spec,
        out_shape=jax.ShapeDtypeStruct((n_rows, s, lanes), BF16),
        compiler_params=_params(("arbitrary",)),
        name="moe_experts",
    )(block_expert, n_used, xbuf, w_gate, w_up, w_down)


def _combine_kernel(d0_ref, d1_ref, n0_ref, n1_ref, gate_ref, x_ref, g_ref, ybuf_ref, o_ref,
                    buf, sem, *, final_norm):
    i = pl.program_id(0)
    tm, d = x_ref.shape
    slot = i % 2

    def copy(dests, s, r, k):
        return pltpu.make_async_copy(ybuf_ref.at[dests[k][0, r]], buf.at[s, k, r], sem.at[s])

    def gather(dests, s):
        def start(r, carry):
            copy(dests, s, r, 0).start(priority=0)
            copy(dests, s, r, 1).start(priority=1)
            return carry

        lax.fori_loop(0, tm, start, 0, unroll=8)

    @pl.when(i == 0)
    def _():
        gather((d0_ref, d1_ref), 0)

    @pl.when(i + 1 < pl.num_programs(0))
    def _():
        gather((n0_ref, n1_ref), 1 - slot)

    def wait(r, carry):
        copy((d0_ref, d1_ref), slot, r, 0).wait()
        copy((d0_ref, d1_ref), slot, r, 1).wait()
        return carry

    lax.fori_loop(0, tm, wait, 0, unroll=8)

    gates = gate_ref[...]
    y = (gates[:, 0:1] * buf[slot, 0].reshape(tm, d).astype(F32)
         + gates[:, 1:2] * buf[slot, 1].reshape(tm, d).astype(F32))
    out = x_ref[...] + y
    if final_norm:
        out = _rms(out, g_ref[...])
    o_ref[...] = out


def combine(dest0, dest1, gates_t, x, ybuf, g_final, final_norm):
    t, d = x.shape
    nt, _, tm = dest0.shape
    _, s, lanes = ybuf.shape
    return pl.pallas_call(
        functools.partial(_combine_kernel, final_norm=final_norm),
        grid=(nt,),
        in_specs=[
            _row_index_spec(tm), _row_index_spec(tm),
            _row_index_spec(tm, ahead=1, last=nt - 1), _row_index_spec(tm, ahead=1, last=nt - 1),
            pl.BlockSpec((tm, 2), lambda i: (i, 0)),
            pl.BlockSpec((tm, d), lambda i: (i, 0)),
            pl.BlockSpec((1, d), lambda i: (0, 0)),
            pl.BlockSpec(memory_space=pl.ANY),
        ],
        out_specs=pl.BlockSpec((tm, d), lambda i: (i, 0)),
        out_shape=jax.ShapeDtypeStruct((t, d), F32),
        scratch_shapes=[pltpu.VMEM((2, 2, tm, s, lanes), ybuf.dtype),
                        pltpu.SemaphoreType.DMA((2,))],
        compiler_params=_params(("arbitrary",)),
        name="moe_combine",
    )(dest0, dest1, dest0, dest1, gates_t, x, g_final.reshape(1, d), ybuf)


def hier_moe_block(layer, x1, h, lt, w_gate, w_up, w_down, g_final, final_norm, tm=MOE_TOKENS):
    t, d = x1.shape
    bm = MOE_ROWS
    eid, gate, rank, cnt = route(lt)
    counts = cnt[:, 0]
    padded = ((counts + bm - 1) // bm) * bm
    pend = jnp.cumsum(padded)
    pstart = pend - padded
    experts_col = jnp.arange(MOE_EXPERTS, dtype=I32)[:, None, None]
    dest = rank + jnp.sum(jnp.where(eid[None] == experts_col, pstart[:, None, None], 0), axis=0)
    n_rows = 2 * t + MOE_EXPERTS * bm
    nblk = n_rows // bm
    n_used = (pend[-1] // bm).astype(I32)
    first_row = jnp.minimum(jnp.arange(nblk, dtype=I32), n_used - 1) * bm
    block_expert = jnp.sum(pend[None, :] <= first_row[:, None], axis=1).astype(I32)
    block_expert = jnp.minimum(block_expert, MOE_EXPERTS - 1)
    dest0 = dest[0].reshape(t // tm, 1, tm)
    dest1 = dest[1].reshape(t // tm, 1, tm)
    xbuf = dispatch(dest0, dest1, h, n_rows)
    ybuf = experts(layer, block_expert, n_used.reshape(1), xbuf, w_gate, w_up, w_down)
    return combine(dest0, dest1, jnp.transpose(gate), x1, ybuf, g_final, final_norm)


def _attn_kernel(sink_ref, q_ref, kp_ref, kc_ref, kn_ref, vp_ref, vc_ref, vn_ref, bias_ref, o_ref):
    n = pl.program_id(1)
    nblk = pl.num_programs(1)
    blk = q_ref.shape[0]
    dh = SWA_HEAD_DIM
    nkv = kc_ref.shape[1] // dh
    masked = bias_ref.shape[1] - 1
    part_prev = jnp.where(n > 0, 0, masked)
    part_next = jnp.where(n < nblk - 1, 2, masked)
    lo = lax.broadcasted_iota(I32, (blk, 2 * dh), 1) < dh
    top = lax.broadcasted_iota(I32, (2 * blk, 2 * dh), 0) < blk
    lo2 = lax.broadcasted_iota(I32, (2 * blk, 2 * dh), 1) < dh
    ones_ext = jnp.concatenate([jnp.where(lo, 1.0, 0.0), jnp.where(lo, 0.0, 1.0)], axis=0).astype(BF16)
    qscale = (dh ** -0.5) * LOG2E

    for kv in range(nkv):
        col = slice((kv // 2) * 2 * dh, (kv // 2 + 1) * 2 * dh)

        def extend(ref):
            x = ref[:, col].astype(F32)
            r = pltpu.roll(x, dh, 1)
            x_lo, x_hi = (x, r) if kv % 2 == 0 else (r, x)
            return jnp.concatenate([jnp.where(lo, x_lo, 0.0), jnp.where(lo, 0.0, x_hi)],
                                   axis=0).astype(BF16)

        q2 = jnp.concatenate([q_ref[:, (2 * kv) * 2 * dh:(2 * kv + 1) * 2 * dh],
                              q_ref[:, (2 * kv + 1) * 2 * dh:(2 * kv + 2) * 2 * dh]], axis=0)
        q2 = (q2.astype(F32) * qscale).astype(BF16)

        def scores(k_ref, part):
            return lax.dot_general(q2, extend(k_ref), NT_DIMS,
                                   preferred_element_type=F32) + bias_ref[kv, part]

        s = [scores(kp_ref, part_prev), scores(kc_ref, 1), scores(kn_ref, part_next)]
        mx = jnp.maximum(jnp.maximum(s[0], s[1]), s[2])
        sk = [jnp.where(top, sink_ref[0, 4 * kv + par], sink_ref[0, 4 * kv + 2 + par])
              for par in range(2)]
        m = [jnp.maximum(jnp.broadcast_to(jnp.max(mx[:, par * blk:(par + 1) * blk], axis=-1,
                                                  keepdims=True), (2 * blk, 2 * dh)), sk[par])
             for par in range(2)]
        acc = jnp.zeros((2 * blk, 4 * dh), F32)
        for sp, v_ref in zip(s, (vp_ref, vc_ref, vn_ref)):
            e = jnp.concatenate([jnp.exp2(sp[:, :blk] - m[0]), jnp.exp2(sp[:, blk:] - m[1])],
                                axis=1).astype(BF16)
            rhs = jnp.concatenate([extend(v_ref), ones_ext], axis=1)
            acc = acc + jnp.dot(e, rhs, preferred_element_type=F32)
        den = acc[:, 2 * dh:] + jnp.exp2(jnp.where(lo2, sk[0] - m[0], sk[1] - m[1]))
        o = (acc[:, :2 * dh] / den).astype(BF16)
        o_ref[:, (2 * kv) * 2 * dh:(2 * kv + 1) * 2 * dh] = o[:blk]
        o_ref[:, (2 * kv + 1) * 2 * dh:(2 * kv + 2) * 2 * dh] = o[blk:]


def banded_attention(qkv3, sink, bias):
    b, seq, width = qkv3.shape
    nh = sink.shape[1]
    qd = nh * SWA_HEAD_DIM
    kvd = (width - qd) // 2
    blk = SWA_BLOCK
    nblk = seq // blk
    kcol, vcol = qd // kvd, qd // kvd + 1

    def band(col, off):
        return pl.BlockSpec((None, blk, kvd),
                            lambda bi, n: (bi, jnp.clip(n + off, 0, nblk - 1), col))

    return pl.pallas_call(
        _attn_kernel,
        grid=(b, nblk),
        in_specs=[
            pl.BlockSpec(memory_space=pltpu.SMEM),
            pl.BlockSpec((None, blk, qd), lambda bi, n: (bi, n, 0)),
            band(kcol, -1), band(kcol, 0), band(kcol, 1),
            band(vcol, -1), band(vcol, 0), band(vcol, 1),
            pl.BlockSpec(bias.shape, lambda bi, n: (0, 0, 0, 0)),
        ],
        out_specs=pl.BlockSpec((None, blk, qd), lambda bi, n: (bi, n, 0)),
        out_shape=jax.ShapeDtypeStruct((b, seq, qd), BF16),
        compiler_params=_params(("parallel", "parallel")),
        name="banded_attention",
    )(sink, qkv3, qkv3, qkv3, qkv3, qkv3, qkv3, qkv3, bias)


def _t5_bucket(rel):
    half = REL_BUCKETS // 2
    max_exact = half // 2
    n = jnp.abs(rel)
    large = max_exact + (jnp.log(jnp.maximum(n, 1).astype(F32) / max_exact)
                         / math.log(REL_MAX_DIST / max_exact) * (half - max_exact)).astype(I32)
    large = jnp.minimum(large, half - 1)
    return jnp.where(rel > 0, half, 0) + jnp.where(n < max_exact, n, large)


def _attention_bias(rel_bias):
    blk = SWA_BLOCK
    nh = rel_bias.shape[1]
    nkv = nh // SWA_Q_PER_KV
    rel = jnp.arange(3 * blk)[None, :] - blk - jnp.arange(blk)[:, None]
    onehot = (_t5_bucket(rel)[..., None] == jnp.arange(REL_BUCKETS)).astype(F32)
    bias = jnp.einsum('ijb,bh->hij', onehot, rel_bias.astype(F32), precision=lax.Precision.HIGHEST)
    bias = jnp.where((jnp.abs(rel) <= SWA_WINDOW)[None], bias, NEG_INF) * LOG2E
    tiles = bias.reshape(nkv, 2, 2, blk, 3, blk).transpose(0, 4, 1, 3, 2, 5)
    tiles = tiles.reshape(nkv, 3, 2 * blk, 2 * blk)
    masked = jnp.full((nkv, 1, 2 * blk, 2 * blk), NEG_INF * LOG2E, F32)
    return jnp.concatenate([tiles, masked], axis=1)


def _rotary_tables(seq, dim):
    inv_freq = ROPE_BASE ** (-jnp.arange(0, dim, 2, dtype=F32) / dim)
    ang = jnp.arange(seq, dtype=F32)[:, None] * inv_freq[None, :]
    cos, sin = jnp.cos(ang), jnp.sin(ang)
    return jnp.concatenate([cos, cos], axis=1), jnp.concatenate([-sin, sin], axis=1)


def kernel(x, norm_mix_g, norm_ffn_g, norm_final_g, hyb_w_in, ret_decay_logit, s5_a_re, s5_a_im, s5_log_step, s5_b_re, s5_b_im, s5_c_re, s5_c_im, s5_d, s5_w_glu, s5_b_glu, hyb_w_out, swa_w_qkv, swa_sink, swa_w_o, rel_bias, moe_w_group, moe_b_group, moe_w_expert_router, moe_b_expert_router, moe_w_gate, moe_w_up, moe_w_down):
    nb, seq, d = x.shape
    t = nb * seq
    depth = norm_mix_g.shape[0]
    xt = x.reshape(t, d)
    for layer in range(depth):
        i = layer // 2
        if layer % 2 == 0:
            w = hyb_w_out.shape[1] // 2
            z = norm_matmul(xt, norm_mix_g[layer], hyb_w_in[i])
            z3 = z.reshape(nb, seq, z.shape[1])
            cos, sin = _rotary_tables(seq, w // RET_HEADS)
            log_gamma = jax.nn.log_sigmoid(ret_decay_logit[i].astype(F32))
            y_ret = retention(z3, log_gamma, cos, sin).reshape(t, w)
            lam, bmat, cmat = _s5_discretize(s5_a_re[i], s5_a_im[i], s5_log_step[i], s5_b_re[i],
                                             s5_b_im[i], s5_c_re[i], s5_c_im[i])
            ys5 = s5_scan(z3, lam, bmat, cmat).reshape(2, t, w)
            mix_in = glu_concat(y_ret, ys5, z, s5_d[i].astype(F32).reshape(-1), s5_w_glu[i],
                                s5_b_glu[i].astype(F32), nb)
            w_mix = hyb_w_out[i]
        else:
            qkv = norm_matmul(xt, norm_mix_g[layer], swa_w_qkv[i])
            mix_in = banded_attention(qkv.reshape(nb, seq, qkv.shape[1]),
                                      swa_sink[i].astype(F32).reshape(1, -1) * LOG2E,
                                      _attention_bias(rel_bias)).reshape(t, -1)
            w_mix = swa_w_o[i]
        r_hi, r_lo, r_bias = _router_operands(moe_w_group[layer], moe_b_group[layer],
                                              moe_w_expert_router[layer], moe_b_expert_router[layer])
        x1, h, lt = proj_norm_router(mix_in, w_mix, xt, norm_ffn_g[layer], r_hi, r_lo, r_bias)
        last = layer == depth - 1
        xt = hier_moe_block(layer, x1, h, lt, moe_w_gate, moe_w_up, moe_w_down,
                            norm_final_g, final_norm=last)
    return xt.reshape(nb, seq, d)
```

```python
import functools
import math

import jax
import jax.numpy as jnp
from jax import lax
from jax.experimental import pallas as pl
from jax.experimental.pallas import tpu as pltpu

F32 = jnp.float32
BF16 = jnp.bfloat16
I32 = jnp.int32

RET_HEADS = 4
RET_CHUNK = 128
S5_GROUP_CH = 16
S5_STATE = 64
SWA_HEAD_DIM = 64
SWA_Q_PER_KV = 4
SWA_WINDOW = 128
SWA_BLOCK = 128
REL_BUCKETS = 32
REL_MAX_DIST = 128
MOE_GROUPS = 4
MOE_EXPERTS_PER_GROUP = 8
MOE_EXPERTS = MOE_GROUPS * MOE_EXPERTS_PER_GROUP
ROPE_BASE = 10000.0
RMS_EPS = 1e-6
GN_EPS = 1e-5
NEG_INF = -1e30
LOG2E = 1.4426950408889634

LANES = 128
SUBLANES = 8
V7X_VMEM_BYTES = 64 * 1024 * 1024
VMEM_LIMIT = V7X_VMEM_BYTES - 8 * 1024 * 1024

MOE_ROWS = 512
MOE_TOKENS = 512
S5_CHUNK = 128
S5_COLS = 512
S5_SCAN_COLS = 1024
S5_KBLK = 128
ROUTER_ROWS = 128
RET_UNROLL = 8

NT_DIMS = (((1,), (1,)), ((), ()))
TN_DIMS = (((0,), (0,)), ((), ()))


def _params(semantics):
    return pltpu.CompilerParams(dimension_semantics=semantics, vmem_limit_bytes=VMEM_LIMIT)


def _rms(x, g):
    ms = jnp.mean(x * x, axis=-1, keepdims=True)
    return (x * lax.rsqrt(ms + RMS_EPS)) * g


def _norm_matmul_kernel(x_ref, g_ref, w_ref, o_ref):
    h = _rms(x_ref[...], g_ref[...])
    o_ref[...] = jnp.dot(h.astype(BF16), w_ref[...],
                         preferred_element_type=F32).astype(o_ref.dtype)


def norm_matmul(x, g, w, tm=512):
    t, d = x.shape
    n = w.shape[1]
    return pl.pallas_call(
        _norm_matmul_kernel,
        grid=(t // tm,),
        in_specs=[
            pl.BlockSpec((tm, d), lambda i: (i, 0)),
            pl.BlockSpec((1, d), lambda i: (0, 0)),
            pl.BlockSpec((d, n), lambda i: (0, 0)),
        ],
        out_specs=pl.BlockSpec((tm, n), lambda i: (i, 0)),
        out_shape=jax.ShapeDtypeStruct((t, n), BF16),
        compiler_params=_params(("parallel",)),
        name="norm_matmul",
    )(x, g.reshape(1, d), w.astype(BF16))


def _retention_kernel(lg_ref, q_ref, k_ref, v_ref, g_ref, cos_ref, sin_ref, o_ref,
                      qr_ref, kr_ref, inc_ref, st_ref, lhs_ref):
    h = pl.program_id(1)
    lg_f = lg_ref[0, h]
    lg_b = lg_ref[1, h]
    seq, dk = q_ref.shape
    c = RET_CHUNK
    nc = seq // c

    cos = cos_ref[...]
    sin = sin_ref[...]
    swap = (lax.broadcasted_iota(I32, (dk, dk), 0)
            == (lax.broadcasted_iota(I32, (dk, dk), 1) + dk // 2) % dk)
    swap = jnp.where(swap, 1.0, 0.0).astype(BF16)
    q = q_ref[...]
    qr_ref[...] = (q.astype(F32) * cos
                   + jnp.dot(q, swap, preferred_element_type=F32) * sin)
    k = k_ref[...]
    kr_ref[...] = (k.astype(F32) * cos
                   + jnp.dot(k, swap, preferred_element_type=F32) * sin) * (dk ** -0.5)

    pos = lax.broadcasted_iota(I32, (c, dk), 0).astype(F32)
    kf_scale = jnp.exp((c - 1.0 - pos) * lg_f)
    qf_scale = jnp.exp((pos + 1.0) * lg_f)
    kb_scale = jnp.exp(pos * lg_b)
    qb_scale = jnp.exp((c - pos) * lg_b)
    rel = (lax.broadcasted_iota(I32, (c, c), 0) - lax.broadcasted_iota(I32, (c, c), 1)).astype(F32)
    mask = jnp.exp(jnp.abs(rel) * jnp.where(rel >= 0, lg_f, lg_b))
    dec_f = jnp.exp(jnp.full((dk, dk), c * lg_f, F32))
    dec_b = jnp.exp(jnp.full((dk, dk), c * lg_b, F32))

    def increments(n, carry):
        rows = pl.ds(pl.multiple_of(n * c, c), c)
        kc = kr_ref[rows, :]
        kk = jnp.concatenate([kc * kf_scale, kc * kb_scale], axis=1).astype(BF16)
        inc_ref[n] = lax.dot_general(kk, v_ref[rows, :], TN_DIMS,
                                     preferred_element_type=F32)
        return carry

    lax.fori_loop(0, nc, increments, 0, unroll=min(RET_UNROLL, nc))

    def fwd(n, state):
        st_ref[n, :dk, :] = state.astype(BF16)
        return state * dec_f + inc_ref[n, :dk, :]

    lax.fori_loop(0, nc, fwd, jnp.zeros((dk, dk), F32))

    def bwd(i, state):
        n = nc - 1 - i
        st_ref[n, dk:, :] = state.astype(BF16)
        return state * dec_b + inc_ref[n, dk:, :]

    lax.fori_loop(0, nc, bwd, jnp.zeros((dk, dk), F32))

    def operands(n, carry):
        rows = pl.ds(pl.multiple_of(n * c, c), c)
        qc = qr_ref[rows, :]
        s = lax.dot_general(qc.astype(BF16), kr_ref[rows, :].astype(BF16), NT_DIMS,
                            preferred_element_type=F32) * mask
        lhs_ref[n] = jnp.concatenate([s, qc * qf_scale, qc * qb_scale], axis=1).astype(BF16)
        return carry

    lax.fori_loop(0, nc, operands, 0, unroll=min(RET_UNROLL, nc))

    def outputs(n, carry):
        rows = pl.ds(pl.multiple_of(n * c, c), c)
        rhs = jnp.concatenate([v_ref[rows, :], st_ref[n]], axis=0)
        kr_ref[rows, :] = jnp.dot(lhs_ref[n], rhs, preferred_element_type=F32)
        return carry

    lax.fori_loop(0, nc, outputs, 0, unroll=min(RET_UNROLL, nc))

    out = kr_ref[...]
    mu = jnp.mean(out, axis=-1, keepdims=True)
    cen = out - mu
    var = jnp.mean(cen * cen, axis=-1, keepdims=True)
    g = g_ref[...].astype(F32)
    o_ref[...] = ((g * jax.nn.sigmoid(g)) * (cen * lax.rsqrt(var + GN_EPS))).astype(BF16)


def retention(z3, log_gamma, cos, sin):
    b, seq, _ = z3.shape
    nh = RET_HEADS
    dk = cos.shape[1]

    def col(off):
        return pl.BlockSpec((None, seq, dk), lambda bi, hi: (bi, 0, off + hi))

    return pl.pallas_call(
        _retention_kernel,
        grid=(b, nh),
        in_specs=[
            pl.BlockSpec(memory_space=pltpu.SMEM),
            col(0), col(nh), col(2 * nh), col(3 * nh),
            pl.BlockSpec((seq, dk), lambda bi, hi: (0, 0)),
            pl.BlockSpec((seq, dk), lambda bi, hi: (0, 0)),
        ],
        out_specs=pl.BlockSpec((None, seq, dk), lambda bi, hi: (bi, 0, hi)),
        out_shape=jax.ShapeDtypeStruct((b, seq, nh * dk), BF16),
        scratch_shapes=[pltpu.VMEM((seq, dk), F32), pltpu.VMEM((seq, dk), F32),
                        pltpu.VMEM((seq // RET_CHUNK, 2 * dk, dk), F32),
                        pltpu.VMEM((seq // RET_CHUNK, 2 * dk, dk), BF16),
                        pltpu.VMEM((seq // RET_CHUNK, RET_CHUNK, 3 * dk), BF16)],
        compiler_params=_params(("parallel", "parallel")),
        name="retention",
    )(log_gamma, z3, z3, z3, z3, cos, sin)


def _s5_kernel(u_ref, lam_ref, b_ref, c_ref, y_ref, up_ref, xre_ref, xim_ref, sre_ref, sim_ref):
    d = pl.program_id(0)
    n = pl.program_id(1)
    nb, cn, width = u_ref.shape
    srows = up_ref.shape[1]
    nk = width // S5_KBLK

    @pl.when(jnp.logical_and(d == 0, n == 0))
    def _():
        up_ref[...] = jnp.zeros_like(up_ref)

    @pl.when(n == 0)
    def _():
        sre_ref[...] = jnp.zeros_like(sre_ref)
        sim_ref[...] = jnp.zeros_like(sim_ref)

    for b in range(nb):
        up_ref[b, b:b + cn, :] = u_ref[b].astype(F32)
    u = up_ref[...].reshape(nb * srows, width).astype(BF16)
    ppc = S5_COLS // LANES
    for kb in range(nk):
        bu = jnp.dot(u[:, kb * S5_KBLK:(kb + 1) * S5_KBLK], b_ref[kb], preferred_element_type=F32)
        for j in range(ppc):
            xre_ref[kb * ppc + j] = bu[:, j * LANES:(j + 1) * LANES]
            xim_ref[kb * ppc + j] = bu[:, S5_COLS + j * LANES:S5_COLS + (j + 1) * LANES]

    pps = S5_SCAN_COLS // LANES
    for cb in range(sre_ref.shape[1] // S5_SCAN_COLS):
        cols = slice(cb * S5_SCAN_COLS, (cb + 1) * S5_SCAN_COLS)
        lr = jnp.broadcast_to(lam_ref[0:1, cols], (nb, S5_SCAN_COLS))
        li = jnp.broadcast_to(lam_ref[1:2, cols], (nb, S5_SCAN_COLS))

        def step(i, carry):
            xr, xi = carry
            t = i + d * (cn - 1 - 2 * i)
            rows = pl.ds(t, nb, stride=srows + 1)
            bur = jnp.concatenate([xre_ref[cb * pps + j, rows, :] for j in range(pps)], axis=1)
            bui = jnp.concatenate([xim_ref[cb * pps + j, rows, :] for j in range(pps)], axis=1)
            nxr = lr * xr - li * xi + bur
            nxi = lr * xi + li * xr + bui
            for j in range(pps):
                xre_ref[cb * pps + j, rows, :] = nxr[:, j * LANES:(j + 1) * LANES]
                xim_ref[cb * pps + j, rows, :] = nxi[:, j * LANES:(j + 1) * LANES]
            return nxr, nxi

        xr, xi = lax.fori_loop(0, cn, step, (sre_ref[:, cols], sim_ref[:, cols]), unroll=8)
        sre_ref[:, cols] = xr
        sim_ref[:, cols] = xi

    for kb in range(nk):
        xr = jnp.concatenate([xre_ref[kb * ppc + j] for j in range(ppc)], axis=1).astype(BF16)
        xi = jnp.concatenate([xim_ref[kb * ppc + j] for j in range(ppc)], axis=1).astype(BF16)
        y = jnp.dot(xr, c_ref[kb, :S5_COLS, :], preferred_element_type=F32)
        y = y + jnp.dot(xi, c_ref[kb, S5_COLS:, :], preferred_element_type=F32)
        y = y.reshape(nb, srows, S5_KBLK)
        for b in range(nb):
            y_ref[b, :, kb * S5_KBLK:(kb + 1) * S5_KBLK] = y[b, b:b + cn, :]


def s5_scan(z3, lam, bmat, cmat):
    nb, seq, zw = z3.shape
    width = bmat.shape[1] * bmat.shape[2]
    nstate = lam.shape[2]
    cn = min(S5_CHUNK, seq)
    nch = seq // cn
    srows = cn + SUBLANES
    ucol = zw // width - 1

    def chunk(d, n):
        return n + d * (nch - 1 - 2 * n)

    return pl.pallas_call(
        _s5_kernel,
        grid=(2, nch),
        in_specs=[
            pl.BlockSpec((nb, cn, width), lambda d, n: (0, chunk(d, n), ucol)),
            pl.BlockSpec((None, 2, nstate), lambda d, n: (d, 0, 0)),
            pl.BlockSpec((None,) + bmat.shape[1:], lambda d, n: (d, 0, 0, 0)),
            pl.BlockSpec((None,) + cmat.shape[1:], lambda d, n: (d, 0, 0, 0)),
        ],
        out_specs=pl.BlockSpec((None, nb, cn, width), lambda d, n: (d, 0, chunk(d, n), 0)),
        out_shape=jax.ShapeDtypeStruct((2, nb, seq, width), F32),
        scratch_shapes=[
            pltpu.VMEM((nb, srows, width), F32),
            pltpu.VMEM((nstate // LANES, nb * srows, LANES), F32),
            pltpu.VMEM((nstate // LANES, nb * srows, LANES), F32),
            pltpu.VMEM((nb, nstate), F32),
            pltpu.VMEM((nb, nstate), F32),
        ],
        compiler_params=_params(("arbitrary", "arbitrary")),
        name="s5_scan",
    )(z3, lam, bmat, cmat)


def _s5_discretize(a_re, a_im, log_step, b_re, b_im, c_re, c_im):
    ng, npst = a_re.shape[1], a_re.shape[2]
    gpb = S5_KBLK // S5_GROUP_CH
    nk = ng // gpb
    eye = jnp.eye(gpb, dtype=F32)
    bre, bim = b_re.astype(F32), b_im.astype(F32)
    lams, bmats, cmats = [], [], []
    for direction in range(2):
        ar = a_re[direction].astype(F32)
        ai = a_im[direction].astype(F32)
        dt = jnp.exp(log_step[direction].astype(F32))[:, None]
        mag = jnp.exp(ar * dt)
        lam_re, lam_im = mag * jnp.cos(ai * dt), mag * jnp.sin(ai * dt)
        nr, ni = lam_re - 1.0, lam_im
        den = ar * ar + ai * ai
        coef_re = (nr * ar + ni * ai) / den
        coef_im = (ni * ar - nr * ai) / den
        bbar_re = coef_re[..., None] * bre - coef_im[..., None] * bim
        bbar_im = coef_re[..., None] * bim + coef_im[..., None] * bre

        def in_blocks(m):
            m4 = m.reshape(nk, gpb, npst, S5_GROUP_CH)
            return jnp.einsum('kgpc,gh->kgchp', m4, eye).reshape(nk, S5_KBLK, gpb * npst)

        def out_blocks(m):
            m4 = m.reshape(nk, gpb, S5_GROUP_CH, npst)
            return jnp.einsum('kgcp,gh->kgphc', m4, eye).reshape(nk, gpb * npst, S5_KBLK)

        lams.append(jnp.stack([lam_re.reshape(-1), lam_im.reshape(-1)]))
        bmats.append(jnp.concatenate([in_blocks(bbar_re), in_blocks(bbar_im)], axis=2))
        cmats.append(jnp.concatenate([out_blocks(c_re[direction].astype(F32)),
                                      -out_blocks(c_im[direction].astype(F32))], axis=1))
    return jnp.stack(lams), jnp.stack(bmats).astype(BF16), jnp.stack(cmats).astype(BF16)


def _glu_kernel(yr_ref, yf_ref, yb_ref, u_ref, d_ref, w_ref, b_ref, o_ref):
    w = yr_ref.shape[1]
    y = u_ref[...].astype(F32) * d_ref[...] + yf_ref[...] + yb_ref[...]
    y = jax.nn.gelu(y)
    gate = jax.nn.sigmoid(jnp.dot(y.astype(BF16), w_ref[...], preferred_element_type=F32) + b_ref[...])
    o_ref[:, :w] = yr_ref[...]
    o_ref[:, w:] = (y * gate).astype(BF16)


def glu_concat(y_ret, ys5, z, d_skip, w_glu, b_glu, nb, tm=512):
    t, w = y_ret.shape
    seq = t // nb
    tm = min(tm, seq)
    nl = seq // tm
    ucol = z.shape[1] // w - 1
    return pl.pallas_call(
        _glu_kernel,
        grid=(nb, nl),
        in_specs=[
            pl.BlockSpec((tm, w), lambda b, i: (b * nl + i, 0)),
            pl.BlockSpec((None, tm, w), lambda b, i: (0, b * nl + i, 0)),
            pl.BlockSpec((None, tm, w), lambda b, i: (1, b * nl + i, 0)),
            pl.BlockSpec((tm, w), lambda b, i: (b * nl + i, ucol)),
            pl.BlockSpec((1, w), lambda b, i: (0, 0)),
            pl.BlockSpec((w, w), lambda b, i: (0, 0)),
            pl.BlockSpec((1, w), lambda b, i: (0, 0)),
        ],
        out_specs=pl.BlockSpec((tm, 2 * w), lambda b, i: (b * nl + i, 0)),
        out_shape=jax.ShapeDtypeStruct((t, 2 * w), BF16),
        compiler_params=_params(("parallel", "parallel")),
        name="glu_concat",
    )(y_ret, ys5, ys5, z, d_skip.reshape(1, w), w_glu.astype(BF16), b_glu.reshape(1, w))


def _proj_kernel(a_ref, w_ref, x_ref, g_ref, rh_ref, rl_ref, rb_ref, x1_ref, h_ref, lt_ref):
    x1 = x_ref[...] + jnp.dot(a_ref[...], w_ref[...], preferred_element_type=F32)
    x1_ref[...] = x1
    h = _rms(x1, g_ref[...])
    h_hi = h.astype(BF16)
    h_ref[...] = h_hi.reshape(h_ref.shape)
    h_lo = (h - h_hi.astype(F32)).astype(BF16)
    lt = lax.dot_general(rh_ref[...], h_hi, NT_DIMS, preferred_element_type=F32)
    lt = lt + lax.dot_general(rh_ref[...], h_lo, NT_DIMS, preferred_element_type=F32)
    lt = lt + lax.dot_general(rl_ref[...], h_hi, NT_DIMS, preferred_element_type=F32)
    lt_ref[...] = lt + rb_ref[...]


def proj_norm_router(a, w, x, g, r_hi, r_lo, r_bias, tm=512):
    t, d = x.shape
    k = a.shape[1]
    nr = r_hi.shape[0]
    return pl.pallas_call(
        _proj_kernel,
        grid=(t // tm,),
        in_specs=[
            pl.BlockSpec((tm, k), lambda i: (i, 0)),
            pl.BlockSpec((k, d), lambda i: (0, 0)),
            pl.BlockSpec((tm, d), lambda i: (i, 0)),
            pl.BlockSpec((1, d), lambda i: (0, 0)),
            pl.BlockSpec((nr, d), lambda i: (0, 0)),
            pl.BlockSpec((nr, d), lambda i: (0, 0)),
            pl.BlockSpec((nr, 1), lambda i: (0, 0)),
        ],
        out_specs=[
            pl.BlockSpec((tm, d), lambda i: (i, 0)),
            pl.BlockSpec((tm, d // LANES, LANES), lambda i: (i, 0, 0)),
            pl.BlockSpec((nr, tm), lambda i: (0, i)),
        ],
        out_shape=[
            jax.ShapeDtypeStruct((t, d), F32),
            jax.ShapeDtypeStruct((t, d // LANES, LANES), BF16),
            jax.ShapeDtypeStruct((nr, t), F32),
        ],
        compiler_params=_params(("parallel",)),
        name="proj_norm_router",
    )(a, w.astype(BF16), x, g.reshape(1, d), r_hi, r_lo, r_bias)


def _router_operands(w_group, b_group, w_er, b_er):
    d = w_group.shape[0]
    wt = jnp.concatenate([
        jnp.transpose(w_er.astype(F32), (0, 2, 1)).reshape(MOE_EXPERTS, d),
        jnp.transpose(w_group.astype(F32)),
        jnp.zeros((ROUTER_ROWS - MOE_EXPERTS - MOE_GROUPS, d), F32)], axis=0)
    bias = jnp.concatenate([
        b_er.astype(F32).reshape(-1), b_group.astype(F32),
        jnp.zeros((ROUTER_ROWS - MOE_EXPERTS - MOE_GROUPS,), F32)]).reshape(ROUTER_ROWS, 1)
    hi = wt.astype(BF16)
    lo = (wt - hi.astype(F32)).astype(BF16)
    return hi, lo, bias


def _route_kernel(lt_ref, eid_ref, gate_ref, rank_ref, cnt_ref, run_ref):
    i = pl.program_id(0)
    tm = lt_ref.shape[1]
    ne, npg, ng = MOE_EXPERTS, MOE_EXPERTS_PER_GROUP, MOE_GROUPS

    @pl.when(i == 0)
    def _():
        run_ref[...] = jnp.zeros_like(run_ref)

    gl = lt_ref[ne:ne + ng, :]
    gmax = jnp.max(gl, axis=0, keepdims=True)
    gidx = lax.broadcasted_iota(I32, (ng, tm), 0)
    gsel = jnp.min(jnp.where(gl == gmax, gidx, ng), axis=0, keepdims=True)
    p_g = 1.0 / jnp.sum(jnp.exp(gl - gmax), axis=0, keepdims=True)

    e8 = lt_ref[(ng - 1) * npg:ng * npg, :]
    for g in range(ng - 2, -1, -1):
        e8 = jnp.where(gsel == g, lt_ref[g * npg:(g + 1) * npg, :], e8)
    eidx = lax.broadcasted_iota(I32, (npg, tm), 0)
    m1 = jnp.max(e8, axis=0, keepdims=True)
    i1 = jnp.min(jnp.where(e8 == m1, eidx, npg), axis=0, keepdims=True)
    e8b = jnp.where(eidx == i1, -jnp.inf, e8)
    m2 = jnp.max(e8b, axis=0, keepdims=True)
    i2 = jnp.min(jnp.where(e8b == m2, eidx, npg), axis=0, keepdims=True)
    t2 = jnp.exp(m2 - m1)
    den = 1.0 + t2
    gate_ref[0:1, :] = (1.0 / den) * p_g
    gate_ref[1:2, :] = (t2 / den) * p_g
    id1 = gsel * npg + i1
    id2 = gsel * npg + i2
    eid_ref[0:1, :] = id1
    eid_ref[1:2, :] = id2

    rows = lax.broadcasted_iota(I32, (ne, tm), 0)
    oh1 = rows == id1
    oh2 = rows == id2
    both = jnp.where(oh1, 1.0, 0.0) + jnp.where(oh2, 1.0, 0.0)
    earlier = (lax.broadcasted_iota(I32, (tm, tm), 0) < lax.broadcasted_iota(I32, (tm, tm), 1))
    prefix = jnp.dot(both.astype(BF16), jnp.where(earlier, 1.0, 0.0).astype(BF16),
                     preferred_element_type=F32)
    base = prefix + run_ref[:, 0:1]
    rank_ref[0:1, :] = jnp.sum(jnp.where(oh1, base, 0.0), axis=0, keepdims=True).astype(I32)
    rank_ref[1:2, :] = jnp.sum(jnp.where(oh2, base, 0.0), axis=0, keepdims=True).astype(I32)
    run = run_ref[...] + jnp.sum(both, axis=1, keepdims=True)
    run_ref[...] = run
    cnt_ref[...] = run.astype(I32)


def route(lt, tm=512):
    nr, t = lt.shape
    two = lambda dt: jax.ShapeDtypeStruct((2, t), dt)
    return pl.pallas_call(
        _route_kernel,
        grid=(t // tm,),
        in_specs=[pl.BlockSpec((nr, tm), lambda i: (0, i))],
        out_specs=[
            pl.BlockSpec((2, tm), lambda i: (0, i)),
            pl.BlockSpec((2, tm), lambda i: (0, i)),
            pl.BlockSpec((2, tm), lambda i: (0, i)),
            pl.BlockSpec((MOE_EXPERTS, LANES), lambda i: (0, 0)),
        ],
        out_shape=[two(I32), two(F32), two(I32),
                   jax.ShapeDtypeStruct((MOE_EXPERTS, LANES), I32)],
        scratch_shapes=[pltpu.VMEM((MOE_EXPERTS, LANES), F32)],
        compiler_params=_params(("arbitrary",)),
        name="route",
    )(lt)


def _dispatch_kernel(d0_ref, d1_ref, h_ref, xin_ref, xbuf_ref, sem):
    del xin_ref
    tm = h_ref.shape[0]
    dests = (d0_ref, d1_ref)

    def copy(r, k):
        return pltpu.make_async_copy(h_ref.at[r], xbuf_ref.at[dests[k][0, r]], sem)

    def start(r, carry):
        copy(r, 0).start(priority=0)
        copy(r, 1).start(priority=1)
        return carry

    lax.fori_loop(0, tm, start, 0, unroll=8)

    def wait(r, carry):
        copy(r, 0).wait()
        copy(r, 1).wait()
        return carry

    lax.fori_loop(0, tm, wait, 0, unroll=8)


def _row_index_spec(tm, ahead=0, last=None):
    def index(i):
        return (i if ahead == 0 else jnp.minimum(i + ahead, last), 0, 0)
    return pl.BlockSpec((None, 1, tm), index, memory_space=pltpu.SMEM)


def dispatch(dest0, dest1, h3, n_rows):
    t, s, lanes = h3.shape
    nt, _, tm = dest0.shape
    return pl.pallas_call(
        _dispatch_kernel,
        grid=(nt,),
        in_specs=[
            _row_index_spec(tm), _row_index_spec(tm),
            pl.BlockSpec((tm, s, lanes), lambda i: (i, 0, 0)),
            pl.BlockSpec(memory_space=pl.ANY),
        ],
        out_specs=pl.BlockSpec(memory_space=pl.ANY),
        out_shape=jax.ShapeDtypeStruct((n_rows, s, lanes), h3.dtype),
        scratch_shapes=[pltpu.SemaphoreType.DMA(())],
        input_output_aliases={3: 0},
        compiler_params=_params(("arbitrary",)),
        name="moe_dispatch",
    )(dest0, dest1, h3, jnp.zeros((n_rows, s, lanes), h3.dtype))


def _experts_kernel(be_ref, nu_ref, nx_ref, x_ref, wg_hbm, wu_hbm, wd_hbm, o_ref,
                    wg_f32, wu_f32, wd_f32, wgb, wub, wdb, slot_ref, sem, *, layer):
    i = pl.program_id(0)
    e = be_ref[i]
    first = i == 0
    changed = jnp.logical_or(first, e != be_ref[jnp.maximum(i - 1, 0)])
    streams = ((wg_hbm, wg_f32), (wu_hbm, wu_f32), (wd_hbm, wd_f32))

    def fetch(expert, slot):
        return [pltpu.make_async_copy(w_hbm.at[layer, expert], w_f32.at[slot], sem.at[slot, j])
                for j, (w_hbm, w_f32) in enumerate(streams)]

    @pl.when(first)
    def _():
        slot_ref[0] = 1
        for c in fetch(e, 0):
            c.start()

    @pl.when(changed)
    def _():
        slot = 1 - slot_ref[0]
        slot_ref[0] = slot
        for c in fetch(e, slot):
            c.wait()
        nxt = nx_ref[i]

        @pl.when(nxt >= 0)
        def _():
            for c in fetch(nxt, 1 - slot):
                c.start()

        wgb[...] = wg_f32[slot].astype(BF16)
        wub[...] = wu_f32[slot].astype(BF16)
        wdb[...] = wd_f32[slot].astype(BF16)

    @pl.when(i < nu_ref[0])
    def _():
        bm, s, lanes = x_ref.shape
        x = x_ref[...].reshape(bm, s * lanes)
        g = jnp.dot(x, wgb[...], preferred_element_type=F32)
        u = jnp.dot(x, wub[...], preferred_element_type=F32)
        a = ((g * jax.nn.sigmoid(g)) * u).astype(BF16)
        y = jnp.dot(a, wdb[...], preferred_element_type=F32)
        o_ref[...] = y.astype(BF16).reshape(o_ref.shape)

    @pl.when(i >= nu_ref[0])
    def _():
        o_ref[...] = jnp.zeros_like(o_ref)


def experts(layer, block_expert, n_used, next_expert, xbuf, w_gate, w_up, w_down):
    n_rows, s, lanes = xbuf.shape
    d = s * lanes
    hid = w_gate.shape[3]
    bm = MOE_ROWS
    nblk = n_rows // bm
    grid_spec = pltpu.PrefetchScalarGridSpec(
        num_scalar_prefetch=3,
        grid=(nblk,),
        in_specs=[
            pl.BlockSpec((bm, s, lanes), lambda i, be, nu, nx: (jnp.minimum(i, nu[0] - 1), 0, 0)),
            pl.BlockSpec(memory_space=pl.ANY),
            pl.BlockSpec(memory_space=pl.ANY),
            pl.BlockSpec(memory_space=pl.ANY),
        ],
        out_specs=pl.BlockSpec((bm, s, lanes), lambda i, be, nu, nx: (i, 0, 0)),
        scratch_shapes=[pltpu.VMEM((2, d, hid), F32), pltpu.VMEM((2, d, hid), F32),
                        pltpu.VMEM((2, hid, d), F32),
                        pltpu.VMEM((d, hid), BF16), pltpu.VMEM((d, hid), BF16),
                        pltpu.VMEM((hid, d), BF16),
                        pltpu.SMEM((1,), I32), pltpu.SemaphoreType.DMA((2, 3))],
    )
    return pl.pallas_call(
        functools.partial(_experts_kernel, layer=layer),
        grid_spec=grid_spec,
        out_shape=jax.ShapeDtypeStruct((n_rows, s, lanes), BF16),
        compiler_params=_params(("arbitrary",)),
        name="moe_experts",
    )(block_expert, n_used, next_expert, xbuf, w_gate, w_up, w_down)


def _combine_kernel(d0_ref, d1_ref, n0_ref, n1_ref, gate_ref, x_ref, g_ref, ybuf_ref, o_ref,
                    buf, sem, *, final_norm):
    i = pl.program_id(0)
    tm, d = x_ref.shape
    slot = i % 2

    def copy(dests, s, r, k):
        return pltpu.make_async_copy(ybuf_ref.at[dests[k][0, r]], buf.at[s, k, r], sem.at[s])

    def gather(dests, s):
        def start(r, carry):
            copy(dests, s, r, 0).start(priority=0)
            copy(dests, s, r, 1).start(priority=1)
            return carry

        lax.fori_loop(0, tm, start, 0, unroll=8)

    @pl.when(i == 0)
    def _():
        gather((d0_ref, d1_ref), 0)

    @pl.when(i + 1 < pl.num_programs(0))
    def _():
        gather((n0_ref, n1_ref), 1 - slot)

    def wait(r, carry):
        copy((d0_ref, d1_ref), slot, r, 0).wait()
        copy((d0_ref, d1_ref), slot, r, 1).wait()
        return carry

    lax.fori_loop(0, tm, wait, 0, unroll=8)

    gates = gate_ref[...]
    y = (gates[:, 0:1] * buf[slot, 0].reshape(tm, d).astype(F32)
         + gates[:, 1:2] * buf[slot, 1].reshape(tm, d).astype(F32))
    out = x_ref[...] + y
    if final_norm:
        out = _rms(out, g_ref[...])
    o_ref[...] = out


def combine(dest0, dest1, gates_t, x, ybuf, g_final, final_norm):
    t, d = x.shape
    nt, _, tm = dest0.shape
    _, s, lanes = ybuf.shape
    return pl.pallas_call(
        functools.partial(_combine_kernel, final_norm=final_norm),
        grid=(nt,),
        in_specs=[
            _row_index_spec(tm), _row_index_spec(tm),
            _row_index_spec(tm, ahead=1, last=nt - 1), _row_index_spec(tm, ahead=1, last=nt - 1),
            pl.BlockSpec((tm, 2), lambda i: (i, 0)),
            pl.BlockSpec((tm, d), lambda i: (i, 0)),
            pl.BlockSpec((1, d), lambda i: (0, 0)),
            pl.BlockSpec(memory_space=pl.ANY),
        ],
        out_specs=pl.BlockSpec((tm, d), lambda i: (i, 0)),
        out_shape=jax.ShapeDtypeStruct((t, d), F32),
        scratch_shapes=[pltpu.VMEM((2, 2, tm, s, lanes), ybuf.dtype),
                        pltpu.SemaphoreType.DMA((2,))],
        compiler_params=_params(("arbitrary",)),
        name="moe_combine",
    )(dest0, dest1, dest0, dest1, gates_t, x, g_final.reshape(1, d), ybuf)


def hier_moe_block(layer, x1, h, lt, w_gate, w_up, w_down, g_final, final_norm, tm=MOE_TOKENS):
    t, d = x1.shape
    bm = MOE_ROWS
    eid, gate, rank, cnt = route(lt)
    counts = cnt[:, 0]
    padded = ((counts + bm - 1) // bm) * bm
    pend = jnp.cumsum(padded)
    pstart = pend - padded
    experts_col = jnp.arange(MOE_EXPERTS, dtype=I32)[:, None, None]
    dest = rank + jnp.sum(jnp.where(eid[None] == experts_col, pstart[:, None, None], 0), axis=0)
    n_rows = 2 * t + MOE_EXPERTS * bm
    nblk = n_rows // bm
    n_used = (pend[-1] // bm).astype(I32)
    first_row = jnp.minimum(jnp.arange(nblk, dtype=I32), n_used - 1) * bm
    block_expert = jnp.sum(pend[None, :] <= first_row[:, None], axis=1).astype(I32)
    block_expert = jnp.minimum(block_expert, MOE_EXPERTS - 1)
    ids = jnp.arange(MOE_EXPERTS, dtype=I32)
    later = jnp.logical_and(ids[None, :] > block_expert[:, None], counts[None, :] > 0)
    next_expert = jnp.min(jnp.where(later, ids[None, :], MOE_EXPERTS), axis=1)
    next_expert = jnp.where(next_expert < MOE_EXPERTS, next_expert, -1).astype(I32)
    dest0 = dest[0].reshape(t // tm, 1, tm)
    dest1 = dest[1].reshape(t // tm, 1, tm)
    xbuf = dispatch(dest0, dest1, h, n_rows)
    ybuf = experts(layer, block_expert, n_used.reshape(1), next_expert, xbuf, w_gate, w_up, w_down)
    return combine(dest0, dest1, jnp.transpose(gate), x1, ybuf, g_final, final_norm)


def _attn_kernel(sink_ref, q_ref, kp_ref, kc_ref, kn_ref, vp_ref, vc_ref, vn_ref, bias_ref, o_ref):
    n = pl.program_id(1)
    nblk = pl.num_programs(1)
    blk = q_ref.shape[0]
    dh = SWA_HEAD_DIM
    nkv = kc_ref.shape[1] // dh
    masked = bias_ref.shape[1] - 1
    part_prev = jnp.where(n > 0, 0, masked)
    part_next = jnp.where(n < nblk - 1, 2, masked)
    lo = lax.broadcasted_iota(I32, (blk, 2 * dh), 1) < dh
    top = lax.broadcasted_iota(I32, (2 * blk, 2 * dh), 0) < blk
    lo2 = lax.broadcasted_iota(I32, (2 * blk, 2 * dh), 1) < dh
    ones_ext = jnp.concatenate([jnp.where(lo, 1.0, 0.0), jnp.where(lo, 0.0, 1.0)], axis=0).astype(BF16)
    qscale = (dh ** -0.5) * LOG2E

    for kv in range(nkv):
        col = slice((kv // 2) * 2 * dh, (kv // 2 + 1) * 2 * dh)

        def extend(ref):
            x = ref[:, col].astype(F32)
            r = pltpu.roll(x, dh, 1)
            x_lo, x_hi = (x, r) if kv % 2 == 0 else (r, x)
            return jnp.concatenate([jnp.where(lo, x_lo, 0.0), jnp.where(lo, 0.0, x_hi)],
                                   axis=0).astype(BF16)

        q2 = jnp.concatenate([q_ref[:, (2 * kv) * 2 * dh:(2 * kv + 1) * 2 * dh],
                              q_ref[:, (2 * kv + 1) * 2 * dh:(2 * kv + 2) * 2 * dh]], axis=0)
        q2 = (q2.astype(F32) * qscale).astype(BF16)

        def scores(k_ref, part):
            return lax.dot_general(q2, extend(k_ref), NT_DIMS,
                                   preferred_element_type=F32) + bias_ref[kv, part]

        s = [scores(kp_ref, part_prev), scores(kc_ref, 1), scores(kn_ref, part_next)]
        mx = jnp.maximum(jnp.maximum(s[0], s[1]), s[2])
        sk = [jnp.where(top, sink_ref[0, 4 * kv + par], sink_ref[0, 4 * kv + 2 + par])
              for par in range(2)]
        m = [jnp.maximum(jnp.broadcast_to(jnp.max(mx[:, par * blk:(par + 1) * blk], axis=-1,
                                                  keepdims=True), (2 * blk, 2 * dh)), sk[par])
             for par in range(2)]
        acc = jnp.zeros((2 * blk, 4 * dh), F32)
        for sp, v_ref in zip(s, (vp_ref, vc_ref, vn_ref)):
            e = jnp.concatenate([jnp.exp2(sp[:, :blk] - m[0]), jnp.exp2(sp[:, blk:] - m[1])],
                                axis=1).astype(BF16)
            rhs = jnp.concatenate([extend(v_ref), ones_ext], axis=1)
            acc = acc + jnp.dot(e, rhs, preferred_element_type=F32)
        den = acc[:, 2 * dh:] + jnp.exp2(jnp.where(lo2, sk[0] - m[0], sk[1] - m[1]))
        o = (acc[:, :2 * dh] / den).astype(BF16)
        o_ref[:, (2 * kv) * 2 * dh:(2 * kv + 1) * 2 * dh] = o[:blk]
        o_ref[:, (2 * kv + 1) * 2 * dh:(2 * kv + 2) * 2 * dh] = o[blk:]


def banded_attention(qkv3, sink, bias):
    b, seq, width = qkv3.shape
    nh = sink.shape[1]
    qd = nh * SWA_HEAD_DIM
    kvd = (width - qd) // 2
    blk = SWA_BLOCK
    nblk = seq // blk
    kcol, vcol = qd // kvd, qd // kvd + 1

    def band(col, off):
        return pl.BlockSpec((None, blk, kvd),
                            lambda bi, n: (bi, jnp.clip(n + off, 0, nblk - 1), col))

    return pl.pallas_call(
        _attn_kernel,
        grid=(b, nblk),
        in_specs=[
            pl.BlockSpec(memory_space=pltpu.SMEM),
            pl.BlockSpec((None, blk, qd), lambda bi, n: (bi, n, 0)),
            band(kcol, -1), band(kcol, 0), band(kcol, 1),
            band(vcol, -1), band(vcol, 0), band(vcol, 1),
            pl.BlockSpec(bias.shape, lambda bi, n: (0, 0, 0, 0)),
        ],
        out_specs=pl.BlockSpec((None, blk, qd), lambda bi, n: (bi, n, 0)),
        out_shape=jax.ShapeDtypeStruct((b, seq, qd), BF16),
        compiler_params=_params(("parallel", "parallel")),
        name="banded_attention",
    )(sink, qkv3, qkv3, qkv3, qkv3, qkv3, qkv3, qkv3, bias)


def _t5_bucket(rel):
    half = REL_BUCKETS // 2
    max_exact = half // 2
    n = jnp.abs(rel)
    large = max_exact + (jnp.log(jnp.maximum(n, 1).astype(F32) / max_exact)
                         / math.log(REL_MAX_DIST / max_exact) * (half - max_exact)).astype(I32)
    large = jnp.minimum(large, half - 1)
    return jnp.where(rel > 0, half, 0) + jnp.where(n < max_exact, n, large)


def _attention_bias(rel_bias):
    blk = SWA_BLOCK
    nh = rel_bias.shape[1]
    nkv = nh // SWA_Q_PER_KV
    rel = jnp.arange(3 * blk)[None, :] - blk - jnp.arange(blk)[:, None]
    onehot = (_t5_bucket(rel)[..., None] == jnp.arange(REL_BUCKETS)).astype(F32)
    bias = jnp.einsum('ijb,bh->hij', onehot, rel_bias.astype(F32), precision=lax.Precision.HIGHEST)
    bias = jnp.where((jnp.abs(rel) <= SWA_WINDOW)[None], bias, NEG_INF) * LOG2E
    tiles = bias.reshape(nkv, 2, 2, blk, 3, blk).transpose(0, 4, 1, 3, 2, 5)
    tiles = tiles.reshape(nkv, 3, 2 * blk, 2 * blk)
    masked = jnp.full((nkv, 1, 2 * blk, 2 * blk), NEG_INF * LOG2E, F32)
    return jnp.concatenate([tiles, masked], axis=1)


def _rotary_tables(seq, dim):
    inv_freq = ROPE_BASE ** (-jnp.arange(0, dim, 2, dtype=F32) / dim)
    ang = jnp.arange(seq, dtype=F32)[:, None] * inv_freq[None, :]
    cos, sin = jnp.cos(ang), jnp.sin(ang)
    return jnp.concatenate([cos, cos], axis=1), jnp.concatenate([-sin, sin], axis=1)


def kernel(x, norm_mix_g, norm_ffn_g, norm_final_g, hyb_w_in, ret_decay_logit, s5_a_re, s5_a_im, s5_log_step, s5_b_re, s5_b_im, s5_c_re, s5_c_im, s5_d, s5_w_glu, s5_b_glu, hyb_w_out, swa_w_qkv, swa_sink, swa_w_o, rel_bias, moe_w_group, moe_b_group, moe_w_expert_router, moe_b_expert_router, moe_w_gate, moe_w_up, moe_w_down):
    nb, seq, d = x.shape
    t = nb * seq
    depth = norm_mix_g.shape[0]
    xt = x.reshape(t, d)
    for layer in range(depth):
        i = layer // 2
        if layer % 2 == 0:
            w = hyb_w_out.shape[1] // 2
            z = norm_matmul(xt, norm_mix_g[layer], hyb_w_in[i])
            z3 = z.reshape(nb, seq, z.shape[1])
            cos, sin = _rotary_tables(seq, w // RET_HEADS)
            log_gamma = jax.nn.log_sigmoid(ret_decay_logit[i].astype(F32))
            y_ret = retention(z3, log_gamma, cos, sin).reshape(t, w)
            lam, bmat, cmat = _s5_discretize(s5_a_re[i], s5_a_im[i], s5_log_step[i], s5_b_re[i],
                                             s5_b_im[i], s5_c_re[i], s5_c_im[i])
            ys5 = s5_scan(z3, lam, bmat, cmat).reshape(2, t, w)
            mix_in = glu_concat(y_ret, ys5, z, s5_d[i].astype(F32).reshape(-1), s5_w_glu[i],
                                s5_b_glu[i].astype(F32), nb)
            w_mix = hyb_w_out[i]
        else:
            qkv = norm_matmul(xt, norm_mix_g[layer], swa_w_qkv[i])
            mix_in = banded_attention(qkv.reshape(nb, seq, qkv.shape[1]),
                                      swa_sink[i].astype(F32).reshape(1, -1) * LOG2E,
                                      _attention_bias(rel_bias)).reshape(t, -1)
            w_mix = swa_w_o[i]
        r_hi, r_lo, r_bias = _router_operands(moe_w_group[layer], moe_b_group[layer],
                                              moe_w_expert_router[layer], moe_b_expert_router[layer])
        x1, h, lt = proj_norm_router(mix_in, w_mix, xt, norm_ffn_g[layer], r_hi, r_lo, r_bias)
        last = layer == depth - 1
        xt = hier_moe_block(layer, x1, h, lt, moe_w_gate, moe_w_up, moe_w_down,
                            norm_final_g, final_norm=last)
    return xt.reshape(nb, seq, d)
```

```python
import functools
import math

import jax
import jax.numpy as jnp
from jax import lax
from jax.experimental import pallas as pl
from jax.experimental.pallas import tpu as pltpu

F32 = jnp.float32
BF16 = jnp.bfloat16
I32 = jnp.int32

RET_HEADS = 4
RET_CHUNK = 128
S5_GROUP_CH = 16
S5_STATE = 64
SWA_HEAD_DIM = 64
SWA_Q_PER_KV = 4
SWA_WINDOW = 128
SWA_BLOCK = 128
REL_BUCKETS = 32
REL_MAX_DIST = 128
MOE_GROUPS = 4
MOE_EXPERTS_PER_GROUP = 8
MOE_EXPERTS = MOE_GROUPS * MOE_EXPERTS_PER_GROUP
ROPE_BASE = 10000.0
RMS_EPS = 1e-6
GN_EPS = 1e-5
NEG_INF = -1e30
LOG2E = 1.4426950408889634

LANES = 128
SUBLANES = 8
V7X_VMEM_BYTES = 64 * 1024 * 1024
VMEM_LIMIT = V7X_VMEM_BYTES - 8 * 1024 * 1024

MOE_ROWS = 512
MOE_TOKENS = 512
S5_CHUNK = 128
S5_COLS = 512
S5_SCAN_COLS = 1024
S5_KBLK = 128
ROUTER_ROWS = 128
RET_UNROLL = 8

NT_DIMS = (((1,), (1,)), ((), ()))
TN_DIMS = (((0,), (0,)), ((), ()))


def _params(semantics):
    return pltpu.CompilerParams(dimension_semantics=semantics, vmem_limit_bytes=VMEM_LIMIT)


def _rms(x, g):
    ms = jnp.mean(x * x, axis=-1, keepdims=True)
    return (x * lax.rsqrt(ms + RMS_EPS)) * g


def _norm_matmul_kernel(x_ref, g_ref, w_ref, o_ref):
    h = _rms(x_ref[...], g_ref[...])
    o_ref[...] = jnp.dot(h.astype(BF16), w_ref[...],
                         preferred_element_type=F32).astype(o_ref.dtype)


def norm_matmul(x, g, w, tm=512):
    t, d = x.shape
    n = w.shape[1]
    return pl.pallas_call(
        _norm_matmul_kernel,
        grid=(t // tm,),
        in_specs=[
            pl.BlockSpec((tm, d), lambda i: (i, 0)),
            pl.BlockSpec((1, d), lambda i: (0, 0)),
            pl.BlockSpec((d, n), lambda i: (0, 0)),
        ],
        out_specs=pl.BlockSpec((tm, n), lambda i: (i, 0)),
        out_shape=jax.ShapeDtypeStruct((t, n), BF16),
        compiler_params=_params(("parallel",)),
        name="norm_matmul",
    )(x, g.reshape(1, d), w.astype(BF16))


def _retention_kernel(lg_ref, q_ref, k_ref, v_ref, g_ref, cos_ref, sin_ref, o_ref,
                      qr_ref, kr_ref, inc_ref, st_ref, lhs_ref):
    h = pl.program_id(1)
    lg_f = lg_ref[0, h]
    lg_b = lg_ref[1, h]
    seq, dk = q_ref.shape
    c = RET_CHUNK
    nc = seq // c

    cos = cos_ref[...]
    sin = sin_ref[...]
    swap = (lax.broadcasted_iota(I32, (dk, dk), 0)
            == (lax.broadcasted_iota(I32, (dk, dk), 1) + dk // 2) % dk)
    swap = jnp.where(swap, 1.0, 0.0).astype(BF16)
    q = q_ref[...]
    qr_ref[...] = (q.astype(F32) * cos
                   + jnp.dot(q, swap, preferred_element_type=F32) * sin)
    k = k_ref[...]
    kr_ref[...] = (k.astype(F32) * cos
                   + jnp.dot(k, swap, preferred_element_type=F32) * sin) * (dk ** -0.5)

    pos = lax.broadcasted_iota(I32, (c, dk), 0).astype(F32)
    kf_scale = jnp.exp((c - 1.0 - pos) * lg_f)
    qf_scale = jnp.exp((pos + 1.0) * lg_f)
    kb_scale = jnp.exp(pos * lg_b)
    qb_scale = jnp.exp((c - pos) * lg_b)
    rel = (lax.broadcasted_iota(I32, (c, c), 0) - lax.broadcasted_iota(I32, (c, c), 1)).astype(F32)
    mask = jnp.exp(jnp.abs(rel) * jnp.where(rel >= 0, lg_f, lg_b))
    dec_f = jnp.exp(jnp.full((dk, dk), c * lg_f, F32))
    dec_b = jnp.exp(jnp.full((dk, dk), c * lg_b, F32))

    def increments(n, carry):
        rows = pl.ds(pl.multiple_of(n * c, c), c)
        kc = kr_ref[rows, :]
        kk = jnp.concatenate([kc * kf_scale, kc * kb_scale], axis=1).astype(BF16)
        inc_ref[n] = lax.dot_general(kk, v_ref[rows, :], TN_DIMS,
                                     preferred_element_type=F32)
        return carry

    lax.fori_loop(0, nc, increments, 0, unroll=min(RET_UNROLL, nc))

    def fwd(n, state):
        st_ref[n, :dk, :] = state.astype(BF16)
        return state * dec_f + inc_ref[n, :dk, :]

    lax.fori_loop(0, nc, fwd, jnp.zeros((dk, dk), F32))

    def bwd(i, state):
        n = nc - 1 - i
        st_ref[n, dk:, :] = state.astype(BF16)
        return state * dec_b + inc_ref[n, dk:, :]

    lax.fori_loop(0, nc, bwd, jnp.zeros((dk, dk), F32))

    def operands(n, carry):
        rows = pl.ds(pl.multiple_of(n * c, c), c)
        qc = qr_ref[rows, :]
        s = lax.dot_general(qc.astype(BF16), kr_ref[rows, :].astype(BF16), NT_DIMS,
                            preferred_element_type=F32) * mask
        lhs_ref[n] = jnp.concatenate([s, qc * qf_scale, qc * qb_scale], axis=1).astype(BF16)
        return carry

    lax.fori_loop(0, nc, operands, 0, unroll=min(RET_UNROLL, nc))

    def outputs(n, carry):
        rows = pl.ds(pl.multiple_of(n * c, c), c)
        rhs = jnp.concatenate([v_ref[rows, :], st_ref[n]], axis=0)
        kr_ref[rows, :] = jnp.dot(lhs_ref[n], rhs, preferred_element_type=F32)
        return carry

    lax.fori_loop(0, nc, outputs, 0, unroll=min(RET_UNROLL, nc))

    out = kr_ref[...]
    mu = jnp.mean(out, axis=-1, keepdims=True)
    cen = out - mu
    var = jnp.mean(cen * cen, axis=-1, keepdims=True)
    g = g_ref[...].astype(F32)
    o_ref[...] = ((g * jax.nn.sigmoid(g)) * (cen * lax.rsqrt(var + GN_EPS))).astype(BF16)


def retention(z3, log_gamma, cos, sin):
    b, seq, _ = z3.shape
    nh = RET_HEADS
    dk = cos.shape[1]

    def col(off):
        return pl.BlockSpec((None, seq, dk), lambda bi, hi: (bi, 0, off + hi))

    return pl.pallas_call(
        _retention_kernel,
        grid=(b, nh),
        in_specs=[
            pl.BlockSpec(memory_space=pltpu.SMEM),
            col(0), col(nh), col(2 * nh), col(3 * nh),
            pl.BlockSpec((seq, dk), lambda bi, hi: (0, 0)),
            pl.BlockSpec((seq, dk), lambda bi, hi: (0, 0)),
        ],
        out_specs=pl.BlockSpec((None, seq, dk), lambda bi, hi: (bi, 0, hi)),
        out_shape=jax.ShapeDtypeStruct((b, seq, nh * dk), BF16),
        scratch_shapes=[pltpu.VMEM((seq, dk), F32), pltpu.VMEM((seq, dk), F32),
                        pltpu.VMEM((seq // RET_CHUNK, 2 * dk, dk), F32),
                        pltpu.VMEM((seq // RET_CHUNK, 2 * dk, dk), BF16),
                        pltpu.VMEM((seq // RET_CHUNK, RET_CHUNK, 3 * dk), BF16)],
        compiler_params=_params(("parallel", "parallel")),
        name="retention",
    )(log_gamma, z3, z3, z3, z3, cos, sin)


def _s5_kernel(u_ref, lam_ref, b_ref, c_ref, y_ref, up_ref, xre_ref, xim_ref, sre_ref, sim_ref):
    d = pl.program_id(0)
    n = pl.program_id(1)
    nb, cn, width = u_ref.shape
    srows = up_ref.shape[1]
    nk = width // S5_KBLK

    @pl.when(jnp.logical_and(d == 0, n == 0))
    def _():
        up_ref[...] = jnp.zeros_like(up_ref)

    @pl.when(n == 0)
    def _():
        sre_ref[...] = jnp.zeros_like(sre_ref)
        sim_ref[...] = jnp.zeros_like(sim_ref)

    for b in range(nb):
        up_ref[b, b:b + cn, :] = u_ref[b].astype(F32)
    u = up_ref[...].reshape(nb * srows, width).astype(BF16)
    ppc = S5_COLS // LANES
    for kb in range(nk):
        bu = jnp.dot(u[:, kb * S5_KBLK:(kb + 1) * S5_KBLK], b_ref[kb], preferred_element_type=F32)
        for j in range(ppc):
            xre_ref[kb * ppc + j] = bu[:, j * LANES:(j + 1) * LANES]
            xim_ref[kb * ppc + j] = bu[:, S5_COLS + j * LANES:S5_COLS + (j + 1) * LANES]

    pps = S5_SCAN_COLS // LANES
    for cb in range(sre_ref.shape[1] // S5_SCAN_COLS):
        cols = slice(cb * S5_SCAN_COLS, (cb + 1) * S5_SCAN_COLS)
        lr = jnp.broadcast_to(lam_ref[0:1, cols], (nb, S5_SCAN_COLS))
        li = jnp.broadcast_to(lam_ref[1:2, cols], (nb, S5_SCAN_COLS))

        def step(i, carry):
            xr, xi = carry
            t = i + d * (cn - 1 - 2 * i)
            rows = pl.ds(t, nb, stride=srows + 1)
            bur = jnp.concatenate([xre_ref[cb * pps + j, rows, :] for j in range(pps)], axis=1)
            bui = jnp.concatenate([xim_ref[cb * pps + j, rows, :] for j in range(pps)], axis=1)
            nxr = lr * xr - li * xi + bur
            nxi = lr * xi + li * xr + bui
            for j in range(pps):
                xre_ref[cb * pps + j, rows, :] = nxr[:, j * LANES:(j + 1) * LANES]
                xim_ref[cb * pps + j, rows, :] = nxi[:, j * LANES:(j + 1) * LANES]
            return nxr, nxi

        xr, xi = lax.fori_loop(0, cn, step, (sre_ref[:, cols], sim_ref[:, cols]), unroll=8)
        sre_ref[:, cols] = xr
        sim_ref[:, cols] = xi

    for kb in range(nk):
        xr = jnp.concatenate([xre_ref[kb * ppc + j] for j in range(ppc)], axis=1).astype(BF16)
        xi = jnp.concatenate([xim_ref[kb * ppc + j] for j in range(ppc)], axis=1).astype(BF16)
        y = jnp.dot(xr, c_ref[kb, :S5_COLS, :], preferred_element_type=F32)
        y = y + jnp.dot(xi, c_ref[kb, S5_COLS:, :], preferred_element_type=F32)
        y = y.reshape(nb, srows, S5_KBLK)
        for b in range(nb):
            y_ref[b, :, kb * S5_KBLK:(kb + 1) * S5_KBLK] = y[b, b:b + cn, :]


def s5_scan(z3, lam, bmat, cmat):
    nb, seq, zw = z3.shape
    width = bmat.shape[1] * bmat.shape[2]
    nstate = lam.shape[2]
    cn = min(S5_CHUNK, seq)
    nch = seq // cn
    srows = cn + SUBLANES
    ucol = zw // width - 1

    def chunk(d, n):
        return n + d * (nch - 1 - 2 * n)

    return pl.pallas_call(
        _s5_kernel,
        grid=(2, nch),
        in_specs=[
            pl.BlockSpec((nb, cn, width), lambda d, n: (0, chunk(d, n), ucol)),
            pl.BlockSpec((None, 2, nstate), lambda d, n: (d, 0, 0)),
            pl.BlockSpec((None,) + bmat.shape[1:], lambda d, n: (d, 0, 0, 0)),
            pl.BlockSpec((None,) + cmat.shape[1:], lambda d, n: (d, 0, 0, 0)),
        ],
        out_specs=pl.BlockSpec((None, nb, cn, width), lambda d, n: (d, 0, chunk(d, n), 0)),
        out_shape=jax.ShapeDtypeStruct((2, nb, seq, width), F32),
        scratch_shapes=[
            pltpu.VMEM((nb, srows, width), F32),
            pltpu.VMEM((nstate // LANES, nb * srows, LANES), F32),
            pltpu.VMEM((nstate // LANES, nb * srows, LANES), F32),
            pltpu.VMEM((nb, nstate), F32),
            pltpu.VMEM((nb, nstate), F32),
        ],
        compiler_params=_params(("arbitrary", "arbitrary")),
        name="s5_scan",
    )(z3, lam, bmat, cmat)


def _s5_discretize(a_re, a_im, log_step, b_re, b_im, c_re, c_im):
    ng, npst = a_re.shape[1], a_re.shape[2]
    gpb = S5_KBLK // S5_GROUP_CH
    nk = ng // gpb
    eye = jnp.eye(gpb, dtype=F32)
    bre, bim = b_re.astype(F32), b_im.astype(F32)
    lams, bmats, cmats = [], [], []
    for direction in range(2):
        ar = a_re[direction].astype(F32)
        ai = a_im[direction].astype(F32)
        dt = jnp.exp(log_step[direction].astype(F32))[:, None]
        mag = jnp.exp(ar * dt)
        lam_re, lam_im = mag * jnp.cos(ai * dt), mag * jnp.sin(ai * dt)
        nr, ni = lam_re - 1.0, lam_im
        den = ar * ar + ai * ai
        coef_re = (nr * ar + ni * ai) / den
        coef_im = (ni * ar - nr * ai) / den
        bbar_re = coef_re[..., None] * bre - coef_im[..., None] * bim
        bbar_im = coef_re[..., None] * bim + coef_im[..., None] * bre

        def in_blocks(m):
            m4 = m.reshape(nk, gpb, npst, S5_GROUP_CH)
            return jnp.einsum('kgpc,gh->kgchp', m4, eye).reshape(nk, S5_KBLK, gpb * npst)

        def out_blocks(m):
            m4 = m.reshape(nk, gpb, S5_GROUP_CH, npst)
            return jnp.einsum('kgcp,gh->kgphc', m4, eye).reshape(nk, gpb * npst, S5_KBLK)

        lams.append(jnp.stack([lam_re.reshape(-1), lam_im.reshape(-1)]))
        bmats.append(jnp.concatenate([in_blocks(bbar_re), in_blocks(bbar_im)], axis=2))
        cmats.append(jnp.concatenate([out_blocks(c_re[direction].astype(F32)),
                                      -out_blocks(c_im[direction].astype(F32))], axis=1))
    return jnp.stack(lams), jnp.stack(bmats).astype(BF16), jnp.stack(cmats).astype(BF16)


def _glu_kernel(yr_ref, yf_ref, yb_ref, u_ref, d_ref, w_ref, b_ref, o_ref):
    w = yr_ref.shape[1]
    y = u_ref[...].astype(F32) * d_ref[...] + yf_ref[...] + yb_ref[...]
    y = jax.nn.gelu(y)
    gate = jax.nn.sigmoid(jnp.dot(y.astype(BF16), w_ref[...], preferred_element_type=F32) + b_ref[...])
    o_ref[:, :w] = yr_ref[...]
    o_ref[:, w:] = (y * gate).astype(BF16)


def glu_concat(y_ret, ys5, z, d_skip, w_glu, b_glu, nb, tm=512):
    t, w = y_ret.shape
    seq = t // nb
    tm = min(tm, seq)
    nl = seq // tm
    ucol = z.shape[1] // w - 1
    return pl.pallas_call(
        _glu_kernel,
        grid=(nb, nl),
        in_specs=[
            pl.BlockSpec((tm, w), lambda b, i: (b * nl + i, 0)),
            pl.BlockSpec((None, tm, w), lambda b, i: (0, b * nl + i, 0)),
            pl.BlockSpec((None, tm, w), lambda b, i: (1, b * nl + i, 0)),
            pl.BlockSpec((tm, w), lambda b, i: (b * nl + i, ucol)),
            pl.BlockSpec((1, w), lambda b, i: (0, 0)),
            pl.BlockSpec((w, w), lambda b, i: (0, 0)),
            pl.BlockSpec((1, w), lambda b, i: (0, 0)),
        ],
        out_specs=pl.BlockSpec((tm, 2 * w), lambda b, i: (b * nl + i, 0)),
        out_shape=jax.ShapeDtypeStruct((t, 2 * w), BF16),
        compiler_params=_params(("parallel", "parallel")),
        name="glu_concat",
    )(y_ret, ys5, ys5, z, d_skip.reshape(1, w), w_glu.astype(BF16), b_glu.reshape(1, w))


def _proj_kernel(a_ref, w_ref, x_ref, g_ref, rh_ref, rl_ref, rb_ref, x1_ref, h_ref, lt_ref):
    x1 = x_ref[...] + jnp.dot(a_ref[...], w_ref[...], preferred_element_type=F32)
    x1_ref[...] = x1
    h = _rms(x1, g_ref[...])
    h_hi = h.astype(BF16)
    h_ref[...] = h_hi.reshape(h_ref.shape)
    h_lo = (h - h_hi.astype(F32)).astype(BF16)
    lt = lax.dot_general(rh_ref[...], h_hi, NT_DIMS, preferred_element_type=F32)
    lt = lt + lax.dot_general(rh_ref[...], h_lo, NT_DIMS, preferred_element_type=F32)
    lt = lt + lax.dot_general(rl_ref[...], h_hi, NT_DIMS, preferred_element_type=F32)
    lt_ref[...] = lt + rb_ref[...]


def proj_norm_router(a, w, x, g, r_hi, r_lo, r_bias, tm=512):
    t, d = x.shape
    k = a.shape[1]
    nr = r_hi.shape[0]
    return pl.pallas_call(
        _proj_kernel,
        grid=(t // tm,),
        in_specs=[
            pl.BlockSpec((tm, k), lambda i: (i, 0)),
            pl.BlockSpec((k, d), lambda i: (0, 0)),
            pl.BlockSpec((tm, d), lambda i: (i, 0)),
            pl.BlockSpec((1, d), lambda i: (0, 0)),
            pl.BlockSpec((nr, d), lambda i: (0, 0)),
            pl.BlockSpec((nr, d), lambda i: (0, 0)),
            pl.BlockSpec((nr, 1), lambda i: (0, 0)),
        ],
        out_specs=[
            pl.BlockSpec((tm, d), lambda i: (i, 0)),
            pl.BlockSpec((tm, d // LANES, LANES), lambda i: (i, 0, 0)),
            pl.BlockSpec((nr, tm), lambda i: (0, i)),
        ],
        out_shape=[
            jax.ShapeDtypeStruct((t, d), F32),
            jax.ShapeDtypeStruct((t, d // LANES, LANES), BF16),
            jax.ShapeDtypeStruct((nr, t), F32),
        ],
        compiler_params=_params(("parallel",)),
        name="proj_norm_router",
    )(a, w.astype(BF16), x, g.reshape(1, d), r_hi, r_lo, r_bias)


def _router_operands(w_group, b_group, w_er, b_er):
    d = w_group.shape[0]
    wt = jnp.concatenate([
        jnp.transpose(w_er.astype(F32), (0, 2, 1)).reshape(MOE_EXPERTS, d),
        jnp.transpose(w_group.astype(F32)),
        jnp.zeros((ROUTER_ROWS - MOE_EXPERTS - MOE_GROUPS, d), F32)], axis=0)
    bias = jnp.concatenate([
        b_er.astype(F32).reshape(-1), b_group.astype(F32),
        jnp.zeros((ROUTER_ROWS - MOE_EXPERTS - MOE_GROUPS,), F32)]).reshape(ROUTER_ROWS, 1)
    hi = wt.astype(BF16)
    lo = (wt - hi.astype(F32)).astype(BF16)
    return hi, lo, bias


def _route_kernel(lt_ref, eid_ref, gate_ref, rank_ref, cnt_ref, run_ref):
    i = pl.program_id(0)
    tm = lt_ref.shape[1]
    ne, npg, ng = MOE_EXPERTS, MOE_EXPERTS_PER_GROUP, MOE_GROUPS

    @pl.when(i == 0)
    def _():
        run_ref[...] = jnp.zeros_like(run_ref)

    gl = lt_ref[ne:ne + ng, :]
    gmax = jnp.max(gl, axis=0, keepdims=True)
    gidx = lax.broadcasted_iota(I32, (ng, tm), 0)
    gsel = jnp.min(jnp.where(gl == gmax, gidx, ng), axis=0, keepdims=True)
    p_g = 1.0 / jnp.sum(jnp.exp(gl - gmax), axis=0, keepdims=True)

    e8 = lt_ref[(ng - 1) * npg:ng * npg, :]
    for g in range(ng - 2, -1, -1):
        e8 = jnp.where(gsel == g, lt_ref[g * npg:(g + 1) * npg, :], e8)
    eidx = lax.broadcasted_iota(I32, (npg, tm), 0)
    m1 = jnp.max(e8, axis=0, keepdims=True)
    i1 = jnp.min(jnp.where(e8 == m1, eidx, npg), axis=0, keepdims=True)
    e8b = jnp.where(eidx == i1, -jnp.inf, e8)
    m2 = jnp.max(e8b, axis=0, keepdims=True)
    i2 = jnp.min(jnp.where(e8b == m2, eidx, npg), axis=0, keepdims=True)
    t2 = jnp.exp(m2 - m1)
    den = 1.0 + t2
    gate_ref[0:1, :] = (1.0 / den) * p_g
    gate_ref[1:2, :] = (t2 / den) * p_g
    id1 = gsel * npg + i1
    id2 = gsel * npg + i2
    eid_ref[0:1, :] = id1
    eid_ref[1:2, :] = id2

    rows = lax.broadcasted_iota(I32, (ne, tm), 0)
    oh1 = rows == id1
    oh2 = rows == id2
    both = jnp.where(oh1, 1.0, 0.0) + jnp.where(oh2, 1.0, 0.0)
    earlier = (lax.broadcasted_iota(I32, (tm, tm), 0) < lax.broadcasted_iota(I32, (tm, tm), 1))
    prefix = jnp.dot(both.astype(BF16), jnp.where(earlier, 1.0, 0.0).astype(BF16),
                     preferred_element_type=F32)
    base = prefix + run_ref[:, 0:1]
    rank_ref[0:1, :] = jnp.sum(jnp.where(oh1, base, 0.0), axis=0, keepdims=True).astype(I32)
    rank_ref[1:2, :] = jnp.sum(jnp.where(oh2, base, 0.0), axis=0, keepdims=True).astype(I32)
    run = run_ref[...] + jnp.sum(both, axis=1, keepdims=True)
    run_ref[...] = run
    cnt_ref[...] = run.astype(I32)


def route(lt, tm=512):
    nr, t = lt.shape
    two = lambda dt: jax.ShapeDtypeStruct((2, t), dt)
    return pl.pallas_call(
        _route_kernel,
        grid=(t // tm,),
        in_specs=[pl.BlockSpec((nr, tm), lambda i: (0, i))],
        out_specs=[
            pl.BlockSpec((2, tm), lambda i: (0, i)),
            pl.BlockSpec((2, tm), lambda i: (0, i)),
            pl.BlockSpec((2, tm), lambda i: (0, i)),
            pl.BlockSpec((MOE_EXPERTS, LANES), lambda i: (0, 0)),
        ],
        out_shape=[two(I32), two(F32), two(I32),
                   jax.ShapeDtypeStruct((MOE_EXPERTS, LANES), I32)],
        scratch_shapes=[pltpu.VMEM((MOE_EXPERTS, LANES), F32)],
        compiler_params=_params(("arbitrary",)),
        name="route",
    )(lt)


def _dispatch_kernel(pad_ref, nu_ref, d0_ref, d1_ref, h_ref, xbuf_ref, zero_ref, sem, zsem):
    i = pl.program_id(0)
    tm = h_ref.shape[0]
    dests = (d0_ref, d1_ref)
    bm = zero_ref.shape[0]
    nblk = xbuf_ref.shape[0] // bm

    def zero_fill(act):
        def per_expert(e, carry):
            pos = pad_ref[0, e]
            length = pad_ref[1, e]
            p = bm // 2
            while p >= 1:
                bit = (length & p) != 0

                @pl.when(bit)
                def _(pos=pos, p=p):
                    act(pltpu.make_async_copy(zero_ref.at[pl.ds(0, p)],
                                              xbuf_ref.at[pl.ds(pos, p)], zsem))

                pos = pos + jnp.where(bit, p, 0)
                p //= 2
            return carry

        lax.fori_loop(0, pad_ref.shape[1], per_expert, 0)

        def per_block(b, carry):
            act(pltpu.make_async_copy(zero_ref, xbuf_ref.at[pl.ds(b * bm, bm)], zsem))
            return carry

        lax.fori_loop(nu_ref[0], nblk, per_block, 0)

    @pl.when(i == 0)
    def _():
        zero_ref[...] = jnp.zeros_like(zero_ref)
        zero_fill(lambda c: c.start())

    def copy(r, k):
        return pltpu.make_async_copy(h_ref.at[r], xbuf_ref.at[dests[k][0, r]], sem)

    def start(r, carry):
        copy(r, 0).start(priority=0)
        copy(r, 1).start(priority=1)
        return carry

    lax.fori_loop(0, tm, start, 0, unroll=8)

    def wait(r, carry):
        copy(r, 0).wait()
        copy(r, 1).wait()
        return carry

    lax.fori_loop(0, tm, wait, 0, unroll=8)

    @pl.when(i == pl.num_programs(0) - 1)
    def _():
        zero_fill(lambda c: c.wait())


def _row_index_spec(tm, ahead=0, last=None):
    def index(i):
        return (i if ahead == 0 else jnp.minimum(i + ahead, last), 0, 0)
    return pl.BlockSpec((None, 1, tm), index, memory_space=pltpu.SMEM)


def dispatch(pad, n_used, dest0, dest1, h3, n_rows):
    t, s, lanes = h3.shape
    nt, _, tm = dest0.shape
    return pl.pallas_call(
        _dispatch_kernel,
        grid=(nt,),
        in_specs=[
            pl.BlockSpec(memory_space=pltpu.SMEM),
            pl.BlockSpec(memory_space=pltpu.SMEM),
            _row_index_spec(tm), _row_index_spec(tm),
            pl.BlockSpec((tm, s, lanes), lambda i: (i, 0, 0)),
        ],
        out_specs=pl.BlockSpec(memory_space=pl.ANY),
        out_shape=jax.ShapeDtypeStruct((n_rows, s, lanes), h3.dtype),
        scratch_shapes=[pltpu.VMEM((MOE_ROWS, s, lanes), h3.dtype),
                        pltpu.SemaphoreType.DMA(()), pltpu.SemaphoreType.DMA(())],
        compiler_params=_params(("arbitrary",)),
        name="moe_dispatch",
    )(pad, n_used, dest0, dest1, h3)


def _experts_kernel(be_ref, nu_ref, nx_ref, x_ref, wg_hbm, wu_hbm, wd_hbm, o_ref,
                    wg_f32, wu_f32, wd_f32, wgb, wub, wdb, slot_ref, sem, *, layer):
    i = pl.program_id(0)
    e = be_ref[i]
    first = i == 0
    changed = jnp.logical_or(first, e != be_ref[jnp.maximum(i - 1, 0)])
    streams = ((wg_hbm, wg_f32), (wu_hbm, wu_f32), (wd_hbm, wd_f32))

    def fetch(expert, slot):
        return [pltpu.make_async_copy(w_hbm.at[layer, expert], w_f32.at[slot], sem.at[slot, j])
                for j, (w_hbm, w_f32) in enumerate(streams)]

    @pl.when(first)
    def _():
        slot_ref[0] = 1
        for c in fetch(e, 0):
            c.start()

    @pl.when(changed)
    def _():
        slot = 1 - slot_ref[0]
        slot_ref[0] = slot
        for c in fetch(e, slot):
            c.wait()
        nxt = nx_ref[i]

        @pl.when(nxt >= 0)
        def _():
            for c in fetch(nxt, 1 - slot):
                c.start()

        wgb[...] = wg_f32[slot].astype(BF16)
        wub[...] = wu_f32[slot].astype(BF16)
        wdb[...] = wd_f32[slot].astype(BF16)

    @pl.when(i < nu_ref[0])
    def _():
        bm, s, lanes = x_ref.shape
        x = x_ref[...].reshape(bm, s * lanes)
        g = jnp.dot(x, wgb[...], preferred_element_type=F32)
        u = jnp.dot(x, wub[...], preferred_element_type=F32)
        a = ((g * jax.nn.sigmoid(g)) * u).astype(BF16)
        y = jnp.dot(a, wdb[...], preferred_element_type=F32)
        o_ref[...] = y.astype(BF16).reshape(o_ref.shape)

    @pl.when(i >= nu_ref[0])
    def _():
        o_ref[...] = jnp.zeros_like(o_ref)


def experts(layer, block_expert, n_used, next_expert, xbuf, w_gate, w_up, w_down):
    n_rows, s, lanes = xbuf.shape
    d = s * lanes
    hid = w_gate.shape[3]
    bm = MOE_ROWS
    nblk = n_rows // bm
    grid_spec = pltpu.PrefetchScalarGridSpec(
        num_scalar_prefetch=3,
        grid=(nblk,),
        in_specs=[
            pl.BlockSpec((bm, s, lanes), lambda i, be, nu, nx: (jnp.minimum(i, nu[0] - 1), 0, 0)),
            pl.BlockSpec(memory_space=pl.ANY),
            pl.BlockSpec(memory_space=pl.ANY),
            pl.BlockSpec(memory_space=pl.ANY),
        ],
        out_specs=pl.BlockSpec((bm, s, lanes), lambda i, be, nu, nx: (i, 0, 0)),
        scratch_shapes=[pltpu.VMEM((2, d, hid), F32), pltpu.VMEM((2, d, hid), F32),
                        pltpu.VMEM((2, hid, d), F32),
                        pltpu.VMEM((d, hid), BF16), pltpu.VMEM((d, hid), BF16),
                        pltpu.VMEM((hid, d), BF16),
                        pltpu.SMEM((1,), I32), pltpu.SemaphoreType.DMA((2, 3))],
    )
    return pl.pallas_call(
        functools.partial(_experts_kernel, layer=layer),
        grid_spec=grid_spec,
        out_shape=jax.ShapeDtypeStruct((n_rows, s, lanes), BF16),
        compiler_params=_params(("arbitrary",)),
        name="moe_experts",
    )(block_expert, n_used, next_expert, xbuf, w_gate, w_up, w_down)


def _combine_kernel(d0_ref, d1_ref, n0_ref, n1_ref, gate_ref, x_ref, g_ref, ybuf_ref, o_ref,
                    buf, sem, *, final_norm):
    i = pl.program_id(0)
    tm, d = x_ref.shape
    slot = i % 2

    def copy(dests, s, r, k):
        return pltpu.make_async_copy(ybuf_ref.at[dests[k][0, r]], buf.at[s, k, r], sem.at[s])

    def gather(dests, s):
        def start(r, carry):
            copy(dests, s, r, 0).start(priority=0)
            copy(dests, s, r, 1).start(priority=1)
            return carry

        lax.fori_loop(0, tm, start, 0, unroll=8)

    @pl.when(i == 0)
    def _():
        gather((d0_ref, d1_ref), 0)

    @pl.when(i + 1 < pl.num_programs(0))
    def _():
        gather((n0_ref, n1_ref), 1 - slot)

    def wait(r, carry):
        copy((d0_ref, d1_ref), slot, r, 0).wait()
        copy((d0_ref, d1_ref), slot, r, 1).wait()
        return carry

    lax.fori_loop(0, tm, wait, 0, unroll=8)

    gates = gate_ref[...]
    y = (gates[:, 0:1] * buf[slot, 0].reshape(tm, d).astype(F32)
         + gates[:, 1:2] * buf[slot, 1].reshape(tm, d).astype(F32))
    out = x_ref[...] + y
    if final_norm:
        out = _rms(out, g_ref[...])
    o_ref[...] = out


def combine(dest0, dest1, gates_t, x, ybuf, g_final, final_norm):
    t, d = x.shape
    nt, _, tm = dest0.shape
    _, s, lanes = ybuf.shape
    return pl.pallas_call(
        functools.partial(_combine_kernel, final_norm=final_norm),
        grid=(nt,),
        in_specs=[
            _row_index_spec(tm), _row_index_spec(tm),
            _row_index_spec(tm, ahead=1, last=nt - 1), _row_index_spec(tm, ahead=1, last=nt - 1),
            pl.BlockSpec((tm, 2), lambda i: (i, 0)),
            pl.BlockSpec((tm, d), lambda i: (i, 0)),
            pl.BlockSpec((1, d), lambda i: (0, 0)),
            pl.BlockSpec(memory_space=pl.ANY),
        ],
        out_specs=pl.BlockSpec((tm, d), lambda i: (i, 0)),
        out_shape=jax.ShapeDtypeStruct((t, d), F32),
        scratch_shapes=[pltpu.VMEM((2, 2, tm, s, lanes), ybuf.dtype),
                        pltpu.SemaphoreType.DMA((2,))],
        compiler_params=_params(("arbitrary",)),
        name="moe_combine",
    )(dest0, dest1, dest0, dest1, gates_t, x, g_final.reshape(1, d), ybuf)


def hier_moe_block(layer, x1, h, lt, w_gate, w_up, w_down, g_final, final_norm, tm=MOE_TOKENS):
    t, d = x1.shape
    bm = MOE_ROWS
    eid, gate, rank, cnt = route(lt)
    counts = cnt[:, 0]
    padded = ((counts + bm - 1) // bm) * bm
    pend = jnp.cumsum(padded)
    pstart = pend - padded
    experts_col = jnp.arange(MOE_EXPERTS, dtype=I32)[:, None, None]
    dest = rank + jnp.sum(jnp.where(eid[None] == experts_col, pstart[:, None, None], 0), axis=0)
    n_rows = 2 * t + MOE_EXPERTS * bm
    nblk = n_rows // bm
    n_used = (pend[-1] // bm).astype(I32)
    first_row = jnp.minimum(jnp.arange(nblk, dtype=I32), n_used - 1) * bm
    block_expert = jnp.sum(pend[None, :] <= first_row[:, None], axis=1).astype(I32)
    block_expert = jnp.minimum(block_expert, MOE_EXPERTS - 1)
    ids = jnp.arange(MOE_EXPERTS, dtype=I32)
    later = jnp.logical_and(ids[None, :] > block_expert[:, None], counts[None, :] > 0)
    next_expert = jnp.min(jnp.where(later, ids[None, :], MOE_EXPERTS), axis=1)
    next_expert = jnp.where(next_expert < MOE_EXPERTS, next_expert, -1).astype(I32)
    dest0 = dest[0].reshape(t // tm, 1, tm)
    dest1 = dest[1].reshape(t // tm, 1, tm)
    pad = jnp.stack([pstart + counts, padded - counts]).astype(I32)
    xbuf = dispatch(pad, n_used.reshape(1), dest0, dest1, h, n_rows)
    ybuf = experts(layer, block_expert, n_used.reshape(1), next_expert, xbuf, w_gate, w_up, w_down)
    return combine(dest0, dest1, jnp.transpose(gate), x1, ybuf, g_final, final_norm)


def _attn_kernel(sink_ref, q_ref, kp_ref, kc_ref, kn_ref, vp_ref, vc_ref, vn_ref, bias_ref, o_ref):
    n = pl.program_id(1)
    nblk = pl.num_programs(1)
    blk = q_ref.shape[0]
    dh = SWA_HEAD_DIM
    nkv = kc_ref.shape[1] // dh
    masked = bias_ref.shape[1] - 1
    part_prev = jnp.where(n > 0, 0, masked)
    part_next = jnp.where(n < nblk - 1, 2, masked)
    lo = lax.broadcasted_iota(I32, (blk, 2 * dh), 1) < dh
    top = lax.broadcasted_iota(I32, (2 * blk, 2 * dh), 0) < blk
    lo2 = lax.broadcasted_iota(I32, (2 * blk, 2 * dh), 1) < dh
    ones_ext = jnp.concatenate([jnp.where(lo, 1.0, 0.0), jnp.where(lo, 0.0, 1.0)], axis=0).astype(BF16)
    qscale = (dh ** -0.5) * LOG2E

    for kv in range(nkv):
        col = slice((kv // 2) * 2 * dh, (kv // 2 + 1) * 2 * dh)

        def extend(ref):
            x = ref[:, col].astype(F32)
            r = pltpu.roll(x, dh, 1)
            x_lo, x_hi = (x, r) if kv % 2 == 0 else (r, x)
            return jnp.concatenate([jnp.where(lo, x_lo, 0.0), jnp.where(lo, 0.0, x_hi)],
                                   axis=0).astype(BF16)

        q2 = jnp.concatenate([q_ref[:, (2 * kv) * 2 * dh:(2 * kv + 1) * 2 * dh],
                              q_ref[:, (2 * kv + 1) * 2 * dh:(2 * kv + 2) * 2 * dh]], axis=0)
        q2 = (q2.astype(F32) * qscale).astype(BF16)

        def scores(k_ref, part):
            return lax.dot_general(q2, extend(k_ref), NT_DIMS,
                                   preferred_element_type=F32) + bias_ref[kv, part]

        s = [scores(kp_ref, part_prev), scores(kc_ref, 1), scores(kn_ref, part_next)]
        mx = jnp.maximum(jnp.maximum(s[0], s[1]), s[2])
        sk = [jnp.where(top, sink_ref[0, 4 * kv + par], sink_ref[0, 4 * kv + 2 + par])
              for par in range(2)]
        m = [jnp.maximum(jnp.broadcast_to(jnp.max(mx[:, par * blk:(par + 1) * blk], axis=-1,
                                                  keepdims=True), (2 * blk, 2 * dh)), sk[par])
             for par in range(2)]
        acc = jnp.zeros((2 * blk, 4 * dh), F32)
        for sp, v_ref in zip(s, (vp_ref, vc_ref, vn_ref)):
            e = jnp.concatenate([jnp.exp2(sp[:, :blk] - m[0]), jnp.exp2(sp[:, blk:] - m[1])],
                                axis=1).astype(BF16)
            rhs = jnp.concatenate([extend(v_ref), ones_ext], axis=1)
            acc = acc + jnp.dot(e, rhs, preferred_element_type=F32)
        den = acc[:, 2 * dh:] + jnp.exp2(jnp.where(lo2, sk[0] - m[0], sk[1] - m[1]))
        o = (acc[:, :2 * dh] / den).astype(BF16)
        o_ref[:, (2 * kv) * 2 * dh:(2 * kv + 1) * 2 * dh] = o[:blk]
        o_ref[:, (2 * kv + 1) * 2 * dh:(2 * kv + 2) * 2 * dh] = o[blk:]


def banded_attention(qkv3, sink, bias):
    b, seq, width = qkv3.shape
    nh = sink.shape[1]
    qd = nh * SWA_HEAD_DIM
    kvd = (width - qd) // 2
    blk = SWA_BLOCK
    nblk = seq // blk
    kcol, vcol = qd // kvd, qd // kvd + 1

    def band(col, off):
        return pl.BlockSpec((None, blk, kvd),
                            lambda bi, n: (bi, jnp.clip(n + off, 0, nblk - 1), col))

    return pl.pallas_call(
        _attn_kernel,
        grid=(b, nblk),
        in_specs=[
            pl.BlockSpec(memory_space=pltpu.SMEM),
            pl.BlockSpec((None, blk, qd), lambda bi, n: (bi, n, 0)),
            band(kcol, -1), band(kcol, 0), band(kcol, 1),
            band(vcol, -1), band(vcol, 0), band(vcol, 1),
            pl.BlockSpec(bias.shape, lambda bi, n: (0, 0, 0, 0)),
        ],
        out_specs=pl.BlockSpec((None, blk, qd), lambda bi, n: (bi, n, 0)),
        out_shape=jax.ShapeDtypeStruct((b, seq, qd), BF16),
        compiler_params=_params(("parallel", "parallel")),
        name="banded_attention",
    )(sink, qkv3, qkv3, qkv3, qkv3, qkv3, qkv3, qkv3, bias)


def _t5_bucket(rel):
    half = REL_BUCKETS // 2
    max_exact = half // 2
    n = jnp.abs(rel)
    large = max_exact + (jnp.log(jnp.maximum(n, 1).astype(F32) / max_exact)
                         / math.log(REL_MAX_DIST / max_exact) * (half - max_exact)).astype(I32)
    large = jnp.minimum(large, half - 1)
    return jnp.where(rel > 0, half, 0) + jnp.where(n < max_exact, n, large)


def _attention_bias(rel_bias):
    blk = SWA_BLOCK
    nh = rel_bias.shape[1]
    nkv = nh // SWA_Q_PER_KV
    rel = jnp.arange(3 * blk)[None, :] - blk - jnp.arange(blk)[:, None]
    onehot = (_t5_bucket(rel)[..., None] == jnp.arange(REL_BUCKETS)).astype(F32)
    bias = jnp.einsum('ijb,bh->hij', onehot, rel_bias.astype(F32), precision=lax.Precision.HIGHEST)
    bias = jnp.where((jnp.abs(rel) <= SWA_WINDOW)[None], bias, NEG_INF) * LOG2E
    tiles = bias.reshape(nkv, 2, 2, blk, 3, blk).transpose(0, 4, 1, 3, 2, 5)
    tiles = tiles.reshape(nkv, 3, 2 * blk, 2 * blk)
    masked = jnp.full((nkv, 1, 2 * blk, 2 * blk), NEG_INF * LOG2E, F32)
    return jnp.concatenate([tiles, masked], axis=1)


def _rotary_tables(seq, dim):
    inv_freq = ROPE_BASE ** (-jnp.arange(0, dim, 2, dtype=F32) / dim)
    ang = jnp.arange(seq, dtype=F32)[:, None] * inv_freq[None, :]
    cos, sin = jnp.cos(ang), jnp.sin(ang)
    return jnp.concatenate([cos, cos], axis=1), jnp.concatenate([-sin, sin], axis=1)


def kernel(x, norm_mix_g, norm_ffn_g, norm_final_g, hyb_w_in, ret_decay_logit, s5_a_re, s5_a_im, s5_log_step, s5_b_re, s5_b_im, s5_c_re, s5_c_im, s5_d, s5_w_glu, s5_b_glu, hyb_w_out, swa_w_qkv, swa_sink, swa_w_o, rel_bias, moe_w_group, moe_b_group, moe_w_expert_router, moe_b_expert_router, moe_w_gate, moe_w_up, moe_w_down):
    nb, seq, d = x.shape
    t = nb * seq
    depth = norm_mix_g.shape[0]
    xt = x.reshape(t, d)
    for layer in range(depth):
        i = layer // 2
        if layer % 2 == 0:
            w = hyb_w_out.shape[1] // 2
            z = norm_matmul(xt, norm_mix_g[layer], hyb_w_in[i])
            z3 = z.reshape(nb, seq, z.shape[1])
            cos, sin = _rotary_tables(seq, w // RET_HEADS)
            log_gamma = jax.nn.log_sigmoid(ret_decay_logit[i].astype(F32))
            y_ret = retention(z3, log_gamma, cos, sin).reshape(t, w)
            lam, bmat, cmat = _s5_discretize(s5_a_re[i], s5_a_im[i], s5_log_step[i], s5_b_re[i],
                                             s5_b_im[i], s5_c_re[i], s5_c_im[i])
            ys5 = s5_scan(z3, lam, bmat, cmat).reshape(2, t, w)
            mix_in = glu_concat(y_ret, ys5, z, s5_d[i].astype(F32).reshape(-1), s5_w_glu[i],
                                s5_b_glu[i].astype(F32), nb)
            w_mix = hyb_w_out[i]
        else:
            qkv = norm_matmul(xt, norm_mix_g[layer], swa_w_qkv[i])
            mix_in = banded_attention(qkv.reshape(nb, seq, qkv.shape[1]),
                                      swa_sink[i].astype(F32).reshape(1, -1) * LOG2E,
                                      _attention_bias(rel_bias)).reshape(t, -1)
            w_mix = swa_w_o[i]
        r_hi, r_lo, r_bias = _router_operands(moe_w_group[layer], moe_b_group[layer],
                                              moe_w_expert_router[layer], moe_b_expert_router[layer])
        x1, h, lt = proj_norm_router(mix_in, w_mix, xt, norm_ffn_g[layer], r_hi, r_lo, r_bias)
        last = layer == depth - 1
        xt = hier_moe_block(layer, x1, h, lt, moe_w_gate, moe_w_up, moe_w_down,
                            norm_final_g, final_norm=last)
    return xt.reshape(nb, seq, d)
```

```python
import functools
import math

import jax
import jax.numpy as jnp
from jax import lax
from jax.experimental import pallas as pl
from jax.experimental.pallas import tpu as pltpu

F32 = jnp.float32
BF16 = jnp.bfloat16
I32 = jnp.int32

RET_HEADS = 4
RET_CHUNK = 128
S5_GROUP_CH = 16
S5_STATE = 64
SWA_HEAD_DIM = 64
SWA_Q_PER_KV = 4
SWA_WINDOW = 128
SWA_BLOCK = 128
REL_BUCKETS = 32
REL_MAX_DIST = 128
MOE_GROUPS = 4
MOE_EXPERTS_PER_GROUP = 8
MOE_EXPERTS = MOE_GROUPS * MOE_EXPERTS_PER_GROUP
ROPE_BASE = 10000.0
RMS_EPS = 1e-6
GN_EPS = 1e-5
NEG_INF = -1e30
LOG2E = 1.4426950408889634

LANES = 128
SUBLANES = 8
V7X_VMEM_BYTES = 64 * 1024 * 1024
VMEM_LIMIT = V7X_VMEM_BYTES - 8 * 1024 * 1024

DENSE_ROWS = 1024
MOE_ROWS = 512
MOE_TOKENS = 512
S5_CHUNK = 128
S5_COLS = 512
S5_SCAN_COLS = 1024
S5_KBLK = 128
ROUTER_ROWS = 128
RET_UNROLL = 8

NT_DIMS = (((1,), (1,)), ((), ()))
TN_DIMS = (((0,), (0,)), ((), ()))


def _params(semantics):
    return pltpu.CompilerParams(dimension_semantics=semantics, vmem_limit_bytes=VMEM_LIMIT)


def _rms(x, g):
    ms = jnp.mean(x * x, axis=-1, keepdims=True)
    return (x * lax.rsqrt(ms + RMS_EPS)) * g


def _norm_matmul_kernel(x_ref, g_ref, w_ref, o_ref):
    h = _rms(x_ref[...], g_ref[...])
    o_ref[...] = jnp.dot(h.astype(BF16), w_ref[...],
                         preferred_element_type=F32).astype(o_ref.dtype)


def norm_matmul(x, g, w, tm=DENSE_ROWS):
    t, d = x.shape
    tm = min(tm, t)
    n = w.shape[1]
    return pl.pallas_call(
        _norm_matmul_kernel,
        grid=(t // tm,),
        in_specs=[
            pl.BlockSpec((tm, d), lambda i: (i, 0)),
            pl.BlockSpec((1, d), lambda i: (0, 0)),
            pl.BlockSpec((d, n), lambda i: (0, 0)),
        ],
        out_specs=pl.BlockSpec((tm, n), lambda i: (i, 0)),
        out_shape=jax.ShapeDtypeStruct((t, n), BF16),
        compiler_params=_params(("parallel",)),
        name="norm_matmul",
    )(x, g.reshape(1, d), w.astype(BF16))


def _retention_kernel(lg_ref, q_ref, k_ref, v_ref, g_ref, cos_ref, sin_ref, o_ref,
                      qr_ref, kr_ref, inc_ref, st_ref, lhs_ref):
    h = pl.program_id(1)
    lg_f = lg_ref[0, h]
    lg_b = lg_ref[1, h]
    seq, dk = q_ref.shape
    c = RET_CHUNK
    nc = seq // c

    cos = cos_ref[...]
    sin = sin_ref[...]
    swap = (lax.broadcasted_iota(I32, (dk, dk), 0)
            == (lax.broadcasted_iota(I32, (dk, dk), 1) + dk // 2) % dk)
    swap = jnp.where(swap, 1.0, 0.0).astype(BF16)
    q = q_ref[...]
    qr_ref[...] = (q.astype(F32) * cos
                   + jnp.dot(q, swap, preferred_element_type=F32) * sin)
    k = k_ref[...]
    kr_ref[...] = (k.astype(F32) * cos
                   + jnp.dot(k, swap, preferred_element_type=F32) * sin) * (dk ** -0.5)

    pos = lax.broadcasted_iota(I32, (c, dk), 0).astype(F32)
    kf_scale = jnp.exp((c - 1.0 - pos) * lg_f)
    qf_scale = jnp.exp((pos + 1.0) * lg_f)
    kb_scale = jnp.exp(pos * lg_b)
    qb_scale = jnp.exp((c - pos) * lg_b)
    rel = (lax.broadcasted_iota(I32, (c, c), 0) - lax.broadcasted_iota(I32, (c, c), 1)).astype(F32)
    mask = jnp.exp(jnp.abs(rel) * jnp.where(rel >= 0, lg_f, lg_b))
    dec_f = jnp.exp(jnp.full((dk, dk), c * lg_f, F32))
    dec_b = jnp.exp(jnp.full((dk, dk), c * lg_b, F32))

    def increments(n, carry):
        rows = pl.ds(pl.multiple_of(n * c, c), c)
        kc = kr_ref[rows, :]
        kk = jnp.concatenate([kc * kf_scale, kc * kb_scale], axis=1).astype(BF16)
        inc_ref[n] = lax.dot_general(kk, v_ref[rows, :], TN_DIMS,
                                     preferred_element_type=F32)
        return carry

    lax.fori_loop(0, nc, increments, 0, unroll=min(RET_UNROLL, nc))

    def fwd(n, state):
        st_ref[n, :dk, :] = state.astype(BF16)
        return state * dec_f + inc_ref[n, :dk, :]

    lax.fori_loop(0, nc, fwd, jnp.zeros((dk, dk), F32))

    def bwd(i, state):
        n = nc - 1 - i
        st_ref[n, dk:, :] = state.astype(BF16)
        return state * dec_b + inc_ref[n, dk:, :]

    lax.fori_loop(0, nc, bwd, jnp.zeros((dk, dk), F32))

    def operands(n, carry):
        rows = pl.ds(pl.multiple_of(n * c, c), c)
        qc = qr_ref[rows, :]
        s = lax.dot_general(qc.astype(BF16), kr_ref[rows, :].astype(BF16), NT_DIMS,
                            preferred_element_type=F32) * mask
        lhs_ref[n] = jnp.concatenate([s, qc * qf_scale, qc * qb_scale], axis=1).astype(BF16)
        return carry

    lax.fori_loop(0, nc, operands, 0, unroll=min(RET_UNROLL, nc))

    def outputs(n, carry):
        rows = pl.ds(pl.multiple_of(n * c, c), c)
        rhs = jnp.concatenate([v_ref[rows, :], st_ref[n]], axis=0)
        kr_ref[rows, :] = jnp.dot(lhs_ref[n], rhs, preferred_element_type=F32)
        return carry

    lax.fori_loop(0, nc, outputs, 0, unroll=min(RET_UNROLL, nc))

    out = kr_ref[...]
    mu = jnp.mean(out, axis=-1, keepdims=True)
    cen = out - mu
    var = jnp.mean(cen * cen, axis=-1, keepdims=True)
    g = g_ref[...].astype(F32)
    o_ref[...] = ((g * jax.nn.sigmoid(g)) * (cen * lax.rsqrt(var + GN_EPS))).astype(BF16)


def retention(z3, log_gamma, cos, sin):
    b, seq, _ = z3.shape
    nh = RET_HEADS
    dk = cos.shape[1]

    def col(off):
        return pl.BlockSpec((None, seq, dk), lambda bi, hi: (bi, 0, off + hi))

    return pl.pallas_call(
        _retention_kernel,
        grid=(b, nh),
        in_specs=[
            pl.BlockSpec(memory_space=pltpu.SMEM),
            col(0), col(nh), col(2 * nh), col(3 * nh),
            pl.BlockSpec((seq, dk), lambda bi, hi: (0, 0)),
            pl.BlockSpec((seq, dk), lambda bi, hi: (0, 0)),
        ],
        out_specs=pl.BlockSpec((None, seq, dk), lambda bi, hi: (bi, 0, hi)),
        out_shape=jax.ShapeDtypeStruct((b, seq, nh * dk), BF16),
        scratch_shapes=[pltpu.VMEM((seq, dk), F32), pltpu.VMEM((seq, dk), F32),
                        pltpu.VMEM((seq // RET_CHUNK, 2 * dk, dk), F32),
                        pltpu.VMEM((seq // RET_CHUNK, 2 * dk, dk), BF16),
                        pltpu.VMEM((seq // RET_CHUNK, RET_CHUNK, 3 * dk), BF16)],
        compiler_params=_params(("parallel", "parallel")),
        name="retention",
    )(log_gamma, z3, z3, z3, z3, cos, sin)


def _s5_kernel(u_ref, lam_ref, b_ref, c_ref, y_ref, up_ref, xre_ref, xim_ref, sre_ref, sim_ref):
    d = pl.program_id(0)
    n = pl.program_id(1)
    nb, cn, width = u_ref.shape
    srows = up_ref.shape[1]
    nk = width // S5_KBLK

    @pl.when(jnp.logical_and(d == 0, n == 0))
    def _():
        up_ref[...] = jnp.zeros_like(up_ref)

    @pl.when(n == 0)
    def _():
        sre_ref[...] = jnp.zeros_like(sre_ref)
        sim_ref[...] = jnp.zeros_like(sim_ref)

    for b in range(nb):
        up_ref[b, b:b + cn, :] = u_ref[b].astype(F32)
    u = up_ref[...].reshape(nb * srows, width).astype(BF16)
    ppc = S5_COLS // LANES
    for kb in range(nk):
        bu = jnp.dot(u[:, kb * S5_KBLK:(kb + 1) * S5_KBLK], b_ref[kb], preferred_element_type=F32)
        for j in range(ppc):
            xre_ref[kb * ppc + j] = bu[:, j * LANES:(j + 1) * LANES]
            xim_ref[kb * ppc + j] = bu[:, S5_COLS + j * LANES:S5_COLS + (j + 1) * LANES]

    pps = S5_SCAN_COLS // LANES
    for cb in range(sre_ref.shape[1] // S5_SCAN_COLS):
        cols = slice(cb * S5_SCAN_COLS, (cb + 1) * S5_SCAN_COLS)
        lr = jnp.broadcast_to(lam_ref[0:1, cols], (nb, S5_SCAN_COLS))
        li = jnp.broadcast_to(lam_ref[1:2, cols], (nb, S5_SCAN_COLS))

        def step(i, carry):
            xr, xi = carry
            t = i + d * (cn - 1 - 2 * i)
            rows = pl.ds(t, nb, stride=srows + 1)
            bur = jnp.concatenate([xre_ref[cb * pps + j, rows, :] for j in range(pps)], axis=1)
            bui = jnp.concatenate([xim_ref[cb * pps + j, rows, :] for j in range(pps)], axis=1)
            nxr = lr * xr - li * xi + bur
            nxi = lr * xi + li * xr + bui
            for j in range(pps):
                xre_ref[cb * pps + j, rows, :] = nxr[:, j * LANES:(j + 1) * LANES]
                xim_ref[cb * pps + j, rows, :] = nxi[:, j * LANES:(j + 1) * LANES]
            return nxr, nxi

        xr, xi = lax.fori_loop(0, cn, step, (sre_ref[:, cols], sim_ref[:, cols]), unroll=8)
        sre_ref[:, cols] = xr
        sim_ref[:, cols] = xi

    for kb in range(nk):
        xr = jnp.concatenate([xre_ref[kb * ppc + j] for j in range(ppc)], axis=1).astype(BF16)
        xi = jnp.concatenate([xim_ref[kb * ppc + j] for j in range(ppc)], axis=1).astype(BF16)
        y = jnp.dot(xr, c_ref[kb, :S5_COLS, :], preferred_element_type=F32)
        y = y + jnp.dot(xi, c_ref[kb, S5_COLS:, :], preferred_element_type=F32)
        y = y.reshape(nb, srows, S5_KBLK)
        for b in range(nb):
            y_ref[b, :, kb * S5_KBLK:(kb + 1) * S5_KBLK] = y[b, b:b + cn, :]


def s5_scan(z3, lam, bmat, cmat):
    nb, seq, zw = z3.shape
    width = bmat.shape[1] * bmat.shape[2]
    nstate = lam.shape[2]
    cn = min(S5_CHUNK, seq)
    nch = seq // cn
    srows = cn + SUBLANES
    ucol = zw // width - 1

    def chunk(d, n):
        return n + d * (nch - 1 - 2 * n)

    return pl.pallas_call(
        _s5_kernel,
        grid=(2, nch),
        in_specs=[
            pl.BlockSpec((nb, cn, width), lambda d, n: (0, chunk(d, n), ucol)),
            pl.BlockSpec((None, 2, nstate), lambda d, n: (d, 0, 0)),
            pl.BlockSpec((None,) + bmat.shape[1:], lambda d, n: (d, 0, 0, 0)),
            pl.BlockSpec((None,) + cmat.shape[1:], lambda d, n: (d, 0, 0, 0)),
        ],
        out_specs=pl.BlockSpec((None, nb, cn, width), lambda d, n: (d, 0, chunk(d, n), 0)),
        out_shape=jax.ShapeDtypeStruct((2, nb, seq, width), F32),
        scratch_shapes=[
            pltpu.VMEM((nb, srows, width), F32),
            pltpu.VMEM((nstate // LANES, nb * srows, LANES), F32),
            pltpu.VMEM((nstate // LANES, nb * srows, LANES), F32),
            pltpu.VMEM((nb, nstate), F32),
            pltpu.VMEM((nb, nstate), F32),
        ],
        compiler_params=_params(("arbitrary", "arbitrary")),
        name="s5_scan",
    )(z3, lam, bmat, cmat)


def _s5_discretize(a_re, a_im, log_step, b_re, b_im, c_re, c_im):
    ng, npst = a_re.shape[1], a_re.shape[2]
    gpb = S5_KBLK // S5_GROUP_CH
    nk = ng // gpb
    eye = jnp.eye(gpb, dtype=F32)
    bre, bim = b_re.astype(F32), b_im.astype(F32)
    lams, bmats, cmats = [], [], []
    for direction in range(2):
        ar = a_re[direction].astype(F32)
        ai = a_im[direction].astype(F32)
        dt = jnp.exp(log_step[direction].astype(F32))[:, None]
        mag = jnp.exp(ar * dt)
        lam_re, lam_im = mag * jnp.cos(ai * dt), mag * jnp.sin(ai * dt)
        nr, ni = lam_re - 1.0, lam_im
        den = ar * ar + ai * ai
        coef_re = (nr * ar + ni * ai) / den
        coef_im = (ni * ar - nr * ai) / den
        bbar_re = coef_re[..., None] * bre - coef_im[..., None] * bim
        bbar_im = coef_re[..., None] * bim + coef_im[..., None] * bre

        def in_blocks(m):
            m4 = m.reshape(nk, gpb, npst, S5_GROUP_CH)
            return jnp.einsum('kgpc,gh->kgchp', m4, eye).reshape(nk, S5_KBLK, gpb * npst)

        def out_blocks(m):
            m4 = m.reshape(nk, gpb, S5_GROUP_CH, npst)
            return jnp.einsum('kgcp,gh->kgphc', m4, eye).reshape(nk, gpb * npst, S5_KBLK)

        lams.append(jnp.stack([lam_re.reshape(-1), lam_im.reshape(-1)]))
        bmats.append(jnp.concatenate([in_blocks(bbar_re), in_blocks(bbar_im)], axis=2))
        cmats.append(jnp.concatenate([out_blocks(c_re[direction].astype(F32)),
                                      -out_blocks(c_im[direction].astype(F32))], axis=1))
    return jnp.stack(lams), jnp.stack(bmats).astype(BF16), jnp.stack(cmats).astype(BF16)


def _s5_glu(yf_ref, yb_ref, u_ref, d_ref, w_ref, b_ref):
    y = u_ref[...].astype(F32) * d_ref[...] + yf_ref[...] + yb_ref[...]
    y = jax.nn.gelu(y)
    gate = jax.nn.sigmoid(jnp.dot(y.astype(BF16), w_ref[...], preferred_element_type=F32) + b_ref[...])
    return (y * gate).astype(BF16)


def _proj_kernel(*refs, s5_glu):
    if s5_glu:
        yr_ref, yf_ref, yb_ref, u_ref, d_ref, wg_ref, bg_ref = refs[:7]
        w_ref, x_ref, g_ref, rh_ref, rl_ref, rb_ref, x1_ref, h_ref, lt_ref = refs[7:]
        half = yr_ref.shape[1]
        mix = jnp.dot(yr_ref[...], w_ref[:half, :], preferred_element_type=F32)
        mix = mix + jnp.dot(_s5_glu(yf_ref, yb_ref, u_ref, d_ref, wg_ref, bg_ref), w_ref[half:, :],
                            preferred_element_type=F32)
    else:
        a_ref, w_ref, x_ref, g_ref, rh_ref, rl_ref, rb_ref, x1_ref, h_ref, lt_ref = refs
        mix = jnp.dot(a_ref[...], w_ref[...], preferred_element_type=F32)
    x1 = x_ref[...] + mix
    x1_ref[...] = x1
    h = _rms(x1, g_ref[...])
    h_hi = h.astype(BF16)
    h_ref[...] = h_hi.reshape(h_ref.shape)
    h_lo = (h - h_hi.astype(F32)).astype(BF16)
    lt = lax.dot_general(rh_ref[...], h_hi, NT_DIMS, preferred_element_type=F32)
    lt = lt + lax.dot_general(rh_ref[...], h_lo, NT_DIMS, preferred_element_type=F32)
    lt = lt + lax.dot_general(rl_ref[...], h_hi, NT_DIMS, preferred_element_type=F32)
    lt_ref[...] = lt + rb_ref[...]


def proj_norm_router(mix_in, w, x, g, r_hi, r_lo, r_bias, tm=DENSE_ROWS):
    t, d = x.shape
    tm = min(tm, t)
    k = w.shape[0]
    nr = r_hi.shape[0]
    s5_glu = isinstance(mix_in, tuple)
    if s5_glu:
        y_ret, ys5, z, d_skip, w_glu, b_glu = mix_in
        half = y_ret.shape[1]
        ucol = z.shape[1] // half - 1
        lead_specs = [
            pl.BlockSpec((tm, half), lambda i: (i, 0)),
            pl.BlockSpec((None, tm, half), lambda i: (0, i, 0)),
            pl.BlockSpec((None, tm, half), lambda i: (1, i, 0)),
            pl.BlockSpec((tm, half), lambda i: (i, ucol)),
            pl.BlockSpec((1, half), lambda i: (0, 0)),
            pl.BlockSpec((half, half), lambda i: (0, 0)),
            pl.BlockSpec((1, half), lambda i: (0, 0)),
        ]
        lead_args = (y_ret, ys5, ys5, z, d_skip.reshape(1, half), w_glu.astype(BF16),
                     b_glu.reshape(1, half))
    else:
        lead_specs = [pl.BlockSpec((tm, k), lambda i: (i, 0))]
        lead_args = (mix_in,)
    return pl.pallas_call(
        functools.partial(_proj_kernel, s5_glu=s5_glu),
        grid=(t // tm,),
        in_specs=lead_specs + [
            pl.BlockSpec((k, d), lambda i: (0, 0)),
            pl.BlockSpec((tm, d), lambda i: (i, 0)),
            pl.BlockSpec((1, d), lambda i: (0, 0)),
            pl.BlockSpec((nr, d), lambda i: (0, 0)),
            pl.BlockSpec((nr, d), lambda i: (0, 0)),
            pl.BlockSpec((nr, 1), lambda i: (0, 0)),
        ],
        out_specs=[
            pl.BlockSpec((tm, d), lambda i: (i, 0)),
            pl.BlockSpec((tm, d // LANES, LANES), lambda i: (i, 0, 0)),
            pl.BlockSpec((nr, tm), lambda i: (0, i)),
        ],
        out_shape=[
            jax.ShapeDtypeStruct((t, d), F32),
            jax.ShapeDtypeStruct((t, d // LANES, LANES), BF16),
            jax.ShapeDtypeStruct((nr, t), F32),
        ],
        compiler_params=_params(("parallel",)),
        name="proj_norm_router",
    )(*lead_args, w.astype(BF16), x, g.reshape(1, d), r_hi, r_lo, r_bias)


def _router_operands(w_group, b_group, w_er, b_er):
    d = w_group.shape[0]
    wt = jnp.concatenate([
        jnp.transpose(w_er.astype(F32), (0, 2, 1)).reshape(MOE_EXPERTS, d),
        jnp.transpose(w_group.astype(F32)),
        jnp.zeros((ROUTER_ROWS - MOE_EXPERTS - MOE_GROUPS, d), F32)], axis=0)
    bias = jnp.concatenate([
        b_er.astype(F32).reshape(-1), b_group.astype(F32),
        jnp.zeros((ROUTER_ROWS - MOE_EXPERTS - MOE_GROUPS,), F32)]).reshape(ROUTER_ROWS, 1)
    hi = wt.astype(BF16)
    lo = (wt - hi.astype(F32)).astype(BF16)
    return hi, lo, bias


def _route_kernel(lt_ref, eid_ref, gate_ref, rank_ref, cnt_ref, run_ref):
    i = pl.program_id(0)
    tm = lt_ref.shape[1]
    ne, npg, ng = MOE_EXPERTS, MOE_EXPERTS_PER_GROUP, MOE_GROUPS

    @pl.when(i == 0)
    def _():
        run_ref[...] = jnp.zeros_like(run_ref)

    gl = lt_ref[ne:ne + ng, :]
    gmax = jnp.max(gl, axis=0, keepdims=True)
    gidx = lax.broadcasted_iota(I32, (ng, tm), 0)
    gsel = jnp.min(jnp.where(gl == gmax, gidx, ng), axis=0, keepdims=True)
    p_g = 1.0 / jnp.sum(jnp.exp(gl - gmax), axis=0, keepdims=True)

    e8 = lt_ref[(ng - 1) * npg:ng * npg, :]
    for g in range(ng - 2, -1, -1):
        e8 = jnp.where(gsel == g, lt_ref[g * npg:(g + 1) * npg, :], e8)
    eidx = lax.broadcasted_iota(I32, (npg, tm), 0)
    m1 = jnp.max(e8, axis=0, keepdims=True)
    i1 = jnp.min(jnp.where(e8 == m1, eidx, npg), axis=0, keepdims=True)
    e8b = jnp.where(eidx == i1, -jnp.inf, e8)
    m2 = jnp.max(e8b, axis=0, keepdims=True)
    i2 = jnp.min(jnp.where(e8b == m2, eidx, npg), axis=0, keepdims=True)
    t2 = jnp.exp(m2 - m1)
    den = 1.0 + t2
    gate_ref[0:1, :] = (1.0 / den) * p_g
    gate_ref[1:2, :] = (t2 / den) * p_g
    id1 = gsel * npg + i1
    id2 = gsel * npg + i2
    eid_ref[0:1, :] = id1
    eid_ref[1:2, :] = id2

    rows = lax.broadcasted_iota(I32, (ne, tm), 0)
    oh1 = rows == id1
    oh2 = rows == id2
    both = jnp.where(oh1, 1.0, 0.0) + jnp.where(oh2, 1.0, 0.0)
    earlier = (lax.broadcasted_iota(I32, (tm, tm), 0) < lax.broadcasted_iota(I32, (tm, tm), 1))
    prefix = jnp.dot(both.astype(BF16), jnp.where(earlier, 1.0, 0.0).astype(BF16),
                     preferred_element_type=F32)
    base = prefix + run_ref[:, 0:1]
    rank_ref[0:1, :] = jnp.sum(jnp.where(oh1, base, 0.0), axis=0, keepdims=True).astype(I32)
    rank_ref[1:2, :] = jnp.sum(jnp.where(oh2, base, 0.0), axis=0, keepdims=True).astype(I32)
    run = run_ref[...] + jnp.sum(both, axis=1, keepdims=True)
    run_ref[...] = run
    cnt_ref[...] = run.astype(I32)


def route(lt, tm=512):
    nr, t = lt.shape
    two = lambda dt: jax.ShapeDtypeStruct((2, t), dt)
    return pl.pallas_call(
        _route_kernel,
        grid=(t // tm,),
        in_specs=[pl.BlockSpec((nr, tm), lambda i: (0, i))],
        out_specs=[
            pl.BlockSpec((2, tm), lambda i: (0, i)),
            pl.BlockSpec((2, tm), lambda i: (0, i)),
            pl.BlockSpec((2, tm), lambda i: (0, i)),
            pl.BlockSpec((MOE_EXPERTS, LANES), lambda i: (0, 0)),
        ],
        out_shape=[two(I32), two(F32), two(I32),
                   jax.ShapeDtypeStruct((MOE_EXPERTS, LANES), I32)],
        scratch_shapes=[pltpu.VMEM((MOE_EXPERTS, LANES), F32)],
        compiler_params=_params(("arbitrary",)),
        name="route",
    )(lt)


def _dispatch_kernel(pad_ref, nu_ref, d0_ref, d1_ref, h_ref, xbuf_ref, zero_ref, sem, zsem):
    i = pl.program_id(0)
    tm = h_ref.shape[0]
    dests = (d0_ref, d1_ref)
    bm = zero_ref.shape[0]
    nblk = xbuf_ref.shape[0] // bm

    def zero_fill(act):
        def per_expert(e, carry):
            pos = pad_ref[0, e]
            length = pad_ref[1, e]
            p = bm // 2
            while p >= 1:
                bit = (length & p) != 0

                @pl.when(bit)
                def _(pos=pos, p=p):
                    act(pltpu.make_async_copy(zero_ref.at[pl.ds(0, p)],
                                              xbuf_ref.at[pl.ds(pos, p)], zsem))

                pos = pos + jnp.where(bit, p, 0)
                p //= 2
            return carry

        lax.fori_loop(0, pad_ref.shape[1], per_expert, 0)

        def per_block(b, carry):
            act(pltpu.make_async_copy(zero_ref, xbuf_ref.at[pl.ds(b * bm, bm)], zsem))
            return carry

        lax.fori_loop(nu_ref[0], nblk, per_block, 0)

    @pl.when(i == 0)
    def _():
        zero_ref[...] = jnp.zeros_like(zero_ref)
        zero_fill(lambda c: c.start())

    def copy(r, k):
        return pltpu.make_async_copy(h_ref.at[r], xbuf_ref.at[dests[k][0, r]], sem)

    def start(r, carry):
        copy(r, 0).start(priority=0)
        copy(r, 1).start(priority=1)
        return carry

    lax.fori_loop(0, tm, start, 0, unroll=8)

    def wait(r, carry):
        copy(r, 0).wait()
        copy(r, 1).wait()
        return carry

    lax.fori_loop(0, tm, wait, 0, unroll=8)

    @pl.when(i == pl.num_programs(0) - 1)
    def _():
        zero_fill(lambda c: c.wait())


def _row_index_spec(tm, ahead=0, last=None):
    def index(i):
        return (i if ahead == 0 else jnp.minimum(i + ahead, last), 0, 0)
    return pl.BlockSpec((None, 1, tm), index, memory_space=pltpu.SMEM)


def dispatch(pad, n_used, dest0, dest1, h3, n_rows):
    t, s, lanes = h3.shape
    nt, _, tm = dest0.shape
    return pl.pallas_call(
        _dispatch_kernel,
        grid=(nt,),
        in_specs=[
            pl.BlockSpec(memory_space=pltpu.SMEM),
            pl.BlockSpec(memory_space=pltpu.SMEM),
            _row_index_spec(tm), _row_index_spec(tm),
            pl.BlockSpec((tm, s, lanes), lambda i: (i, 0, 0)),
        ],
        out_specs=pl.BlockSpec(memory_space=pl.ANY),
        out_shape=jax.ShapeDtypeStruct((n_rows, s, lanes), h3.dtype),
        scratch_shapes=[pltpu.VMEM((MOE_ROWS, s, lanes), h3.dtype),
                        pltpu.SemaphoreType.DMA(()), pltpu.SemaphoreType.DMA(())],
        compiler_params=_params(("arbitrary",)),
        name="moe_dispatch",
    )(pad, n_used, dest0, dest1, h3)


def _experts_kernel(be_ref, nu_ref, nx_ref, x_ref, wg_hbm, wu_hbm, wd_hbm, o_ref,
                    wg_f32, wu_f32, wd_f32, wgb, wub, wdb, slot_ref, sem, *, layer):
    i = pl.program_id(0)
    e = be_ref[i]
    first = i == 0
    changed = jnp.logical_or(first, e != be_ref[jnp.maximum(i - 1, 0)])
    streams = ((wg_hbm, wg_f32), (wu_hbm, wu_f32), (wd_hbm, wd_f32))

    def fetch(expert, slot):
        return [pltpu.make_async_copy(w_hbm.at[layer, expert], w_f32.at[slot], sem.at[slot, j])
                for j, (w_hbm, w_f32) in enumerate(streams)]

    @pl.when(first)
    def _():
        slot_ref[0] = 1
        for c in fetch(e, 0):
            c.start()

    @pl.when(changed)
    def _():
        slot = 1 - slot_ref[0]
        slot_ref[0] = slot
        for c in fetch(e, slot):
            c.wait()
        nxt = nx_ref[i]

        @pl.when(nxt >= 0)
        def _():
            for c in fetch(nxt, 1 - slot):
                c.start()

        wgb[...] = wg_f32[slot].astype(BF16)
        wub[...] = wu_f32[slot].astype(BF16)
        wdb[...] = wd_f32[slot].astype(BF16)

    @pl.when(i < nu_ref[0])
    def _():
        bm, s, lanes = x_ref.shape
        x = x_ref[...].reshape(bm, s * lanes)
        g = jnp.dot(x, wgb[...], preferred_element_type=F32)
        u = jnp.dot(x, wub[...], preferred_element_type=F32)
        a = ((g * jax.nn.sigmoid(g)) * u).astype(BF16)
        y = jnp.dot(a, wdb[...], preferred_element_type=F32)
        o_ref[...] = y.astype(BF16).reshape(o_ref.shape)

    @pl.when(i >= nu_ref[0])
    def _():
        o_ref[...] = jnp.zeros_like(o_ref)


def experts(layer, block_expert, n_used, next_expert, xbuf, w_gate, w_up, w_down):
    n_rows, s, lanes = xbuf.shape
    d = s * lanes
    hid = w_gate.shape[3]
    bm = MOE_ROWS
    nblk = n_rows // bm
    grid_spec = pltpu.PrefetchScalarGridSpec(
        num_scalar_prefetch=3,
        grid=(nblk,),
        in_specs=[
            pl.BlockSpec((bm, s, lanes), lambda i, be, nu, nx: (jnp.minimum(i, nu[0] - 1), 0, 0)),
            pl.BlockSpec(memory_space=pl.ANY),
            pl.BlockSpec(memory_space=pl.ANY),
            pl.BlockSpec(memory_space=pl.ANY),
        ],
        out_specs=pl.BlockSpec((bm, s, lanes), lambda i, be, nu, nx: (i, 0, 0)),
        scratch_shapes=[pltpu.VMEM((2, d, hid), F32), pltpu.VMEM((2, d, hid), F32),
                        pltpu.VMEM((2, hid, d), F32),
                        pltpu.VMEM((d, hid), BF16), pltpu.VMEM((d, hid), BF16),
                        pltpu.VMEM((hid, d), BF16),
                        pltpu.SMEM((1,), I32), pltpu.SemaphoreType.DMA((2, 3))],
    )
    return pl.pallas_call(
        functools.partial(_experts_kernel, layer=layer),
        grid_spec=grid_spec,
        out_shape=jax.ShapeDtypeStruct((n_rows, s, lanes), BF16),
        compiler_params=_params(("arbitrary",)),
        name="moe_experts",
    )(block_expert, n_used, next_expert, xbuf, w_gate, w_up, w_down)


def _combine_kernel(d0_ref, d1_ref, n0_ref, n1_ref, gate_ref, x_ref, g_ref, ybuf_ref, o_ref,
                    buf, sem, *, final_norm):
    i = pl.program_id(0)
    tm, d = x_ref.shape
    slot = i % 2

    def copy(dests, s, r, k):
        return pltpu.make_async_copy(ybuf_ref.at[dests[k][0, r]], buf.at[s, k, r], sem.at[s])

    def gather(dests, s):
        def start(r, carry):
            copy(dests, s, r, 0).start(priority=0)
            copy(dests, s, r, 1).start(priority=1)
            return carry

        lax.fori_loop(0, tm, start, 0, unroll=8)

    @pl.when(i == 0)
    def _():
        gather((d0_ref, d1_ref), 0)

    @pl.when(i + 1 < pl.num_programs(0))
    def _():
        gather((n0_ref, n1_ref), 1 - slot)

    def wait(r, carry):
        copy((d0_ref, d1_ref), slot, r, 0).wait()
        copy((d0_ref, d1_ref), slot, r, 1).wait()
        return carry

    lax.fori_loop(0, tm, wait, 0, unroll=8)

    gates = gate_ref[...]
    y = (gates[:, 0:1] * buf[slot, 0].reshape(tm, d).astype(F32)
         + gates[:, 1:2] * buf[slot, 1].reshape(tm, d).astype(F32))
    out = x_ref[...] + y
    if final_norm:
        out = _rms(out, g_ref[...])
    o_ref[...] = out


def combine(dest0, dest1, gates_t, x, ybuf, g_final, final_norm):
    t, d = x.shape
    nt, _, tm = dest0.shape
    _, s, lanes = ybuf.shape
    return pl.pallas_call(
        functools.partial(_combine_kernel, final_norm=final_norm),
        grid=(nt,),
        in_specs=[
            _row_index_spec(tm), _row_index_spec(tm),
            _row_index_spec(tm, ahead=1, last=nt - 1), _row_index_spec(tm, ahead=1, last=nt - 1),
            pl.BlockSpec((tm, 2), lambda i: (i, 0)),
            pl.BlockSpec((tm, d), lambda i: (i, 0)),
            pl.BlockSpec((1, d), lambda i: (0, 0)),
            pl.BlockSpec(memory_space=pl.ANY),
        ],
        out_specs=pl.BlockSpec((tm, d), lambda i: (i, 0)),
        out_shape=jax.ShapeDtypeStruct((t, d), F32),
        scratch_shapes=[pltpu.VMEM((2, 2, tm, s, lanes), ybuf.dtype),
                        pltpu.SemaphoreType.DMA((2,))],
        compiler_params=_params(("arbitrary",)),
        name="moe_combine",
    )(dest0, dest1, dest0, dest1, gates_t, x, g_final.reshape(1, d), ybuf)


def hier_moe_block(layer, x1, h, lt, w_gate, w_up, w_down, g_final, final_norm, tm=MOE_TOKENS):
    t, d = x1.shape
    bm = MOE_ROWS
    eid, gate, rank, cnt = route(lt)
    counts = cnt[:, 0]
    padded = ((counts + bm - 1) // bm) * bm
    pend = jnp.cumsum(padded)
    pstart = pend - padded
    experts_col = jnp.arange(MOE_EXPERTS, dtype=I32)[:, None, None]
    dest = rank + jnp.sum(jnp.where(eid[None] == experts_col, pstart[:, None, None], 0), axis=0)
    n_rows = 2 * t + MOE_EXPERTS * bm
    nblk = n_rows // bm
    n_used = (pend[-1] // bm).astype(I32)
    first_row = jnp.minimum(jnp.arange(nblk, dtype=I32), n_used - 1) * bm
    block_expert = jnp.sum(pend[None, :] <= first_row[:, None], axis=1).astype(I32)
    block_expert = jnp.minimum(block_expert, MOE_EXPERTS - 1)
    ids = jnp.arange(MOE_EXPERTS, dtype=I32)
    later = jnp.logical_and(ids[None, :] > block_expert[:, None], counts[None, :] > 0)
    next_expert = jnp.min(jnp.where(later, ids[None, :], MOE_EXPERTS), axis=1)
    next_expert = jnp.where(next_expert < MOE_EXPERTS, next_expert, -1).astype(I32)
    dest0 = dest[0].reshape(t // tm, 1, tm)
    dest1 = dest[1].reshape(t // tm, 1, tm)
    pad = jnp.stack([pstart + counts, padded - counts]).astype(I32)
    xbuf = dispatch(pad, n_used.reshape(1), dest0, dest1, h, n_rows)
    ybuf = experts(layer, block_expert, n_used.reshape(1), next_expert, xbuf, w_gate, w_up, w_down)
    return combine(dest0, dest1, jnp.transpose(gate), x1, ybuf, g_final, final_norm)


def _attn_kernel(sink_ref, q_ref, kp_ref, kc_ref, kn_ref, vp_ref, vc_ref, vn_ref, bias_ref, o_ref):
    n = pl.program_id(1)
    nblk = pl.num_programs(1)
    blk = q_ref.shape[0]
    dh = SWA_HEAD_DIM
    nkv = kc_ref.shape[1] // dh
    masked = bias_ref.shape[1] - 1
    part_prev = jnp.where(n > 0, 0, masked)
    part_next = jnp.where(n < nblk - 1, 2, masked)
    lo = lax.broadcasted_iota(I32, (blk, 2 * dh), 1) < dh
    top = lax.broadcasted_iota(I32, (2 * blk, 2 * dh), 0) < blk
    lo2 = lax.broadcasted_iota(I32, (2 * blk, 2 * dh), 1) < dh
    ones_ext = jnp.concatenate([jnp.where(lo, 1.0, 0.0), jnp.where(lo, 0.0, 1.0)], axis=0).astype(BF16)
    qscale = (dh ** -0.5) * LOG2E

    for kv in range(nkv):
        col = slice((kv // 2) * 2 * dh, (kv // 2 + 1) * 2 * dh)

        def extend(ref):
            x = ref[:, col].astype(F32)
            r = pltpu.roll(x, dh, 1)
            x_lo, x_hi = (x, r) if kv % 2 == 0 else (r, x)
            return jnp.concatenate([jnp.where(lo, x_lo, 0.0), jnp.where(lo, 0.0, x_hi)],
                                   axis=0).astype(BF16)

        q2 = jnp.concatenate([q_ref[:, (2 * kv) * 2 * dh:(2 * kv + 1) * 2 * dh],
                              q_ref[:, (2 * kv + 1) * 2 * dh:(2 * kv + 2) * 2 * dh]], axis=0)
        q2 = (q2.astype(F32) * qscale).astype(BF16)

        def scores(k_ref, part):
            return lax.dot_general(q2, extend(k_ref), NT_DIMS,
                                   preferred_element_type=F32) + bias_ref[kv, part]

        s = [scores(kp_ref, part_prev), scores(kc_ref, 1), scores(kn_ref, part_next)]
        mx = jnp.maximum(jnp.maximum(s[0], s[1]), s[2])
        sk = [jnp.where(top, sink_ref[0, 4 * kv + par], sink_ref[0, 4 * kv + 2 + par])
              for par in range(2)]
        m = [jnp.maximum(jnp.broadcast_to(jnp.max(mx[:, par * blk:(par + 1) * blk], axis=-1,
                                                  keepdims=True), (2 * blk, 2 * dh)), sk[par])
             for par in range(2)]
        acc = jnp.zeros((2 * blk, 4 * dh), F32)
        for sp, v_ref in zip(s, (vp_ref, vc_ref, vn_ref)):
            e = jnp.concatenate([jnp.exp2(sp[:, :blk] - m[0]), jnp.exp2(sp[:, blk:] - m[1])],
                                axis=1).astype(BF16)
            rhs = jnp.concatenate([extend(v_ref), ones_ext], axis=1)
            acc = acc + jnp.dot(e, rhs, preferred_element_type=F32)
        den = acc[:, 2 * dh:] + jnp.exp2(jnp.where(lo2, sk[0] - m[0], sk[1] - m[1]))
        o = (acc[:, :2 * dh] / den).astype(BF16)
        o_ref[:, (2 * kv) * 2 * dh:(2 * kv + 1) * 2 * dh] = o[:blk]
        o_ref[:, (2 * kv + 1) * 2 * dh:(2 * kv + 2) * 2 * dh] = o[blk:]


def banded_attention(qkv3, sink, bias):
    b, seq, width = qkv3.shape
    nh = sink.shape[1]
    qd = nh * SWA_HEAD_DIM
    kvd = (width - qd) // 2
    blk = SWA_BLOCK
    nblk = seq // blk
    kcol, vcol = qd // kvd, qd // kvd + 1

    def band(col, off):
        return pl.BlockSpec((None, blk, kvd),
                            lambda bi, n: (bi, jnp.clip(n + off, 0, nblk - 1), col))

    return pl.pallas_call(
        _attn_kernel,
        grid=(b, nblk),
        in_specs=[
            pl.BlockSpec(memory_space=pltpu.SMEM),
            pl.BlockSpec((None, blk, qd), lambda bi, n: (bi, n, 0)),
            band(kcol, -1), band(kcol, 0), band(kcol, 1),
            band(vcol, -1), band(vcol, 0), band(vcol, 1),
            pl.BlockSpec(bias.shape, lambda bi, n: (0, 0, 0, 0)),
        ],
        out_specs=pl.BlockSpec((None, blk, qd), lambda bi, n: (bi, n, 0)),
        out_shape=jax.ShapeDtypeStruct((b, seq, qd), BF16),
        compiler_params=_params(("parallel", "parallel")),
        name="banded_attention",
    )(sink, qkv3, qkv3, qkv3, qkv3, qkv3, qkv3, qkv3, bias)


def _t5_bucket(rel):
    half = REL_BUCKETS // 2
    max_exact = half // 2
    n = jnp.abs(rel)
    large = max_exact + (jnp.log(jnp.maximum(n, 1).astype(F32) / max_exact)
                         / math.log(REL_MAX_DIST / max_exact) * (half - max_exact)).astype(I32)
    large = jnp.minimum(large, half - 1)
    return jnp.where(rel > 0, half, 0) + jnp.where(n < max_exact, n, large)


def _attention_bias(rel_bias):
    blk = SWA_BLOCK
    nh = rel_bias.shape[1]
    nkv = nh // SWA_Q_PER_KV
    rel = jnp.arange(3 * blk)[None, :] - blk - jnp.arange(blk)[:, None]
    onehot = (_t5_bucket(rel)[..., None] == jnp.arange(REL_BUCKETS)).astype(F32)
    bias = jnp.einsum('ijb,bh->hij', onehot, rel_bias.astype(F32), precision=lax.Precision.HIGHEST)
    bias = jnp.where((jnp.abs(rel) <= SWA_WINDOW)[None], bias, NEG_INF) * LOG2E
    tiles = bias.reshape(nkv, 2, 2, blk, 3, blk).transpose(0, 4, 1, 3, 2, 5)
    tiles = tiles.reshape(nkv, 3, 2 * blk, 2 * blk)
    masked = jnp.full((nkv, 1, 2 * blk, 2 * blk), NEG_INF * LOG2E, F32)
    return jnp.concatenate([tiles, masked], axis=1)


def _rotary_tables(seq, dim):
    inv_freq = ROPE_BASE ** (-jnp.arange(0, dim, 2, dtype=F32) / dim)
    ang = jnp.arange(seq, dtype=F32)[:, None] * inv_freq[None, :]
    cos, sin = jnp.cos(ang), jnp.sin(ang)
    return jnp.concatenate([cos, cos], axis=1), jnp.concatenate([-sin, sin], axis=1)


def kernel(x, norm_mix_g, norm_ffn_g, norm_final_g, hyb_w_in, ret_decay_logit, s5_a_re, s5_a_im, s5_log_step, s5_b_re, s5_b_im, s5_c_re, s5_c_im, s5_d, s5_w_glu, s5_b_glu, hyb_w_out, swa_w_qkv, swa_sink, swa_w_o, rel_bias, moe_w_group, moe_b_group, moe_w_expert_router, moe_b_expert_router, moe_w_gate, moe_w_up, moe_w_down):
    nb, seq, d = x.shape
    t = nb * seq
    depth = norm_mix_g.shape[0]
    xt = x.reshape(t, d)
    for layer in range(depth):
        i = layer // 2
        if layer % 2 == 0:
            w = hyb_w_out.shape[1] // 2
            z = norm_matmul(xt, norm_mix_g[layer], hyb_w_in[i])
            z3 = z.reshape(nb, seq, z.shape[1])
            cos, sin = _rotary_tables(seq, w // RET_HEADS)
            log_gamma = jax.nn.log_sigmoid(ret_decay_logit[i].astype(F32))
            y_ret = retention(z3, log_gamma, cos, sin).reshape(t, w)
            lam, bmat, cmat = _s5_discretize(s5_a_re[i], s5_a_im[i], s5_log_step[i], s5_b_re[i],
                                             s5_b_im[i], s5_c_re[i], s5_c_im[i])
            ys5 = s5_scan(z3, lam, bmat, cmat).reshape(2, t, w)
            mix_in = (y_ret, ys5, z, s5_d[i].astype(F32).reshape(-1), s5_w_glu[i],
                      s5_b_glu[i].astype(F32))
            w_mix = hyb_w_out[i]
        else:
            qkv = norm_matmul(xt, norm_mix_g[layer], swa_w_qkv[i])
            mix_in = banded_attention(qkv.reshape(nb, seq, qkv.shape[1]),
                                      swa_sink[i].astype(F32).reshape(1, -1) * LOG2E,
                                      _attention_bias(rel_bias)).reshape(t, -1)
            w_mix = swa_w_o[i]
        r_hi, r_lo, r_bias = _router_operands(moe_w_group[layer], moe_b_group[layer],
                                              moe_w_expert_router[layer], moe_b_expert_router[layer])
        x1, h, lt = proj_norm_router(mix_in, w_mix, xt, norm_ffn_g[layer], r_hi, r_lo, r_bias)
        last = layer == depth - 1
        xt = hier_moe_block(layer, x1, h, lt, moe_w_gate, moe_w_up, moe_w_down,
                            norm_final_g, final_norm=last)
    return xt.reshape(nb, seq, d)
```

```python
import functools
import math

import jax
import jax.numpy as jnp
from jax import lax
from jax.experimental import pallas as pl
from jax.experimental.pallas import tpu as pltpu

F32 = jnp.float32
BF16 = jnp.bfloat16
I32 = jnp.int32

RET_HEADS = 4
RET_CHUNK = 128
S5_GROUP_CH = 16
S5_STATE = 64
SWA_HEAD_DIM = 64
SWA_Q_PER_KV = 4
SWA_WINDOW = 128
SWA_BLOCK = 128
REL_BUCKETS = 32
REL_MAX_DIST = 128
MOE_GROUPS = 4
MOE_EXPERTS_PER_GROUP = 8
MOE_EXPERTS = MOE_GROUPS * MOE_EXPERTS_PER_GROUP
ROPE_BASE = 10000.0
RMS_EPS = 1e-6
GN_EPS = 1e-5
NEG_INF = -1e30
LOG2E = 1.4426950408889634

LANES = 128
SUBLANES = 8
V7X_VMEM_BYTES = 64 * 1024 * 1024
VMEM_LIMIT = V7X_VMEM_BYTES - 8 * 1024 * 1024

DENSE_ROWS = 1024
MOE_ROWS = 512
MOE_TOKENS = 512
S5_CHUNK = 128
S5_COLS = 512
S5_SCAN_COLS = 1024
S5_KBLK = 128
ROUTER_ROWS = 128
RET_UNROLL = 8

NT_DIMS = (((1,), (1,)), ((), ()))
TN_DIMS = (((0,), (0,)), ((), ()))


def _params(semantics):
    return pltpu.CompilerParams(dimension_semantics=semantics, vmem_limit_bytes=VMEM_LIMIT)


def _rms(x, g):
    ms = jnp.mean(x * x, axis=-1, keepdims=True)
    return (x * lax.rsqrt(ms + RMS_EPS)) * g


def _norm_matmul_kernel(x_ref, g_ref, w_ref, o_ref):
    h = _rms(x_ref[...], g_ref[...])
    o_ref[...] = jnp.dot(h.astype(BF16), w_ref[...],
                         preferred_element_type=F32).astype(o_ref.dtype)


def norm_matmul(x, g, w, tm=DENSE_ROWS):
    t, d = x.shape
    tm = min(tm, t)
    n = w.shape[1]
    return pl.pallas_call(
        _norm_matmul_kernel,
        grid=(t // tm,),
        in_specs=[
            pl.BlockSpec((tm, d), lambda i: (i, 0)),
            pl.BlockSpec((1, d), lambda i: (0, 0)),
            pl.BlockSpec((d, n), lambda i: (0, 0)),
        ],
        out_specs=pl.BlockSpec((tm, n), lambda i: (i, 0)),
        out_shape=jax.ShapeDtypeStruct((t, n), BF16),
        compiler_params=_params(("parallel",)),
        name="norm_matmul",
    )(x, g.reshape(1, d), w.astype(BF16))


def _retention_kernel(lg_ref, q_ref, k_ref, v_ref, g_ref, cos_ref, sin_ref, o_ref,
                      qr_ref, kr_ref, inc_ref, st_ref, lhs_ref):
    h = pl.program_id(1)
    lg_f = lg_ref[0, h]
    lg_b = lg_ref[1, h]
    seq, dk = q_ref.shape
    c = RET_CHUNK
    nc = seq // c

    cos = cos_ref[...]
    sin = sin_ref[...]
    swap = (lax.broadcasted_iota(I32, (dk, dk), 0)
            == (lax.broadcasted_iota(I32, (dk, dk), 1) + dk // 2) % dk)
    swap = jnp.where(swap, 1.0, 0.0).astype(BF16)
    q = q_ref[...]
    qr_ref[...] = (q.astype(F32) * cos
                   + jnp.dot(q, swap, preferred_element_type=F32) * sin)
    k = k_ref[...]
    kr_ref[...] = (k.astype(F32) * cos
                   + jnp.dot(k, swap, preferred_element_type=F32) * sin) * (dk ** -0.5)

    pos = lax.broadcasted_iota(I32, (c, dk), 0).astype(F32)
    kf_scale = jnp.exp((c - 1.0 - pos) * lg_f)
    qf_scale = jnp.exp((pos + 1.0) * lg_f)
    kb_scale = jnp.exp(pos * lg_b)
    qb_scale = jnp.exp((c - pos) * lg_b)
    rel = (lax.broadcasted_iota(I32, (c, c), 0) - lax.broadcasted_iota(I32, (c, c), 1)).astype(F32)
    mask = jnp.exp(jnp.abs(rel) * jnp.where(rel >= 0, lg_f, lg_b))
    dec_f = jnp.exp(jnp.full((dk, dk), c * lg_f, F32))
    dec_b = jnp.exp(jnp.full((dk, dk), c * lg_b, F32))

    def increments(n, carry):
        rows = pl.ds(pl.multiple_of(n * c, c), c)
        kc = kr_ref[rows, :]
        kk = jnp.concatenate([kc * kf_scale, kc * kb_scale], axis=1).astype(BF16)
        inc_ref[n] = lax.dot_general(kk, v_ref[rows, :], TN_DIMS,
                                     preferred_element_type=F32)
        return carry

    lax.fori_loop(0, nc, increments, 0, unroll=min(RET_UNROLL, nc))

    def fwd(n, state):
        st_ref[n, :dk, :] = state.astype(BF16)
        return state * dec_f + inc_ref[n, :dk, :]

    lax.fori_loop(0, nc, fwd, jnp.zeros((dk, dk), F32))

    def bwd(i, state):
        n = nc - 1 - i
        st_ref[n, dk:, :] = state.astype(BF16)
        return state * dec_b + inc_ref[n, dk:, :]

    lax.fori_loop(0, nc, bwd, jnp.zeros((dk, dk), F32))

    def operands(n, carry):
        rows = pl.ds(pl.multiple_of(n * c, c), c)
        qc = qr_ref[rows, :]
        s = lax.dot_general(qc.astype(BF16), kr_ref[rows, :].astype(BF16), NT_DIMS,
                            preferred_element_type=F32) * mask
        lhs_ref[n] = jnp.concatenate([s, qc * qf_scale, qc * qb_scale], axis=1).astype(BF16)
        return carry

    lax.fori_loop(0, nc, operands, 0, unroll=min(RET_UNROLL, nc))

    def outputs(n, carry):
        rows = pl.ds(pl.multiple_of(n * c, c), c)
        rhs = jnp.concatenate([v_ref[rows, :], st_ref[n]], axis=0)
        kr_ref[rows, :] = jnp.dot(lhs_ref[n], rhs, preferred_element_type=F32)
        return carry

    lax.fori_loop(0, nc, outputs, 0, unroll=min(RET_UNROLL, nc))

    out = kr_ref[...]
    mu = jnp.mean(out, axis=-1, keepdims=True)
    cen = out - mu
    var = jnp.mean(cen * cen, axis=-1, keepdims=True)
    g = g_ref[...].astype(F32)
    o_ref[...] = ((g * jax.nn.sigmoid(g)) * (cen * lax.rsqrt(var + GN_EPS))).astype(BF16)


def retention(z3, log_gamma, cos, sin):
    b, seq, _ = z3.shape
    nh = RET_HEADS
    dk = cos.shape[1]

    def col(off):
        return pl.BlockSpec((None, seq, dk), lambda bi, hi: (bi, 0, off + hi))

    return pl.pallas_call(
        _retention_kernel,
        grid=(b, nh),
        in_specs=[
            pl.BlockSpec(memory_space=pltpu.SMEM),
            col(0), col(nh), col(2 * nh), col(3 * nh),
            pl.BlockSpec((seq, dk), lambda bi, hi: (0, 0)),
            pl.BlockSpec((seq, dk), lambda bi, hi: (0, 0)),
        ],
        out_specs=pl.BlockSpec((None, seq, dk), lambda bi, hi: (bi, 0, hi)),
        out_shape=jax.ShapeDtypeStruct((b, seq, nh * dk), BF16),
        scratch_shapes=[pltpu.VMEM((seq, dk), F32), pltpu.VMEM((seq, dk), F32),
                        pltpu.VMEM((seq // RET_CHUNK, 2 * dk, dk), F32),
                        pltpu.VMEM((seq // RET_CHUNK, 2 * dk, dk), BF16),
                        pltpu.VMEM((seq // RET_CHUNK, RET_CHUNK, 3 * dk), BF16)],
        compiler_params=_params(("parallel", "parallel")),
        name="retention",
    )(log_gamma, z3, z3, z3, z3, cos, sin)


def _s5_kernel(u_ref, lam_ref, b_ref, c_ref, y_ref, sk_ref, utm_ref, ytm_ref, xre_ref, xim_ref,
               sre_ref, sim_ref):
    d = pl.program_id(0)
    n = pl.program_id(1)
    nb, cn, width = u_ref.shape
    srows = sk_ref.shape[1] // nb
    nk = width // S5_KBLK
    npl = width // LANES

    @pl.when(n == 0)
    def _():
        sre_ref[...] = jnp.zeros_like(sre_ref)
        sim_ref[...] = jnp.zeros_like(sim_ref)

    def skew_rows(b):
        return slice(b * srows + b, b * srows + b + cn)

    for b in range(nb):
        ub = u_ref[b].astype(F32)
        for p in range(npl):
            sk_ref[p, skew_rows(b), :] = ub[:, p * LANES:(p + 1) * LANES]

    def to_time_major(t, carry):
        rows = pl.ds(pl.multiple_of(t * nb, nb), nb)
        for p in range(npl):
            utm_ref[rows, p * LANES:(p + 1) * LANES] = sk_ref[p, pl.ds(t, nb, stride=srows + 1), :]
        return carry

    lax.fori_loop(0, cn, to_time_major, 0, unroll=8)

    u = utm_ref[...].astype(BF16)
    ppc = S5_COLS // LANES
    for kb in range(nk):
        bu = jnp.dot(u[:, kb * S5_KBLK:(kb + 1) * S5_KBLK], b_ref[kb], preferred_element_type=F32)
        for j in range(ppc):
            xre_ref[kb * ppc + j] = bu[:, j * LANES:(j + 1) * LANES]
            xim_ref[kb * ppc + j] = bu[:, S5_COLS + j * LANES:S5_COLS + (j + 1) * LANES]

    pps = S5_SCAN_COLS // LANES
    for cb in range(sre_ref.shape[1] // S5_SCAN_COLS):
        cols = slice(cb * S5_SCAN_COLS, (cb + 1) * S5_SCAN_COLS)
        lr = jnp.broadcast_to(lam_ref[0:1, cols], (nb, S5_SCAN_COLS))
        li = jnp.broadcast_to(lam_ref[1:2, cols], (nb, S5_SCAN_COLS))

        def step(i, carry):
            xr, xi = carry
            t = i + d * (cn - 1 - 2 * i)
            rows = pl.ds(pl.multiple_of(t * nb, nb), nb)
            bur = jnp.concatenate([xre_ref[cb * pps + j, rows, :] for j in range(pps)], axis=1)
            bui = jnp.concatenate([xim_ref[cb * pps + j, rows, :] for j in range(pps)], axis=1)
            nxr = lr * xr - li * xi + bur
            nxi = lr * xi + li * xr + bui
            for j in range(pps):
                xre_ref[cb * pps + j, rows, :] = nxr[:, j * LANES:(j + 1) * LANES]
                xim_ref[cb * pps + j, rows, :] = nxi[:, j * LANES:(j + 1) * LANES]
            return nxr, nxi

        xr, xi = lax.fori_loop(0, cn, step, (sre_ref[:, cols], sim_ref[:, cols]), unroll=8)
        sre_ref[:, cols] = xr
        sim_ref[:, cols] = xi

    for kb in range(nk):
        xr = jnp.concatenate([xre_ref[kb * ppc + j] for j in range(ppc)], axis=1).astype(BF16)
        xi = jnp.concatenate([xim_ref[kb * ppc + j] for j in range(ppc)], axis=1).astype(BF16)
        y = jnp.dot(xr, c_ref[kb, :S5_COLS, :], preferred_element_type=F32)
        ytm_ref[kb] = y + jnp.dot(xi, c_ref[kb, S5_COLS:, :], preferred_element_type=F32)

    def to_batch_major(t, carry):
        rows = pl.ds(pl.multiple_of(t * nb, nb), nb)
        for p in range(npl):
            sk_ref[p, pl.ds(t, nb, stride=srows + 1), :] = ytm_ref[p, rows, :]
        return carry

    lax.fori_loop(0, cn, to_batch_major, 0, unroll=8)

    for b in range(nb):
        for p in range(npl):
            y_ref[b, :, p * LANES:(p + 1) * LANES] = sk_ref[p, skew_rows(b), :]


def s5_scan(z3, lam, bmat, cmat):
    nb, seq, zw = z3.shape
    width = bmat.shape[1] * bmat.shape[2]
    nstate = lam.shape[2]
    cn = min(S5_CHUNK, seq)
    nch = seq // cn
    srows = cn + SUBLANES
    ucol = zw // width - 1

    def chunk(d, n):
        return n + d * (nch - 1 - 2 * n)

    return pl.pallas_call(
        _s5_kernel,
        grid=(2, nch),
        in_specs=[
            pl.BlockSpec((nb, cn, width), lambda d, n: (0, chunk(d, n), ucol)),
            pl.BlockSpec((None, 2, nstate), lambda d, n: (d, 0, 0)),
            pl.BlockSpec((None,) + bmat.shape[1:], lambda d, n: (d, 0, 0, 0)),
            pl.BlockSpec((None,) + cmat.shape[1:], lambda d, n: (d, 0, 0, 0)),
        ],
        out_specs=pl.BlockSpec((None, nb, cn, width), lambda d, n: (d, 0, chunk(d, n), 0)),
        out_shape=jax.ShapeDtypeStruct((2, nb, seq, width), F32),
        scratch_shapes=[
            pltpu.VMEM((width // LANES, nb * srows, LANES), F32),
            pltpu.VMEM((cn * nb, width), F32),
            pltpu.VMEM((width // LANES, cn * nb, LANES), F32),
            pltpu.VMEM((nstate // LANES, cn * nb, LANES), F32),
            pltpu.VMEM((nstate // LANES, cn * nb, LANES), F32),
            pltpu.VMEM((nb, nstate), F32),
            pltpu.VMEM((nb, nstate), F32),
        ],
        compiler_params=_params(("arbitrary", "arbitrary")),
        name="s5_scan",
    )(z3, lam, bmat, cmat)


def _s5_discretize(a_re, a_im, log_step, b_re, b_im, c_re, c_im):
    ng, npst = a_re.shape[1], a_re.shape[2]
    gpb = S5_KBLK // S5_GROUP_CH
    nk = ng // gpb
    eye = jnp.eye(gpb, dtype=F32)
    bre, bim = b_re.astype(F32), b_im.astype(F32)
    lams, bmats, cmats = [], [], []
    for direction in range(2):
        ar = a_re[direction].astype(F32)
        ai = a_im[direction].astype(F32)
        dt = jnp.exp(log_step[direction].astype(F32))[:, None]
        mag = jnp.exp(ar * dt)
        lam_re, lam_im = mag * jnp.cos(ai * dt), mag * jnp.sin(ai * dt)
        nr, ni = lam_re - 1.0, lam_im
        den = ar * ar + ai * ai
        coef_re = (nr * ar + ni * ai) / den
        coef_im = (ni * ar - nr * ai) / den
        bbar_re = coef_re[..., None] * bre - coef_im[..., None] * bim
        bbar_im = coef_re[..., None] * bim + coef_im[..., None] * bre

        def in_blocks(m):
            m4 = m.reshape(nk, gpb, npst, S5_GROUP_CH)
            return jnp.einsum('kgpc,gh->kgchp', m4, eye).reshape(nk, S5_KBLK, gpb * npst)

        def out_blocks(m):
            m4 = m.reshape(nk, gpb, S5_GROUP_CH, npst)
            return jnp.einsum('kgcp,gh->kgphc', m4, eye).reshape(nk, gpb * npst, S5_KBLK)

        lams.append(jnp.stack([lam_re.reshape(-1), lam_im.reshape(-1)]))
        bmats.append(jnp.concatenate([in_blocks(bbar_re), in_blocks(bbar_im)], axis=2))
        cmats.append(jnp.concatenate([out_blocks(c_re[direction].astype(F32)),
                                      -out_blocks(c_im[direction].astype(F32))], axis=1))
    return jnp.stack(lams), jnp.stack(bmats).astype(BF16), jnp.stack(cmats).astype(BF16)


def _s5_glu(yf_ref, yb_ref, u_ref, d_ref, w_ref, b_ref):
    y = u_ref[...].astype(F32) * d_ref[...] + yf_ref[...] + yb_ref[...]
    y = jax.nn.gelu(y)
    gate = jax.nn.sigmoid(jnp.dot(y.astype(BF16), w_ref[...], preferred_element_type=F32) + b_ref[...])
    return (y * gate).astype(BF16)


def _proj_kernel(*refs, s5_glu):
    if s5_glu:
        yr_ref, yf_ref, yb_ref, u_ref, d_ref, wg_ref, bg_ref = refs[:7]
        w_ref, x_ref, g_ref, rh_ref, rl_ref, rb_ref, x1_ref, h_ref, lt_ref = refs[7:]
        half = yr_ref.shape[1]
        mix = jnp.dot(yr_ref[...], w_ref[:half, :], preferred_element_type=F32)
        mix = mix + jnp.dot(_s5_glu(yf_ref, yb_ref, u_ref, d_ref, wg_ref, bg_ref), w_ref[half:, :],
                            preferred_element_type=F32)
    else:
        a_ref, w_ref, x_ref, g_ref, rh_ref, rl_ref, rb_ref, x1_ref, h_ref, lt_ref = refs
        mix = jnp.dot(a_ref[...], w_ref[...], preferred_element_type=F32)
    x1 = x_ref[...] + mix
    x1_ref[...] = x1
    h = _rms(x1, g_ref[...])
    h_hi = h.astype(BF16)
    h_ref[...] = h_hi.reshape(h_ref.shape)
    h_lo = (h - h_hi.astype(F32)).astype(BF16)
    lt = lax.dot_general(rh_ref[...], h_hi, NT_DIMS, preferred_element_type=F32)
    lt = lt + lax.dot_general(rh_ref[...], h_lo, NT_DIMS, preferred_element_type=F32)
    lt = lt + lax.dot_general(rl_ref[...], h_hi, NT_DIMS, preferred_element_type=F32)
    lt_ref[...] = lt + rb_ref[...]


def proj_norm_router(mix_in, w, x, g, r_hi, r_lo, r_bias, tm=DENSE_ROWS):
    t, d = x.shape
    tm = min(tm, t)
    k = w.shape[0]
    nr = r_hi.shape[0]
    s5_glu = isinstance(mix_in, tuple)
    if s5_glu:
        y_ret, ys5, z, d_skip, w_glu, b_glu = mix_in
        half = y_ret.shape[1]
        ucol = z.shape[1] // half - 1
        lead_specs = [
            pl.BlockSpec((tm, half), lambda i: (i, 0)),
            pl.BlockSpec((None, tm, half), lambda i: (0, i, 0)),
            pl.BlockSpec((None, tm, half), lambda i: (1, i, 0)),
            pl.BlockSpec((tm, half), lambda i: (i, ucol)),
            pl.BlockSpec((1, half), lambda i: (0, 0)),
            pl.BlockSpec((half, half), lambda i: (0, 0)),
            pl.BlockSpec((1, half), lambda i: (0, 0)),
        ]
        lead_args = (y_ret, ys5, ys5, z, d_skip.reshape(1, half), w_glu.astype(BF16),
                     b_glu.reshape(1, half))
    else:
        lead_specs = [pl.BlockSpec((tm, k), lambda i: (i, 0))]
        lead_args = (mix_in,)
    return pl.pallas_call(
        functools.partial(_proj_kernel, s5_glu=s5_glu),
        grid=(t // tm,),
        in_specs=lead_specs + [
            pl.BlockSpec((k, d), lambda i: (0, 0)),
            pl.BlockSpec((tm, d), lambda i: (i, 0)),
            pl.BlockSpec((1, d), lambda i: (0, 0)),
            pl.BlockSpec((nr, d), lambda i: (0, 0)),
            pl.BlockSpec((nr, d), lambda i: (0, 0)),
            pl.BlockSpec((nr, 1), lambda i: (0, 0)),
        ],
        out_specs=[
            pl.BlockSpec((tm, d), lambda i: (i, 0)),
            pl.BlockSpec((tm, d // LANES, LANES), lambda i: (i, 0, 0)),
            pl.BlockSpec((nr, tm), lambda i: (0, i)),
        ],
        out_shape=[
            jax.ShapeDtypeStruct((t, d), F32),
            jax.ShapeDtypeStruct((t, d // LANES, LANES), BF16),
            jax.ShapeDtypeStruct((nr, t), F32),
        ],
        compiler_params=_params(("parallel",)),
        name="proj_norm_router",
    )(*lead_args, w.astype(BF16), x, g.reshape(1, d), r_hi, r_lo, r_bias)


def _router_operands(w_group, b_group, w_er, b_er):
    d = w_group.shape[0]
    wt = jnp.concatenate([
        jnp.transpose(w_er.astype(F32), (0, 2, 1)).reshape(MOE_EXPERTS, d),
        jnp.transpose(w_group.astype(F32)),
        jnp.zeros((ROUTER_ROWS - MOE_EXPERTS - MOE_GROUPS, d), F32)], axis=0)
    bias = jnp.concatenate([
        b_er.astype(F32).reshape(-1), b_group.astype(F32),
        jnp.zeros((ROUTER_ROWS - MOE_EXPERTS - MOE_GROUPS,), F32)]).reshape(ROUTER_ROWS, 1)
    hi = wt.astype(BF16)
    lo = (wt - hi.astype(F32)).astype(BF16)
    return hi, lo, bias


def _route_kernel(lt_ref, eid_ref, gate_ref, rank_ref, cnt_ref, run_ref):
    i = pl.program_id(0)
    tm = lt_ref.shape[1]
    ne, npg, ng = MOE_EXPERTS, MOE_EXPERTS_PER_GROUP, MOE_GROUPS

    @pl.when(i == 0)
    def _():
        run_ref[...] = jnp.zeros_like(run_ref)

    gl = lt_ref[ne:ne + ng, :]
    gmax = jnp.max(gl, axis=0, keepdims=True)
    gidx = lax.broadcasted_iota(I32, (ng, tm), 0)
    gsel = jnp.min(jnp.where(gl == gmax, gidx, ng), axis=0, keepdims=True)
    p_g = 1.0 / jnp.sum(jnp.exp(gl - gmax), axis=0, keepdims=True)

    e8 = lt_ref[(ng - 1) * npg:ng * npg, :]
    for g in range(ng - 2, -1, -1):
        e8 = jnp.where(gsel == g, lt_ref[g * npg:(g + 1) * npg, :], e8)
    eidx = lax.broadcasted_iota(I32, (npg, tm), 0)
    m1 = jnp.max(e8, axis=0, keepdims=True)
    i1 = jnp.min(jnp.where(e8 == m1, eidx, npg), axis=0, keepdims=True)
    e8b = jnp.where(eidx == i1, -jnp.inf, e8)
    m2 = jnp.max(e8b, axis=0, keepdims=True)
    i2 = jnp.min(jnp.where(e8b == m2, eidx, npg), axis=0, keepdims=True)
    t2 = jnp.exp(m2 - m1)
    den = 1.0 + t2
    gate_ref[0:1, :] = (1.0 / den) * p_g
    gate_ref[1:2, :] = (t2 / den) * p_g
    id1 = gsel * npg + i1
    id2 = gsel * npg + i2
    eid_ref[0:1, :] = id1
    eid_ref[1:2, :] = id2

    rows = lax.broadcasted_iota(I32, (ne, tm), 0)
    oh1 = rows == id1
    oh2 = rows == id2
    both = jnp.where(oh1, 1.0, 0.0) + jnp.where(oh2, 1.0, 0.0)
    earlier = (lax.broadcasted_iota(I32, (tm, tm), 0) < lax.broadcasted_iota(I32, (tm, tm), 1))
    prefix = jnp.dot(both.astype(BF16), jnp.where(earlier, 1.0, 0.0).astype(BF16),
                     preferred_element_type=F32)
    base = prefix + run_ref[:, 0:1]
    rank_ref[0:1, :] = jnp.sum(jnp.where(oh1, base, 0.0), axis=0, keepdims=True).astype(I32)
    rank_ref[1:2, :] = jnp.sum(jnp.where(oh2, base, 0.0), axis=0, keepdims=True).astype(I32)
    run = run_ref[...] + jnp.sum(both, axis=1, keepdims=True)
    run_ref[...] = run
    cnt_ref[...] = run.astype(I32)


def route(lt, tm=512):
    nr, t = lt.shape
    two = lambda dt: jax.ShapeDtypeStruct((2, t), dt)
    return pl.pallas_call(
        _route_kernel,
        grid=(t // tm,),
        in_specs=[pl.BlockSpec((nr, tm), lambda i: (0, i))],
        out_specs=[
            pl.BlockSpec((2, tm), lambda i: (0, i)),
            pl.BlockSpec((2, tm), lambda i: (0, i)),
            pl.BlockSpec((2, tm), lambda i: (0, i)),
            pl.BlockSpec((MOE_EXPERTS, LANES), lambda i: (0, 0)),
        ],
        out_shape=[two(I32), two(F32), two(I32),
                   jax.ShapeDtypeStruct((MOE_EXPERTS, LANES), I32)],
        scratch_shapes=[pltpu.VMEM((MOE_EXPERTS, LANES), F32)],
        compiler_params=_params(("arbitrary",)),
        name="route",
    )(lt)


def _dispatch_kernel(pad_ref, nu_ref, d0_ref, d1_ref, h_ref, xbuf_ref, zero_ref, sem, zsem):
    i = pl.program_id(0)
    tm = h_ref.shape[0]
    dests = (d0_ref, d1_ref)
    bm = zero_ref.shape[0]
    nblk = xbuf_ref.shape[0] // bm

    def zero_fill(act):
        def per_expert(e, carry):
            pos = pad_ref[0, e]
            length = pad_ref[1, e]
            p = bm // 2
            while p >= 1:
                bit = (length & p) != 0

                @pl.when(bit)
                def _(pos=pos, p=p):
                    act(pltpu.make_async_copy(zero_ref.at[pl.ds(0, p)],
                                              xbuf_ref.at[pl.ds(pos, p)], zsem))

                pos = pos + jnp.where(bit, p, 0)
                p //= 2
            return carry

        lax.fori_loop(0, pad_ref.shape[1], per_expert, 0)

        def per_block(b, carry):
            act(pltpu.make_async_copy(zero_ref, xbuf_ref.at[pl.ds(b * bm, bm)], zsem))
            return carry

        lax.fori_loop(nu_ref[0], nblk, per_block, 0)

    @pl.when(i == 0)
    def _():
        zero_ref[...] = jnp.zeros_like(zero_ref)
        zero_fill(lambda c: c.start())

    def copy(r, k):
        return pltpu.make_async_copy(h_ref.at[r], xbuf_ref.at[dests[k][0, r]], sem)

    def start(r, carry):
        copy(r, 0).start(priority=0)
        copy(r, 1).start(priority=1)
        return carry

    lax.fori_loop(0, tm, start, 0, unroll=8)

    def wait(r, carry):
        copy(r, 0).wait()
        copy(r, 1).wait()
        return carry

    lax.fori_loop(0, tm, wait, 0, unroll=8)

    @pl.when(i == pl.num_programs(0) - 1)
    def _():
        zero_fill(lambda c: c.wait())


def _row_index_spec(tm, ahead=0, last=None):
    def index(i):
        return (i if ahead == 0 else jnp.minimum(i + ahead, last), 0, 0)
    return pl.BlockSpec((None, 1, tm), index, memory_space=pltpu.SMEM)


def dispatch(pad, n_used, dest0, dest1, h3, n_rows):
    t, s, lanes = h3.shape
    nt, _, tm = dest0.shape
    return pl.pallas_call(
        _dispatch_kernel,
        grid=(nt,),
        in_specs=[
            pl.BlockSpec(memory_space=pltpu.SMEM),
            pl.BlockSpec(memory_space=pltpu.SMEM),
            _row_index_spec(tm), _row_index_spec(tm),
            pl.BlockSpec((tm, s, lanes), lambda i: (i, 0, 0)),
        ],
        out_specs=pl.BlockSpec(memory_space=pl.ANY),
        out_shape=jax.ShapeDtypeStruct((n_rows, s, lanes), h3.dtype),
        scratch_shapes=[pltpu.VMEM((MOE_ROWS, s, lanes), h3.dtype),
                        pltpu.SemaphoreType.DMA(()), pltpu.SemaphoreType.DMA(())],
        compiler_params=_params(("arbitrary",)),
        name="moe_dispatch",
    )(pad, n_used, dest0, dest1, h3)


def _experts_kernel(be_ref, nu_ref, nx_ref, x_ref, wg_hbm, wu_hbm, wd_hbm, o_ref,
                    wg_f32, wu_f32, wd_f32, wgb, wub, wdb, slot_ref, sem, *, layer):
    i = pl.program_id(0)
    e = be_ref[i]
    first = i == 0
    changed = jnp.logical_or(first, e != be_ref[jnp.maximum(i - 1, 0)])
    streams = ((wg_hbm, wg_f32), (wu_hbm, wu_f32), (wd_hbm, wd_f32))

    def fetch(expert, slot):
        return [pltpu.make_async_copy(w_hbm.at[layer, expert], w_f32.at[slot], sem.at[slot, j])
                for j, (w_hbm, w_f32) in enumerate(streams)]

    @pl.when(first)
    def _():
        slot_ref[0] = 1
        for c in fetch(e, 0):
            c.start()

    @pl.when(changed)
    def _():
        slot = 1 - slot_ref[0]
        slot_ref[0] = slot
        for c in fetch(e, slot):
            c.wait()
        nxt = nx_ref[i]

        @pl.when(nxt >= 0)
        def _():
            for c in fetch(nxt, 1 - slot):
                c.start()

        wgb[...] = wg_f32[slot].astype(BF16)
        wub[...] = wu_f32[slot].astype(BF16)
        wdb[...] = wd_f32[slot].astype(BF16)

    @pl.when(i < nu_ref[0])
    def _():
        bm, s, lanes = x_ref.shape
        x = x_ref[...].reshape(bm, s * lanes)
        g = jnp.dot(x, wgb[...], preferred_element_type=F32)
        u = jnp.dot(x, wub[...], preferred_element_type=F32)
        a = ((g * jax.nn.sigmoid(g)) * u).astype(BF16)
        y = jnp.dot(a, wdb[...], preferred_element_type=F32)
        o_ref[...] = y.astype(BF16).reshape(o_ref.shape)

    @pl.when(i >= nu_ref[0])
    def _():
        o_ref[...] = jnp.zeros_like(o_ref)


def experts(layer, block_expert, n_used, next_expert, xbuf, w_gate, w_up, w_down):
    n_rows, s, lanes = xbuf.shape
    d = s * lanes
    hid = w_gate.shape[3]
    bm = MOE_ROWS
    nblk = n_rows // bm
    grid_spec = pltpu.PrefetchScalarGridSpec(
        num_scalar_prefetch=3,
        grid=(nblk,),
        in_specs=[
            pl.BlockSpec((bm, s, lanes), lambda i, be, nu, nx: (jnp.minimum(i, nu[0] - 1), 0, 0)),
            pl.BlockSpec(memory_space=pl.ANY),
            pl.BlockSpec(memory_space=pl.ANY),
            pl.BlockSpec(memory_space=pl.ANY),
        ],
        out_specs=pl.BlockSpec((bm, s, lanes), lambda i, be, nu, nx: (i, 0, 0)),
        scratch_shapes=[pltpu.VMEM((2, d, hid), F32), pltpu.VMEM((2, d, hid), F32),
                        pltpu.VMEM((2, hid, d), F32),
                        pltpu.VMEM((d, hid), BF16), pltpu.VMEM((d, hid), BF16),
                        pltpu.VMEM((hid, d), BF16),
                        pltpu.SMEM((1,), I32), pltpu.SemaphoreType.DMA((2, 3))],
    )
    return pl.pallas_call(
        functools.partial(_experts_kernel, layer=layer),
        grid_spec=grid_spec,
        out_shape=jax.ShapeDtypeStruct((n_rows, s, lanes), BF16),
        compiler_params=_params(("arbitrary",)),
        name="moe_experts",
    )(block_expert, n_used, next_expert, xbuf, w_gate, w_up, w_down)


def _combine_kernel(d0_ref, d1_ref, n0_ref, n1_ref, gate_ref, x_ref, g_ref, ybuf_ref, o_ref,
                    buf, sem, *, final_norm):
    i = pl.program_id(0)
    tm, d = x_ref.shape
    slot = i % 2

    def copy(dests, s, r, k):
        return pltpu.make_async_copy(ybuf_ref.at[dests[k][0, r]], buf.at[s, k, r], sem.at[s])

    def gather(dests, s):
        def start(r, carry):
            copy(dests, s, r, 0).start(priority=0)
            copy(dests, s, r, 1).start(priority=1)
            return carry

        lax.fori_loop(0, tm, start, 0, unroll=8)

    @pl.when(i == 0)
    def _():
        gather((d0_ref, d1_ref), 0)

    @pl.when(i + 1 < pl.num_programs(0))
    def _():
        gather((n0_ref, n1_ref), 1 - slot)

    def wait(r, carry):
        copy((d0_ref, d1_ref), slot, r, 0).wait()
        copy((d0_ref, d1_ref), slot, r, 1).wait()
        return carry

    lax.fori_loop(0, tm, wait, 0, unroll=8)

    gates = gate_ref[...]
    y = (gates[:, 0:1] * buf[slot, 0].reshape(tm, d).astype(F32)
         + gates[:, 1:2] * buf[slot, 1].reshape(tm, d).astype(F32))
    out = x_ref[...] + y
    if final_norm:
        out = _rms(out, g_ref[...])
    o_ref[...] = out


def combine(dest0, dest1, gates_t, x, ybuf, g_final, final_norm):
    t, d = x.shape
    nt, _, tm = dest0.shape
    _, s, lanes = ybuf.shape
    return pl.pallas_call(
        functools.partial(_combine_kernel, final_norm=final_norm),
        grid=(nt,),
        in_specs=[
            _row_index_spec(tm), _row_index_spec(tm),
            _row_index_spec(tm, ahead=1, last=nt - 1), _row_index_spec(tm, ahead=1, last=nt - 1),
            pl.BlockSpec((tm, 2), lambda i: (i, 0)),
            pl.BlockSpec((tm, d), lambda i: (i, 0)),
            pl.BlockSpec((1, d), lambda i: (0, 0)),
            pl.BlockSpec(memory_space=pl.ANY),
        ],
        out_specs=pl.BlockSpec((tm, d), lambda i: (i, 0)),
        out_shape=jax.ShapeDtypeStruct((t, d), F32),
        scratch_shapes=[pltpu.VMEM((2, 2, tm, s, lanes), ybuf.dtype),
                        pltpu.SemaphoreType.DMA((2,))],
        compiler_params=_params(("arbitrary",)),
        name="moe_combine",
    )(dest0, dest1, dest0, dest1, gates_t, x, g_final.reshape(1, d), ybuf)


def hier_moe_block(layer, x1, h, lt, w_gate, w_up, w_down, g_final, final_norm, tm=MOE_TOKENS):
    t, d = x1.shape
    bm = MOE_ROWS
    eid, gate, rank, cnt = route(lt)
    counts = cnt[:, 0]
    padded = ((counts + bm - 1) // bm) * bm
    pend = jnp.cumsum(padded)
    pstart = pend - padded
    experts_col = jnp.arange(MOE_EXPERTS, dtype=I32)[:, None, None]
    dest = rank + jnp.sum(jnp.where(eid[None] == experts_col, pstart[:, None, None], 0), axis=0)
    n_rows = 2 * t + MOE_EXPERTS * bm
    nblk = n_rows // bm
    n_used = (pend[-1] // bm).astype(I32)
    first_row = jnp.minimum(jnp.arange(nblk, dtype=I32), n_used - 1) * bm
    block_expert = jnp.sum(pend[None, :] <= first_row[:, None], axis=1).astype(I32)
    block_expert = jnp.minimum(block_expert, MOE_EXPERTS - 1)
    ids = jnp.arange(MOE_EXPERTS, dtype=I32)
    later = jnp.logical_and(ids[None, :] > block_expert[:, None], counts[None, :] > 0)
    next_expert = jnp.min(jnp.where(later, ids[None, :], MOE_EXPERTS), axis=1)
    next_expert = jnp.where(next_expert < MOE_EXPERTS, next_expert, -1).astype(I32)
    dest0 = dest[0].reshape(t // tm, 1, tm)
    dest1 = dest[1].reshape(t // tm, 1, tm)
    pad = jnp.stack([pstart + counts, padded - counts]).astype(I32)
    xbuf = dispatch(pad, n_used.reshape(1), dest0, dest1, h, n_rows)
    ybuf = experts(layer, block_expert, n_used.reshape(1), next_expert, xbuf, w_gate, w_up, w_down)
    return combine(dest0, dest1, jnp.transpose(gate), x1, ybuf, g_final, final_norm)


def _attn_kernel(sink_ref, q_ref, kp_ref, kc_ref, kn_ref, vp_ref, vc_ref, vn_ref, bias_ref, o_ref):
    n = pl.program_id(1)
    nblk = pl.num_programs(1)
    blk = q_ref.shape[0]
    dh = SWA_HEAD_DIM
    nkv = kc_ref.shape[1] // dh
    masked = bias_ref.shape[1] - 1
    part_prev = jnp.where(n > 0, 0, masked)
    part_next = jnp.where(n < nblk - 1, 2, masked)
    lo = lax.broadcasted_iota(I32, (blk, 2 * dh), 1) < dh
    top = lax.broadcasted_iota(I32, (2 * blk, 2 * dh), 0) < blk
    lo2 = lax.broadcasted_iota(I32, (2 * blk, 2 * dh), 1) < dh
    ones_ext = jnp.concatenate([jnp.where(lo, 1.0, 0.0), jnp.where(lo, 0.0, 1.0)], axis=0).astype(BF16)
    qscale = (dh ** -0.5) * LOG2E

    for kv in range(nkv):
        col = slice((kv // 2) * 2 * dh, (kv // 2 + 1) * 2 * dh)

        def extend(ref):
            x = ref[:, col].astype(F32)
            r = pltpu.roll(x, dh, 1)
            x_lo, x_hi = (x, r) if kv % 2 == 0 else (r, x)
            return jnp.concatenate([jnp.where(lo, x_lo, 0.0), jnp.where(lo, 0.0, x_hi)],
                                   axis=0).astype(BF16)

        q2 = jnp.concatenate([q_ref[:, (2 * kv) * 2 * dh:(2 * kv + 1) * 2 * dh],
                              q_ref[:, (2 * kv + 1) * 2 * dh:(2 * kv + 2) * 2 * dh]], axis=0)
        q2 = (q2.astype(F32) * qscale).astype(BF16)

        def scores(k_ref, part):
            return lax.dot_general(q2, extend(k_ref), NT_DIMS,
                                   preferred_element_type=F32) + bias_ref[kv, part]

        s = [scores(kp_ref, part_prev), scores(kc_ref, 1), scores(kn_ref, part_next)]
        mx = jnp.maximum(jnp.maximum(s[0], s[1]), s[2])
        sk = [jnp.where(top, sink_ref[0, 4 * kv + par], sink_ref[0, 4 * kv + 2 + par])
              for par in range(2)]
        m = [jnp.maximum(jnp.broadcast_to(jnp.max(mx[:, par * blk:(par + 1) * blk], axis=-1,
                                                  keepdims=True), (2 * blk, 2 * dh)), sk[par])
             for par in range(2)]
        acc = jnp.zeros((2 * blk, 4 * dh), F32)
        for sp, v_ref in zip(s, (vp_ref, vc_ref, vn_ref)):
            e = jnp.concatenate([jnp.exp2(sp[:, :blk] - m[0]), jnp.exp2(sp[:, blk:] - m[1])],
                                axis=1).astype(BF16)
            rhs = jnp.concatenate([extend(v_ref), ones_ext], axis=1)
            acc = acc + jnp.dot(e, rhs, preferred_element_type=F32)
        den = acc[:, 2 * dh:] + jnp.exp2(jnp.where(lo2, sk[0] - m[0], sk[1] - m[1]))
        o = (acc[:, :2 * dh] / den).astype(BF16)
        o_ref[:, (2 * kv) * 2 * dh:(2 * kv + 1) * 2 * dh] = o[:blk]
        o_ref[:, (2 * kv + 1) * 2 * dh:(2 * kv + 2) * 2 * dh] = o[blk:]


def banded_attention(qkv3, sink, bias):
    b, seq, width = qkv3.shape
    nh = sink.shape[1]
    qd = nh * SWA_HEAD_DIM
    kvd = (width - qd) // 2
    blk = SWA_BLOCK
    nblk = seq // blk
    kcol, vcol = qd // kvd, qd // kvd + 1

    def band(col, off):
        return pl.BlockSpec((None, blk, kvd),
                            lambda bi, n: (bi, jnp.clip(n + off, 0, nblk - 1), col))

    return pl.pallas_call(
        _attn_kernel,
        grid=(b, nblk),
        in_specs=[
            pl.BlockSpec(memory_space=pltpu.SMEM),
            pl.BlockSpec((None, blk, qd), lambda bi, n: (bi, n, 0)),
            band(kcol, -1), band(kcol, 0), band(kcol, 1),
            band(vcol, -1), band(vcol, 0), band(vcol, 1),
            pl.BlockSpec(bias.shape, lambda bi, n: (0, 0, 0, 0)),
        ],
        out_specs=pl.BlockSpec((None, blk, qd), lambda bi, n: (bi, n, 0)),
        out_shape=jax.ShapeDtypeStruct((b, seq, qd), BF16),
        compiler_params=_params(("parallel", "parallel")),
        name="banded_attention",
    )(sink, qkv3, qkv3, qkv3, qkv3, qkv3, qkv3, qkv3, bias)


def _t5_bucket(rel):
    half = REL_BUCKETS // 2
    max_exact = half // 2
    n = jnp.abs(rel)
    large = max_exact + (jnp.log(jnp.maximum(n, 1).astype(F32) / max_exact)
                         / math.log(REL_MAX_DIST / max_exact) * (half - max_exact)).astype(I32)
    large = jnp.minimum(large, half - 1)
    return jnp.where(rel > 0, half, 0) + jnp.where(n < max_exact, n, large)


def _attention_bias(rel_bias):
    blk = SWA_BLOCK
    nh = rel_bias.shape[1]
    nkv = nh // SWA_Q_PER_KV
    rel = jnp.arange(3 * blk)[None, :] - blk - jnp.arange(blk)[:, None]
    onehot = (_t5_bucket(rel)[..., None] == jnp.arange(REL_BUCKETS)).astype(F32)
    bias = jnp.einsum('ijb,bh->hij', onehot, rel_bias.astype(F32), precision=lax.Precision.HIGHEST)
    bias = jnp.where((jnp.abs(rel) <= SWA_WINDOW)[None], bias, NEG_INF) * LOG2E
    tiles = bias.reshape(nkv, 2, 2, blk, 3, blk).transpose(0, 4, 1, 3, 2, 5)
    tiles = tiles.reshape(nkv, 3, 2 * blk, 2 * blk)
    masked = jnp.full((nkv, 1, 2 * blk, 2 * blk), NEG_INF * LOG2E, F32)
    return jnp.concatenate([tiles, masked], axis=1)


def _rotary_tables(seq, dim):
    inv_freq = ROPE_BASE ** (-jnp.arange(0, dim, 2, dtype=F32) / dim)
    ang = jnp.arange(seq, dtype=F32)[:, None] * inv_freq[None, :]
    cos, sin = jnp.cos(ang), jnp.sin(ang)
    return jnp.concatenate([cos, cos], axis=1), jnp.concatenate([-sin, sin], axis=1)


def kernel(x, norm_mix_g, norm_ffn_g, norm_final_g, hyb_w_in, ret_decay_logit, s5_a_re, s5_a_im, s5_log_step, s5_b_re, s5_b_im, s5_c_re, s5_c_im, s5_d, s5_w_glu, s5_b_glu, hyb_w_out, swa_w_qkv, swa_sink, swa_w_o, rel_bias, moe_w_group, moe_b_group, moe_w_expert_router, moe_b_expert_router, moe_w_gate, moe_w_up, moe_w_down):
    nb, seq, d = x.shape
    t = nb * seq
    depth = norm_mix_g.shape[0]
    xt = x.reshape(t, d)
    for layer in range(depth):
        i = layer // 2
        if layer % 2 == 0:
            w = hyb_w_out.shape[1] // 2
            z = norm_matmul(xt, norm_mix_g[layer], hyb_w_in[i])
            z3 = z.reshape(nb, seq, z.shape[1])
            cos, sin = _rotary_tables(seq, w // RET_HEADS)
            log_gamma = jax.nn.log_sigmoid(ret_decay_logit[i].astype(F32))
            y_ret = retention(z3, log_gamma, cos, sin).reshape(t, w)
            lam, bmat, cmat = _s5_discretize(s5_a_re[i], s5_a_im[i], s5_log_step[i], s5_b_re[i],
                                             s5_b_im[i], s5_c_re[i], s5_c_im[i])
            ys5 = s5_scan(z3, lam, bmat, cmat).reshape(2, t, w)
            mix_in = (y_ret, ys5, z, s5_d[i].astype(F32).reshape(-1), s5_w_glu[i],
                      s5_b_glu[i].astype(F32))
            w_mix = hyb_w_out[i]
        else:
            qkv = norm_matmul(xt, norm_mix_g[layer], swa_w_qkv[i])
            mix_in = banded_attention(qkv.reshape(nb, seq, qkv.shape[1]),
                                      swa_sink[i].astype(F32).reshape(1, -1) * LOG2E,
                                      _attention_bias(rel_bias)).reshape(t, -1)
            w_mix = swa_w_o[i]
        r_hi, r_lo, r_bias = _router_operands(moe_w_group[layer], moe_b_group[layer],
                                              moe_w_expert_router[layer], moe_b_expert_router[layer])
        x1, h, lt = proj_norm_router(mix_in, w_mix, xt, norm_ffn_g[layer], r_hi, r_lo, r_bias)
        last = layer == depth - 1
        xt = hier_moe_block(layer, x1, h, lt, moe_w_gate, moe_w_up, moe_w_down,
                            norm_final_g, final_norm=last)
    return xt.reshape(nb, seq, d)
```

```python
import functools
import math

import jax
import jax.numpy as jnp
from jax import lax
from jax.experimental import pallas as pl
from jax.experimental.pallas import tpu as pltpu

F32 = jnp.float32
BF16 = jnp.bfloat16
I32 = jnp.int32

RET_HEADS = 4
RET_CHUNK = 128
S5_GROUP_CH = 16
S5_STATE = 64
SWA_HEAD_DIM = 64
SWA_Q_PER_KV = 4
SWA_WINDOW = 128
SWA_BLOCK = 128
REL_BUCKETS = 32
REL_MAX_DIST = 128
MOE_GROUPS = 4
MOE_EXPERTS_PER_GROUP = 8
MOE_EXPERTS = MOE_GROUPS * MOE_EXPERTS_PER_GROUP
ROPE_BASE = 10000.0
RMS_EPS = 1e-6
GN_EPS = 1e-5
NEG_INF = -1e30
LOG2E = 1.4426950408889634

LANES = 128
SUBLANES = 8
V7X_VMEM_BYTES = 64 * 1024 * 1024
VMEM_LIMIT = V7X_VMEM_BYTES - 8 * 1024 * 1024

DENSE_ROWS = 1024
MOE_ROWS = 512
MOE_TOKENS = 512
S5_CHUNK = 128
S5_COLS = 512
S5_SCAN_COLS = 512
S5_KBLK = 128
ROUTER_ROWS = 128
RET_UNROLL = 8

NT_DIMS = (((1,), (1,)), ((), ()))
TN_DIMS = (((0,), (0,)), ((), ()))


def _params(semantics):
    return pltpu.CompilerParams(dimension_semantics=semantics, vmem_limit_bytes=VMEM_LIMIT)


def _rms(x, g):
    ms = jnp.mean(x * x, axis=-1, keepdims=True)
    return (x * lax.rsqrt(ms + RMS_EPS)) * g


def _norm_matmul_kernel(x_ref, g_ref, w_ref, o_ref):
    h = _rms(x_ref[...], g_ref[...])
    o_ref[...] = jnp.dot(h.astype(BF16), w_ref[...],
                         preferred_element_type=F32).astype(o_ref.dtype)


def norm_matmul(x, g, w, tm=DENSE_ROWS):
    t, d = x.shape
    tm = min(tm, t)
    n = w.shape[1]
    return pl.pallas_call(
        _norm_matmul_kernel,
        grid=(t // tm,),
        in_specs=[
            pl.BlockSpec((tm, d), lambda i: (i, 0)),
            pl.BlockSpec((1, d), lambda i: (0, 0)),
            pl.BlockSpec((d, n), lambda i: (0, 0)),
        ],
        out_specs=pl.BlockSpec((tm, n), lambda i: (i, 0)),
        out_shape=jax.ShapeDtypeStruct((t, n), BF16),
        compiler_params=_params(("parallel",)),
        name="norm_matmul",
    )(x, g.reshape(1, d), w.astype(BF16))


def _retention_kernel(lg_ref, q_ref, k_ref, v_ref, g_ref, cos_ref, sin_ref, o_ref,
                      qr_ref, kr_ref, inc_ref, st_ref, lhs_ref):
    h = pl.program_id(1)
    lg_f = lg_ref[0, h]
    lg_b = lg_ref[1, h]
    seq, dk = q_ref.shape
    c = RET_CHUNK
    nc = seq // c

    cos = cos_ref[...]
    sin = sin_ref[...]
    swap = (lax.broadcasted_iota(I32, (dk, dk), 0)
            == (lax.broadcasted_iota(I32, (dk, dk), 1) + dk // 2) % dk)
    swap = jnp.where(swap, 1.0, 0.0).astype(BF16)
    q = q_ref[...]
    qr_ref[...] = (q.astype(F32) * cos
                   + jnp.dot(q, swap, preferred_element_type=F32) * sin)
    k = k_ref[...]
    kr_ref[...] = (k.astype(F32) * cos
                   + jnp.dot(k, swap, preferred_element_type=F32) * sin) * (dk ** -0.5)

    pos = lax.broadcasted_iota(I32, (c, dk), 0).astype(F32)
    kf_scale = jnp.exp((c - 1.0 - pos) * lg_f)
    qf_scale = jnp.exp((pos + 1.0) * lg_f)
    kb_scale = jnp.exp(pos * lg_b)
    qb_scale = jnp.exp((c - pos) * lg_b)
    rel = (lax.broadcasted_iota(I32, (c, c), 0) - lax.broadcasted_iota(I32, (c, c), 1)).astype(F32)
    mask = jnp.exp(jnp.abs(rel) * jnp.where(rel >= 0, lg_f, lg_b))
    dec_f = jnp.exp(jnp.full((dk, dk), c * lg_f, F32))
    dec_b = jnp.exp(jnp.full((dk, dk), c * lg_b, F32))

    def increments(n, carry):
        rows = pl.ds(pl.multiple_of(n * c, c), c)
        kc = kr_ref[rows, :]
        kk = jnp.concatenate([kc * kf_scale, kc * kb_scale], axis=1).astype(BF16)
        inc_ref[n] = lax.dot_general(kk, v_ref[rows, :], TN_DIMS,
                                     preferred_element_type=F32)
        return carry

    lax.fori_loop(0, nc, increments, 0, unroll=min(RET_UNROLL, nc))

    def fwd(n, state):
        st_ref[n, :dk, :] = state.astype(BF16)
        return state * dec_f + inc_ref[n, :dk, :]

    lax.fori_loop(0, nc, fwd, jnp.zeros((dk, dk), F32))

    def bwd(i, state):
        n = nc - 1 - i
        st_ref[n, dk:, :] = state.astype(BF16)
        return state * dec_b + inc_ref[n, dk:, :]

    lax.fori_loop(0, nc, bwd, jnp.zeros((dk, dk), F32))

    def operands(n, carry):
        rows = pl.ds(pl.multiple_of(n * c, c), c)
        qc = qr_ref[rows, :]
        s = lax.dot_general(qc.astype(BF16), kr_ref[rows, :].astype(BF16), NT_DIMS,
                            preferred_element_type=F32) * mask
        lhs_ref[n] = jnp.concatenate([s, qc * qf_scale, qc * qb_scale], axis=1).astype(BF16)
        return carry

    lax.fori_loop(0, nc, operands, 0, unroll=min(RET_UNROLL, nc))

    def outputs(n, carry):
        rows = pl.ds(pl.multiple_of(n * c, c), c)
        rhs = jnp.concatenate([v_ref[rows, :], st_ref[n]], axis=0)
        kr_ref[rows, :] = jnp.dot(lhs_ref[n], rhs, preferred_element_type=F32)
        return carry

    lax.fori_loop(0, nc, outputs, 0, unroll=min(RET_UNROLL, nc))

    out = kr_ref[...]
    mu = jnp.mean(out, axis=-1, keepdims=True)
    cen = out - mu
    var = jnp.mean(cen * cen, axis=-1, keepdims=True)
    g = g_ref[...].astype(F32)
    o_ref[...] = ((g * jax.nn.sigmoid(g)) * (cen * lax.rsqrt(var + GN_EPS))).astype(BF16)


def retention(z3, log_gamma, cos, sin):
    b, seq, _ = z3.shape
    nh = RET_HEADS
    dk = cos.shape[1]

    def col(off):
        return pl.BlockSpec((None, seq, dk), lambda bi, hi: (bi, 0, off + hi))

    return pl.pallas_call(
        _retention_kernel,
        grid=(b, nh),
        in_specs=[
            pl.BlockSpec(memory_space=pltpu.SMEM),
            col(0), col(nh), col(2 * nh), col(3 * nh),
            pl.BlockSpec((seq, dk), lambda bi, hi: (0, 0)),
            pl.BlockSpec((seq, dk), lambda bi, hi: (0, 0)),
        ],
        out_specs=pl.BlockSpec((None, seq, dk), lambda bi, hi: (bi, 0, hi)),
        out_shape=jax.ShapeDtypeStruct((b, seq, nh * dk), BF16),
        scratch_shapes=[pltpu.VMEM((seq, dk), F32), pltpu.VMEM((seq, dk), F32),
                        pltpu.VMEM((seq // RET_CHUNK, 2 * dk, dk), F32),
                        pltpu.VMEM((seq // RET_CHUNK, 2 * dk, dk), BF16),
                        pltpu.VMEM((seq // RET_CHUNK, RET_CHUNK, 3 * dk), BF16)],
        compiler_params=_params(("parallel", "parallel")),
        name="retention",
    )(log_gamma, z3, z3, z3, z3, cos, sin)


def _s5_kernel(u_ref, lam_ref, b_ref, c_ref, y_ref, sk_ref, utm_ref, ytm_ref, sre_ref, sim_ref,
               *hist):
    nkb = len(hist) // 2

    def xre(plane):
        return hist[plane // (S5_COLS // LANES)].at[plane % (S5_COLS // LANES)]

    def xim(plane):
        return hist[nkb + plane // (S5_COLS // LANES)].at[plane % (S5_COLS // LANES)]

    d = pl.program_id(0)
    n = pl.program_id(1)
    nb, cn, width = u_ref.shape
    srows = sk_ref.shape[1] // nb
    nk = width // S5_KBLK
    npl = width // LANES

    @pl.when(n == 0)
    def _():
        sre_ref[...] = jnp.zeros_like(sre_ref)
        sim_ref[...] = jnp.zeros_like(sim_ref)

    def skew_rows(b):
        return slice(b * srows + b, b * srows + b + cn)

    for b in range(nb):
        ub = u_ref[b].astype(F32)
        for p in range(npl):
            sk_ref[p, skew_rows(b), :] = ub[:, p * LANES:(p + 1) * LANES]

    def to_time_major(t, carry):
        rows = pl.ds(pl.multiple_of(t * nb, nb), nb)
        for p in range(npl):
            utm_ref[rows, p * LANES:(p + 1) * LANES] = sk_ref[p, pl.ds(t, nb, stride=srows + 1), :]
        return carry

    lax.fori_loop(0, cn, to_time_major, 0, unroll=8)

    u = utm_ref[...].astype(BF16)
    ppc = S5_COLS // LANES
    for kb in range(nk):
        bu = jnp.dot(u[:, kb * S5_KBLK:(kb + 1) * S5_KBLK], b_ref[kb], preferred_element_type=F32)
        for j in range(ppc):
            xre(kb * ppc + j)[...] = bu[:, j * LANES:(j + 1) * LANES]
            xim(kb * ppc + j)[...] = bu[:, S5_COLS + j * LANES:S5_COLS + (j + 1) * LANES]

    pps = S5_SCAN_COLS // LANES
    for cb in range(sre_ref.shape[1] // S5_SCAN_COLS):
        cols = slice(cb * S5_SCAN_COLS, (cb + 1) * S5_SCAN_COLS)
        lr = jnp.broadcast_to(lam_ref[0:1, cols], (nb, S5_SCAN_COLS))
        li = jnp.broadcast_to(lam_ref[1:2, cols], (nb, S5_SCAN_COLS))

        def step(i, carry):
            xr, xi = carry
            t = i + d * (cn - 1 - 2 * i)
            rows = pl.ds(pl.multiple_of(t * nb, nb), nb)
            bur = jnp.concatenate([xre(cb * pps + j)[rows, :] for j in range(pps)], axis=1)
            bui = jnp.concatenate([xim(cb * pps + j)[rows, :] for j in range(pps)], axis=1)
            nxr = lr * xr - li * xi + bur
            nxi = lr * xi + li * xr + bui
            for j in range(pps):
                xre(cb * pps + j)[rows, :] = nxr[:, j * LANES:(j + 1) * LANES]
                xim(cb * pps + j)[rows, :] = nxi[:, j * LANES:(j + 1) * LANES]
            return nxr, nxi

        xr, xi = lax.fori_loop(0, cn, step, (sre_ref[:, cols], sim_ref[:, cols]), unroll=True)
        sre_ref[:, cols] = xr
        sim_ref[:, cols] = xi

    for kb in range(nk):
        xr = jnp.concatenate([xre(kb * ppc + j)[...] for j in range(ppc)], axis=1).astype(BF16)
        xi = jnp.concatenate([xim(kb * ppc + j)[...] for j in range(ppc)], axis=1).astype(BF16)
        y = jnp.dot(xr, c_ref[kb, :S5_COLS, :], preferred_element_type=F32)
        ytm_ref[kb] = y + jnp.dot(xi, c_ref[kb, S5_COLS:, :], preferred_element_type=F32)

    def to_batch_major(t, carry):
        rows = pl.ds(pl.multiple_of(t * nb, nb), nb)
        for p in range(npl):
            sk_ref[p, pl.ds(t, nb, stride=srows + 1), :] = ytm_ref[p, rows, :]
        return carry

    lax.fori_loop(0, cn, to_batch_major, 0, unroll=8)

    for b in range(nb):
        for p in range(npl):
            y_ref[b, :, p * LANES:(p + 1) * LANES] = sk_ref[p, skew_rows(b), :]


def s5_scan(z3, lam, bmat, cmat):
    nb, seq, zw = z3.shape
    width = bmat.shape[1] * bmat.shape[2]
    nstate = lam.shape[2]
    cn = min(S5_CHUNK, seq)
    nch = seq // cn
    srows = cn + SUBLANES
    ucol = zw // width - 1

    def chunk(d, n):
        return n + d * (nch - 1 - 2 * n)

    return pl.pallas_call(
        _s5_kernel,
        grid=(2, nch),
        in_specs=[
            pl.BlockSpec((nb, cn, width), lambda d, n: (0, chunk(d, n), ucol)),
            pl.BlockSpec((None, 2, nstate), lambda d, n: (d, 0, 0)),
            pl.BlockSpec((None,) + bmat.shape[1:], lambda d, n: (d, 0, 0, 0)),
            pl.BlockSpec((None,) + cmat.shape[1:], lambda d, n: (d, 0, 0, 0)),
        ],
        out_specs=pl.BlockSpec((None, nb, cn, width), lambda d, n: (d, 0, chunk(d, n), 0)),
        out_shape=jax.ShapeDtypeStruct((2, nb, seq, width), F32),
        scratch_shapes=[
            pltpu.VMEM((width // LANES, nb * srows, LANES), F32),
            pltpu.VMEM((cn * nb, width), F32),
            pltpu.VMEM((width // LANES, cn * nb, LANES), F32),
            pltpu.VMEM((nb, nstate), F32),
            pltpu.VMEM((nb, nstate), F32),
        ] + [pltpu.VMEM((S5_COLS // LANES, cn * nb, LANES), F32)] * (2 * (nstate // S5_COLS)),
        compiler_params=_params(("arbitrary", "arbitrary")),
        name="s5_scan",
    )(z3, lam, bmat, cmat)


def _s5_discretize(a_re, a_im, log_step, b_re, b_im, c_re, c_im):
    ng, npst = a_re.shape[1], a_re.shape[2]
    gpb = S5_KBLK // S5_GROUP_CH
    nk = ng // gpb
    eye = jnp.eye(gpb, dtype=F32)
    bre, bim = b_re.astype(F32), b_im.astype(F32)
    lams, bmats, cmats = [], [], []
    for direction in range(2):
        ar = a_re[direction].astype(F32)
        ai = a_im[direction].astype(F32)
        dt = jnp.exp(log_step[direction].astype(F32))[:, None]
        mag = jnp.exp(ar * dt)
        lam_re, lam_im = mag * jnp.cos(ai * dt), mag * jnp.sin(ai * dt)
        nr, ni = lam_re - 1.0, lam_im
        den = ar * ar + ai * ai
        coef_re = (nr * ar + ni * ai) / den
        coef_im = (ni * ar - nr * ai) / den
        bbar_re = coef_re[..., None] * bre - coef_im[..., None] * bim
        bbar_im = coef_re[..., None] * bim + coef_im[..., None] * bre

        def in_blocks(m):
            m4 = m.reshape(nk, gpb, npst, S5_GROUP_CH)
            return jnp.einsum('kgpc,gh->kgchp', m4, eye).reshape(nk, S5_KBLK, gpb * npst)

        def out_blocks(m):
            m4 = m.reshape(nk, gpb, S5_GROUP_CH, npst)
            return jnp.einsum('kgcp,gh->kgphc', m4, eye).reshape(nk, gpb * npst, S5_KBLK)

        lams.append(jnp.stack([lam_re.reshape(-1), lam_im.reshape(-1)]))
        bmats.append(jnp.concatenate([in_blocks(bbar_re), in_blocks(bbar_im)], axis=2))
        cmats.append(jnp.concatenate([out_blocks(c_re[direction].astype(F32)),
                                      -out_blocks(c_im[direction].astype(F32))], axis=1))
    return jnp.stack(lams), jnp.stack(bmats).astype(BF16), jnp.stack(cmats).astype(BF16)


def _s5_glu(yf_ref, yb_ref, u_ref, d_ref, w_ref, b_ref):
    y = u_ref[...].astype(F32) * d_ref[...] + yf_ref[...] + yb_ref[...]
    y = jax.nn.gelu(y)
    gate = jax.nn.sigmoid(jnp.dot(y.astype(BF16), w_ref[...], preferred_element_type=F32) + b_ref[...])
    return (y * gate).astype(BF16)


def _proj_kernel(*refs, s5_glu):
    if s5_glu:
        yr_ref, yf_ref, yb_ref, u_ref, d_ref, wg_ref, bg_ref = refs[:7]
        w_ref, x_ref, g_ref, rh_ref, rl_ref, rb_ref, x1_ref, h_ref, lt_ref = refs[7:]
        half = yr_ref.shape[1]
        mix = jnp.dot(yr_ref[...], w_ref[:half, :], preferred_element_type=F32)
        mix = mix + jnp.dot(_s5_glu(yf_ref, yb_ref, u_ref, d_ref, wg_ref, bg_ref), w_ref[half:, :],
                            preferred_element_type=F32)
    else:
        a_ref, w_ref, x_ref, g_ref, rh_ref, rl_ref, rb_ref, x1_ref, h_ref, lt_ref = refs
        mix = jnp.dot(a_ref[...], w_ref[...], preferred_element_type=F32)
    x1 = x_ref[...] + mix
    x1_ref[...] = x1
    h = _rms(x1, g_ref[...])
    h_hi = h.astype(BF16)
    h_ref[...] = h_hi.reshape(h_ref.shape)
    h_lo = (h - h_hi.astype(F32)).astype(BF16)
    lt = lax.dot_general(rh_ref[...], h_hi, NT_DIMS, preferred_element_type=F32)
    lt = lt + lax.dot_general(rh_ref[...], h_lo, NT_DIMS, preferred_element_type=F32)
    lt = lt + lax.dot_general(rl_ref[...], h_hi, NT_DIMS, preferred_element_type=F32)
    lt_ref[...] = lt + rb_ref[...]


def proj_norm_router(mix_in, w, x, g, r_hi, r_lo, r_bias, tm=DENSE_ROWS):
    t, d = x.shape
    tm = min(tm, t)
    k = w.shape[0]
    nr = r_hi.shape[0]
    s5_glu = isinstance(mix_in, tuple)
    if s5_glu:
        y_ret, ys5, z, d_skip, w_glu, b_glu = mix_in
        half = y_ret.shape[1]
        ucol = z.shape[1] // half - 1
        lead_specs = [
            pl.BlockSpec((tm, half), lambda i: (i, 0)),
            pl.BlockSpec((None, tm, half), lambda i: (0, i, 0)),
            pl.BlockSpec((None, tm, half), lambda i: (1, i, 0)),
            pl.BlockSpec((tm, half), lambda i: (i, ucol)),
            pl.BlockSpec((1, half), lambda i: (0, 0)),
            pl.BlockSpec((half, half), lambda i: (0, 0)),
            pl.BlockSpec((1, half), lambda i: (0, 0)),
        ]
        lead_args = (y_ret, ys5, ys5, z, d_skip.reshape(1, half), w_glu.astype(BF16),
                     b_glu.reshape(1, half))
    else:
        lead_specs = [pl.BlockSpec((tm, k), lambda i: (i, 0))]
        lead_args = (mix_in,)
    return pl.pallas_call(
        functools.partial(_proj_kernel, s5_glu=s5_glu),
        grid=(t // tm,),
        in_specs=lead_specs + [
            pl.BlockSpec((k, d), lambda i: (0, 0)),
            pl.BlockSpec((tm, d), lambda i: (i, 0)),
            pl.BlockSpec((1, d), lambda i: (0, 0)),
            pl.BlockSpec((nr, d), lambda i: (0, 0)),
            pl.BlockSpec((nr, d), lambda i: (0, 0)),
            pl.BlockSpec((nr, 1), lambda i: (0, 0)),
        ],
        out_specs=[
            pl.BlockSpec((tm, d), lambda i: (i, 0)),
            pl.BlockSpec((tm, d // LANES, LANES), lambda i: (i, 0, 0)),
            pl.BlockSpec((nr, tm), lambda i: (0, i)),
        ],
        out_shape=[
            jax.ShapeDtypeStruct((t, d), F32),
            jax.ShapeDtypeStruct((t, d // LANES, LANES), BF16),
            jax.ShapeDtypeStruct((nr, t), F32),
        ],
        compiler_params=_params(("parallel",)),
        name="proj_norm_router",
    )(*lead_args, w.astype(BF16), x, g.reshape(1, d), r_hi, r_lo, r_bias)


def _router_operands(w_group, b_group, w_er, b_er):
    d = w_group.shape[0]
    wt = jnp.concatenate([
        jnp.transpose(w_er.astype(F32), (0, 2, 1)).reshape(MOE_EXPERTS, d),
        jnp.transpose(w_group.astype(F32)),
        jnp.zeros((ROUTER_ROWS - MOE_EXPERTS - MOE_GROUPS, d), F32)], axis=0)
    bias = jnp.concatenate([
        b_er.astype(F32).reshape(-1), b_group.astype(F32),
        jnp.zeros((ROUTER_ROWS - MOE_EXPERTS - MOE_GROUPS,), F32)]).reshape(ROUTER_ROWS, 1)
    hi = wt.astype(BF16)
    lo = (wt - hi.astype(F32)).astype(BF16)
    return hi, lo, bias


def _route_kernel(lt_ref, eid_ref, gate_ref, rank_ref, cnt_ref, run_ref):
    i = pl.program_id(0)
    tm = lt_ref.shape[1]
    ne, npg, ng = MOE_EXPERTS, MOE_EXPERTS_PER_GROUP, MOE_GROUPS

    @pl.when(i == 0)
    def _():
        run_ref[...] = jnp.zeros_like(run_ref)

    gl = lt_ref[ne:ne + ng, :]
    gmax = jnp.max(gl, axis=0, keepdims=True)
    gidx = lax.broadcasted_iota(I32, (ng, tm), 0)
    gsel = jnp.min(jnp.where(gl == gmax, gidx, ng), axis=0, keepdims=True)
    p_g = 1.0 / jnp.sum(jnp.exp(gl - gmax), axis=0, keepdims=True)

    e8 = lt_ref[(ng - 1) * npg:ng * npg, :]
    for g in range(ng - 2, -1, -1):
        e8 = jnp.where(gsel == g, lt_ref[g * npg:(g + 1) * npg, :], e8)
    eidx = lax.broadcasted_iota(I32, (npg, tm), 0)
    m1 = jnp.max(e8, axis=0, keepdims=True)
    i1 = jnp.min(jnp.where(e8 == m1, eidx, npg), axis=0, keepdims=True)
    e8b = jnp.where(eidx == i1, -jnp.inf, e8)
    m2 = jnp.max(e8b, axis=0, keepdims=True)
    i2 = jnp.min(jnp.where(e8b == m2, eidx, npg), axis=0, keepdims=True)
    t2 = jnp.exp(m2 - m1)
    den = 1.0 + t2
    gate_ref[0:1, :] = (1.0 / den) * p_g
    gate_ref[1:2, :] = (t2 / den) * p_g
    id1 = gsel * npg + i1
    id2 = gsel * npg + i2
    eid_ref[0:1, :] = id1
    eid_ref[1:2, :] = id2

    rows = lax.broadcasted_iota(I32, (ne, tm), 0)
    oh1 = rows == id1
    oh2 = rows == id2
    both = jnp.where(oh1, 1.0, 0.0) + jnp.where(oh2, 1.0, 0.0)
    earlier = (lax.broadcasted_iota(I32, (tm, tm), 0) < lax.broadcasted_iota(I32, (tm, tm), 1))
    prefix = jnp.dot(both.astype(BF16), jnp.where(earlier, 1.0, 0.0).astype(BF16),
                     preferred_element_type=F32)
    base = prefix + run_ref[:, 0:1]
    rank_ref[0:1, :] = jnp.sum(jnp.where(oh1, base, 0.0), axis=0, keepdims=True).astype(I32)
    rank_ref[1:2, :] = jnp.sum(jnp.where(oh2, base, 0.0), axis=0, keepdims=True).astype(I32)
    run = run_ref[...] + jnp.sum(both, axis=1, keepdims=True)
    run_ref[...] = run
    cnt_ref[...] = run.astype(I32)


def route(lt, tm=512):
    nr, t = lt.shape
    two = lambda dt: jax.ShapeDtypeStruct((2, t), dt)
    return pl.pallas_call(
        _route_kernel,
        grid=(t // tm,),
        in_specs=[pl.BlockSpec((nr, tm), lambda i: (0, i))],
        out_specs=[
            pl.BlockSpec((2, tm), lambda i: (0, i)),
            pl.BlockSpec((2, tm), lambda i: (0, i)),
            pl.BlockSpec((2, tm), lambda i: (0, i)),
            pl.BlockSpec((MOE_EXPERTS, LANES), lambda i: (0, 0)),
        ],
        out_shape=[two(I32), two(F32), two(I32),
                   jax.ShapeDtypeStruct((MOE_EXPERTS, LANES), I32)],
        scratch_shapes=[pltpu.VMEM((MOE_EXPERTS, LANES), F32)],
        compiler_params=_params(("arbitrary",)),
        name="route",
    )(lt)


def _dispatch_kernel(pad_ref, nu_ref, d0_ref, d1_ref, h_ref, xbuf_ref, zero_ref, sem, zsem):
    i = pl.program_id(0)
    tm = h_ref.shape[0]
    dests = (d0_ref, d1_ref)
    bm = zero_ref.shape[0]
    nblk = xbuf_ref.shape[0] // bm

    def zero_fill(act):
        def per_expert(e, carry):
            pos = pad_ref[0, e]
            length = pad_ref[1, e]
            p = bm // 2
            while p >= 1:
                bit = (length & p) != 0

                @pl.when(bit)
                def _(pos=pos, p=p):
                    act(pltpu.make_async_copy(zero_ref.at[pl.ds(0, p)],
                                              xbuf_ref.at[pl.ds(pos, p)], zsem))

                pos = pos + jnp.where(bit, p, 0)
                p //= 2
            return carry

        lax.fori_loop(0, pad_ref.shape[1], per_expert, 0)

        def per_block(b, carry):
            act(pltpu.make_async_copy(zero_ref, xbuf_ref.at[pl.ds(b * bm, bm)], zsem))
            return carry

        lax.fori_loop(nu_ref[0], nblk, per_block, 0)

    @pl.when(i == 0)
    def _():
        zero_ref[...] = jnp.zeros_like(zero_ref)
        zero_fill(lambda c: c.start())

    def copy(r, k):
        return pltpu.make_async_copy(h_ref.at[r], xbuf_ref.at[dests[k][0, r]], sem)

    def start(r, carry):
        copy(r, 0).start(priority=0)
        copy(r, 1).start(priority=1)
        return carry

    lax.fori_loop(0, tm, start, 0, unroll=8)

    def wait(r, carry):
        copy(r, 0).wait()
        copy(r, 1).wait()
        return carry

    lax.fori_loop(0, tm, wait, 0, unroll=8)

    @pl.when(i == pl.num_programs(0) - 1)
    def _():
        zero_fill(lambda c: c.wait())


def _row_index_spec(tm, ahead=0, last=None):
    def index(i):
        return (i if ahead == 0 else jnp.minimum(i + ahead, last), 0, 0)
    return pl.BlockSpec((None, 1, tm), index, memory_space=pltpu.SMEM)


def dispatch(pad, n_used, dest0, dest1, h3, n_rows):
    t, s, lanes = h3.shape
    nt, _, tm = dest0.shape
    return pl.pallas_call(
        _dispatch_kernel,
        grid=(nt,),
        in_specs=[
            pl.BlockSpec(memory_space=pltpu.SMEM),
            pl.BlockSpec(memory_space=pltpu.SMEM),
            _row_index_spec(tm), _row_index_spec(tm),
            pl.BlockSpec((tm, s, lanes), lambda i: (i, 0, 0)),
        ],
        out_specs=pl.BlockSpec(memory_space=pl.ANY),
        out_shape=jax.ShapeDtypeStruct((n_rows, s, lanes), h3.dtype),
        scratch_shapes=[pltpu.VMEM((MOE_ROWS, s, lanes), h3.dtype),
                        pltpu.SemaphoreType.DMA(()), pltpu.SemaphoreType.DMA(())],
        compiler_params=_params(("arbitrary",)),
        name="moe_dispatch",
    )(pad, n_used, dest0, dest1, h3)


def _experts_kernel(be_ref, nu_ref, nx_ref, x_ref, wg_hbm, wu_hbm, wd_hbm, o_ref,
                    wg_f32, wu_f32, wd_f32, wgb, wub, wdb, slot_ref, sem, *, layer):
    i = pl.program_id(0)
    e = be_ref[i]
    first = i == 0
    changed = jnp.logical_or(first, e != be_ref[jnp.maximum(i - 1, 0)])
    streams = ((wg_hbm, wg_f32), (wu_hbm, wu_f32), (wd_hbm, wd_f32))

    def fetch(expert, slot):
        return [pltpu.make_async_copy(w_hbm.at[layer, expert], w_f32.at[slot], sem.at[slot, j])
                for j, (w_hbm, w_f32) in enumerate(streams)]

    @pl.when(first)
    def _():
        slot_ref[0] = 1
        for c in fetch(e, 0):
            c.start()

    @pl.when(changed)
    def _():
        slot = 1 - slot_ref[0]
        slot_ref[0] = slot
        for c in fetch(e, slot):
            c.wait()
        nxt = nx_ref[i]

        @pl.when(nxt >= 0)
        def _():
            for c in fetch(nxt, 1 - slot):
                c.start()

        wgb[...] = wg_f32[slot].astype(BF16)
        wub[...] = wu_f32[slot].astype(BF16)
        wdb[...] = wd_f32[slot].astype(BF16)

    @pl.when(i < nu_ref[0])
    def _():
        bm, s, lanes = x_ref.shape
        x = x_ref[...].reshape(bm, s * lanes)
        g = jnp.dot(x, wgb[...], preferred_element_type=F32)
        u = jnp.dot(x, wub[...], preferred_element_type=F32)
        a = ((g * jax.nn.sigmoid(g)) * u).astype(BF16)
        y = jnp.dot(a, wdb[...], preferred_element_type=F32)
        o_ref[...] = y.astype(BF16).reshape(o_ref.shape)

    @pl.when(i >= nu_ref[0])
    def _():
        o_ref[...] = jnp.zeros_like(o_ref)


def experts(layer, block_expert, n_used, next_expert, xbuf, w_gate, w_up, w_down):
    n_rows, s, lanes = xbuf.shape
    d = s * lanes
    hid = w_gate.shape[3]
    bm = MOE_ROWS
    nblk = n_rows // bm
    grid_spec = pltpu.PrefetchScalarGridSpec(
        num_scalar_prefetch=3,
        grid=(nblk,),
        in_specs=[
            pl.BlockSpec((bm, s, lanes), lambda i, be, nu, nx: (jnp.minimum(i, nu[0] - 1), 0, 0)),
            pl.BlockSpec(memory_space=pl.ANY),
            pl.BlockSpec(memory_space=pl.ANY),
            pl.BlockSpec(memory_space=pl.ANY),
        ],
        out_specs=pl.BlockSpec((bm, s, lanes), lambda i, be, nu, nx: (i, 0, 0)),
        scratch_shapes=[pltpu.VMEM((2, d, hid), F32), pltpu.VMEM((2, d, hid), F32),
                        pltpu.VMEM((2, hid, d), F32),
                        pltpu.VMEM((d, hid), BF16), pltpu.VMEM((d, hid), BF16),
                        pltpu.VMEM((hid, d), BF16),
                        pltpu.SMEM((1,), I32), pltpu.SemaphoreType.DMA((2, 3))],
    )
    return pl.pallas_call(
        functools.partial(_experts_kernel, layer=layer),
        grid_spec=grid_spec,
        out_shape=jax.ShapeDtypeStruct((n_rows, s, lanes), BF16),
        compiler_params=_params(("arbitrary",)),
        name="moe_experts",
    )(block_expert, n_used, next_expert, xbuf, w_gate, w_up, w_down)


def _combine_kernel(d0_ref, d1_ref, n0_ref, n1_ref, gate_ref, x_ref, g_ref, ybuf_ref, o_ref,
                    buf, sem, *, final_norm):
    i = pl.program_id(0)
    tm, d = x_ref.shape
    slot = i % 2

    def copy(dests, s, r, k):
        return pltpu.make_async_copy(ybuf_ref.at[dests[k][0, r]], buf.at[s, k, r], sem.at[s])

    def gather(dests, s):
        def start(r, carry):
            copy(dests, s, r, 0).start(priority=0)
            copy(dests, s, r, 1).start(priority=1)
            return carry

        lax.fori_loop(0, tm, start, 0, unroll=8)

    @pl.when(i == 0)
    def _():
        gather((d0_ref, d1_ref), 0)

    @pl.when(i + 1 < pl.num_programs(0))
    def _():
        gather((n0_ref, n1_ref), 1 - slot)

    def wait(r, carry):
        copy((d0_ref, d1_ref), slot, r, 0).wait()
        copy((d0_ref, d1_ref), slot, r, 1).wait()
        return carry

    lax.fori_loop(0, tm, wait, 0, unroll=8)

    gates = gate_ref[...]
    y = (gates[:, 0:1] * buf[slot, 0].reshape(tm, d).astype(F32)
         + gates[:, 1:2] * buf[slot, 1].reshape(tm, d).astype(F32))
    out = x_ref[...] + y
    if final_norm:
        out = _rms(out, g_ref[...])
    o_ref[...] = out


def combine(dest0, dest1, gates_t, x, ybuf, g_final, final_norm):
    t, d = x.shape
    nt, _, tm = dest0.shape
    _, s, lanes = ybuf.shape
    return pl.pallas_call(
        functools.partial(_combine_kernel, final_norm=final_norm),
        grid=(nt,),
        in_specs=[
            _row_index_spec(tm), _row_index_spec(tm),
            _row_index_spec(tm, ahead=1, last=nt - 1), _row_index_spec(tm, ahead=1, last=nt - 1),
            pl.BlockSpec((tm, 2), lambda i: (i, 0)),
            pl.BlockSpec((tm, d), lambda i: (i, 0)),
            pl.BlockSpec((1, d), lambda i: (0, 0)),
            pl.BlockSpec(memory_space=pl.ANY),
        ],
        out_specs=pl.BlockSpec((tm, d), lambda i: (i, 0)),
        out_shape=jax.ShapeDtypeStruct((t, d), F32),
        scratch_shapes=[pltpu.VMEM((2, 2, tm, s, lanes), ybuf.dtype),
                        pltpu.SemaphoreType.DMA((2,))],
        compiler_params=_params(("arbitrary",)),
        name="moe_combine",
    )(dest0, dest1, dest0, dest1, gates_t, x, g_final.reshape(1, d), ybuf)


def hier_moe_block(layer, x1, h, lt, w_gate, w_up, w_down, g_final, final_norm, tm=MOE_TOKENS):
    t, d = x1.shape
    bm = MOE_ROWS
    eid, gate, rank, cnt = route(lt)
    counts = cnt[:, 0]
    padded = ((counts + bm - 1) // bm) * bm
    pend = jnp.cumsum(padded)
    pstart = pend - padded
    experts_col = jnp.arange(MOE_EXPERTS, dtype=I32)[:, None, None]
    dest = rank + jnp.sum(jnp.where(eid[None] == experts_col, pstart[:, None, None], 0), axis=0)
    n_rows = 2 * t + MOE_EXPERTS * bm
    nblk = n_rows // bm
    n_used = (pend[-1] // bm).astype(I32)
    first_row = jnp.minimum(jnp.arange(nblk, dtype=I32), n_used - 1) * bm
    block_expert = jnp.sum(pend[None, :] <= first_row[:, None], axis=1).astype(I32)
    block_expert = jnp.minimum(block_expert, MOE_EXPERTS - 1)
    ids = jnp.arange(MOE_EXPERTS, dtype=I32)
    later = jnp.logical_and(ids[None, :] > block_expert[:, None], counts[None, :] > 0)
    next_expert = jnp.min(jnp.where(later, ids[None, :], MOE_EXPERTS), axis=1)
    next_expert = jnp.where(next_expert < MOE_EXPERTS, next_expert, -1).astype(I32)
    dest0 = dest[0].reshape(t // tm, 1, tm)
    dest1 = dest[1].reshape(t // tm, 1, tm)
    pad = jnp.stack([pstart + counts, padded - counts]).astype(I32)
    xbuf = dispatch(pad, n_used.reshape(1), dest0, dest1, h, n_rows)
    ybuf = experts(layer, block_expert, n_used.reshape(1), next_expert, xbuf, w_gate, w_up, w_down)
    return combine(dest0, dest1, jnp.transpose(gate), x1, ybuf, g_final, final_norm)


def _attn_kernel(sink_ref, q_ref, kp_ref, kc_ref, kn_ref, vp_ref, vc_ref, vn_ref, bias_ref, o_ref):
    n = pl.program_id(1)
    nblk = pl.num_programs(1)
    blk = q_ref.shape[0]
    dh = SWA_HEAD_DIM
    nkv = kc_ref.shape[1] // dh
    masked = bias_ref.shape[1] - 1
    part_prev = jnp.where(n > 0, 0, masked)
    part_next = jnp.where(n < nblk - 1, 2, masked)
    lo = lax.broadcasted_iota(I32, (blk, 2 * dh), 1) < dh
    top = lax.broadcasted_iota(I32, (2 * blk, 2 * dh), 0) < blk
    lo2 = lax.broadcasted_iota(I32, (2 * blk, 2 * dh), 1) < dh
    ones_ext = jnp.concatenate([jnp.where(lo, 1.0, 0.0), jnp.where(lo, 0.0, 1.0)], axis=0).astype(BF16)
    qscale = (dh ** -0.5) * LOG2E

    for kv in range(nkv):
        col = slice((kv // 2) * 2 * dh, (kv // 2 + 1) * 2 * dh)

        def extend(ref):
            x = ref[:, col].astype(F32)
            r = pltpu.roll(x, dh, 1)
            x_lo, x_hi = (x, r) if kv % 2 == 0 else (r, x)
            return jnp.concatenate([jnp.where(lo, x_lo, 0.0), jnp.where(lo, 0.0, x_hi)],
                                   axis=0).astype(BF16)

        q2 = jnp.concatenate([q_ref[:, (2 * kv) * 2 * dh:(2 * kv + 1) * 2 * dh],
                              q_ref[:, (2 * kv + 1) * 2 * dh:(2 * kv + 2) * 2 * dh]], axis=0)
        q2 = (q2.astype(F32) * qscale).astype(BF16)

        def scores(k_ref, part):
            return lax.dot_general(q2, extend(k_ref), NT_DIMS,
                                   preferred_element_type=F32) + bias_ref[kv, part]

        s = [scores(kp_ref, part_prev), scores(kc_ref, 1), scores(kn_ref, part_next)]
        mx = jnp.maximum(jnp.maximum(s[0], s[1]), s[2])
        sk = [jnp.where(top, sink_ref[0, 4 * kv + par], sink_ref[0, 4 * kv + 2 + par])
              for par in range(2)]
        m = [jnp.maximum(jnp.broadcast_to(jnp.max(mx[:, par * blk:(par + 1) * blk], axis=-1,
                                                  keepdims=True), (2 * blk, 2 * dh)), sk[par])
             for par in range(2)]
        acc = jnp.zeros((2 * blk, 4 * dh), F32)
        for sp, v_ref in zip(s, (vp_ref, vc_ref, vn_ref)):
            e = jnp.concatenate([jnp.exp2(sp[:, :blk] - m[0]), jnp.exp2(sp[:, blk:] - m[1])],
                                axis=1).astype(BF16)
            rhs = jnp.concatenate([extend(v_ref), ones_ext], axis=1)
            acc = acc + jnp.dot(e, rhs, preferred_element_type=F32)
        den = acc[:, 2 * dh:] + jnp.exp2(jnp.where(lo2, sk[0] - m[0], sk[1] - m[1]))
        o = (acc[:, :2 * dh] / den).astype(BF16)
        o_ref[:, (2 * kv) * 2 * dh:(2 * kv + 1) * 2 * dh] = o[:blk]
        o_ref[:, (2 * kv + 1) * 2 * dh:(2 * kv + 2) * 2 * dh] = o[blk:]


def banded_attention(qkv3, sink, bias):
    b, seq, width = qkv3.shape
    nh = sink.shape[1]
    qd = nh * SWA_HEAD_DIM
    kvd = (width - qd) // 2
    blk = SWA_BLOCK
    nblk = seq // blk
    kcol, vcol = qd // kvd, qd // kvd + 1

    def band(col, off):
        return pl.BlockSpec((None, blk, kvd),
                            lambda bi, n: (bi, jnp.clip(n + off, 0, nblk - 1), col))

    return pl.pallas_call(
        _attn_kernel,
        grid=(b, nblk),
        in_specs=[
            pl.BlockSpec(memory_space=pltpu.SMEM),
            pl.BlockSpec((None, blk, qd), lambda bi, n: (bi, n, 0)),
            band(kcol, -1), band(kcol, 0), band(kcol, 1),
            band(vcol, -1), band(vcol, 0), band(vcol, 1),
            pl.BlockSpec(bias.shape, lambda bi, n: (0, 0, 0, 0)),
        ],
        out_specs=pl.BlockSpec((None, blk, qd), lambda bi, n: (bi, n, 0)),
        out_shape=jax.ShapeDtypeStruct((b, seq, qd), BF16),
        compiler_params=_params(("parallel", "parallel")),
        name="banded_attention",
    )(sink, qkv3, qkv3, qkv3, qkv3, qkv3, qkv3, qkv3, bias)


def _t5_bucket(rel):
    half = REL_BUCKETS // 2
    max_exact = half // 2
    n = jnp.abs(rel)
    large = max_exact + (jnp.log(jnp.maximum(n, 1).astype(F32) / max_exact)
                         / math.log(REL_MAX_DIST / max_exact) * (half - max_exact)).astype(I32)
    large = jnp.minimum(large, half - 1)
    return jnp.where(rel > 0, half, 0) + jnp.where(n < max_exact, n, large)


def _attention_bias(rel_bias):
    blk = SWA_BLOCK
    nh = rel_bias.shape[1]
    nkv = nh // SWA_Q_PER_KV
    rel = jnp.arange(3 * blk)[None, :] - blk - jnp.arange(blk)[:, None]
    onehot = (_t5_bucket(rel)[..., None] == jnp.arange(REL_BUCKETS)).astype(F32)
    bias = jnp.einsum('ijb,bh->hij', onehot, rel_bias.astype(F32), precision=lax.Precision.HIGHEST)
    bias = jnp.where((jnp.abs(rel) <= SWA_WINDOW)[None], bias, NEG_INF) * LOG2E
    tiles = bias.reshape(nkv, 2, 2, blk, 3, blk).transpose(0, 4, 1, 3, 2, 5)
    tiles = tiles.reshape(nkv, 3, 2 * blk, 2 * blk)
    masked = jnp.full((nkv, 1, 2 * blk, 2 * blk), NEG_INF * LOG2E, F32)
    return jnp.concatenate([tiles, masked], axis=1)


def _rotary_tables(seq, dim):
    inv_freq = ROPE_BASE ** (-jnp.arange(0, dim, 2, dtype=F32) / dim)
    ang = jnp.arange(seq, dtype=F32)[:, None] * inv_freq[None, :]
    cos, sin = jnp.cos(ang), jnp.sin(ang)
    return jnp.concatenate([cos, cos], axis=1), jnp.concatenate([-sin, sin], axis=1)


def kernel(x, norm_mix_g, norm_ffn_g, norm_final_g, hyb_w_in, ret_decay_logit, s5_a_re, s5_a_im, s5_log_step, s5_b_re, s5_b_im, s5_c_re, s5_c_im, s5_d, s5_w_glu, s5_b_glu, hyb_w_out, swa_w_qkv, swa_sink, swa_w_o, rel_bias, moe_w_group, moe_b_group, moe_w_expert_router, moe_b_expert_router, moe_w_gate, moe_w_up, moe_w_down):
    nb, seq, d = x.shape
    t = nb * seq
    depth = norm_mix_g.shape[0]
    xt = x.reshape(t, d)
    for layer in range(depth):
        i = layer // 2
        if layer % 2 == 0:
            w = hyb_w_out.shape[1] // 2
            z = norm_matmul(xt, norm_mix_g[layer], hyb_w_in[i])
            z3 = z.reshape(nb, seq, z.shape[1])
            cos, sin = _rotary_tables(seq, w // RET_HEADS)
            log_gamma = jax.nn.log_sigmoid(ret_decay_logit[i].astype(F32))
            y_ret = retention(z3, log_gamma, cos, sin).reshape(t, w)
            lam, bmat, cmat = _s5_discretize(s5_a_re[i], s5_a_im[i], s5_log_step[i], s5_b_re[i],
                                             s5_b_im[i], s5_c_re[i], s5_c_im[i])
            ys5 = s5_scan(z3, lam, bmat, cmat).reshape(2, t, w)
            mix_in = (y_ret, ys5, z, s5_d[i].astype(F32).reshape(-1), s5_w_glu[i],
                      s5_b_glu[i].astype(F32))
            w_mix = hyb_w_out[i]
        else:
            qkv = norm_matmul(xt, norm_mix_g[layer], swa_w_qkv[i])
            mix_in = banded_attention(qkv.reshape(nb, seq, qkv.shape[1]),
                                      swa_sink[i].astype(F32).reshape(1, -1) * LOG2E,
                                      _attention_bias(rel_bias)).reshape(t, -1)
            w_mix = swa_w_o[i]
        r_hi, r_lo, r_bias = _router_operands(moe_w_group[layer], moe_b_group[layer],
                                              moe_w_expert_router[layer], moe_b_expert_router[layer])
        x1, h, lt = proj_norm_router(mix_in, w_mix, xt, norm_ffn_g[layer], r_hi, r_lo, r_bias)
        last = layer == depth - 1
        xt = hier_moe_block(layer, x1, h, lt, moe_w_gate, moe_w_up, moe_w_down,
                            norm_final_g, final_norm=last)
    return xt.reshape(nb, seq, d)
```

```python
import functools
import math

import jax
import jax.numpy as jnp
from jax import lax
from jax.experimental import pallas as pl
from jax.experimental.pallas import tpu as pltpu

F32 = jnp.float32
BF16 = jnp.bfloat16
I32 = jnp.int32

RET_HEADS = 4
RET_CHUNK = 128
S5_GROUP_CH = 16
S5_STATE = 64
SWA_HEAD_DIM = 64
SWA_Q_PER_KV = 4
SWA_WINDOW = 128
SWA_BLOCK = 128
REL_BUCKETS = 32
REL_MAX_DIST = 128
MOE_GROUPS = 4
MOE_EXPERTS_PER_GROUP = 8
MOE_EXPERTS = MOE_GROUPS * MOE_EXPERTS_PER_GROUP
ROPE_BASE = 10000.0
RMS_EPS = 1e-6
GN_EPS = 1e-5
NEG_INF = -1e30
LOG2E = 1.4426950408889634

LANES = 128
SUBLANES = 8
V7X_VMEM_BYTES = 64 * 1024 * 1024
VMEM_LIMIT = V7X_VMEM_BYTES - 8 * 1024 * 1024

DENSE_ROWS = 1024
MOE_ROWS = 512
MOE_TOKENS = 512
S5_CHUNK = 128
S5_COLS = 512
S5_SCAN_COLS = 512
S5_KBLK = 128
ROUTER_ROWS = 128

NT_DIMS = (((1,), (1,)), ((), ()))
TN_DIMS = (((0,), (0,)), ((), ()))


def _params(semantics):
    return pltpu.CompilerParams(dimension_semantics=semantics, vmem_limit_bytes=VMEM_LIMIT)


def _rms(x, g):
    ms = jnp.mean(x * x, axis=-1, keepdims=True)
    return (x * lax.rsqrt(ms + RMS_EPS)) * g


def _norm_matmul_kernel(x_ref, g_ref, w_ref, o_ref):
    h = _rms(x_ref[...], g_ref[...])
    o_ref[...] = jnp.dot(h.astype(BF16), w_ref[...],
                         preferred_element_type=F32).astype(o_ref.dtype)


def norm_matmul(x, g, w, tm=DENSE_ROWS):
    t, d = x.shape
    tm = min(tm, t)
    n = w.shape[1]
    return pl.pallas_call(
        _norm_matmul_kernel,
        grid=(t // tm,),
        in_specs=[
            pl.BlockSpec((tm, d), lambda i: (i, 0)),
            pl.BlockSpec((1, d), lambda i: (0, 0)),
            pl.BlockSpec((d, n), lambda i: (0, 0)),
        ],
        out_specs=pl.BlockSpec((tm, n), lambda i: (i, 0)),
        out_shape=jax.ShapeDtypeStruct((t, n), BF16),
        compiler_params=_params(("parallel",)),
        name="norm_matmul",
    )(x, g.reshape(1, d), w.astype(BF16))


def _retention_kernel(lg_ref, q_ref, k_ref, v_ref, g_ref, cos_ref, sin_ref, o_ref,
                      qr_ref, kr_ref, inc_ref, st_ref, lhs_ref):
    h = pl.program_id(1)
    lg_f = lg_ref[0, h]
    lg_b = lg_ref[1, h]
    seq, dk = q_ref.shape
    c = RET_CHUNK
    nc = seq // c

    cos = cos_ref[...]
    sin = sin_ref[...]
    swap = (lax.broadcasted_iota(I32, (dk, dk), 0)
            == (lax.broadcasted_iota(I32, (dk, dk), 1) + dk // 2) % dk)
    swap = jnp.where(swap, 1.0, 0.0).astype(BF16)
    q = q_ref[...]
    qr_ref[...] = (q.astype(F32) * cos
                   + jnp.dot(q, swap, preferred_element_type=F32) * sin)
    k = k_ref[...]
    kr_ref[...] = (k.astype(F32) * cos
                   + jnp.dot(k, swap, preferred_element_type=F32) * sin) * (dk ** -0.5)

    pos = lax.broadcasted_iota(I32, (c, dk), 0).astype(F32)
    kf_scale = jnp.exp((c - 1.0 - pos) * lg_f)
    qf_scale = jnp.exp((pos + 1.0) * lg_f)
    kb_scale = jnp.exp(pos * lg_b)
    qb_scale = jnp.exp((c - pos) * lg_b)
    rel = (lax.broadcasted_iota(I32, (c, c), 0) - lax.broadcasted_iota(I32, (c, c), 1)).astype(F32)
    mask = jnp.exp(jnp.abs(rel) * jnp.where(rel >= 0, lg_f, lg_b))
    dec_f = jnp.exp(jnp.full((dk, dk), c * lg_f, F32))
    dec_b = jnp.exp(jnp.full((dk, dk), c * lg_b, F32))

    def increments(n, carry):
        rows = pl.ds(pl.multiple_of(n * c, c), c)
        kc = kr_ref[rows, :]
        kk = jnp.concatenate([kc * kf_scale, kc * kb_scale], axis=1).astype(BF16)
        inc_ref[n] = lax.dot_general(kk, v_ref[rows, :], TN_DIMS,
                                     preferred_element_type=F32)
        return carry

    lax.fori_loop(0, nc, increments, 0, unroll=True)

    def fwd(n, state):
        st_ref[n, :dk, :] = state.astype(BF16)
        return state * dec_f + inc_ref[n, :dk, :]

    lax.fori_loop(0, nc, fwd, jnp.zeros((dk, dk), F32), unroll=True)

    def bwd(i, state):
        n = nc - 1 - i
        st_ref[n, dk:, :] = state.astype(BF16)
        return state * dec_b + inc_ref[n, dk:, :]

    lax.fori_loop(0, nc, bwd, jnp.zeros((dk, dk), F32), unroll=True)

    def operands(n, carry):
        rows = pl.ds(pl.multiple_of(n * c, c), c)
        qc = qr_ref[rows, :]
        s = lax.dot_general(qc.astype(BF16), kr_ref[rows, :].astype(BF16), NT_DIMS,
                            preferred_element_type=F32) * mask
        lhs_ref[n] = jnp.concatenate([s, qc * qf_scale, qc * qb_scale], axis=1).astype(BF16)
        return carry

    lax.fori_loop(0, nc, operands, 0, unroll=True)

    def outputs(n, carry):
        rows = pl.ds(pl.multiple_of(n * c, c), c)
        rhs = jnp.concatenate([v_ref[rows, :], st_ref[n]], axis=0)
        kr_ref[rows, :] = jnp.dot(lhs_ref[n], rhs, preferred_element_type=F32)
        return carry

    lax.fori_loop(0, nc, outputs, 0, unroll=True)

    out = kr_ref[...]
    mu = jnp.mean(out, axis=-1, keepdims=True)
    cen = out - mu
    var = jnp.mean(cen * cen, axis=-1, keepdims=True)
    g = g_ref[...].astype(F32)
    o_ref[...] = ((g * jax.nn.sigmoid(g)) * (cen * lax.rsqrt(var + GN_EPS))).astype(BF16)


def retention(z3, log_gamma, cos, sin):
    b, seq, _ = z3.shape
    nh = RET_HEADS
    dk = cos.shape[1]

    def col(off):
        return pl.BlockSpec((None, seq, dk), lambda bi, hi: (bi, 0, off + hi))

    return pl.pallas_call(
        _retention_kernel,
        grid=(b, nh),
        in_specs=[
            pl.BlockSpec(memory_space=pltpu.SMEM),
            col(0), col(nh), col(2 * nh), col(3 * nh),
            pl.BlockSpec((seq, dk), lambda bi, hi: (0, 0)),
            pl.BlockSpec((seq, dk), lambda bi, hi: (0, 0)),
        ],
        out_specs=pl.BlockSpec((None, seq, dk), lambda bi, hi: (bi, 0, hi)),
        out_shape=jax.ShapeDtypeStruct((b, seq, nh * dk), BF16),
        scratch_shapes=[pltpu.VMEM((seq, dk), F32), pltpu.VMEM((seq, dk), F32),
                        pltpu.VMEM((seq // RET_CHUNK, 2 * dk, dk), F32),
                        pltpu.VMEM((seq // RET_CHUNK, 2 * dk, dk), BF16),
                        pltpu.VMEM((seq // RET_CHUNK, RET_CHUNK, 3 * dk), BF16)],
        compiler_params=_params(("parallel", "parallel")),
        name="retention",
    )(log_gamma, z3, z3, z3, z3, cos, sin)


def _s5_kernel(u_ref, lam_ref, b_ref, c_ref, y_ref, sk_ref, utm_ref, ytm_ref, sre_ref, sim_ref,
               *hist):
    nkb = len(hist) // 2

    def xre(plane):
        return hist[plane // (S5_COLS // LANES)].at[plane % (S5_COLS // LANES)]

    def xim(plane):
        return hist[nkb + plane // (S5_COLS // LANES)].at[plane % (S5_COLS // LANES)]

    d = pl.program_id(0)
    n = pl.program_id(1)
    nb, cn, width = u_ref.shape
    srows = sk_ref.shape[1] // nb
    nk = width // S5_KBLK
    npl = width // LANES

    @pl.when(n == 0)
    def _():
        sre_ref[...] = jnp.zeros_like(sre_ref)
        sim_ref[...] = jnp.zeros_like(sim_ref)

    def skew_rows(b):
        return slice(b * srows + b, b * srows + b + cn)

    for b in range(nb):
        ub = u_ref[b].astype(F32)
        for p in range(npl):
            sk_ref[p, skew_rows(b), :] = ub[:, p * LANES:(p + 1) * LANES]

    def to_time_major(t, carry):
        rows = pl.ds(pl.multiple_of(t * nb, nb), nb)
        for p in range(npl):
            utm_ref[rows, p * LANES:(p + 1) * LANES] = sk_ref[p, pl.ds(t, nb, stride=srows + 1), :]
        return carry

    lax.fori_loop(0, cn, to_time_major, 0, unroll=8)

    u = utm_ref[...].astype(BF16)
    ppc = S5_COLS // LANES
    for kb in range(nk):
        bu = jnp.dot(u[:, kb * S5_KBLK:(kb + 1) * S5_KBLK], b_ref[kb], preferred_element_type=F32)
        for j in range(ppc):
            xre(kb * ppc + j)[...] = bu[:, j * LANES:(j + 1) * LANES]
            xim(kb * ppc + j)[...] = bu[:, S5_COLS + j * LANES:S5_COLS + (j + 1) * LANES]

    pps = S5_SCAN_COLS // LANES
    for cb in range(sre_ref.shape[1] // S5_SCAN_COLS):
        cols = slice(cb * S5_SCAN_COLS, (cb + 1) * S5_SCAN_COLS)
        lr = jnp.broadcast_to(lam_ref[0:1, cols], (nb, S5_SCAN_COLS))
        li = jnp.broadcast_to(lam_ref[1:2, cols], (nb, S5_SCAN_COLS))

        def step(i, carry):
            xr, xi = carry
            t = i + d * (cn - 1 - 2 * i)
            rows = pl.ds(pl.multiple_of(t * nb, nb), nb)
            bur = jnp.concatenate([xre(cb * pps + j)[rows, :] for j in range(pps)], axis=1)
            bui = jnp.concatenate([xim(cb * pps + j)[rows, :] for j in range(pps)], axis=1)
            nxr = lr * xr - li * xi + bur
            nxi = lr * xi + li * xr + bui
            for j in range(pps):
                xre(cb * pps + j)[rows, :] = nxr[:, j * LANES:(j + 1) * LANES]
                xim(cb * pps + j)[rows, :] = nxi[:, j * LANES:(j + 1) * LANES]
            return nxr, nxi

        xr, xi = lax.fori_loop(0, cn, step, (sre_ref[:, cols], sim_ref[:, cols]), unroll=True)
        sre_ref[:, cols] = xr
        sim_ref[:, cols] = xi

    for kb in range(nk):
        xr = jnp.concatenate([xre(kb * ppc + j)[...] for j in range(ppc)], axis=1).astype(BF16)
        xi = jnp.concatenate([xim(kb * ppc + j)[...] for j in range(ppc)], axis=1).astype(BF16)
        y = jnp.dot(xr, c_ref[kb, :S5_COLS, :], preferred_element_type=F32)
        ytm_ref[kb] = y + jnp.dot(xi, c_ref[kb, S5_COLS:, :], preferred_element_type=F32)

    def to_batch_major(t, carry):
        rows = pl.ds(pl.multiple_of(t * nb, nb), nb)
        for p in range(npl):
            sk_ref[p, pl.ds(t, nb, stride=srows + 1), :] = ytm_ref[p, rows, :]
        return carry

    lax.fori_loop(0, cn, to_batch_major, 0, unroll=8)

    for b in range(nb):
        for p in range(npl):
            y_ref[b, :, p * LANES:(p + 1) * LANES] = sk_ref[p, skew_rows(b), :]


def s5_scan(z3, lam, bmat, cmat):
    nb, seq, zw = z3.shape
    width = bmat.shape[1] * bmat.shape[2]
    nstate = lam.shape[2]
    cn = min(S5_CHUNK, seq)
    nch = seq // cn
    srows = cn + SUBLANES
    ucol = zw // width - 1

    def chunk(d, n):
        return n + d * (nch - 1 - 2 * n)

    return pl.pallas_call(
        _s5_kernel,
        grid=(2, nch),
        in_specs=[
            pl.BlockSpec((nb, cn, width), lambda d, n: (0, chunk(d, n), ucol)),
            pl.BlockSpec((None, 2, nstate), lambda d, n: (d, 0, 0)),
            pl.BlockSpec((None,) + bmat.shape[1:], lambda d, n: (d, 0, 0, 0)),
            pl.BlockSpec((None,) + cmat.shape[1:], lambda d, n: (d, 0, 0, 0)),
        ],
        out_specs=pl.BlockSpec((None, nb, cn, width), lambda d, n: (d, 0, chunk(d, n), 0)),
        out_shape=jax.ShapeDtypeStruct((2, nb, seq, width), F32),
        scratch_shapes=[
            pltpu.VMEM((width // LANES, nb * srows, LANES), F32),
            pltpu.VMEM((cn * nb, width), F32),
            pltpu.VMEM((width // LANES, cn * nb, LANES), F32),
            pltpu.VMEM((nb, nstate), F32),
            pltpu.VMEM((nb, nstate), F32),
        ] + [pltpu.VMEM((S5_COLS // LANES, cn * nb, LANES), F32)] * (2 * (nstate // S5_COLS)),
        compiler_params=_params(("arbitrary", "arbitrary")),
        name="s5_scan",
    )(z3, lam, bmat, cmat)


def _s5_discretize(a_re, a_im, log_step, b_re, b_im, c_re, c_im):
    ng, npst = a_re.shape[1], a_re.shape[2]
    gpb = S5_KBLK // S5_GROUP_CH
    nk = ng // gpb
    eye = jnp.eye(gpb, dtype=F32)
    bre, bim = b_re.astype(F32), b_im.astype(F32)
    lams, bmats, cmats = [], [], []
    for direction in range(2):
        ar = a_re[direction].astype(F32)
        ai = a_im[direction].astype(F32)
        dt = jnp.exp(log_step[direction].astype(F32))[:, None]
        mag = jnp.exp(ar * dt)
        lam_re, lam_im = mag * jnp.cos(ai * dt), mag * jnp.sin(ai * dt)
        nr, ni = lam_re - 1.0, lam_im
        den = ar * ar + ai * ai
        coef_re = (nr * ar + ni * ai) / den
        coef_im = (ni * ar - nr * ai) / den
        bbar_re = coef_re[..., None] * bre - coef_im[..., None] * bim
        bbar_im = coef_re[..., None] * bim + coef_im[..., None] * bre

        def in_blocks(m):
            m4 = m.reshape(nk, gpb, npst, S5_GROUP_CH)
            return jnp.einsum('kgpc,gh->kgchp', m4, eye).reshape(nk, S5_KBLK, gpb * npst)

        def out_blocks(m):
            m4 = m.reshape(nk, gpb, S5_GROUP_CH, npst)
            return jnp.einsum('kgcp,gh->kgphc', m4, eye).reshape(nk, gpb * npst, S5_KBLK)

        lams.append(jnp.stack([lam_re.reshape(-1), lam_im.reshape(-1)]))
        bmats.append(jnp.concatenate([in_blocks(bbar_re), in_blocks(bbar_im)], axis=2))
        cmats.append(jnp.concatenate([out_blocks(c_re[direction].astype(F32)),
                                      -out_blocks(c_im[direction].astype(F32))], axis=1))
    return jnp.stack(lams), jnp.stack(bmats).astype(BF16), jnp.stack(cmats).astype(BF16)


def _s5_glu(yf_ref, yb_ref, u_ref, d_ref, w_ref, b_ref):
    y = u_ref[...].astype(F32) * d_ref[...] + yf_ref[...] + yb_ref[...]
    y = jax.nn.gelu(y)
    gate = jax.nn.sigmoid(jnp.dot(y.astype(BF16), w_ref[...], preferred_element_type=F32) + b_ref[...])
    return (y * gate).astype(BF16)


def _proj_kernel(*refs, s5_glu):
    if s5_glu:
        yr_ref, yf_ref, yb_ref, u_ref, d_ref, wg_ref, bg_ref = refs[:7]
        w_ref, x_ref, g_ref, rh_ref, rl_ref, rb_ref, x1_ref, h_ref, lt_ref = refs[7:]
        half = yr_ref.shape[1]
        mix = jnp.dot(yr_ref[...], w_ref[:half, :], preferred_element_type=F32)
        mix = mix + jnp.dot(_s5_glu(yf_ref, yb_ref, u_ref, d_ref, wg_ref, bg_ref), w_ref[half:, :],
                            preferred_element_type=F32)
    else:
        a_ref, w_ref, x_ref, g_ref, rh_ref, rl_ref, rb_ref, x1_ref, h_ref, lt_ref = refs
        mix = jnp.dot(a_ref[...], w_ref[...], preferred_element_type=F32)
    x1 = x_ref[...] + mix
    x1_ref[...] = x1
    h = _rms(x1, g_ref[...])
    h_hi = h.astype(BF16)
    h_ref[...] = h_hi.reshape(h_ref.shape)
    h_lo = (h - h_hi.astype(F32)).astype(BF16)
    lt = lax.dot_general(rh_ref[...], h_hi, NT_DIMS, preferred_element_type=F32)
    lt = lt + lax.dot_general(rh_ref[...], h_lo, NT_DIMS, preferred_element_type=F32)
    lt = lt + lax.dot_general(rl_ref[...], h_hi, NT_DIMS, preferred_element_type=F32)
    lt_ref[...] = lt + rb_ref[...]


def proj_norm_router(mix_in, w, x, g, r_hi, r_lo, r_bias, tm=DENSE_ROWS):
    t, d = x.shape
    tm = min(tm, t)
    k = w.shape[0]
    nr = r_hi.shape[0]
    s5_glu = isinstance(mix_in, tuple)
    if s5_glu:
        y_ret, ys5, z, d_skip, w_glu, b_glu = mix_in
        half = y_ret.shape[1]
        ucol = z.shape[1] // half - 1
        lead_specs = [
            pl.BlockSpec((tm, half), lambda i: (i, 0)),
            pl.BlockSpec((None, tm, half), lambda i: (0, i, 0)),
            pl.BlockSpec((None, tm, half), lambda i: (1, i, 0)),
            pl.BlockSpec((tm, half), lambda i: (i, ucol)),
            pl.BlockSpec((1, half), lambda i: (0, 0)),
            pl.BlockSpec((half, half), lambda i: (0, 0)),
            pl.BlockSpec((1, half), lambda i: (0, 0)),
        ]
        lead_args = (y_ret, ys5, ys5, z, d_skip.reshape(1, half), w_glu.astype(BF16),
                     b_glu.reshape(1, half))
    else:
        lead_specs = [pl.BlockSpec((tm, k), lambda i: (i, 0))]
        lead_args = (mix_in,)
    return pl.pallas_call(
        functools.partial(_proj_kernel, s5_glu=s5_glu),
        grid=(t // tm,),
        in_specs=lead_specs + [
            pl.BlockSpec((k, d), lambda i: (0, 0)),
            pl.BlockSpec((tm, d), lambda i: (i, 0)),
            pl.BlockSpec((1, d), lambda i: (0, 0)),
            pl.BlockSpec((nr, d), lambda i: (0, 0)),
            pl.BlockSpec((nr, d), lambda i: (0, 0)),
            pl.BlockSpec((nr, 1), lambda i: (0, 0)),
        ],
        out_specs=[
            pl.BlockSpec((tm, d), lambda i: (i, 0)),
            pl.BlockSpec((tm, d // LANES, LANES), lambda i: (i, 0, 0)),
            pl.BlockSpec((nr, tm), lambda i: (0, i)),
        ],
        out_shape=[
            jax.ShapeDtypeStruct((t, d), F32),
            jax.ShapeDtypeStruct((t, d // LANES, LANES), BF16),
            jax.ShapeDtypeStruct((nr, t), F32),
        ],
        compiler_params=_params(("parallel",)),
        name="proj_norm_router",
    )(*lead_args, w.astype(BF16), x, g.reshape(1, d), r_hi, r_lo, r_bias)


def _router_operands(w_group, b_group, w_er, b_er):
    d = w_group.shape[0]
    wt = jnp.concatenate([
        jnp.transpose(w_er.astype(F32), (0, 2, 1)).reshape(MOE_EXPERTS, d),
        jnp.transpose(w_group.astype(F32)),
        jnp.zeros((ROUTER_ROWS - MOE_EXPERTS - MOE_GROUPS, d), F32)], axis=0)
    bias = jnp.concatenate([
        b_er.astype(F32).reshape(-1), b_group.astype(F32),
        jnp.zeros((ROUTER_ROWS - MOE_EXPERTS - MOE_GROUPS,), F32)]).reshape(ROUTER_ROWS, 1)
    hi = wt.astype(BF16)
    lo = (wt - hi.astype(F32)).astype(BF16)
    return hi, lo, bias


def _route_kernel(lt_ref, eid_ref, gate_ref, rank_ref, cnt_ref, run_ref):
    i = pl.program_id(0)
    tm = lt_ref.shape[1]
    ne, npg, ng = MOE_EXPERTS, MOE_EXPERTS_PER_GROUP, MOE_GROUPS

    @pl.when(i == 0)
    def _():
        run_ref[...] = jnp.zeros_like(run_ref)

    gl = lt_ref[ne:ne + ng, :]
    gmax = jnp.max(gl, axis=0, keepdims=True)
    gidx = lax.broadcasted_iota(I32, (ng, tm), 0)
    gsel = jnp.min(jnp.where(gl == gmax, gidx, ng), axis=0, keepdims=True)
    p_g = 1.0 / jnp.sum(jnp.exp(gl - gmax), axis=0, keepdims=True)

    e8 = lt_ref[(ng - 1) * npg:ng * npg, :]
    for g in range(ng - 2, -1, -1):
        e8 = jnp.where(gsel == g, lt_ref[g * npg:(g + 1) * npg, :], e8)
    eidx = lax.broadcasted_iota(I32, (npg, tm), 0)
    m1 = jnp.max(e8, axis=0, keepdims=True)
    i1 = jnp.min(jnp.where(e8 == m1, eidx, npg), axis=0, keepdims=True)
    e8b = jnp.where(eidx == i1, -jnp.inf, e8)
    m2 = jnp.max(e8b, axis=0, keepdims=True)
    i2 = jnp.min(jnp.where(e8b == m2, eidx, npg), axis=0, keepdims=True)
    t2 = jnp.exp(m2 - m1)
    den = 1.0 + t2
    gate_ref[0:1, :] = (1.0 / den) * p_g
    gate_ref[1:2, :] = (t2 / den) * p_g
    id1 = gsel * npg + i1
    id2 = gsel * npg + i2
    eid_ref[0:1, :] = id1
    eid_ref[1:2, :] = id2

    rows = lax.broadcasted_iota(I32, (ne, tm), 0)
    oh1 = rows == id1
    oh2 = rows == id2
    both = jnp.where(oh1, 1.0, 0.0) + jnp.where(oh2, 1.0, 0.0)
    earlier = (lax.broadcasted_iota(I32, (tm, tm), 0) < lax.broadcasted_iota(I32, (tm, tm), 1))
    prefix = jnp.dot(both.astype(BF16), jnp.where(earlier, 1.0, 0.0).astype(BF16),
                     preferred_element_type=F32)
    base = prefix + run_ref[:, 0:1]
    rank_ref[0:1, :] = jnp.sum(jnp.where(oh1, base, 0.0), axis=0, keepdims=True).astype(I32)
    rank_ref[1:2, :] = jnp.sum(jnp.where(oh2, base, 0.0), axis=0, keepdims=True).astype(I32)
    run = run_ref[...] + jnp.sum(both, axis=1, keepdims=True)
    run_ref[...] = run
    cnt_ref[...] = run.astype(I32)


def route(lt, tm=512):
    nr, t = lt.shape
    two = lambda dt: jax.ShapeDtypeStruct((2, t), dt)
    return pl.pallas_call(
        _route_kernel,
        grid=(t // tm,),
        in_specs=[pl.BlockSpec((nr, tm), lambda i: (0, i))],
        out_specs=[
            pl.BlockSpec((2, tm), lambda i: (0, i)),
            pl.BlockSpec((2, tm), lambda i: (0, i)),
            pl.BlockSpec((2, tm), lambda i: (0, i)),
            pl.BlockSpec((MOE_EXPERTS, LANES), lambda i: (0, 0)),
        ],
        out_shape=[two(I32), two(F32), two(I32),
                   jax.ShapeDtypeStruct((MOE_EXPERTS, LANES), I32)],
        scratch_shapes=[pltpu.VMEM((MOE_EXPERTS, LANES), F32)],
        compiler_params=_params(("arbitrary",)),
        name="route",
    )(lt)


def _dispatch_kernel(pad_ref, nu_ref, d0_ref, d1_ref, h_ref, xbuf_ref, zero_ref, sem, zsem):
    i = pl.program_id(0)
    tm = h_ref.shape[0]
    dests = (d0_ref, d1_ref)
    bm = zero_ref.shape[0]
    nblk = xbuf_ref.shape[0] // bm

    def zero_fill(act):
        def per_expert(e, carry):
            pos = pad_ref[0, e]
            length = pad_ref[1, e]
            p = bm // 2
            while p >= 1:
                bit = (length & p) != 0

                @pl.when(bit)
                def _(pos=pos, p=p):
                    act(pltpu.make_async_copy(zero_ref.at[pl.ds(0, p)],
                                              xbuf_ref.at[pl.ds(pos, p)], zsem))

                pos = pos + jnp.where(bit, p, 0)
                p //= 2
            return carry

        lax.fori_loop(0, pad_ref.shape[1], per_expert, 0)

        def per_block(b, carry):
            act(pltpu.make_async_copy(zero_ref, xbuf_ref.at[pl.ds(b * bm, bm)], zsem))
            return carry

        lax.fori_loop(nu_ref[0], nblk, per_block, 0)

    @pl.when(i == 0)
    def _():
        zero_ref[...] = jnp.zeros_like(zero_ref)
        zero_fill(lambda c: c.start())

    def copy(r, k):
        return pltpu.make_async_copy(h_ref.at[r], xbuf_ref.at[dests[k][0, r]], sem)

    def start(r, carry):
        copy(r, 0).start(priority=0)
        copy(r, 1).start(priority=1)
        return carry

    lax.fori_loop(0, tm, start, 0, unroll=8)

    def wait(r, carry):
        copy(r, 0).wait()
        copy(r, 1).wait()
        return carry

    lax.fori_loop(0, tm, wait, 0, unroll=8)

    @pl.when(i == pl.num_programs(0) - 1)
    def _():
        zero_fill(lambda c: c.wait())


def _row_index_spec(tm, ahead=0, last=None):
    def index(i):
        return (i if ahead == 0 else jnp.minimum(i + ahead, last), 0, 0)
    return pl.BlockSpec((None, 1, tm), index, memory_space=pltpu.SMEM)


def dispatch(pad, n_used, dest0, dest1, h3, n_rows):
    t, s, lanes = h3.shape
    nt, _, tm = dest0.shape
    return pl.pallas_call(
        _dispatch_kernel,
        grid=(nt,),
        in_specs=[
            pl.BlockSpec(memory_space=pltpu.SMEM),
            pl.BlockSpec(memory_space=pltpu.SMEM),
            _row_index_spec(tm), _row_index_spec(tm),
            pl.BlockSpec((tm, s, lanes), lambda i: (i, 0, 0)),
        ],
        out_specs=pl.BlockSpec(memory_space=pl.ANY),
        out_shape=jax.ShapeDtypeStruct((n_rows, s, lanes), h3.dtype),
        scratch_shapes=[pltpu.VMEM((MOE_ROWS, s, lanes), h3.dtype),
                        pltpu.SemaphoreType.DMA(()), pltpu.SemaphoreType.DMA(())],
        compiler_params=_params(("arbitrary",)),
        name="moe_dispatch",
    )(pad, n_used, dest0, dest1, h3)


def _experts_kernel(be_ref, nu_ref, nx_ref, x_ref, wg_hbm, wu_hbm, wd_hbm, o_ref,
                    wg_f32, wu_f32, wd_f32, wgb, wub, wdb, slot_ref, sem, *, layer):
    i = pl.program_id(0)
    e = be_ref[i]
    first = i == 0
    changed = jnp.logical_or(first, e != be_ref[jnp.maximum(i - 1, 0)])
    streams = ((wg_hbm, wg_f32), (wu_hbm, wu_f32), (wd_hbm, wd_f32))

    def fetch(expert, slot):
        return [pltpu.make_async_copy(w_hbm.at[layer, expert], w_f32.at[slot], sem.at[slot, j])
                for j, (w_hbm, w_f32) in enumerate(streams)]

    @pl.when(first)
    def _():
        slot_ref[0] = 1
        for c in fetch(e, 0):
            c.start()

    @pl.when(changed)
    def _():
        slot = 1 - slot_ref[0]
        slot_ref[0] = slot
        for c in fetch(e, slot):
            c.wait()
        nxt = nx_ref[i]

        @pl.when(nxt >= 0)
        def _():
            for c in fetch(nxt, 1 - slot):
                c.start()

        wgb[...] = wg_f32[slot].astype(BF16)
        wub[...] = wu_f32[slot].astype(BF16)
        wdb[...] = wd_f32[slot].astype(BF16)

    @pl.when(i < nu_ref[0])
    def _():
        bm, s, lanes = x_ref.shape
        x = x_ref[...].reshape(bm, s * lanes)
        g = jnp.dot(x, wgb[...], preferred_element_type=F32)
        u = jnp.dot(x, wub[...], preferred_element_type=F32)
        a = ((g * jax.nn.sigmoid(g)) * u).astype(BF16)
        y = jnp.dot(a, wdb[...], preferred_element_type=F32)
        o_ref[...] = y.astype(BF16).reshape(o_ref.shape)

    @pl.when(i >= nu_ref[0])
    def _():
        o_ref[...] = jnp.zeros_like(o_ref)


def experts(layer, block_expert, n_used, next_expert, xbuf, w_gate, w_up, w_down):
    n_rows, s, lanes = xbuf.shape
    d = s * lanes
    hid = w_gate.shape[3]
    bm = MOE_ROWS
    nblk = n_rows // bm
    grid_spec = pltpu.PrefetchScalarGridSpec(
        num_scalar_prefetch=3,
        grid=(nblk,),
        in_specs=[
            pl.BlockSpec((bm, s, lanes), lambda i, be, nu, nx: (jnp.minimum(i, nu[0] - 1), 0, 0)),
            pl.BlockSpec(memory_space=pl.ANY),
            pl.BlockSpec(memory_space=pl.ANY),
            pl.BlockSpec(memory_space=pl.ANY),
        ],
        out_specs=pl.BlockSpec((bm, s, lanes), lambda i, be, nu, nx: (i, 0, 0)),
        scratch_shapes=[pltpu.VMEM((2, d, hid), F32), pltpu.VMEM((2, d, hid), F32),
                        pltpu.VMEM((2, hid, d), F32),
                        pltpu.VMEM((d, hid), BF16), pltpu.VMEM((d, hid), BF16),
                        pltpu.VMEM((hid, d), BF16),
                        pltpu.SMEM((1,), I32), pltpu.SemaphoreType.DMA((2, 3))],
    )
    return pl.pallas_call(
        functools.partial(_experts_kernel, layer=layer),
        grid_spec=grid_spec,
        out_shape=jax.ShapeDtypeStruct((n_rows, s, lanes), BF16),
        compiler_params=_params(("arbitrary",)),
        name="moe_experts",
    )(block_expert, n_used, next_expert, xbuf, w_gate, w_up, w_down)


def _combine_kernel(d0_ref, d1_ref, n0_ref, n1_ref, gate_ref, x_ref, g_ref, ybuf_ref, o_ref,
                    buf, sem, *, final_norm):
    i = pl.program_id(0)
    tm, d = x_ref.shape
    slot = i % 2

    def copy(dests, s, r, k):
        return pltpu.make_async_copy(ybuf_ref.at[dests[k][0, r]], buf.at[s, k, r], sem.at[s])

    def gather(dests, s):
        def start(r, carry):
            copy(dests, s, r, 0).start(priority=0)
            copy(dests, s, r, 1).start(priority=1)
            return carry

        lax.fori_loop(0, tm, start, 0, unroll=8)

    @pl.when(i == 0)
    def _():
        gather((d0_ref, d1_ref), 0)

    @pl.when(i + 1 < pl.num_programs(0))
    def _():
        gather((n0_ref, n1_ref), 1 - slot)

    def wait(r, carry):
        copy((d0_ref, d1_ref), slot, r, 0).wait()
        copy((d0_ref, d1_ref), slot, r, 1).wait()
        return carry

    lax.fori_loop(0, tm, wait, 0, unroll=8)

    gates = gate_ref[...]
    y = (gates[:, 0:1] * buf[slot, 0].reshape(tm, d).astype(F32)
         + gates[:, 1:2] * buf[slot, 1].reshape(tm, d).astype(F32))
    out = x_ref[...] + y
    if final_norm:
        out = _rms(out, g_ref[...])
    o_ref[...] = out


def combine(dest0, dest1, gates_t, x, ybuf, g_final, final_norm):
    t, d = x.shape
    nt, _, tm = dest0.shape
    _, s, lanes = ybuf.shape
    return pl.pallas_call(
        functools.partial(_combine_kernel, final_norm=final_norm),
        grid=(nt,),
        in_specs=[
            _row_index_spec(tm), _row_index_spec(tm),
            _row_index_spec(tm, ahead=1, last=nt - 1), _row_index_spec(tm, ahead=1, last=nt - 1),
            pl.BlockSpec((tm, 2), lambda i: (i, 0)),
            pl.BlockSpec((tm, d), lambda i: (i, 0)),
            pl.BlockSpec((1, d), lambda i: (0, 0)),
            pl.BlockSpec(memory_space=pl.ANY),
        ],
        out_specs=pl.BlockSpec((tm, d), lambda i: (i, 0)),
        out_shape=jax.ShapeDtypeStruct((t, d), F32),
        scratch_shapes=[pltpu.VMEM((2, 2, tm, s, lanes), ybuf.dtype),
                        pltpu.SemaphoreType.DMA((2,))],
        compiler_params=_params(("arbitrary",)),
        name="moe_combine",
    )(dest0, dest1, dest0, dest1, gates_t, x, g_final.reshape(1, d), ybuf)


def hier_moe_block(layer, x1, h, lt, w_gate, w_up, w_down, g_final, final_norm, tm=MOE_TOKENS):
    t, d = x1.shape
    bm = MOE_ROWS
    eid, gate, rank, cnt = route(lt)
    counts = cnt[:, 0]
    padded = ((counts + bm - 1) // bm) * bm
    pend = jnp.cumsum(padded)
    pstart = pend - padded
    experts_col = jnp.arange(MOE_EXPERTS, dtype=I32)[:, None, None]
    dest = rank + jnp.sum(jnp.where(eid[None] == experts_col, pstart[:, None, None], 0), axis=0)
    n_rows = 2 * t + MOE_EXPERTS * bm
    nblk = n_rows // bm
    n_used = (pend[-1] // bm).astype(I32)
    first_row = jnp.minimum(jnp.arange(nblk, dtype=I32), n_used - 1) * bm
    block_expert = jnp.sum(pend[None, :] <= first_row[:, None], axis=1).astype(I32)
    block_expert = jnp.minimum(block_expert, MOE_EXPERTS - 1)
    ids = jnp.arange(MOE_EXPERTS, dtype=I32)
    later = jnp.logical_and(ids[None, :] > block_expert[:, None], counts[None, :] > 0)
    next_expert = jnp.min(jnp.where(later, ids[None, :], MOE_EXPERTS), axis=1)
    next_expert = jnp.where(next_expert < MOE_EXPERTS, next_expert, -1).astype(I32)
    dest0 = dest[0].reshape(t // tm, 1, tm)
    dest1 = dest[1].reshape(t // tm, 1, tm)
    pad = jnp.stack([pstart + counts, padded - counts]).astype(I32)
    xbuf = dispatch(pad, n_used.reshape(1), dest0, dest1, h, n_rows)
    ybuf = experts(layer, block_expert, n_used.reshape(1), next_expert, xbuf, w_gate, w_up, w_down)
    return combine(dest0, dest1, jnp.transpose(gate), x1, ybuf, g_final, final_norm)


def _attn_kernel(sink_ref, q_ref, kp_ref, kc_ref, kn_ref, vp_ref, vc_ref, vn_ref, bias_ref, o_ref):
    n = pl.program_id(1)
    nblk = pl.num_programs(1)
    blk = q_ref.shape[0]
    dh = SWA_HEAD_DIM
    nkv = kc_ref.shape[1] // dh
    masked = bias_ref.shape[1] - 1
    part_prev = jnp.where(n > 0, 0, masked)
    part_next = jnp.where(n < nblk - 1, 2, masked)
    lo = lax.broadcasted_iota(I32, (blk, 2 * dh), 1) < dh
    top = lax.broadcasted_iota(I32, (2 * blk, 2 * dh), 0) < blk
    lo2 = lax.broadcasted_iota(I32, (2 * blk, 2 * dh), 1) < dh
    ones_ext = jnp.concatenate([jnp.where(lo, 1.0, 0.0), jnp.where(lo, 0.0, 1.0)], axis=0).astype(BF16)
    qscale = (dh ** -0.5) * LOG2E

    for kv in range(nkv):
        col = slice((kv // 2) * 2 * dh, (kv // 2 + 1) * 2 * dh)

        def extend(ref):
            x = ref[:, col].astype(F32)
            r = pltpu.roll(x, dh, 1)
            x_lo, x_hi = (x, r) if kv % 2 == 0 else (r, x)
            return jnp.concatenate([jnp.where(lo, x_lo, 0.0), jnp.where(lo, 0.0, x_hi)],
                                   axis=0).astype(BF16)

        q2 = jnp.concatenate([q_ref[:, (2 * kv) * 2 * dh:(2 * kv + 1) * 2 * dh],
                              q_ref[:, (2 * kv + 1) * 2 * dh:(2 * kv + 2) * 2 * dh]], axis=0)
        q2 = (q2.astype(F32) * qscale).astype(BF16)

        def scores(k_ref, part):
            return lax.dot_general(q2, extend(k_ref), NT_DIMS,
                                   preferred_element_type=F32) + bias_ref[kv, part]

        s = [scores(kp_ref, part_prev), scores(kc_ref, 1), scores(kn_ref, part_next)]
        mx = jnp.maximum(jnp.maximum(s[0], s[1]), s[2])
        sk = [jnp.where(top, sink_ref[0, 4 * kv + par], sink_ref[0, 4 * kv + 2 + par])
              for par in range(2)]
        m = [jnp.maximum(jnp.broadcast_to(jnp.max(mx[:, par * blk:(par + 1) * blk], axis=-1,
                                                  keepdims=True), (2 * blk, 2 * dh)), sk[par])
             for par in range(2)]
        acc = jnp.zeros((2 * blk, 4 * dh), F32)
        for sp, v_ref in zip(s, (vp_ref, vc_ref, vn_ref)):
            e = jnp.concatenate([jnp.exp2(sp[:, :blk] - m[0]), jnp.exp2(sp[:, blk:] - m[1])],
                                axis=1).astype(BF16)
            rhs = jnp.concatenate([extend(v_ref), ones_ext], axis=1)
            acc = acc + jnp.dot(e, rhs, preferred_element_type=F32)
        den = acc[:, 2 * dh:] + jnp.exp2(jnp.where(lo2, sk[0] - m[0], sk[1] - m[1]))
        o = (acc[:, :2 * dh] / den).astype(BF16)
        o_ref[:, (2 * kv) * 2 * dh:(2 * kv + 1) * 2 * dh] = o[:blk]
        o_ref[:, (2 * kv + 1) * 2 * dh:(2 * kv + 2) * 2 * dh] = o[blk:]


def banded_attention(qkv3, sink, bias):
    b, seq, width = qkv3.shape
    nh = sink.shape[1]
    qd = nh * SWA_HEAD_DIM
    kvd = (width - qd) // 2
    blk = SWA_BLOCK
    nblk = seq // blk
    kcol, vcol = qd // kvd, qd // kvd + 1

    def band(col, off):
        return pl.BlockSpec((None, blk, kvd),
                            lambda bi, n: (bi, jnp.clip(n + off, 0, nblk - 1), col))

    return pl.pallas_call(
        _attn_kernel,
        grid=(b, nblk),
        in_specs=[
            pl.BlockSpec(memory_space=pltpu.SMEM),
            pl.BlockSpec((None, blk, qd), lambda bi, n: (bi, n, 0)),
            band(kcol, -1), band(kcol, 0), band(kcol, 1),
            band(vcol, -1), band(vcol, 0), band(vcol, 1),
            pl.BlockSpec(bias.shape, lambda bi, n: (0, 0, 0, 0)),
        ],
        out_specs=pl.BlockSpec((None, blk, qd), lambda bi, n: (bi, n, 0)),
        out_shape=jax.ShapeDtypeStruct((b, seq, qd), BF16),
        compiler_params=_params(("parallel", "parallel")),
        name="banded_attention",
    )(sink, qkv3, qkv3, qkv3, qkv3, qkv3, qkv3, qkv3, bias)


def _t5_bucket(rel):
    half = REL_BUCKETS // 2
    max_exact = half // 2
    n = jnp.abs(rel)
    large = max_exact + (jnp.log(jnp.maximum(n, 1).astype(F32) / max_exact)
                         / math.log(REL_MAX_DIST / max_exact) * (half - max_exact)).astype(I32)
    large = jnp.minimum(large, half - 1)
    return jnp.where(rel > 0, half, 0) + jnp.where(n < max_exact, n, large)


def _attention_bias(rel_bias):
    blk = SWA_BLOCK
    nh = rel_bias.shape[1]
    nkv = nh // SWA_Q_PER_KV
    rel = jnp.arange(3 * blk)[None, :] - blk - jnp.arange(blk)[:, None]
    onehot = (_t5_bucket(rel)[..., None] == jnp.arange(REL_BUCKETS)).astype(F32)
    bias = jnp.einsum('ijb,bh->hij', onehot, rel_bias.astype(F32), precision=lax.Precision.HIGHEST)
    bias = jnp.where((jnp.abs(rel) <= SWA_WINDOW)[None], bias, NEG_INF) * LOG2E
    tiles = bias.reshape(nkv, 2, 2, blk, 3, blk).transpose(0, 4, 1, 3, 2, 5)
    tiles = tiles.reshape(nkv, 3, 2 * blk, 2 * blk)
    masked = jnp.full((nkv, 1, 2 * blk, 2 * blk), NEG_INF * LOG2E, F32)
    return jnp.concatenate([tiles, masked], axis=1)


def _rotary_tables(seq, dim):
    inv_freq = ROPE_BASE ** (-jnp.arange(0, dim, 2, dtype=F32) / dim)
    ang = jnp.arange(seq, dtype=F32)[:, None] * inv_freq[None, :]
    cos, sin = jnp.cos(ang), jnp.sin(ang)
    return jnp.concatenate([cos, cos], axis=1), jnp.concatenate([-sin, sin], axis=1)


def kernel(x, norm_mix_g, norm_ffn_g, norm_final_g, hyb_w_in, ret_decay_logit, s5_a_re, s5_a_im, s5_log_step, s5_b_re, s5_b_im, s5_c_re, s5_c_im, s5_d, s5_w_glu, s5_b_glu, hyb_w_out, swa_w_qkv, swa_sink, swa_w_o, rel_bias, moe_w_group, moe_b_group, moe_w_expert_router, moe_b_expert_router, moe_w_gate, moe_w_up, moe_w_down):
    nb, seq, d = x.shape
    t = nb * seq
    depth = norm_mix_g.shape[0]
    xt = x.reshape(t, d)
    for layer in range(depth):
        i = layer // 2
        if layer % 2 == 0:
            w = hyb_w_out.shape[1] // 2
            z = norm_matmul(xt, norm_mix_g[layer], hyb_w_in[i])
            z3 = z.reshape(nb, seq, z.shape[1])
            cos, sin = _rotary_tables(seq, w // RET_HEADS)
            log_gamma = jax.nn.log_sigmoid(ret_decay_logit[i].astype(F32))
            y_ret = retention(z3, log_gamma, cos, sin).reshape(t, w)
            lam, bmat, cmat = _s5_discretize(s5_a_re[i], s5_a_im[i], s5_log_step[i], s5_b_re[i],
                                             s5_b_im[i], s5_c_re[i], s5_c_im[i])
            ys5 = s5_scan(z3, lam, bmat, cmat).reshape(2, t, w)
            mix_in = (y_ret, ys5, z, s5_d[i].astype(F32).reshape(-1), s5_w_glu[i],
                      s5_b_glu[i].astype(F32))
            w_mix = hyb_w_out[i]
        else:
            qkv = norm_matmul(xt, norm_mix_g[layer], swa_w_qkv[i])
            mix_in = banded_attention(qkv.reshape(nb, seq, qkv.shape[1]),
                                      swa_sink[i].astype(F32).reshape(1, -1) * LOG2E,
                                      _attention_bias(rel_bias)).reshape(t, -1)
            w_mix = swa_w_o[i]
        r_hi, r_lo, r_bias = _router_operands(moe_w_group[layer], moe_b_group[layer],
                                              moe_w_expert_router[layer], moe_b_expert_router[layer])
        x1, h, lt = proj_norm_router(mix_in, w_mix, xt, norm_ffn_g[layer], r_hi, r_lo, r_bias)
        last = layer == depth - 1
        xt = hier_moe_block(layer, x1, h, lt, moe_w_gate, moe_w_up, moe_w_down,
                            norm_final_g, final_norm=last)
    return xt.reshape(nb, seq, d)
```

```python
import functools
import math

import jax
import jax.numpy as jnp
from jax import lax
from jax.experimental import pallas as pl
from jax.experimental.pallas import tpu as pltpu

F32 = jnp.float32
BF16 = jnp.bfloat16
I32 = jnp.int32

RET_HEADS = 4
RET_CHUNK = 128
S5_GROUP_CH = 16
S5_STATE = 64
SWA_HEAD_DIM = 64
SWA_Q_PER_KV = 4
SWA_WINDOW = 128
SWA_BLOCK = 128
REL_BUCKETS = 32
REL_MAX_DIST = 128
MOE_GROUPS = 4
MOE_EXPERTS_PER_GROUP = 8
MOE_EXPERTS = MOE_GROUPS * MOE_EXPERTS_PER_GROUP
ROPE_BASE = 10000.0
RMS_EPS = 1e-6
GN_EPS = 1e-5
NEG_INF = -1e30
LOG2E = 1.4426950408889634

LANES = 128
SUBLANES = 8
V7X_VMEM_BYTES = 64 * 1024 * 1024
VMEM_LIMIT = V7X_VMEM_BYTES - 8 * 1024 * 1024

DENSE_ROWS = 1024
MOE_ROWS = 512
MOE_TOKENS = 512
COMBINE_GROUPS = 8
S5_CHUNK = 128
S5_COLS = 512
S5_SCAN_COLS = 512
S5_KBLK = 128
ROUTER_ROWS = 128

NT_DIMS = (((1,), (1,)), ((), ()))
TN_DIMS = (((0,), (0,)), ((), ()))


def _params(semantics):
    return pltpu.CompilerParams(dimension_semantics=semantics, vmem_limit_bytes=VMEM_LIMIT)


def _rms(x, g):
    ms = jnp.mean(x * x, axis=-1, keepdims=True)
    return (x * lax.rsqrt(ms + RMS_EPS)) * g


def _norm_matmul_kernel(x_ref, g_ref, w_ref, o_ref):
    h = _rms(x_ref[...], g_ref[...])
    o_ref[...] = jnp.dot(h.astype(BF16), w_ref[...],
                         preferred_element_type=F32).astype(o_ref.dtype)


def norm_matmul(x, g, w, tm=DENSE_ROWS):
    t, d = x.shape
    tm = min(tm, t)
    n = w.shape[1]
    return pl.pallas_call(
        _norm_matmul_kernel,
        grid=(t // tm,),
        in_specs=[
            pl.BlockSpec((tm, d), lambda i: (i, 0)),
            pl.BlockSpec((1, d), lambda i: (0, 0)),
            pl.BlockSpec((d, n), lambda i: (0, 0)),
        ],
        out_specs=pl.BlockSpec((tm, n), lambda i: (i, 0)),
        out_shape=jax.ShapeDtypeStruct((t, n), BF16),
        compiler_params=_params(("parallel",)),
        name="norm_matmul",
    )(x, g.reshape(1, d), w.astype(BF16))


def _retention_kernel(lg_ref, q_ref, k_ref, v_ref, g_ref, cos_ref, sin_ref, o_ref,
                      qr_ref, kr_ref, inc_ref, st_ref, lhs_ref):
    h = pl.program_id(1)
    lg_f = lg_ref[0, h]
    lg_b = lg_ref[1, h]
    seq, dk = q_ref.shape
    c = RET_CHUNK
    nc = seq // c

    cos = cos_ref[...]
    sin = sin_ref[...]
    swap = (lax.broadcasted_iota(I32, (dk, dk), 0)
            == (lax.broadcasted_iota(I32, (dk, dk), 1) + dk // 2) % dk)
    swap = jnp.where(swap, 1.0, 0.0).astype(BF16)
    q = q_ref[...]
    qr_ref[...] = (q.astype(F32) * cos
                   + jnp.dot(q, swap, preferred_element_type=F32) * sin)
    k = k_ref[...]
    kr_ref[...] = (k.astype(F32) * cos
                   + jnp.dot(k, swap, preferred_element_type=F32) * sin) * (dk ** -0.5)

    pos = lax.broadcasted_iota(I32, (c, dk), 0).astype(F32)
    kf_scale = jnp.exp((c - 1.0 - pos) * lg_f)
    qf_scale = jnp.exp((pos + 1.0) * lg_f)
    kb_scale = jnp.exp(pos * lg_b)
    qb_scale = jnp.exp((c - pos) * lg_b)
    rel = (lax.broadcasted_iota(I32, (c, c), 0) - lax.broadcasted_iota(I32, (c, c), 1)).astype(F32)
    mask = jnp.exp(jnp.abs(rel) * jnp.where(rel >= 0, lg_f, lg_b))
    dec_f = jnp.exp(jnp.full((dk, dk), c * lg_f, F32))
    dec_b = jnp.exp(jnp.full((dk, dk), c * lg_b, F32))

    def increments(n, carry):
        rows = pl.ds(pl.multiple_of(n * c, c), c)
        kc = kr_ref[rows, :]
        kk = jnp.concatenate([kc * kf_scale, kc * kb_scale], axis=1).astype(BF16)
        inc_ref[n] = lax.dot_general(kk, v_ref[rows, :], TN_DIMS,
                                     preferred_element_type=F32)
        return carry

    lax.fori_loop(0, nc, increments, 0, unroll=True)

    def fwd(n, state):
        st_ref[n, :dk, :] = state.astype(BF16)
        return state * dec_f + inc_ref[n, :dk, :]

    lax.fori_loop(0, nc, fwd, jnp.zeros((dk, dk), F32), unroll=True)

    def bwd(i, state):
        n = nc - 1 - i
        st_ref[n, dk:, :] = state.astype(BF16)
        return state * dec_b + inc_ref[n, dk:, :]

    lax.fori_loop(0, nc, bwd, jnp.zeros((dk, dk), F32), unroll=True)

    def operands(n, carry):
        rows = pl.ds(pl.multiple_of(n * c, c), c)
        qc = qr_ref[rows, :]
        s = lax.dot_general(qc.astype(BF16), kr_ref[rows, :].astype(BF16), NT_DIMS,
                            preferred_element_type=F32) * mask
        lhs_ref[n] = jnp.concatenate([s, qc * qf_scale, qc * qb_scale], axis=1).astype(BF16)
        return carry

    lax.fori_loop(0, nc, operands, 0, unroll=True)

    def outputs(n, carry):
        rows = pl.ds(pl.multiple_of(n * c, c), c)
        rhs = jnp.concatenate([v_ref[rows, :], st_ref[n]], axis=0)
        kr_ref[rows, :] = jnp.dot(lhs_ref[n], rhs, preferred_element_type=F32)
        return carry

    lax.fori_loop(0, nc, outputs, 0, unroll=True)

    out = kr_ref[...]
    mu = jnp.mean(out, axis=-1, keepdims=True)
    cen = out - mu
    var = jnp.mean(cen * cen, axis=-1, keepdims=True)
    g = g_ref[...].astype(F32)
    o_ref[...] = ((g * jax.nn.sigmoid(g)) * (cen * lax.rsqrt(var + GN_EPS))).astype(BF16)


def retention(z3, log_gamma, cos, sin):
    b, seq, _ = z3.shape
    nh = RET_HEADS
    dk = cos.shape[1]

    def col(off):
        return pl.BlockSpec((None, seq, dk), lambda bi, hi: (bi, 0, off + hi))

    return pl.pallas_call(
        _retention_kernel,
        grid=(b, nh),
        in_specs=[
            pl.BlockSpec(memory_space=pltpu.SMEM),
            col(0), col(nh), col(2 * nh), col(3 * nh),
            pl.BlockSpec((seq, dk), lambda bi, hi: (0, 0)),
            pl.BlockSpec((seq, dk), lambda bi, hi: (0, 0)),
        ],
        out_specs=pl.BlockSpec((None, seq, dk), lambda bi, hi: (bi, 0, hi)),
        out_shape=jax.ShapeDtypeStruct((b, seq, nh * dk), BF16),
        scratch_shapes=[pltpu.VMEM((seq, dk), F32), pltpu.VMEM((seq, dk), F32),
                        pltpu.VMEM((seq // RET_CHUNK, 2 * dk, dk), F32),
                        pltpu.VMEM((seq // RET_CHUNK, 2 * dk, dk), BF16),
                        pltpu.VMEM((seq // RET_CHUNK, RET_CHUNK, 3 * dk), BF16)],
        compiler_params=_params(("parallel", "parallel")),
        name="retention",
    )(log_gamma, z3, z3, z3, z3, cos, sin)


def _s5_kernel(u_ref, lam_ref, b_ref, c_ref, y_ref, sk_ref, utm_ref, ytm_ref, sre_ref, sim_ref,
               *hist):
    nkb = len(hist) // 2

    def xre(plane):
        return hist[plane // (S5_COLS // LANES)].at[plane % (S5_COLS // LANES)]

    def xim(plane):
        return hist[nkb + plane // (S5_COLS // LANES)].at[plane % (S5_COLS // LANES)]

    d = pl.program_id(0)
    n = pl.program_id(1)
    nb, cn, width = u_ref.shape
    srows = sk_ref.shape[1] // nb
    nk = width // S5_KBLK
    npl = width // LANES

    @pl.when(n == 0)
    def _():
        sre_ref[...] = jnp.zeros_like(sre_ref)
        sim_ref[...] = jnp.zeros_like(sim_ref)

    def skew_rows(b):
        return slice(b * srows + b, b * srows + b + cn)

    for b in range(nb):
        ub = u_ref[b].astype(F32)
        for p in range(npl):
            sk_ref[p, skew_rows(b), :] = ub[:, p * LANES:(p + 1) * LANES]

    def to_time_major(t, carry):
        rows = pl.ds(pl.multiple_of(t * nb, nb), nb)
        for p in range(npl):
            utm_ref[rows, p * LANES:(p + 1) * LANES] = sk_ref[p, pl.ds(t, nb, stride=srows + 1), :]
        return carry

    lax.fori_loop(0, cn, to_time_major, 0, unroll=8)

    u = utm_ref[...].astype(BF16)
    ppc = S5_COLS // LANES
    for kb in range(nk):
        bu = jnp.dot(u[:, kb * S5_KBLK:(kb + 1) * S5_KBLK], b_ref[kb], preferred_element_type=F32)
        for j in range(ppc):
            xre(kb * ppc + j)[...] = bu[:, j * LANES:(j + 1) * LANES]
            xim(kb * ppc + j)[...] = bu[:, S5_COLS + j * LANES:S5_COLS + (j + 1) * LANES]

    pps = S5_SCAN_COLS // LANES
    for cb in range(sre_ref.shape[1] // S5_SCAN_COLS):
        cols = slice(cb * S5_SCAN_COLS, (cb + 1) * S5_SCAN_COLS)
        lr = jnp.broadcast_to(lam_ref[0:1, cols], (nb, S5_SCAN_COLS))
        li = jnp.broadcast_to(lam_ref[1:2, cols], (nb, S5_SCAN_COLS))

        def step(i, carry):
            xr, xi = carry
            t = i + d * (cn - 1 - 2 * i)
            rows = pl.ds(pl.multiple_of(t * nb, nb), nb)
            bur = jnp.concatenate([xre(cb * pps + j)[rows, :] for j in range(pps)], axis=1)
            bui = jnp.concatenate([xim(cb * pps + j)[rows, :] for j in range(pps)], axis=1)
            nxr = lr * xr - li * xi + bur
            nxi = lr * xi + li * xr + bui
            for j in range(pps):
                xre(cb * pps + j)[rows, :] = nxr[:, j * LANES:(j + 1) * LANES]
                xim(cb * pps + j)[rows, :] = nxi[:, j * LANES:(j + 1) * LANES]
            return nxr, nxi

        xr, xi = lax.fori_loop(0, cn, step, (sre_ref[:, cols], sim_ref[:, cols]), unroll=True)
        sre_ref[:, cols] = xr
        sim_ref[:, cols] = xi

    for kb in range(nk):
        xr = jnp.concatenate([xre(kb * ppc + j)[...] for j in range(ppc)], axis=1).astype(BF16)
        xi = jnp.concatenate([xim(kb * ppc + j)[...] for j in range(ppc)], axis=1).astype(BF16)
        y = jnp.dot(xr, c_ref[kb, :S5_COLS, :], preferred_element_type=F32)
        ytm_ref[kb] = y + jnp.dot(xi, c_ref[kb, S5_COLS:, :], preferred_element_type=F32)

    def to_batch_major(t, carry):
        rows = pl.ds(pl.multiple_of(t * nb, nb), nb)
        for p in range(npl):
            sk_ref[p, pl.ds(t, nb, stride=srows + 1), :] = ytm_ref[p, rows, :]
        return carry

    lax.fori_loop(0, cn, to_batch_major, 0, unroll=8)

    for b in range(nb):
        for p in range(npl):
            y_ref[b, :, p * LANES:(p + 1) * LANES] = sk_ref[p, skew_rows(b), :]


def s5_scan(z3, lam, bmat, cmat):
    nb, seq, zw = z3.shape
    width = bmat.shape[1] * bmat.shape[2]
    nstate = lam.shape[2]
    cn = min(S5_CHUNK, seq)
    nch = seq // cn
    srows = cn + SUBLANES
    ucol = zw // width - 1

    def chunk(d, n):
        return n + d * (nch - 1 - 2 * n)

    return pl.pallas_call(
        _s5_kernel,
        grid=(2, nch),
        in_specs=[
            pl.BlockSpec((nb, cn, width), lambda d, n: (0, chunk(d, n), ucol)),
            pl.BlockSpec((None, 2, nstate), lambda d, n: (d, 0, 0)),
            pl.BlockSpec((None,) + bmat.shape[1:], lambda d, n: (d, 0, 0, 0)),
            pl.BlockSpec((None,) + cmat.shape[1:], lambda d, n: (d, 0, 0, 0)),
        ],
        out_specs=pl.BlockSpec((None, nb, cn, width), lambda d, n: (d, 0, chunk(d, n), 0)),
        out_shape=jax.ShapeDtypeStruct((2, nb, seq, width), F32),
        scratch_shapes=[
            pltpu.VMEM((width // LANES, nb * srows, LANES), F32),
            pltpu.VMEM((cn * nb, width), F32),
            pltpu.VMEM((width // LANES, cn * nb, LANES), F32),
            pltpu.VMEM((nb, nstate), F32),
            pltpu.VMEM((nb, nstate), F32),
        ] + [pltpu.VMEM((S5_COLS // LANES, cn * nb, LANES), F32)] * (2 * (nstate // S5_COLS)),
        compiler_params=_params(("arbitrary", "arbitrary")),
        name="s5_scan",
    )(z3, lam, bmat, cmat)


def _s5_discretize(a_re, a_im, log_step, b_re, b_im, c_re, c_im):
    ng, npst = a_re.shape[1], a_re.shape[2]
    gpb = S5_KBLK // S5_GROUP_CH
    nk = ng // gpb
    eye = jnp.eye(gpb, dtype=F32)
    bre, bim = b_re.astype(F32), b_im.astype(F32)
    lams, bmats, cmats = [], [], []
    for direction in range(2):
        ar = a_re[direction].astype(F32)
        ai = a_im[direction].astype(F32)
        dt = jnp.exp(log_step[direction].astype(F32))[:, None]
        mag = jnp.exp(ar * dt)
        lam_re, lam_im = mag * jnp.cos(ai * dt), mag * jnp.sin(ai * dt)
        nr, ni = lam_re - 1.0, lam_im
        den = ar * ar + ai * ai
        coef_re = (nr * ar + ni * ai) / den
        coef_im = (ni * ar - nr * ai) / den
        bbar_re = coef_re[..., None] * bre - coef_im[..., None] * bim
        bbar_im = coef_re[..., None] * bim + coef_im[..., None] * bre

        def in_blocks(m):
            m4 = m.reshape(nk, gpb, npst, S5_GROUP_CH)
            return jnp.einsum('kgpc,gh->kgchp', m4, eye).reshape(nk, S5_KBLK, gpb * npst)

        def out_blocks(m):
            m4 = m.reshape(nk, gpb, S5_GROUP_CH, npst)
            return jnp.einsum('kgcp,gh->kgphc', m4, eye).reshape(nk, gpb * npst, S5_KBLK)

        lams.append(jnp.stack([lam_re.reshape(-1), lam_im.reshape(-1)]))
        bmats.append(jnp.concatenate([in_blocks(bbar_re), in_blocks(bbar_im)], axis=2))
        cmats.append(jnp.concatenate([out_blocks(c_re[direction].astype(F32)),
                                      -out_blocks(c_im[direction].astype(F32))], axis=1))
    return jnp.stack(lams), jnp.stack(bmats).astype(BF16), jnp.stack(cmats).astype(BF16)


def _s5_glu(yf_ref, yb_ref, u_ref, d_ref, w_ref, b_ref):
    y = u_ref[...].astype(F32) * d_ref[...] + yf_ref[...] + yb_ref[...]
    y = jax.nn.gelu(y)
    gate = jax.nn.sigmoid(jnp.dot(y.astype(BF16), w_ref[...], preferred_element_type=F32) + b_ref[...])
    return (y * gate).astype(BF16)


def _proj_kernel(*refs, s5_glu):
    if s5_glu:
        yr_ref, yf_ref, yb_ref, u_ref, d_ref, wg_ref, bg_ref = refs[:7]
        w_ref, x_ref, g_ref, rh_ref, rl_ref, rb_ref, x1_ref, h_ref, lt_ref = refs[7:]
        half = yr_ref.shape[1]
        mix = jnp.dot(yr_ref[...], w_ref[:half, :], preferred_element_type=F32)
        mix = mix + jnp.dot(_s5_glu(yf_ref, yb_ref, u_ref, d_ref, wg_ref, bg_ref), w_ref[half:, :],
                            preferred_element_type=F32)
    else:
        a_ref, w_ref, x_ref, g_ref, rh_ref, rl_ref, rb_ref, x1_ref, h_ref, lt_ref = refs
        mix = jnp.dot(a_ref[...], w_ref[...], preferred_element_type=F32)
    x1 = x_ref[...] + mix
    x1_ref[...] = x1
    h = _rms(x1, g_ref[...])
    h_hi = h.astype(BF16)
    h_ref[...] = h_hi.reshape(h_ref.shape)
    h_lo = (h - h_hi.astype(F32)).astype(BF16)
    lt = lax.dot_general(rh_ref[...], h_hi, NT_DIMS, preferred_element_type=F32)
    lt = lt + lax.dot_general(rh_ref[...], h_lo, NT_DIMS, preferred_element_type=F32)
    lt = lt + lax.dot_general(rl_ref[...], h_hi, NT_DIMS, preferred_element_type=F32)
    lt_ref[...] = lt + rb_ref[...]


def proj_norm_router(mix_in, w, x, g, r_hi, r_lo, r_bias, tm=DENSE_ROWS):
    t, d = x.shape
    tm = min(tm, t)
    k = w.shape[0]
    nr = r_hi.shape[0]
    s5_glu = isinstance(mix_in, tuple)
    if s5_glu:
        y_ret, ys5, z, d_skip, w_glu, b_glu = mix_in
        half = y_ret.shape[1]
        ucol = z.shape[1] // half - 1
        lead_specs = [
            pl.BlockSpec((tm, half), lambda i: (i, 0)),
            pl.BlockSpec((None, tm, half), lambda i: (0, i, 0)),
            pl.BlockSpec((None, tm, half), lambda i: (1, i, 0)),
            pl.BlockSpec((tm, half), lambda i: (i, ucol)),
            pl.BlockSpec((1, half), lambda i: (0, 0)),
            pl.BlockSpec((half, half), lambda i: (0, 0)),
            pl.BlockSpec((1, half), lambda i: (0, 0)),
        ]
        lead_args = (y_ret, ys5, ys5, z, d_skip.reshape(1, half), w_glu.astype(BF16),
                     b_glu.reshape(1, half))
    else:
        lead_specs = [pl.BlockSpec((tm, k), lambda i: (i, 0))]
        lead_args = (mix_in,)
    return pl.pallas_call(
        functools.partial(_proj_kernel, s5_glu=s5_glu),
        grid=(t // tm,),
        in_specs=lead_specs + [
            pl.BlockSpec((k, d), lambda i: (0, 0)),
            pl.BlockSpec((tm, d), lambda i: (i, 0)),
            pl.BlockSpec((1, d), lambda i: (0, 0)),
            pl.BlockSpec((nr, d), lambda i: (0, 0)),
            pl.BlockSpec((nr, d), lambda i: (0, 0)),
            pl.BlockSpec((nr, 1), lambda i: (0, 0)),
        ],
        out_specs=[
            pl.BlockSpec((tm, d), lambda i: (i, 0)),
            pl.BlockSpec((tm, d // LANES, LANES), lambda i: (i, 0, 0)),
            pl.BlockSpec((nr, tm), lambda i: (0, i)),
        ],
        out_shape=[
            jax.ShapeDtypeStruct((t, d), F32),
            jax.ShapeDtypeStruct((t, d // LANES, LANES), BF16),
            jax.ShapeDtypeStruct((nr, t), F32),
        ],
        compiler_params=_params(("parallel",)),
        name="proj_norm_router",
    )(*lead_args, w.astype(BF16), x, g.reshape(1, d), r_hi, r_lo, r_bias)


def _router_operands(w_group, b_group, w_er, b_er):
    d = w_group.shape[0]
    wt = jnp.concatenate([
        jnp.transpose(w_er.astype(F32), (0, 2, 1)).reshape(MOE_EXPERTS, d),
        jnp.transpose(w_group.astype(F32)),
        jnp.zeros((ROUTER_ROWS - MOE_EXPERTS - MOE_GROUPS, d), F32)], axis=0)
    bias = jnp.concatenate([
        b_er.astype(F32).reshape(-1), b_group.astype(F32),
        jnp.zeros((ROUTER_ROWS - MOE_EXPERTS - MOE_GROUPS,), F32)]).reshape(ROUTER_ROWS, 1)
    hi = wt.astype(BF16)
    lo = (wt - hi.astype(F32)).astype(BF16)
    return hi, lo, bias


def _route_kernel(lt_ref, eid_ref, gate_ref, rank_ref, cnt_ref, run_ref):
    i = pl.program_id(0)
    tm = lt_ref.shape[1]
    ne, npg, ng = MOE_EXPERTS, MOE_EXPERTS_PER_GROUP, MOE_GROUPS

    @pl.when(i == 0)
    def _():
        run_ref[...] = jnp.zeros_like(run_ref)

    gl = lt_ref[ne:ne + ng, :]
    gmax = jnp.max(gl, axis=0, keepdims=True)
    gidx = lax.broadcasted_iota(I32, (ng, tm), 0)
    gsel = jnp.min(jnp.where(gl == gmax, gidx, ng), axis=0, keepdims=True)
    p_g = 1.0 / jnp.sum(jnp.exp(gl - gmax), axis=0, keepdims=True)

    e8 = lt_ref[(ng - 1) * npg:ng * npg, :]
    for g in range(ng - 2, -1, -1):
        e8 = jnp.where(gsel == g, lt_ref[g * npg:(g + 1) * npg, :], e8)
    eidx = lax.broadcasted_iota(I32, (npg, tm), 0)
    m1 = jnp.max(e8, axis=0, keepdims=True)
    i1 = jnp.min(jnp.where(e8 == m1, eidx, npg), axis=0, keepdims=True)
    e8b = jnp.where(eidx == i1, -jnp.inf, e8)
    m2 = jnp.max(e8b, axis=0, keepdims=True)
    i2 = jnp.min(jnp.where(e8b == m2, eidx, npg), axis=0, keepdims=True)
    t2 = jnp.exp(m2 - m1)
    den = 1.0 + t2
    gate_ref[0:1, :] = (1.0 / den) * p_g
    gate_ref[1:2, :] = (t2 / den) * p_g
    id1 = gsel * npg + i1
    id2 = gsel * npg + i2
    eid_ref[0:1, :] = id1
    eid_ref[1:2, :] = id2

    rows = lax.broadcasted_iota(I32, (ne, tm), 0)
    oh1 = rows == id1
    oh2 = rows == id2
    both = jnp.where(oh1, 1.0, 0.0) + jnp.where(oh2, 1.0, 0.0)
    earlier = (lax.broadcasted_iota(I32, (tm, tm), 0) < lax.broadcasted_iota(I32, (tm, tm), 1))
    prefix = jnp.dot(both.astype(BF16), jnp.where(earlier, 1.0, 0.0).astype(BF16),
                     preferred_element_type=F32)
    base = prefix + run_ref[:, 0:1]
    rank_ref[0:1, :] = jnp.sum(jnp.where(oh1, base, 0.0), axis=0, keepdims=True).astype(I32)
    rank_ref[1:2, :] = jnp.sum(jnp.where(oh2, base, 0.0), axis=0, keepdims=True).astype(I32)
    run = run_ref[...] + jnp.sum(both, axis=1, keepdims=True)
    run_ref[...] = run
    cnt_ref[...] = run.astype(I32)


def route(lt, tm=512):
    nr, t = lt.shape
    two = lambda dt: jax.ShapeDtypeStruct((2, t), dt)
    return pl.pallas_call(
        _route_kernel,
        grid=(t // tm,),
        in_specs=[pl.BlockSpec((nr, tm), lambda i: (0, i))],
        out_specs=[
            pl.BlockSpec((2, tm), lambda i: (0, i)),
            pl.BlockSpec((2, tm), lambda i: (0, i)),
            pl.BlockSpec((2, tm), lambda i: (0, i)),
            pl.BlockSpec((MOE_EXPERTS, LANES), lambda i: (0, 0)),
        ],
        out_shape=[two(I32), two(F32), two(I32),
                   jax.ShapeDtypeStruct((MOE_EXPERTS, LANES), I32)],
        scratch_shapes=[pltpu.VMEM((MOE_EXPERTS, LANES), F32)],
        compiler_params=_params(("arbitrary",)),
        name="route",
    )(lt)


def _dispatch_kernel(pad_ref, nu_ref, d0_ref, d1_ref, h_ref, xbuf_ref, zero_ref, sem, zsem):
    i = pl.program_id(0)
    tm = h_ref.shape[0]
    dests = (d0_ref, d1_ref)
    bm = zero_ref.shape[0]
    nblk = xbuf_ref.shape[0] // bm

    def zero_fill(act):
        def per_expert(e, carry):
            pos = pad_ref[0, e]
            length = pad_ref[1, e]
            p = bm // 2
            while p >= 1:
                bit = (length & p) != 0

                @pl.when(bit)
                def _(pos=pos, p=p):
                    act(pltpu.make_async_copy(zero_ref.at[pl.ds(0, p)],
                                              xbuf_ref.at[pl.ds(pos, p)], zsem))

                pos = pos + jnp.where(bit, p, 0)
                p //= 2
            return carry

        lax.fori_loop(0, pad_ref.shape[1], per_expert, 0)

        def per_block(b, carry):
            act(pltpu.make_async_copy(zero_ref, xbuf_ref.at[pl.ds(b * bm, bm)], zsem))
            return carry

        lax.fori_loop(nu_ref[0], nblk, per_block, 0)

    @pl.when(i == 0)
    def _():
        zero_ref[...] = jnp.zeros_like(zero_ref)
        zero_fill(lambda c: c.start())

    def copy(r, k):
        return pltpu.make_async_copy(h_ref.at[r], xbuf_ref.at[dests[k][0, r]], sem)

    def start(r, carry):
        copy(r, 0).start(priority=0)
        copy(r, 1).start(priority=1)
        return carry

    lax.fori_loop(0, tm, start, 0, unroll=8)

    def wait(r, carry):
        copy(r, 0).wait()
        copy(r, 1).wait()
        return carry

    lax.fori_loop(0, tm, wait, 0, unroll=8)

    @pl.when(i == pl.num_programs(0) - 1)
    def _():
        zero_fill(lambda c: c.wait())


def _row_index_spec(tm, ahead=0, last=None):
    def index(i):
        return (i if ahead == 0 else jnp.minimum(i + ahead, last), 0, 0)
    return pl.BlockSpec((None, 1, tm), index, memory_space=pltpu.SMEM)


def dispatch(pad, n_used, dest0, dest1, h3, n_rows):
    t, s, lanes = h3.shape
    nt, _, tm = dest0.shape
    return pl.pallas_call(
        _dispatch_kernel,
        grid=(nt,),
        in_specs=[
            pl.BlockSpec(memory_space=pltpu.SMEM),
            pl.BlockSpec(memory_space=pltpu.SMEM),
            _row_index_spec(tm), _row_index_spec(tm),
            pl.BlockSpec((tm, s, lanes), lambda i: (i, 0, 0)),
        ],
        out_specs=pl.BlockSpec(memory_space=pl.ANY),
        out_shape=jax.ShapeDtypeStruct((n_rows, s, lanes), h3.dtype),
        scratch_shapes=[pltpu.VMEM((MOE_ROWS, s, lanes), h3.dtype),
                        pltpu.SemaphoreType.DMA(()), pltpu.SemaphoreType.DMA(())],
        compiler_params=_params(("arbitrary",)),
        name="moe_dispatch",
    )(pad, n_used, dest0, dest1, h3)


def _experts_kernel(be_ref, nu_ref, nx_ref, x_ref, wg_hbm, wu_hbm, wd_hbm, o_ref,
                    wg_f32, wu_f32, wd_f32, wgb, wub, wdb, slot_ref, sem, *, layer):
    i = pl.program_id(0)
    e = be_ref[i]
    first = i == 0
    changed = jnp.logical_or(first, e != be_ref[jnp.maximum(i - 1, 0)])
    streams = ((wg_hbm, wg_f32), (wu_hbm, wu_f32), (wd_hbm, wd_f32))

    def fetch(expert, slot):
        return [pltpu.make_async_copy(w_hbm.at[layer, expert], w_f32.at[slot], sem.at[slot, j])
                for j, (w_hbm, w_f32) in enumerate(streams)]

    @pl.when(first)
    def _():
        slot_ref[0] = 1
        for c in fetch(e, 0):
            c.start()

    @pl.when(changed)
    def _():
        slot = 1 - slot_ref[0]
        slot_ref[0] = slot
        for c in fetch(e, slot):
            c.wait()
        nxt = nx_ref[i]

        @pl.when(nxt >= 0)
        def _():
            for c in fetch(nxt, 1 - slot):
                c.start()

        wgb[...] = wg_f32[slot].astype(BF16)
        wub[...] = wu_f32[slot].astype(BF16)
        wdb[...] = wd_f32[slot].astype(BF16)

    @pl.when(i < nu_ref[0])
    def _():
        bm, s, lanes = x_ref.shape
        x = x_ref[...].reshape(bm, s * lanes)
        g = jnp.dot(x, wgb[...], preferred_element_type=F32)
        u = jnp.dot(x, wub[...], preferred_element_type=F32)
        a = ((g * jax.nn.sigmoid(g)) * u).astype(BF16)
        y = jnp.dot(a, wdb[...], preferred_element_type=F32)
        o_ref[...] = y.astype(BF16).reshape(o_ref.shape)

    @pl.when(i >= nu_ref[0])
    def _():
        o_ref[...] = jnp.zeros_like(o_ref)


def experts(layer, block_expert, n_used, next_expert, xbuf, w_gate, w_up, w_down):
    n_rows, s, lanes = xbuf.shape
    d = s * lanes
    hid = w_gate.shape[3]
    bm = MOE_ROWS
    nblk = n_rows // bm
    grid_spec = pltpu.PrefetchScalarGridSpec(
        num_scalar_prefetch=3,
        grid=(nblk,),
        in_specs=[
            pl.BlockSpec((bm, s, lanes), lambda i, be, nu, nx: (jnp.minimum(i, nu[0] - 1), 0, 0)),
            pl.BlockSpec(memory_space=pl.ANY),
            pl.BlockSpec(memory_space=pl.ANY),
            pl.BlockSpec(memory_space=pl.ANY),
        ],
        out_specs=pl.BlockSpec((bm, s, lanes), lambda i, be, nu, nx: (i, 0, 0)),
        scratch_shapes=[pltpu.VMEM((2, d, hid), F32), pltpu.VMEM((2, d, hid), F32),
                        pltpu.VMEM((2, hid, d), F32),
                        pltpu.VMEM((d, hid), BF16), pltpu.VMEM((d, hid), BF16),
                        pltpu.VMEM((hid, d), BF16),
                        pltpu.SMEM((1,), I32), pltpu.SemaphoreType.DMA((2, 3))],
    )
    return pl.pallas_call(
        functools.partial(_experts_kernel, layer=layer),
        grid_spec=grid_spec,
        out_shape=jax.ShapeDtypeStruct((n_rows, s, lanes), BF16),
        compiler_params=_params(("arbitrary",)),
        name="moe_experts",
    )(block_expert, n_used, next_expert, xbuf, w_gate, w_up, w_down)


def _combine_kernel(d0_ref, d1_ref, n0_ref, n1_ref, gate_ref, x_ref, g_ref, ybuf_ref, o_ref,
                    buf, sem, *, final_norm):
    i = pl.program_id(0)
    tm, d = x_ref.shape
    slot = i % 2

    last = i + 1 == pl.num_programs(0)
    other = 1 - slot

    def copy(dests, s, r, k):
        return pltpu.make_async_copy(ybuf_ref.at[dests[k][0, r]], buf.at[s, k, r], sem.at[s])

    def start(dests, s, r):
        copy(dests, s, r, 0).start(priority=0)
        copy(dests, s, r, 1).start(priority=1)

    def wait_all(s):
        def wait(r, carry):
            copy((d0_ref, d1_ref), s, r, 0).wait()
            copy((d0_ref, d1_ref), s, r, 1).wait()
            return carry

        lax.fori_loop(0, tm, wait, 0, unroll=8)

    @pl.when(i == 0)
    def _():
        lax.fori_loop(0, tm, lambda r, c: (start((d0_ref, d1_ref), 0, r), c)[1], 0, unroll=8)

    wait_all(slot)

    group = tm // COMBINE_GROUPS
    for p in range(COMBINE_GROUPS):
        rows = slice(p * group, (p + 1) * group)
        for r in range(p * group, (p + 1) * group):
            start((n0_ref, n1_ref), other, r)
        gates = gate_ref[rows, :]
        y = (gates[:, 0:1] * buf[slot, 0, rows].reshape(group, d).astype(F32)
             + gates[:, 1:2] * buf[slot, 1, rows].reshape(group, d).astype(F32))
        out = x_ref[rows, :] + y
        if final_norm:
            out = _rms(out, g_ref[...])
        o_ref[rows, :] = out

    @pl.when(last)
    def _():
        wait_all(other)


def combine(dest0, dest1, gates_t, x, ybuf, g_final, final_norm):
    t, d = x.shape
    nt, _, tm = dest0.shape
    _, s, lanes = ybuf.shape
    return pl.pallas_call(
        functools.partial(_combine_kernel, final_norm=final_norm),
        grid=(nt,),
        in_specs=[
            _row_index_spec(tm), _row_index_spec(tm),
            _row_index_spec(tm, ahead=1, last=nt - 1), _row_index_spec(tm, ahead=1, last=nt - 1),
            pl.BlockSpec((tm, 2), lambda i: (i, 0)),
            pl.BlockSpec((tm, d), lambda i: (i, 0)),
            pl.BlockSpec((1, d), lambda i: (0, 0)),
            pl.BlockSpec(memory_space=pl.ANY),
        ],
        out_specs=pl.BlockSpec((tm, d), lambda i: (i, 0)),
        out_shape=jax.ShapeDtypeStruct((t, d), F32),
        scratch_shapes=[pltpu.VMEM((2, 2, tm, s, lanes), ybuf.dtype),
                        pltpu.SemaphoreType.DMA((2,))],
        compiler_params=_params(("arbitrary",)),
        name="moe_combine",
    )(dest0, dest1, dest0, dest1, gates_t, x, g_final.reshape(1, d), ybuf)


def hier_moe_block(layer, x1, h, lt, w_gate, w_up, w_down, g_final, final_norm, tm=MOE_TOKENS):
    t, d = x1.shape
    bm = MOE_ROWS
    eid, gate, rank, cnt = route(lt)
    counts = cnt[:, 0]
    padded = ((counts + bm - 1) // bm) * bm
    pend = jnp.cumsum(padded)
    pstart = pend - padded
    experts_col = jnp.arange(MOE_EXPERTS, dtype=I32)[:, None, None]
    dest = rank + jnp.sum(jnp.where(eid[None] == experts_col, pstart[:, None, None], 0), axis=0)
    n_rows = 2 * t + MOE_EXPERTS * bm
    nblk = n_rows // bm
    n_used = (pend[-1] // bm).astype(I32)
    first_row = jnp.minimum(jnp.arange(nblk, dtype=I32), n_used - 1) * bm
    block_expert = jnp.sum(pend[None, :] <= first_row[:, None], axis=1).astype(I32)
    block_expert = jnp.minimum(block_expert, MOE_EXPERTS - 1)
    ids = jnp.arange(MOE_EXPERTS, dtype=I32)
    later = jnp.logical_and(ids[None, :] > block_expert[:, None], counts[None, :] > 0)
    next_expert = jnp.min(jnp.where(later, ids[None, :], MOE_EXPERTS), axis=1)
    next_expert = jnp.where(next_expert < MOE_EXPERTS, next_expert, -1).astype(I32)
    dest0 = dest[0].reshape(t // tm, 1, tm)
    dest1 = dest[1].reshape(t // tm, 1, tm)
    pad = jnp.stack([pstart + counts, padded - counts]).astype(I32)
    xbuf = dispatch(pad, n_used.reshape(1), dest0, dest1, h, n_rows)
    ybuf = experts(layer, block_expert, n_used.reshape(1), next_expert, xbuf, w_gate, w_up, w_down)
    return combine(dest0, dest1, jnp.transpose(gate), x1, ybuf, g_final, final_norm)


def _attn_kernel(sink_ref, q_ref, kp_ref, kc_ref, kn_ref, vp_ref, vc_ref, vn_ref, bias_ref, o_ref):
    n = pl.program_id(1)
    nblk = pl.num_programs(1)
    blk = q_ref.shape[0]
    dh = SWA_HEAD_DIM
    nkv = kc_ref.shape[1] // dh
    masked = bias_ref.shape[1] - 1
    part_prev = jnp.where(n > 0, 0, masked)
    part_next = jnp.where(n < nblk - 1, 2, masked)
    lo = lax.broadcasted_iota(I32, (blk, 2 * dh), 1) < dh
    top = lax.broadcasted_iota(I32, (2 * blk, 2 * dh), 0) < blk
    lo2 = lax.broadcasted_iota(I32, (2 * blk, 2 * dh), 1) < dh
    ones_ext = jnp.concatenate([jnp.where(lo, 1.0, 0.0), jnp.where(lo, 0.0, 1.0)], axis=0).astype(BF16)
    qscale = (dh ** -0.5) * LOG2E

    for kv in range(nkv):
        col = slice((kv // 2) * 2 * dh, (kv // 2 + 1) * 2 * dh)

        def extend(ref):
            x = ref[:, col].astype(F32)
            r = pltpu.roll(x, dh, 1)
            x_lo, x_hi = (x, r) if kv % 2 == 0 else (r, x)
            return jnp.concatenate([jnp.where(lo, x_lo, 0.0), jnp.where(lo, 0.0, x_hi)],
                                   axis=0).astype(BF16)

        q2 = jnp.concatenate([q_ref[:, (2 * kv) * 2 * dh:(2 * kv + 1) * 2 * dh],
                              q_ref[:, (2 * kv + 1) * 2 * dh:(2 * kv + 2) * 2 * dh]], axis=0)
        q2 = (q2.astype(F32) * qscale).astype(BF16)

        def scores(k_ref, part):
            return lax.dot_general(q2, extend(k_ref), NT_DIMS,
                                   preferred_element_type=F32) + bias_ref[kv, part]

        s = [scores(kp_ref, part_prev), scores(kc_ref, 1), scores(kn_ref, part_next)]
        mx = jnp.maximum(jnp.maximum(s[0], s[1]), s[2])
        sk = [jnp.where(top, sink_ref[0, 4 * kv + par], sink_ref[0, 4 * kv + 2 + par])
              for par in range(2)]
        m = [jnp.maximum(jnp.broadcast_to(jnp.max(mx[:, par * blk:(par + 1) * blk], axis=-1,
                                                  keepdims=True), (2 * blk, 2 * dh)), sk[par])
             for par in range(2)]
        acc = jnp.zeros((2 * blk, 4 * dh), F32)
        for sp, v_ref in zip(s, (vp_ref, vc_ref, vn_ref)):
            e = jnp.concatenate([jnp.exp2(sp[:, :blk] - m[0]), jnp.exp2(sp[:, blk:] - m[1])],
                                axis=1).astype(BF16)
            rhs = jnp.concatenate([extend(v_ref), ones_ext], axis=1)
            acc = acc + jnp.dot(e, rhs, preferred_element_type=F32)
        den = acc[:, 2 * dh:] + jnp.exp2(jnp.where(lo2, sk[0] - m[0], sk[1] - m[1]))
        o = (acc[:, :2 * dh] / den).astype(BF16)
        o_ref[:, (2 * kv) * 2 * dh:(2 * kv + 1) * 2 * dh] = o[:blk]
        o_ref[:, (2 * kv + 1) * 2 * dh:(2 * kv + 2) * 2 * dh] = o[blk:]


def banded_attention(qkv3, sink, bias):
    b, seq, width = qkv3.shape
    nh = sink.shape[1]
    qd = nh * SWA_HEAD_DIM
    kvd = (width - qd) // 2
    blk = SWA_BLOCK
    nblk = seq // blk
    kcol, vcol = qd // kvd, qd // kvd + 1

    def band(col, off):
        return pl.BlockSpec((None, blk, kvd),
                            lambda bi, n: (bi, jnp.clip(n + off, 0, nblk - 1), col))

    return pl.pallas_call(
        _attn_kernel,
        grid=(b, nblk),
        in_specs=[
            pl.BlockSpec(memory_space=pltpu.SMEM),
            pl.BlockSpec((None, blk, qd), lambda bi, n: (bi, n, 0)),
            band(kcol, -1), band(kcol, 0), band(kcol, 1),
            band(vcol, -1), band(vcol, 0), band(vcol, 1),
            pl.BlockSpec(bias.shape, lambda bi, n: (0, 0, 0, 0)),
        ],
        out_specs=pl.BlockSpec((None, blk, qd), lambda bi, n: (bi, n, 0)),
        out_shape=jax.ShapeDtypeStruct((b, seq, qd), BF16),
        compiler_params=_params(("parallel", "parallel")),
        name="banded_attention",
    )(sink, qkv3, qkv3, qkv3, qkv3, qkv3, qkv3, qkv3, bias)


def _t5_bucket(rel):
    half = REL_BUCKETS // 2
    max_exact = half // 2
    n = jnp.abs(rel)
    large = max_exact + (jnp.log(jnp.maximum(n, 1).astype(F32) / max_exact)
                         / math.log(REL_MAX_DIST / max_exact) * (half - max_exact)).astype(I32)
    large = jnp.minimum(large, half - 1)
    return jnp.where(rel > 0, half, 0) + jnp.where(n < max_exact, n, large)


def _attention_bias(rel_bias):
    blk = SWA_BLOCK
    nh = rel_bias.shape[1]
    nkv = nh // SWA_Q_PER_KV
    rel = jnp.arange(3 * blk)[None, :] - blk - jnp.arange(blk)[:, None]
    onehot = (_t5_bucket(rel)[..., None] == jnp.arange(REL_BUCKETS)).astype(F32)
    bias = jnp.einsum('ijb,bh->hij', onehot, rel_bias.astype(F32), precision=lax.Precision.HIGHEST)
    bias = jnp.where((jnp.abs(rel) <= SWA_WINDOW)[None], bias, NEG_INF) * LOG2E
    tiles = bias.reshape(nkv, 2, 2, blk, 3, blk).transpose(0, 4, 1, 3, 2, 5)
    tiles = tiles.reshape(nkv, 3, 2 * blk, 2 * blk)
    masked = jnp.full((nkv, 1, 2 * blk, 2 * blk), NEG_INF * LOG2E, F32)
    return jnp.concatenate([tiles, masked], axis=1)


def _rotary_tables(seq, dim):
    inv_freq = ROPE_BASE ** (-jnp.arange(0, dim, 2, dtype=F32) / dim)
    ang = jnp.arange(seq, dtype=F32)[:, None] * inv_freq[None, :]
    cos, sin = jnp.cos(ang), jnp.sin(ang)
    return jnp.concatenate([cos, cos], axis=1), jnp.concatenate([-sin, sin], axis=1)


def kernel(x, norm_mix_g, norm_ffn_g, norm_final_g, hyb_w_in, ret_decay_logit, s5_a_re, s5_a_im, s5_log_step, s5_b_re, s5_b_im, s5_c_re, s5_c_im, s5_d, s5_w_glu, s5_b_glu, hyb_w_out, swa_w_qkv, swa_sink, swa_w_o, rel_bias, moe_w_group, moe_b_group, moe_w_expert_router, moe_b_expert_router, moe_w_gate, moe_w_up, moe_w_down):
    nb, seq, d = x.shape
    t = nb * seq
    depth = norm_mix_g.shape[0]
    xt = x.reshape(t, d)
    for layer in range(depth):
        i = layer // 2
        if layer % 2 == 0:
            w = hyb_w_out.shape[1] // 2
            z = norm_matmul(xt, norm_mix_g[layer], hyb_w_in[i])
            z3 = z.reshape(nb, seq, z.shape[1])
            cos, sin = _rotary_tables(seq, w // RET_HEADS)
            log_gamma = jax.nn.log_sigmoid(ret_decay_logit[i].astype(F32))
            y_ret = retention(z3, log_gamma, cos, sin).reshape(t, w)
            lam, bmat, cmat = _s5_discretize(s5_a_re[i], s5_a_im[i], s5_log_step[i], s5_b_re[i],
                                             s5_b_im[i], s5_c_re[i], s5_c_im[i])
            ys5 = s5_scan(z3, lam, bmat, cmat).reshape(2, t, w)
            mix_in = (y_ret, ys5, z, s5_d[i].astype(F32).reshape(-1), s5_w_glu[i],
                      s5_b_glu[i].astype(F32))
            w_mix = hyb_w_out[i]
        else:
            qkv = norm_matmul(xt, norm_mix_g[layer], swa_w_qkv[i])
            mix_in = banded_attention(qkv.reshape(nb, seq, qkv.shape[1]),
                                      swa_sink[i].astype(F32).reshape(1, -1) * LOG2E,
                                      _attention_bias(rel_bias)).reshape(t, -1)
            w_mix = swa_w_o[i]
        r_hi, r_lo, r_bias = _router_operands(moe_w_group[layer], moe_b_group[layer],
                                              moe_w_expert_router[layer], moe_b_expert_router[layer])
        x1, h, lt = proj_norm_router(mix_in, w_mix, xt, norm_ffn_g[layer], r_hi, r_lo, r_bias)
        last = layer == depth - 1
        xt = hier_moe_block(layer, x1, h, lt, moe_w_gate, moe_w_up, moe_w_down,
                            norm_final_g, final_norm=last)
    return xt.reshape(nb, seq, d)
```

```python
import functools
import math

import jax
import jax.numpy as jnp
from jax import lax
from jax.experimental import pallas as pl
from jax.experimental.pallas import tpu as pltpu

F32 = jnp.float32
BF16 = jnp.bfloat16
I32 = jnp.int32

RET_HEADS = 4
RET_CHUNK = 128
S5_GROUP_CH = 16
S5_STATE = 64
SWA_HEAD_DIM = 64
SWA_Q_PER_KV = 4
SWA_WINDOW = 128
SWA_BLOCK = 128
REL_BUCKETS = 32
REL_MAX_DIST = 128
MOE_GROUPS = 4
MOE_EXPERTS_PER_GROUP = 8
MOE_EXPERTS = MOE_GROUPS * MOE_EXPERTS_PER_GROUP
ROPE_BASE = 10000.0
RMS_EPS = 1e-6
GN_EPS = 1e-5
NEG_INF = -1e30
LOG2E = 1.4426950408889634

LANES = 128
SUBLANES = 8
V7X_VMEM_BYTES = 64 * 1024 * 1024
VMEM_LIMIT = V7X_VMEM_BYTES - 8 * 1024 * 1024

DENSE_ROWS = 1024
MOE_ROWS = 512
MOE_TOKENS = 1024
COMBINE_GROUPS = 16
S5_CHUNK = 128
S5_COLS = 512
S5_SCAN_COLS = 512
S5_KBLK = 128
ROUTER_ROWS = 128

NT_DIMS = (((1,), (1,)), ((), ()))
TN_DIMS = (((0,), (0,)), ((), ()))


def _params(semantics):
    return pltpu.CompilerParams(dimension_semantics=semantics, vmem_limit_bytes=VMEM_LIMIT)


def _rms(x, g):
    ms = jnp.mean(x * x, axis=-1, keepdims=True)
    return (x * lax.rsqrt(ms + RMS_EPS)) * g


def _norm_matmul_kernel(x_ref, g_ref, w_ref, o_ref):
    h = _rms(x_ref[...], g_ref[...])
    o_ref[...] = jnp.dot(h.astype(BF16), w_ref[...],
                         preferred_element_type=F32).astype(o_ref.dtype)


def norm_matmul(x, g, w, tm=DENSE_ROWS):
    t, d = x.shape
    tm = min(tm, t)
    n = w.shape[1]
    return pl.pallas_call(
        _norm_matmul_kernel,
        grid=(t // tm,),
        in_specs=[
            pl.BlockSpec((tm, d), lambda i: (i, 0)),
            pl.BlockSpec((1, d), lambda i: (0, 0)),
            pl.BlockSpec((d, n), lambda i: (0, 0)),
        ],
        out_specs=pl.BlockSpec((tm, n), lambda i: (i, 0)),
        out_shape=jax.ShapeDtypeStruct((t, n), BF16),
        compiler_params=_params(("parallel",)),
        name="norm_matmul",
    )(x, g.reshape(1, d), w.astype(BF16))


def _retention_kernel(lg_ref, q_ref, k_ref, v_ref, g_ref, cos_ref, sin_ref, o_ref,
                      qr_ref, kr_ref, inc_ref, st_ref, lhs_ref):
    h = pl.program_id(1)
    lg_f = lg_ref[0, h]
    lg_b = lg_ref[1, h]
    seq, dk = q_ref.shape
    c = RET_CHUNK
    nc = seq // c

    cos = cos_ref[...]
    sin = sin_ref[...]
    swap = (lax.broadcasted_iota(I32, (dk, dk), 0)
            == (lax.broadcasted_iota(I32, (dk, dk), 1) + dk // 2) % dk)
    swap = jnp.where(swap, 1.0, 0.0).astype(BF16)
    q = q_ref[...]
    qr_ref[...] = (q.astype(F32) * cos
                   + jnp.dot(q, swap, preferred_element_type=F32) * sin)
    k = k_ref[...]
    kr_ref[...] = (k.astype(F32) * cos
                   + jnp.dot(k, swap, preferred_element_type=F32) * sin) * (dk ** -0.5)

    pos = lax.broadcasted_iota(I32, (c, dk), 0).astype(F32)
    kf_scale = jnp.exp((c - 1.0 - pos) * lg_f)
    qf_scale = jnp.exp((pos + 1.0) * lg_f)
    kb_scale = jnp.exp(pos * lg_b)
    qb_scale = jnp.exp((c - pos) * lg_b)
    rel = (lax.broadcasted_iota(I32, (c, c), 0) - lax.broadcasted_iota(I32, (c, c), 1)).astype(F32)
    mask = jnp.exp(jnp.abs(rel) * jnp.where(rel >= 0, lg_f, lg_b))
    dec_f = jnp.exp(jnp.full((dk, dk), c * lg_f, F32))
    dec_b = jnp.exp(jnp.full((dk, dk), c * lg_b, F32))

    def increments(n, carry):
        rows = pl.ds(pl.multiple_of(n * c, c), c)
        kc = kr_ref[rows, :]
        kk = jnp.concatenate([kc * kf_scale, kc * kb_scale], axis=1).astype(BF16)
        inc_ref[n] = lax.dot_general(kk, v_ref[rows, :], TN_DIMS,
                                     preferred_element_type=F32)
        return carry

    lax.fori_loop(0, nc, increments, 0, unroll=True)

    def fwd(n, state):
        st_ref[n, :dk, :] = state.astype(BF16)
        return state * dec_f + inc_ref[n, :dk, :]

    lax.fori_loop(0, nc, fwd, jnp.zeros((dk, dk), F32), unroll=True)

    def bwd(i, state):
        n = nc - 1 - i
        st_ref[n, dk:, :] = state.astype(BF16)
        return state * dec_b + inc_ref[n, dk:, :]

    lax.fori_loop(0, nc, bwd, jnp.zeros((dk, dk), F32), unroll=True)

    def operands(n, carry):
        rows = pl.ds(pl.multiple_of(n * c, c), c)
        qc = qr_ref[rows, :]
        s = lax.dot_general(qc.astype(BF16), kr_ref[rows, :].astype(BF16), NT_DIMS,
                            preferred_element_type=F32) * mask
        lhs_ref[n] = jnp.concatenate([s, qc * qf_scale, qc * qb_scale], axis=1).astype(BF16)
        return carry

    lax.fori_loop(0, nc, operands, 0, unroll=True)

    def outputs(n, carry):
        rows = pl.ds(pl.multiple_of(n * c, c), c)
        rhs = jnp.concatenate([v_ref[rows, :], st_ref[n]], axis=0)
        kr_ref[rows, :] = jnp.dot(lhs_ref[n], rhs, preferred_element_type=F32)
        return carry

    lax.fori_loop(0, nc, outputs, 0, unroll=True)

    out = kr_ref[...]
    mu = jnp.mean(out, axis=-1, keepdims=True)
    cen = out - mu
    var = jnp.mean(cen * cen, axis=-1, keepdims=True)
    g = g_ref[...].astype(F32)
    o_ref[...] = ((g * jax.nn.sigmoid(g)) * (cen * lax.rsqrt(var + GN_EPS))).astype(BF16)


def retention(z3, log_gamma, cos, sin):
    b, seq, _ = z3.shape
    nh = RET_HEADS
    dk = cos.shape[1]

    def col(off):
        return pl.BlockSpec((None, seq, dk), lambda bi, hi: (bi, 0, off + hi))

    return pl.pallas_call(
        _retention_kernel,
        grid=(b, nh),
        in_specs=[
            pl.BlockSpec(memory_space=pltpu.SMEM),
            col(0), col(nh), col(2 * nh), col(3 * nh),
            pl.BlockSpec((seq, dk), lambda bi, hi: (0, 0)),
            pl.BlockSpec((seq, dk), lambda bi, hi: (0, 0)),
        ],
        out_specs=pl.BlockSpec((None, seq, dk), lambda bi, hi: (bi, 0, hi)),
        out_shape=jax.ShapeDtypeStruct((b, seq, nh * dk), BF16),
        scratch_shapes=[pltpu.VMEM((seq, dk), F32), pltpu.VMEM((seq, dk), F32),
                        pltpu.VMEM((seq // RET_CHUNK, 2 * dk, dk), F32),
                        pltpu.VMEM((seq // RET_CHUNK, 2 * dk, dk), BF16),
                        pltpu.VMEM((seq // RET_CHUNK, RET_CHUNK, 3 * dk), BF16)],
        compiler_params=_params(("parallel", "parallel")),
        name="retention",
    )(log_gamma, z3, z3, z3, z3, cos, sin)


def _s5_kernel(u_ref, lam_ref, b_ref, c_ref, y_ref, sk_ref, utm_ref, ytm_ref, sre_ref, sim_ref,
               *hist):
    nkb = len(hist) // 2

    def xre(plane):
        return hist[plane // (S5_COLS // LANES)].at[plane % (S5_COLS // LANES)]

    def xim(plane):
        return hist[nkb + plane // (S5_COLS // LANES)].at[plane % (S5_COLS // LANES)]

    d = pl.program_id(0)
    n = pl.program_id(1)
    nb, cn, width = u_ref.shape
    srows = sk_ref.shape[1] // nb
    nk = width // S5_KBLK
    npl = width // LANES

    @pl.when(n == 0)
    def _():
        sre_ref[...] = jnp.zeros_like(sre_ref)
        sim_ref[...] = jnp.zeros_like(sim_ref)

    def skew_rows(b):
        return slice(b * srows + b, b * srows + b + cn)

    for b in range(nb):
        ub = u_ref[b].astype(F32)
        for p in range(npl):
            sk_ref[p, skew_rows(b), :] = ub[:, p * LANES:(p + 1) * LANES]

    def to_time_major(t, carry):
        rows = pl.ds(pl.multiple_of(t * nb, nb), nb)
        for p in range(npl):
            utm_ref[rows, p * LANES:(p + 1) * LANES] = sk_ref[p, pl.ds(t, nb, stride=srows + 1), :]
        return carry

    lax.fori_loop(0, cn, to_time_major, 0, unroll=8)

    u = utm_ref[...].astype(BF16)
    ppc = S5_COLS // LANES
    for kb in range(nk):
        bu = jnp.dot(u[:, kb * S5_KBLK:(kb + 1) * S5_KBLK], b_ref[kb], preferred_element_type=F32)
        for j in range(ppc):
            xre(kb * ppc + j)[...] = bu[:, j * LANES:(j + 1) * LANES]
            xim(kb * ppc + j)[...] = bu[:, S5_COLS + j * LANES:S5_COLS + (j + 1) * LANES]

    pps = S5_SCAN_COLS // LANES
    for cb in range(sre_ref.shape[1] // S5_SCAN_COLS):
        cols = slice(cb * S5_SCAN_COLS, (cb + 1) * S5_SCAN_COLS)
        lr = jnp.broadcast_to(lam_ref[0:1, cols], (nb, S5_SCAN_COLS))
        li = jnp.broadcast_to(lam_ref[1:2, cols], (nb, S5_SCAN_COLS))

        def step(i, carry):
            xr, xi = carry
            t = i + d * (cn - 1 - 2 * i)
            rows = pl.ds(pl.multiple_of(t * nb, nb), nb)
            bur = jnp.concatenate([xre(cb * pps + j)[rows, :] for j in range(pps)], axis=1)
            bui = jnp.concatenate([xim(cb * pps + j)[rows, :] for j in range(pps)], axis=1)
            nxr = lr * xr - li * xi + bur
            nxi = lr * xi + li * xr + bui
            for j in range(pps):
                xre(cb * pps + j)[rows, :] = nxr[:, j * LANES:(j + 1) * LANES]
                xim(cb * pps + j)[rows, :] = nxi[:, j * LANES:(j + 1) * LANES]
            return nxr, nxi

        xr, xi = lax.fori_loop(0, cn, step, (sre_ref[:, cols], sim_ref[:, cols]), unroll=True)
        sre_ref[:, cols] = xr
        sim_ref[:, cols] = xi

    for kb in range(nk):
        xr = jnp.concatenate([xre(kb * ppc + j)[...] for j in range(ppc)], axis=1).astype(BF16)
        xi = jnp.concatenate([xim(kb * ppc + j)[...] for j in range(ppc)], axis=1).astype(BF16)
        y = jnp.dot(xr, c_ref[kb, :S5_COLS, :], preferred_element_type=F32)
        ytm_ref[kb] = y + jnp.dot(xi, c_ref[kb, S5_COLS:, :], preferred_element_type=F32)

    def to_batch_major(t, carry):
        rows = pl.ds(pl.multiple_of(t * nb, nb), nb)
        for p in range(npl):
            sk_ref[p, pl.ds(t, nb, stride=srows + 1), :] = ytm_ref[p, rows, :]
        return carry

    lax.fori_loop(0, cn, to_batch_major, 0, unroll=8)

    for b in range(nb):
        for p in range(npl):
            y_ref[b, :, p * LANES:(p + 1) * LANES] = sk_ref[p, skew_rows(b), :]


def s5_scan(z3, lam, bmat, cmat):
    nb, seq, zw = z3.shape
    width = bmat.shape[1] * bmat.shape[2]
    nstate = lam.shape[2]
    cn = min(S5_CHUNK, seq)
    nch = seq // cn
    srows = cn + SUBLANES
    ucol = zw // width - 1

    def chunk(d, n):
        return n + d * (nch - 1 - 2 * n)

    return pl.pallas_call(
        _s5_kernel,
        grid=(2, nch),
        in_specs=[
            pl.BlockSpec((nb, cn, width), lambda d, n: (0, chunk(d, n), ucol)),
            pl.BlockSpec((None, 2, nstate), lambda d, n: (d, 0, 0)),
            pl.BlockSpec((None,) + bmat.shape[1:], lambda d, n: (d, 0, 0, 0)),
            pl.BlockSpec((None,) + cmat.shape[1:], lambda d, n: (d, 0, 0, 0)),
        ],
        out_specs=pl.BlockSpec((None, nb, cn, width), lambda d, n: (d, 0, chunk(d, n), 0)),
        out_shape=jax.ShapeDtypeStruct((2, nb, seq, width), F32),
        scratch_shapes=[
            pltpu.VMEM((width // LANES, nb * srows, LANES), F32),
            pltpu.VMEM((cn * nb, width), F32),
            pltpu.VMEM((width // LANES, cn * nb, LANES), F32),
            pltpu.VMEM((nb, nstate), F32),
            pltpu.VMEM((nb, nstate), F32),
        ] + [pltpu.VMEM((S5_COLS // LANES, cn * nb, LANES), F32)] * (2 * (nstate // S5_COLS)),
        compiler_params=_params(("arbitrary", "arbitrary")),
        name="s5_scan",
    )(z3, lam, bmat, cmat)


def _s5_discretize(a_re, a_im, log_step, b_re, b_im, c_re, c_im):
    ng, npst = a_re.shape[1], a_re.shape[2]
    gpb = S5_KBLK // S5_GROUP_CH
    nk = ng // gpb
    eye = jnp.eye(gpb, dtype=F32)
    bre, bim = b_re.astype(F32), b_im.astype(F32)
    lams, bmats, cmats = [], [], []
    for direction in range(2):
        ar = a_re[direction].astype(F32)
        ai = a_im[direction].astype(F32)
        dt = jnp.exp(log_step[direction].astype(F32))[:, None]
        mag = jnp.exp(ar * dt)
        lam_re, lam_im = mag * jnp.cos(ai * dt), mag * jnp.sin(ai * dt)
        nr, ni = lam_re - 1.0, lam_im
        den = ar * ar + ai * ai
        coef_re = (nr * ar + ni * ai) / den
        coef_im = (ni * ar - nr * ai) / den
        bbar_re = coef_re[..., None] * bre - coef_im[..., None] * bim
        bbar_im = coef_re[..., None] * bim + coef_im[..., None] * bre

        def in_blocks(m):
            m4 = m.reshape(nk, gpb, npst, S5_GROUP_CH)
            return jnp.einsum('kgpc,gh->kgchp', m4, eye).reshape(nk, S5_KBLK, gpb * npst)

        def out_blocks(m):
            m4 = m.reshape(nk, gpb, S5_GROUP_CH, npst)
            return jnp.einsum('kgcp,gh->kgphc', m4, eye).reshape(nk, gpb * npst, S5_KBLK)

        lams.append(jnp.stack([lam_re.reshape(-1), lam_im.reshape(-1)]))
        bmats.append(jnp.concatenate([in_blocks(bbar_re), in_blocks(bbar_im)], axis=2))
        cmats.append(jnp.concatenate([out_blocks(c_re[direction].astype(F32)),
                                      -out_blocks(c_im[direction].astype(F32))], axis=1))
    return jnp.stack(lams), jnp.stack(bmats).astype(BF16), jnp.stack(cmats).astype(BF16)


def _s5_glu(yf_ref, yb_ref, u_ref, d_ref, w_ref, b_ref):
    y = u_ref[...].astype(F32) * d_ref[...] + yf_ref[...] + yb_ref[...]
    y = jax.nn.gelu(y)
    gate = jax.nn.sigmoid(jnp.dot(y.astype(BF16), w_ref[...], preferred_element_type=F32) + b_ref[...])
    return (y * gate).astype(BF16)


def _proj_kernel(*refs, s5_glu):
    if s5_glu:
        yr_ref, yf_ref, yb_ref, u_ref, d_ref, wg_ref, bg_ref = refs[:7]
        w_ref, x_ref, g_ref, rh_ref, rl_ref, rb_ref, x1_ref, h_ref, lt_ref = refs[7:]
        half = yr_ref.shape[1]
        mix = jnp.dot(yr_ref[...], w_ref[:half, :], preferred_element_type=F32)
        mix = mix + jnp.dot(_s5_glu(yf_ref, yb_ref, u_ref, d_ref, wg_ref, bg_ref), w_ref[half:, :],
                            preferred_element_type=F32)
    else:
        a_ref, w_ref, x_ref, g_ref, rh_ref, rl_ref, rb_ref, x1_ref, h_ref, lt_ref = refs
        mix = jnp.dot(a_ref[...], w_ref[...], preferred_element_type=F32)
    x1 = x_ref[...] + mix
    x1_ref[...] = x1
    h = _rms(x1, g_ref[...])
    h_hi = h.astype(BF16)
    h_ref[...] = h_hi.reshape(h_ref.shape)
    h_lo = (h - h_hi.astype(F32)).astype(BF16)
    lt = lax.dot_general(rh_ref[...], h_hi, NT_DIMS, preferred_element_type=F32)
    lt = lt + lax.dot_general(rh_ref[...], h_lo, NT_DIMS, preferred_element_type=F32)
    lt = lt + lax.dot_general(rl_ref[...], h_hi, NT_DIMS, preferred_element_type=F32)
    lt_ref[...] = lt + rb_ref[...]


def proj_norm_router(mix_in, w, x, g, r_hi, r_lo, r_bias, tm=DENSE_ROWS):
    t, d = x.shape
    tm = min(tm, t)
    k = w.shape[0]
    nr = r_hi.shape[0]
    s5_glu = isinstance(mix_in, tuple)
    if s5_glu:
        y_ret, ys5, z, d_skip, w_glu, b_glu = mix_in
        half = y_ret.shape[1]
        ucol = z.shape[1] // half - 1
        lead_specs = [
            pl.BlockSpec((tm, half), lambda i: (i, 0)),
            pl.BlockSpec((None, tm, half), lambda i: (0, i, 0)),
            pl.BlockSpec((None, tm, half), lambda i: (1, i, 0)),
            pl.BlockSpec((tm, half), lambda i: (i, ucol)),
            pl.BlockSpec((1, half), lambda i: (0, 0)),
            pl.BlockSpec((half, half), lambda i: (0, 0)),
            pl.BlockSpec((1, half), lambda i: (0, 0)),
        ]
        lead_args = (y_ret, ys5, ys5, z, d_skip.reshape(1, half), w_glu.astype(BF16),
                     b_glu.reshape(1, half))
    else:
        lead_specs = [pl.BlockSpec((tm, k), lambda i: (i, 0))]
        lead_args = (mix_in,)
    return pl.pallas_call(
        functools.partial(_proj_kernel, s5_glu=s5_glu),
        grid=(t // tm,),
        in_specs=lead_specs + [
            pl.BlockSpec((k, d), lambda i: (0, 0)),
            pl.BlockSpec((tm, d), lambda i: (i, 0)),
            pl.BlockSpec((1, d), lambda i: (0, 0)),
            pl.BlockSpec((nr, d), lambda i: (0, 0)),
            pl.BlockSpec((nr, d), lambda i: (0, 0)),
            pl.BlockSpec((nr, 1), lambda i: (0, 0)),
        ],
        out_specs=[
            pl.BlockSpec((tm, d), lambda i: (i, 0)),
            pl.BlockSpec((tm, d // LANES, LANES), lambda i: (i, 0, 0)),
            pl.BlockSpec((nr, tm), lambda i: (0, i)),
        ],
        out_shape=[
            jax.ShapeDtypeStruct((t, d), F32),
            jax.ShapeDtypeStruct((t, d // LANES, LANES), BF16),
            jax.ShapeDtypeStruct((nr, t), F32),
        ],
        compiler_params=_params(("parallel",)),
        name="proj_norm_router",
    )(*lead_args, w.astype(BF16), x, g.reshape(1, d), r_hi, r_lo, r_bias)


def _router_operands(w_group, b_group, w_er, b_er):
    d = w_group.shape[0]
    wt = jnp.concatenate([
        jnp.transpose(w_er.astype(F32), (0, 2, 1)).reshape(MOE_EXPERTS, d),
        jnp.transpose(w_group.astype(F32)),
        jnp.zeros((ROUTER_ROWS - MOE_EXPERTS - MOE_GROUPS, d), F32)], axis=0)
    bias = jnp.concatenate([
        b_er.astype(F32).reshape(-1), b_group.astype(F32),
        jnp.zeros((ROUTER_ROWS - MOE_EXPERTS - MOE_GROUPS,), F32)]).reshape(ROUTER_ROWS, 1)
    hi = wt.astype(BF16)
    lo = (wt - hi.astype(F32)).astype(BF16)
    return hi, lo, bias


def _route_kernel(lt_ref, eid_ref, gate_ref, rank_ref, cnt_ref, run_ref):
    i = pl.program_id(0)
    tm = lt_ref.shape[1]
    ne, npg, ng = MOE_EXPERTS, MOE_EXPERTS_PER_GROUP, MOE_GROUPS

    @pl.when(i == 0)
    def _():
        run_ref[...] = jnp.zeros_like(run_ref)

    gl = lt_ref[ne:ne + ng, :]
    gmax = jnp.max(gl, axis=0, keepdims=True)
    gidx = lax.broadcasted_iota(I32, (ng, tm), 0)
    gsel = jnp.min(jnp.where(gl == gmax, gidx, ng), axis=0, keepdims=True)
    p_g = 1.0 / jnp.sum(jnp.exp(gl - gmax), axis=0, keepdims=True)

    e8 = lt_ref[(ng - 1) * npg:ng * npg, :]
    for g in range(ng - 2, -1, -1):
        e8 = jnp.where(gsel == g, lt_ref[g * npg:(g + 1) * npg, :], e8)
    eidx = lax.broadcasted_iota(I32, (npg, tm), 0)
    m1 = jnp.max(e8, axis=0, keepdims=True)
    i1 = jnp.min(jnp.where(e8 == m1, eidx, npg), axis=0, keepdims=True)
    e8b = jnp.where(eidx == i1, -jnp.inf, e8)
    m2 = jnp.max(e8b, axis=0, keepdims=True)
    i2 = jnp.min(jnp.where(e8b == m2, eidx, npg), axis=0, keepdims=True)
    t2 = jnp.exp(m2 - m1)
    den = 1.0 + t2
    gate_ref[0:1, :] = (1.0 / den) * p_g
    gate_ref[1:2, :] = (t2 / den) * p_g
    id1 = gsel * npg + i1
    id2 = gsel * npg + i2
    eid_ref[0:1, :] = id1
    eid_ref[1:2, :] = id2

    rows = lax.broadcasted_iota(I32, (ne, tm), 0)
    oh1 = rows == id1
    oh2 = rows == id2
    both = jnp.where(oh1, 1.0, 0.0) + jnp.where(oh2, 1.0, 0.0)
    earlier = (lax.broadcasted_iota(I32, (tm, tm), 0) < lax.broadcasted_iota(I32, (tm, tm), 1))
    prefix = jnp.dot(both.astype(BF16), jnp.where(earlier, 1.0, 0.0).astype(BF16),
                     preferred_element_type=F32)
    base = prefix + run_ref[:, 0:1]
    rank_ref[0:1, :] = jnp.sum(jnp.where(oh1, base, 0.0), axis=0, keepdims=True).astype(I32)
    rank_ref[1:2, :] = jnp.sum(jnp.where(oh2, base, 0.0), axis=0, keepdims=True).astype(I32)
    run = run_ref[...] + jnp.sum(both, axis=1, keepdims=True)
    run_ref[...] = run
    cnt_ref[...] = run.astype(I32)


def route(lt, tm=512):
    nr, t = lt.shape
    two = lambda dt: jax.ShapeDtypeStruct((2, t), dt)
    return pl.pallas_call(
        _route_kernel,
        grid=(t // tm,),
        in_specs=[pl.BlockSpec((nr, tm), lambda i: (0, i))],
        out_specs=[
            pl.BlockSpec((2, tm), lambda i: (0, i)),
            pl.BlockSpec((2, tm), lambda i: (0, i)),
            pl.BlockSpec((2, tm), lambda i: (0, i)),
            pl.BlockSpec((MOE_EXPERTS, LANES), lambda i: (0, 0)),
        ],
        out_shape=[two(I32), two(F32), two(I32),
                   jax.ShapeDtypeStruct((MOE_EXPERTS, LANES), I32)],
        scratch_shapes=[pltpu.VMEM((MOE_EXPERTS, LANES), F32)],
        compiler_params=_params(("arbitrary",)),
        name="route",
    )(lt)


def _dispatch_kernel(pad_ref, nu_ref, d0_ref, d1_ref, h_ref, xbuf_ref, zero_ref, sem, zsem):
    i = pl.program_id(0)
    tm = h_ref.shape[0]
    dests = (d0_ref, d1_ref)
    bm = zero_ref.shape[0]
    nblk = xbuf_ref.shape[0] // bm

    def zero_fill(act):
        def per_expert(e, carry):
            pos = pad_ref[0, e]
            length = pad_ref[1, e]
            p = bm // 2
            while p >= 1:
                bit = (length & p) != 0

                @pl.when(bit)
                def _(pos=pos, p=p):
                    act(pltpu.make_async_copy(zero_ref.at[pl.ds(0, p)],
                                              xbuf_ref.at[pl.ds(pos, p)], zsem))

                pos = pos + jnp.where(bit, p, 0)
                p //= 2
            return carry

        lax.fori_loop(0, pad_ref.shape[1], per_expert, 0)

        def per_block(b, carry):
            act(pltpu.make_async_copy(zero_ref, xbuf_ref.at[pl.ds(b * bm, bm)], zsem))
            return carry

        lax.fori_loop(nu_ref[0], nblk, per_block, 0)

    @pl.when(i == 0)
    def _():
        zero_ref[...] = jnp.zeros_like(zero_ref)
        zero_fill(lambda c: c.start())

    def copy(r, k):
        return pltpu.make_async_copy(h_ref.at[r], xbuf_ref.at[dests[k][0, r]], sem)

    def start(r, carry):
        copy(r, 0).start(priority=0)
        copy(r, 1).start(priority=1)
        return carry

    lax.fori_loop(0, tm, start, 0, unroll=8)

    def wait(r, carry):
        copy(r, 0).wait()
        copy(r, 1).wait()
        return carry

    lax.fori_loop(0, tm, wait, 0, unroll=8)

    @pl.when(i == pl.num_programs(0) - 1)
    def _():
        zero_fill(lambda c: c.wait())


def _row_index_spec(tm, ahead=0, last=None):
    def index(i):
        return (i if ahead == 0 else jnp.minimum(i + ahead, last), 0, 0)
    return pl.BlockSpec((None, 1, tm), index, memory_space=pltpu.SMEM)


def dispatch(pad, n_used, dest0, dest1, h3, n_rows):
    t, s, lanes = h3.shape
    nt, _, tm = dest0.shape
    return pl.pallas_call(
        _dispatch_kernel,
        grid=(nt,),
        in_specs=[
            pl.BlockSpec(memory_space=pltpu.SMEM),
            pl.BlockSpec(memory_space=pltpu.SMEM),
            _row_index_spec(tm), _row_index_spec(tm),
            pl.BlockSpec((tm, s, lanes), lambda i: (i, 0, 0)),
        ],
        out_specs=pl.BlockSpec(memory_space=pl.ANY),
        out_shape=jax.ShapeDtypeStruct((n_rows, s, lanes), h3.dtype),
        scratch_shapes=[pltpu.VMEM((MOE_ROWS, s, lanes), h3.dtype),
                        pltpu.SemaphoreType.DMA(()), pltpu.SemaphoreType.DMA(())],
        compiler_params=_params(("arbitrary",)),
        name="moe_dispatch",
    )(pad, n_used, dest0, dest1, h3)


def _experts_kernel(be_ref, nu_ref, nx_ref, x_ref, wg_hbm, wu_hbm, wd_hbm, o_ref,
                    wg_f32, wu_f32, wd_f32, wgb, wub, wdb, slot_ref, sem, *, layer):
    i = pl.program_id(0)
    e = be_ref[i]
    first = i == 0
    changed = jnp.logical_or(first, e != be_ref[jnp.maximum(i - 1, 0)])
    streams = ((wg_hbm, wg_f32), (wu_hbm, wu_f32), (wd_hbm, wd_f32))

    def fetch(expert, slot):
        return [pltpu.make_async_copy(w_hbm.at[layer, expert], w_f32.at[slot], sem.at[slot, j])
                for j, (w_hbm, w_f32) in enumerate(streams)]

    @pl.when(first)
    def _():
        slot_ref[0] = 1
        for c in fetch(e, 0):
            c.start()

    @pl.when(changed)
    def _():
        slot = 1 - slot_ref[0]
        slot_ref[0] = slot
        for c in fetch(e, slot):
            c.wait()
        nxt = nx_ref[i]

        @pl.when(nxt >= 0)
        def _():
            for c in fetch(nxt, 1 - slot):
                c.start()

        wgb[...] = wg_f32[slot].astype(BF16)
        wub[...] = wu_f32[slot].astype(BF16)
        wdb[...] = wd_f32[slot].astype(BF16)

    @pl.when(i < nu_ref[0])
    def _():
        bm, s, lanes = x_ref.shape
        x = x_ref[...].reshape(bm, s * lanes)
        g = jnp.dot(x, wgb[...], preferred_element_type=F32)
        u = jnp.dot(x, wub[...], preferred_element_type=F32)
        a = ((g * jax.nn.sigmoid(g)) * u).astype(BF16)
        y = jnp.dot(a, wdb[...], preferred_element_type=F32)
        o_ref[...] = y.astype(BF16).reshape(o_ref.shape)

    @pl.when(i >= nu_ref[0])
    def _():
        o_ref[...] = jnp.zeros_like(o_ref)


def experts(layer, block_expert, n_used, next_expert, xbuf, w_gate, w_up, w_down):
    n_rows, s, lanes = xbuf.shape
    d = s * lanes
    hid = w_gate.shape[3]
    bm = MOE_ROWS
    nblk = n_rows // bm
    grid_spec = pltpu.PrefetchScalarGridSpec(
        num_scalar_prefetch=3,
        grid=(nblk,),
        in_specs=[
            pl.BlockSpec((bm, s, lanes), lambda i, be, nu, nx: (jnp.minimum(i, nu[0] - 1), 0, 0)),
            pl.BlockSpec(memory_space=pl.ANY),
            pl.BlockSpec(memory_space=pl.ANY),
            pl.BlockSpec(memory_space=pl.ANY),
        ],
        out_specs=pl.BlockSpec((bm, s, lanes), lambda i, be, nu, nx: (i, 0, 0)),
        scratch_shapes=[pltpu.VMEM((2, d, hid), F32), pltpu.VMEM((2, d, hid), F32),
                        pltpu.VMEM((2, hid, d), F32),
                        pltpu.VMEM((d, hid), BF16), pltpu.VMEM((d, hid), BF16),
                        pltpu.VMEM((hid, d), BF16),
                        pltpu.SMEM((1,), I32), pltpu.SemaphoreType.DMA((2, 3))],
    )
    return pl.pallas_call(
        functools.partial(_experts_kernel, layer=layer),
        grid_spec=grid_spec,
        out_shape=jax.ShapeDtypeStruct((n_rows, s, lanes), BF16),
        compiler_params=_params(("arbitrary",)),
        name="moe_experts",
    )(block_expert, n_used, next_expert, xbuf, w_gate, w_up, w_down)


def _combine_kernel(d0_ref, d1_ref, n0_ref, n1_ref, gate_ref, x_ref, g_ref, ybuf_ref, o_ref,
                    buf, sem, *, final_norm):
    i = pl.program_id(0)
    tm, d = x_ref.shape
    slot = i % 2

    last = i + 1 == pl.num_programs(0)
    other = 1 - slot

    def copy(dests, s, r, k):
        return pltpu.make_async_copy(ybuf_ref.at[dests[k][0, r]], buf.at[s, k, r], sem.at[s])

    def start(dests, s, r):
        copy(dests, s, r, 0).start(priority=0)
        copy(dests, s, r, 1).start(priority=1)

    def wait_all(s):
        def wait(r, carry):
            copy((d0_ref, d1_ref), s, r, 0).wait()
            copy((d0_ref, d1_ref), s, r, 1).wait()
            return carry

        lax.fori_loop(0, tm, wait, 0, unroll=8)

    @pl.when(i == 0)
    def _():
        lax.fori_loop(0, tm, lambda r, c: (start((d0_ref, d1_ref), 0, r), c)[1], 0, unroll=8)

    wait_all(slot)

    group = tm // COMBINE_GROUPS
    for p in range(COMBINE_GROUPS):
        rows = slice(p * group, (p + 1) * group)
        for r in range(p * group, (p + 1) * group):
            start((n0_ref, n1_ref), other, r)
        gates = gate_ref[rows, :]
        y = (gates[:, 0:1] * buf[slot, 0, rows].reshape(group, d).astype(F32)
             + gates[:, 1:2] * buf[slot, 1, rows].reshape(group, d).astype(F32))
        out = x_ref[rows, :] + y
        if final_norm:
            out = _rms(out, g_ref[...])
        o_ref[rows, :] = out

    @pl.when(last)
    def _():
        wait_all(other)


def combine(dest0, dest1, gates_t, x, ybuf, g_final, final_norm):
    t, d = x.shape
    nt, _, tm = dest0.shape
    _, s, lanes = ybuf.shape
    return pl.pallas_call(
        functools.partial(_combine_kernel, final_norm=final_norm),
        grid=(nt,),
        in_specs=[
            _row_index_spec(tm), _row_index_spec(tm),
            _row_index_spec(tm, ahead=1, last=nt - 1), _row_index_spec(tm, ahead=1, last=nt - 1),
            pl.BlockSpec((tm, 2), lambda i: (i, 0)),
            pl.BlockSpec((tm, d), lambda i: (i, 0)),
            pl.BlockSpec((1, d), lambda i: (0, 0)),
            pl.BlockSpec(memory_space=pl.ANY),
        ],
        out_specs=pl.BlockSpec((tm, d), lambda i: (i, 0)),
        out_shape=jax.ShapeDtypeStruct((t, d), F32),
        scratch_shapes=[pltpu.VMEM((2, 2, tm, s, lanes), ybuf.dtype),
                        pltpu.SemaphoreType.DMA((2,))],
        compiler_params=_params(("arbitrary",)),
        name="moe_combine",
    )(dest0, dest1, dest0, dest1, gates_t, x, g_final.reshape(1, d), ybuf)


def hier_moe_block(layer, x1, h, lt, w_gate, w_up, w_down, g_final, final_norm, tm=MOE_TOKENS):
    t, d = x1.shape
    bm = MOE_ROWS
    eid, gate, rank, cnt = route(lt)
    counts = cnt[:, 0]
    padded = ((counts + bm - 1) // bm) * bm
    pend = jnp.cumsum(padded)
    pstart = pend - padded
    experts_col = jnp.arange(MOE_EXPERTS, dtype=I32)[:, None, None]
    dest = rank + jnp.sum(jnp.where(eid[None] == experts_col, pstart[:, None, None], 0), axis=0)
    n_rows = 2 * t + MOE_EXPERTS * bm
    nblk = n_rows // bm
    n_used = (pend[-1] // bm).astype(I32)
    first_row = jnp.minimum(jnp.arange(nblk, dtype=I32), n_used - 1) * bm
    block_expert = jnp.sum(pend[None, :] <= first_row[:, None], axis=1).astype(I32)
    block_expert = jnp.minimum(block_expert, MOE_EXPERTS - 1)
    ids = jnp.arange(MOE_EXPERTS, dtype=I32)
    later = jnp.logical_and(ids[None, :] > block_expert[:, None], counts[None, :] > 0)
    next_expert = jnp.min(jnp.where(later, ids[None, :], MOE_EXPERTS), axis=1)
    next_expert = jnp.where(next_expert < MOE_EXPERTS, next_expert, -1).astype(I32)
    dest0 = dest[0].reshape(t // tm, 1, tm)
    dest1 = dest[1].reshape(t // tm, 1, tm)
    pad = jnp.stack([pstart + counts, padded - counts]).astype(I32)
    xbuf = dispatch(pad, n_used.reshape(1), dest0, dest1, h, n_rows)
    ybuf = experts(layer, block_expert, n_used.reshape(1), next_expert, xbuf, w_gate, w_up, w_down)
    return combine(dest0, dest1, jnp.transpose(gate), x1, ybuf, g_final, final_norm)


def _attn_kernel(sink_ref, q_ref, kp_ref, kc_ref, kn_ref, vp_ref, vc_ref, vn_ref, bias_ref, o_ref):
    n = pl.program_id(1)
    nblk = pl.num_programs(1)
    blk = q_ref.shape[0]
    dh = SWA_HEAD_DIM
    nkv = kc_ref.shape[1] // dh
    masked = bias_ref.shape[1] - 1
    part_prev = jnp.where(n > 0, 0, masked)
    part_next = jnp.where(n < nblk - 1, 2, masked)
    lo = lax.broadcasted_iota(I32, (blk, 2 * dh), 1) < dh
    top = lax.broadcasted_iota(I32, (2 * blk, 2 * dh), 0) < blk
    lo2 = lax.broadcasted_iota(I32, (2 * blk, 2 * dh), 1) < dh
    ones_ext = jnp.concatenate([jnp.where(lo, 1.0, 0.0), jnp.where(lo, 0.0, 1.0)], axis=0).astype(BF16)
    qscale = (dh ** -0.5) * LOG2E

    for kv in range(nkv):
        col = slice((kv // 2) * 2 * dh, (kv // 2 + 1) * 2 * dh)

        def extend(ref):
            x = ref[:, col].astype(F32)
            r = pltpu.roll(x, dh, 1)
            x_lo, x_hi = (x, r) if kv % 2 == 0 else (r, x)
            return jnp.concatenate([jnp.where(lo, x_lo, 0.0), jnp.where(lo, 0.0, x_hi)],
                                   axis=0).astype(BF16)

        q2 = jnp.concatenate([q_ref[:, (2 * kv) * 2 * dh:(2 * kv + 1) * 2 * dh],
                              q_ref[:, (2 * kv + 1) * 2 * dh:(2 * kv + 2) * 2 * dh]], axis=0)
        q2 = (q2.astype(F32) * qscale).astype(BF16)

        def scores(k_ref, part):
            return lax.dot_general(q2, extend(k_ref), NT_DIMS,
                                   preferred_element_type=F32) + bias_ref[kv, part]

        s = [scores(kp_ref, part_prev), scores(kc_ref, 1), scores(kn_ref, part_next)]
        mx = jnp.maximum(jnp.maximum(s[0], s[1]), s[2])
        sk = [jnp.where(top, sink_ref[0, 4 * kv + par], sink_ref[0, 4 * kv + 2 + par])
              for par in range(2)]
        m = [jnp.maximum(jnp.broadcast_to(jnp.max(mx[:, par * blk:(par + 1) * blk], axis=-1,
                                                  keepdims=True), (2 * blk, 2 * dh)), sk[par])
             for par in range(2)]
        acc = jnp.zeros((2 * blk, 4 * dh), F32)
        for sp, v_ref in zip(s, (vp_ref, vc_ref, vn_ref)):
            e = jnp.concatenate([jnp.exp2(sp[:, :blk] - m[0]), jnp.exp2(sp[:, blk:] - m[1])],
                                axis=1).astype(BF16)
            rhs = jnp.concatenate([extend(v_ref), ones_ext], axis=1)
            acc = acc + jnp.dot(e, rhs, preferred_element_type=F32)
        den = acc[:, 2 * dh:] + jnp.exp2(jnp.where(lo2, sk[0] - m[0], sk[1] - m[1]))
        o = (acc[:, :2 * dh] / den).astype(BF16)
        o_ref[:, (2 * kv) * 2 * dh:(2 * kv + 1) * 2 * dh] = o[:blk]
        o_ref[:, (2 * kv + 1) * 2 * dh:(2 * kv + 2) * 2 * dh] = o[blk:]


def banded_attention(qkv3, sink, bias):
    b, seq, width = qkv3.shape
    nh = sink.shape[1]
    qd = nh * SWA_HEAD_DIM
    kvd = (width - qd) // 2
    blk = SWA_BLOCK
    nblk = seq // blk
    kcol, vcol = qd // kvd, qd // kvd + 1

    def band(col, off):
        return pl.BlockSpec((None, blk, kvd),
                            lambda bi, n: (bi, jnp.clip(n + off, 0, nblk - 1), col))

    return pl.pallas_call(
        _attn_kernel,
        grid=(b, nblk),
        in_specs=[
            pl.BlockSpec(memory_space=pltpu.SMEM),
            pl.BlockSpec((None, blk, qd), lambda bi, n: (bi, n, 0)),
            band(kcol, -1), band(kcol, 0), band(kcol, 1),
            band(vcol, -1), band(vcol, 0), band(vcol, 1),
            pl.BlockSpec(bias.shape, lambda bi, n: (0, 0, 0, 0)),
        ],
        out_specs=pl.BlockSpec((None, blk, qd), lambda bi, n: (bi, n, 0)),
        out_shape=jax.ShapeDtypeStruct((b, seq, qd), BF16),
        compiler_params=_params(("parallel", "parallel")),
        name="banded_attention",
    )(sink, qkv3, qkv3, qkv3, qkv3, qkv3, qkv3, qkv3, bias)


def _t5_bucket(rel):
    half = REL_BUCKETS // 2
    max_exact = half // 2
    n = jnp.abs(rel)
    large = max_exact + (jnp.log(jnp.maximum(n, 1).astype(F32) / max_exact)
                         / math.log(REL_MAX_DIST / max_exact) * (half - max_exact)).astype(I32)
    large = jnp.minimum(large, half - 1)
    return jnp.where(rel > 0, half, 0) + jnp.where(n < max_exact, n, large)


def _attention_bias(rel_bias):
    blk = SWA_BLOCK
    nh = rel_bias.shape[1]
    nkv = nh // SWA_Q_PER_KV
    rel = jnp.arange(3 * blk)[None, :] - blk - jnp.arange(blk)[:, None]
    onehot = (_t5_bucket(rel)[..., None] == jnp.arange(REL_BUCKETS)).astype(F32)
    bias = jnp.einsum('ijb,bh->hij', onehot, rel_bias.astype(F32), precision=lax.Precision.HIGHEST)
    bias = jnp.where((jnp.abs(rel) <= SWA_WINDOW)[None], bias, NEG_INF) * LOG2E
    tiles = bias.reshape(nkv, 2, 2, blk, 3, blk).transpose(0, 4, 1, 3, 2, 5)
    tiles = tiles.reshape(nkv, 3, 2 * blk, 2 * blk)
    masked = jnp.full((nkv, 1, 2 * blk, 2 * blk), NEG_INF * LOG2E, F32)
    return jnp.concatenate([tiles, masked], axis=1)


def _rotary_tables(seq, dim):
    inv_freq = ROPE_BASE ** (-jnp.arange(0, dim, 2, dtype=F32) / dim)
    ang = jnp.arange(seq, dtype=F32)[:, None] * inv_freq[None, :]
    cos, sin = jnp.cos(ang), jnp.sin(ang)
    return jnp.concatenate([cos, cos], axis=1), jnp.concatenate([-sin, sin], axis=1)


def kernel(x, norm_mix_g, norm_ffn_g, norm_final_g, hyb_w_in, ret_decay_logit, s5_a_re, s5_a_im, s5_log_step, s5_b_re, s5_b_im, s5_c_re, s5_c_im, s5_d, s5_w_glu, s5_b_glu, hyb_w_out, swa_w_qkv, swa_sink, swa_w_o, rel_bias, moe_w_group, moe_b_group, moe_w_expert_router, moe_b_expert_router, moe_w_gate, moe_w_up, moe_w_down):
    nb, seq, d = x.shape
    t = nb * seq
    depth = norm_mix_g.shape[0]
    xt = x.reshape(t, d)
    for layer in range(depth):
        i = layer // 2
        if layer % 2 == 0:
            w = hyb_w_out.shape[1] // 2
            z = norm_matmul(xt, norm_mix_g[layer], hyb_w_in[i])
            z3 = z.reshape(nb, seq, z.shape[1])
            cos, sin = _rotary_tables(seq, w // RET_HEADS)
            log_gamma = jax.nn.log_sigmoid(ret_decay_logit[i].astype(F32))
            y_ret = retention(z3, log_gamma, cos, sin).reshape(t, w)
            lam, bmat, cmat = _s5_discretize(s5_a_re[i], s5_a_im[i], s5_log_step[i], s5_b_re[i],
                                             s5_b_im[i], s5_c_re[i], s5_c_im[i])
            ys5 = s5_scan(z3, lam, bmat, cmat).reshape(2, t, w)
            mix_in = (y_ret, ys5, z, s5_d[i].astype(F32).reshape(-1), s5_w_glu[i],
                      s5_b_glu[i].astype(F32))
            w_mix = hyb_w_out[i]
        else:
            qkv = norm_matmul(xt, norm_mix_g[layer], swa_w_qkv[i])
            mix_in = banded_attention(qkv.reshape(nb, seq, qkv.shape[1]),
                                      swa_sink[i].astype(F32).reshape(1, -1) * LOG2E,
                                      _attention_bias(rel_bias)).reshape(t, -1)
            w_mix = swa_w_o[i]
        r_hi, r_lo, r_bias = _router_operands(moe_w_group[layer], moe_b_group[layer],
                                              moe_w_expert_router[layer], moe_b_expert_router[layer])
        x1, h, lt = proj_norm_router(mix_in, w_mix, xt, norm_ffn_g[layer], r_hi, r_lo, r_bias)
        last = layer == depth - 1
        xt = hier_moe_block(layer, x1, h, lt, moe_w_gate, moe_w_up, moe_w_down,
                            norm_final_g, final_norm=last)
    return xt.reshape(nb, seq, d)
```

```python
import functools
import math

import jax
import jax.numpy as jnp
from jax import lax
from jax.experimental import pallas as pl
from jax.experimental.pallas import tpu as pltpu

F32 = jnp.float32
BF16 = jnp.bfloat16
I32 = jnp.int32

RET_HEADS = 4
RET_CHUNK = 128
S5_GROUP_CH = 16
S5_STATE = 64
SWA_HEAD_DIM = 64
SWA_Q_PER_KV = 4
SWA_WINDOW = 128
SWA_BLOCK = 128
REL_BUCKETS = 32
REL_MAX_DIST = 128
MOE_GROUPS = 4
MOE_EXPERTS_PER_GROUP = 8
MOE_EXPERTS = MOE_GROUPS * MOE_EXPERTS_PER_GROUP
ROPE_BASE = 10000.0
RMS_EPS = 1e-6
GN_EPS = 1e-5
NEG_INF = -1e30
LOG2E = 1.4426950408889634

LANES = 128
SUBLANES = 8
V7X_VMEM_BYTES = 64 * 1024 * 1024
VMEM_LIMIT = V7X_VMEM_BYTES - 8 * 1024 * 1024

DENSE_ROWS = 1024
MOE_ROWS = 512
MOE_TOKENS = 1024
COMBINE_GROUPS = 16
S5_CHUNK = 128
S5_COLS = 512
S5_SCAN_COLS = 512
S5_KBLK = 128
ROUTER_ROWS = 128

NT_DIMS = (((1,), (1,)), ((), ()))
TN_DIMS = (((0,), (0,)), ((), ()))


def _params(semantics):
    return pltpu.CompilerParams(dimension_semantics=semantics, vmem_limit_bytes=VMEM_LIMIT)


def _rms(x, g):
    ms = jnp.mean(x * x, axis=-1, keepdims=True)
    return (x * lax.rsqrt(ms + RMS_EPS)) * g


def _norm_matmul_kernel(x_ref, g_ref, w_ref, o_ref):
    h = _rms(x_ref[...], g_ref[...])
    o_ref[...] = jnp.dot(h.astype(BF16), w_ref[...],
                         preferred_element_type=F32).astype(o_ref.dtype)


def norm_matmul(x, g, w, tm=DENSE_ROWS):
    t, d = x.shape
    tm = min(tm, t)
    n = w.shape[1]
    return pl.pallas_call(
        _norm_matmul_kernel,
        grid=(t // tm,),
        in_specs=[
            pl.BlockSpec((tm, d), lambda i: (i, 0)),
            pl.BlockSpec((1, d), lambda i: (0, 0)),
            pl.BlockSpec((d, n), lambda i: (0, 0)),
        ],
        out_specs=pl.BlockSpec((tm, n), lambda i: (i, 0)),
        out_shape=jax.ShapeDtypeStruct((t, n), BF16),
        compiler_params=_params(("parallel",)),
        name="norm_matmul",
    )(x, g.reshape(1, d), w.astype(BF16))


def _retention_kernel(lg_ref, q_ref, k_ref, v_ref, g_ref, cos_ref, sin_ref, o_ref,
                      qr_ref, kr_ref, inc_ref, st_ref, lhs_ref):
    h = pl.program_id(1)
    lg_f = lg_ref[0, h]
    lg_b = lg_ref[1, h]
    seq, dk = q_ref.shape
    c = RET_CHUNK
    nc = seq // c

    cos = cos_ref[...]
    sin = sin_ref[...]
    swap = (lax.broadcasted_iota(I32, (dk, dk), 0)
            == (lax.broadcasted_iota(I32, (dk, dk), 1) + dk // 2) % dk)
    swap = jnp.where(swap, 1.0, 0.0).astype(BF16)
    q = q_ref[...]
    qr_ref[...] = (q.astype(F32) * cos
                   + jnp.dot(q, swap, preferred_element_type=F32) * sin)
    k = k_ref[...]
    kr_ref[...] = (k.astype(F32) * cos
                   + jnp.dot(k, swap, preferred_element_type=F32) * sin) * (dk ** -0.5)

    pos = lax.broadcasted_iota(I32, (c, dk), 0).astype(F32)
    kf_scale = jnp.exp((c - 1.0 - pos) * lg_f)
    qf_scale = jnp.exp((pos + 1.0) * lg_f)
    kb_scale = jnp.exp(pos * lg_b)
    qb_scale = jnp.exp((c - pos) * lg_b)
    rel = (lax.broadcasted_iota(I32, (c, c), 0) - lax.broadcasted_iota(I32, (c, c), 1)).astype(F32)
    mask = jnp.exp(jnp.abs(rel) * jnp.where(rel >= 0, lg_f, lg_b))
    dec_f = jnp.exp(jnp.full((dk, dk), c * lg_f, F32))
    dec_b = jnp.exp(jnp.full((dk, dk), c * lg_b, F32))

    def increments(n, carry):
        rows = pl.ds(pl.multiple_of(n * c, c), c)
        kc = kr_ref[rows, :]
        kk = jnp.concatenate([kc * kf_scale, kc * kb_scale], axis=1).astype(BF16)
        inc_ref[n] = lax.dot_general(kk, v_ref[rows, :], TN_DIMS,
                                     preferred_element_type=F32)
        return carry

    lax.fori_loop(0, nc, increments, 0, unroll=True)

    def fwd(n, state):
        st_ref[n, :dk, :] = state.astype(BF16)
        return state * dec_f + inc_ref[n, :dk, :]

    lax.fori_loop(0, nc, fwd, jnp.zeros((dk, dk), F32), unroll=True)

    def bwd(i, state):
        n = nc - 1 - i
        st_ref[n, dk:, :] = state.astype(BF16)
        return state * dec_b + inc_ref[n, dk:, :]

    lax.fori_loop(0, nc, bwd, jnp.zeros((dk, dk), F32), unroll=True)

    def operands(n, carry):
        rows = pl.ds(pl.multiple_of(n * c, c), c)
        qc = qr_ref[rows, :]
        s = lax.dot_general(qc.astype(BF16), kr_ref[rows, :].astype(BF16), NT_DIMS,
                            preferred_element_type=F32) * mask
        lhs_ref[n] = jnp.concatenate([s, qc * qf_scale, qc * qb_scale], axis=1).astype(BF16)
        return carry

    lax.fori_loop(0, nc, operands, 0, unroll=True)

    def outputs(n, carry):
        rows = pl.ds(pl.multiple_of(n * c, c), c)
        rhs = jnp.concatenate([v_ref[rows, :], st_ref[n]], axis=0)
        kr_ref[rows, :] = jnp.dot(lhs_ref[n], rhs, preferred_element_type=F32)
        return carry

    lax.fori_loop(0, nc, outputs, 0, unroll=True)

    out = kr_ref[...]
    mu = jnp.mean(out, axis=-1, keepdims=True)
    cen = out - mu
    var = jnp.mean(cen * cen, axis=-1, keepdims=True)
    g = g_ref[...].astype(F32)
    o_ref[...] = ((g * jax.nn.sigmoid(g)) * (cen * lax.rsqrt(var + GN_EPS))).astype(BF16)


def retention(z3, log_gamma, cos, sin):
    b, seq, _ = z3.shape
    nh = RET_HEADS
    dk = cos.shape[1]

    def col(off):
        return pl.BlockSpec((None, seq, dk), lambda bi, hi: (bi, 0, off + hi))

    return pl.pallas_call(
        _retention_kernel,
        grid=(b, nh),
        in_specs=[
            pl.BlockSpec(memory_space=pltpu.SMEM),
            col(0), col(nh), col(2 * nh), col(3 * nh),
            pl.BlockSpec((seq, dk), lambda bi, hi: (0, 0)),
            pl.BlockSpec((seq, dk), lambda bi, hi: (0, 0)),
        ],
        out_specs=pl.BlockSpec((None, seq, dk), lambda bi, hi: (bi, 0, hi)),
        out_shape=jax.ShapeDtypeStruct((b, seq, nh * dk), BF16),
        scratch_shapes=[pltpu.VMEM((seq, dk), F32), pltpu.VMEM((seq, dk), F32),
                        pltpu.VMEM((seq // RET_CHUNK, 2 * dk, dk), F32),
                        pltpu.VMEM((seq // RET_CHUNK, 2 * dk, dk), BF16),
                        pltpu.VMEM((seq // RET_CHUNK, RET_CHUNK, 3 * dk), BF16)],
        compiler_params=_params(("parallel", "parallel")),
        name="retention",
    )(log_gamma, z3, z3, z3, z3, cos, sin)


def _s5_kernel(u_ref, lam_ref, b_ref, c_ref, y_ref, sk_ref, utm_ref, ytm_ref, sre_ref, sim_ref,
               *hist):
    nkb = len(hist) // 2

    def xre(plane):
        return hist[plane // (S5_COLS // LANES)].at[plane % (S5_COLS // LANES)]

    def xim(plane):
        return hist[nkb + plane // (S5_COLS // LANES)].at[plane % (S5_COLS // LANES)]

    d = pl.program_id(0)
    n = pl.program_id(1)
    nb, cn, width = u_ref.shape
    srows = sk_ref.shape[1] // nb
    nk = width // S5_KBLK
    npl = width // LANES

    @pl.when(n == 0)
    def _():
        sre_ref[...] = jnp.zeros_like(sre_ref)
        sim_ref[...] = jnp.zeros_like(sim_ref)

    def skew_rows(b):
        return slice(b * srows + b, b * srows + b + cn)

    for b in range(nb):
        ub = u_ref[b].astype(F32)
        for p in range(npl):
            sk_ref[p, skew_rows(b), :] = ub[:, p * LANES:(p + 1) * LANES]

    def to_time_major(t, carry):
        rows = pl.ds(pl.multiple_of(t * nb, nb), nb)
        for p in range(npl):
            utm_ref[rows, p * LANES:(p + 1) * LANES] = sk_ref[p, pl.ds(t, nb, stride=srows + 1), :]
        return carry

    lax.fori_loop(0, cn, to_time_major, 0, unroll=True)

    u = utm_ref[...].astype(BF16)
    ppc = S5_COLS // LANES
    for kb in range(nk):
        bu = jnp.dot(u[:, kb * S5_KBLK:(kb + 1) * S5_KBLK], b_ref[kb], preferred_element_type=F32)
        for j in range(ppc):
            xre(kb * ppc + j)[...] = bu[:, j * LANES:(j + 1) * LANES]
            xim(kb * ppc + j)[...] = bu[:, S5_COLS + j * LANES:S5_COLS + (j + 1) * LANES]

    pps = S5_SCAN_COLS // LANES
    for cb in range(sre_ref.shape[1] // S5_SCAN_COLS):
        cols = slice(cb * S5_SCAN_COLS, (cb + 1) * S5_SCAN_COLS)
        lr = jnp.broadcast_to(lam_ref[0:1, cols], (nb, S5_SCAN_COLS))
        li = jnp.broadcast_to(lam_ref[1:2, cols], (nb, S5_SCAN_COLS))

        def step(i, carry):
            xr, xi = carry
            t = i + d * (cn - 1 - 2 * i)
            rows = pl.ds(pl.multiple_of(t * nb, nb), nb)
            bur = jnp.concatenate([xre(cb * pps + j)[rows, :] for j in range(pps)], axis=1)
            bui = jnp.concatenate([xim(cb * pps + j)[rows, :] for j in range(pps)], axis=1)
            nxr = lr * xr - li * xi + bur
            nxi = lr * xi + li * xr + bui
            for j in range(pps):
                xre(cb * pps + j)[rows, :] = nxr[:, j * LANES:(j + 1) * LANES]
                xim(cb * pps + j)[rows, :] = nxi[:, j * LANES:(j + 1) * LANES]
            return nxr, nxi

        xr, xi = lax.fori_loop(0, cn, step, (sre_ref[:, cols], sim_ref[:, cols]), unroll=True)
        sre_ref[:, cols] = xr
        sim_ref[:, cols] = xi

    for kb in range(nk):
        xr = jnp.concatenate([xre(kb * ppc + j)[...] for j in range(ppc)], axis=1).astype(BF16)
        xi = jnp.concatenate([xim(kb * ppc + j)[...] for j in range(ppc)], axis=1).astype(BF16)
        y = jnp.dot(xr, c_ref[kb, :S5_COLS, :], preferred_element_type=F32)
        ytm_ref[kb] = y + jnp.dot(xi, c_ref[kb, S5_COLS:, :], preferred_element_type=F32)

    def to_batch_major(t, carry):
        rows = pl.ds(pl.multiple_of(t * nb, nb), nb)
        for p in range(npl):
            sk_ref[p, pl.ds(t, nb, stride=srows + 1), :] = ytm_ref[p, rows, :]
        return carry

    lax.fori_loop(0, cn, to_batch_major, 0, unroll=True)

    for b in range(nb):
        for p in range(npl):
            y_ref[b, :, p * LANES:(p + 1) * LANES] = sk_ref[p, skew_rows(b), :]


def s5_scan(z3, lam, bmat, cmat):
    nb, seq, zw = z3.shape
    width = bmat.shape[1] * bmat.shape[2]
    nstate = lam.shape[2]
    cn = min(S5_CHUNK, seq)
    nch = seq // cn
    srows = cn + SUBLANES
    ucol = zw // width - 1

    def chunk(d, n):
        return n + d * (nch - 1 - 2 * n)

    return pl.pallas_call(
        _s5_kernel,
        grid=(2, nch),
        in_specs=[
            pl.BlockSpec((nb, cn, width), lambda d, n: (0, chunk(d, n), ucol)),
            pl.BlockSpec((None, 2, nstate), lambda d, n: (d, 0, 0)),
            pl.BlockSpec((None,) + bmat.shape[1:], lambda d, n: (d, 0, 0, 0)),
            pl.BlockSpec((None,) + cmat.shape[1:], lambda d, n: (d, 0, 0, 0)),
        ],
        out_specs=pl.BlockSpec((None, nb, cn, width), lambda d, n: (d, 0, chunk(d, n), 0)),
        out_shape=jax.ShapeDtypeStruct((2, nb, seq, width), F32),
        scratch_shapes=[
            pltpu.VMEM((width // LANES, nb * srows, LANES), F32),
            pltpu.VMEM((cn * nb, width), F32),
            pltpu.VMEM((width // LANES, cn * nb, LANES), F32),
            pltpu.VMEM((nb, nstate), F32),
            pltpu.VMEM((nb, nstate), F32),
        ] + [pltpu.VMEM((S5_COLS // LANES, cn * nb, LANES), F32)] * (2 * (nstate // S5_COLS)),
        compiler_params=_params(("arbitrary", "arbitrary")),
        name="s5_scan",
    )(z3, lam, bmat, cmat)


def _s5_discretize(a_re, a_im, log_step, b_re, b_im, c_re, c_im):
    ng, npst = a_re.shape[1], a_re.shape[2]
    gpb = S5_KBLK // S5_GROUP_CH
    nk = ng // gpb
    eye = jnp.eye(gpb, dtype=F32)
    bre, bim = b_re.astype(F32), b_im.astype(F32)
    lams, bmats, cmats = [], [], []
    for direction in range(2):
        ar = a_re[direction].astype(F32)
        ai = a_im[direction].astype(F32)
        dt = jnp.exp(log_step[direction].astype(F32))[:, None]
        mag = jnp.exp(ar * dt)
        lam_re, lam_im = mag * jnp.cos(ai * dt), mag * jnp.sin(ai * dt)
        nr, ni = lam_re - 1.0, lam_im
        den = ar * ar + ai * ai
        coef_re = (nr * ar + ni * ai) / den
        coef_im = (ni * ar - nr * ai) / den
        bbar_re = coef_re[..., None] * bre - coef_im[..., None] * bim
        bbar_im = coef_re[..., None] * bim + coef_im[..., None] * bre

        def in_blocks(m):
            m4 = m.reshape(nk, gpb, npst, S5_GROUP_CH)
            return jnp.einsum('kgpc,gh->kgchp', m4, eye).reshape(nk, S5_KBLK, gpb * npst)

        def out_blocks(m):
            m4 = m.reshape(nk, gpb, S5_GROUP_CH, npst)
            return jnp.einsum('kgcp,gh->kgphc', m4, eye).reshape(nk, gpb * npst, S5_KBLK)

        lams.append(jnp.stack([lam_re.reshape(-1), lam_im.reshape(-1)]))
        bmats.append(jnp.concatenate([in_blocks(bbar_re), in_blocks(bbar_im)], axis=2))
        cmats.append(jnp.concatenate([out_blocks(c_re[direction].astype(F32)),
                                      -out_blocks(c_im[direction].astype(F32))], axis=1))
    return jnp.stack(lams), jnp.stack(bmats).astype(BF16), jnp.stack(cmats).astype(BF16)


def _s5_glu(yf_ref, yb_ref, u_ref, d_ref, w_ref, b_ref):
    y = u_ref[...].astype(F32) * d_ref[...] + yf_ref[...] + yb_ref[...]
    y = jax.nn.gelu(y)
    gate = jax.nn.sigmoid(jnp.dot(y.astype(BF16), w_ref[...], preferred_element_type=F32) + b_ref[...])
    return (y * gate).astype(BF16)


def _proj_kernel(*refs, s5_glu):
    if s5_glu:
        yr_ref, yf_ref, yb_ref, u_ref, d_ref, wg_ref, bg_ref = refs[:7]
        w_ref, x_ref, g_ref, rh_ref, rl_ref, rb_ref, x1_ref, h_ref, lt_ref = refs[7:]
        half = yr_ref.shape[1]
        mix = jnp.dot(yr_ref[...], w_ref[:half, :], preferred_element_type=F32)
        mix = mix + jnp.dot(_s5_glu(yf_ref, yb_ref, u_ref, d_ref, wg_ref, bg_ref), w_ref[half:, :],
                            preferred_element_type=F32)
    else:
        a_ref, w_ref, x_ref, g_ref, rh_ref, rl_ref, rb_ref, x1_ref, h_ref, lt_ref = refs
        mix = jnp.dot(a_ref[...], w_ref[...], preferred_element_type=F32)
    x1 = x_ref[...] + mix
    x1_ref[...] = x1
    h = _rms(x1, g_ref[...])
    h_hi = h.astype(BF16)
    h_ref[...] = h_hi.reshape(h_ref.shape)
    h_lo = (h - h_hi.astype(F32)).astype(BF16)
    lt = lax.dot_general(rh_ref[...], h_hi, NT_DIMS, preferred_element_type=F32)
    lt = lt + lax.dot_general(rh_ref[...], h_lo, NT_DIMS, preferred_element_type=F32)
    lt = lt + lax.dot_general(rl_ref[...], h_hi, NT_DIMS, preferred_element_type=F32)
    lt_ref[...] = lt + rb_ref[...]


def proj_norm_router(mix_in, w, x, g, r_hi, r_lo, r_bias, tm=DENSE_ROWS):
    t, d = x.shape
    tm = min(tm, t)
    k = w.shape[0]
    nr = r_hi.shape[0]
    s5_glu = isinstance(mix_in, tuple)
    if s5_glu:
        y_ret, ys5, z, d_skip, w_glu, b_glu = mix_in
        half = y_ret.shape[1]
        ucol = z.shape[1] // half - 1
        lead_specs = [
            pl.BlockSpec((tm, half), lambda i: (i, 0)),
            pl.BlockSpec((None, tm, half), lambda i: (0, i, 0)),
            pl.BlockSpec((None, tm, half), lambda i: (1, i, 0)),
            pl.BlockSpec((tm, half), lambda i: (i, ucol)),
            pl.BlockSpec((1, half), lambda i: (0, 0)),
            pl.BlockSpec((half, half), lambda i: (0, 0)),
            pl.BlockSpec((1, half), lambda i: (0, 0)),
        ]
        lead_args = (y_ret, ys5, ys5, z, d_skip.reshape(1, half), w_glu.astype(BF16),
                     b_glu.reshape(1, half))
    else:
        lead_specs = [pl.BlockSpec((tm, k), lambda i: (i, 0))]
        lead_args = (mix_in,)
    return pl.pallas_call(
        functools.partial(_proj_kernel, s5_glu=s5_glu),
        grid=(t // tm,),
        in_specs=lead_specs + [
            pl.BlockSpec((k, d), lambda i: (0, 0)),
            pl.BlockSpec((tm, d), lambda i: (i, 0)),
            pl.BlockSpec((1, d), lambda i: (0, 0)),
            pl.BlockSpec((nr, d), lambda i: (0, 0)),
            pl.BlockSpec((nr, d), lambda i: (0, 0)),
            pl.BlockSpec((nr, 1), lambda i: (0, 0)),
        ],
        out_specs=[
            pl.BlockSpec((tm, d), lambda i: (i, 0)),
            pl.BlockSpec((tm, d // LANES, LANES), lambda i: (i, 0, 0)),
            pl.BlockSpec((nr, tm), lambda i: (0, i)),
        ],
        out_shape=[
            jax.ShapeDtypeStruct((t, d), F32),
            jax.ShapeDtypeStruct((t, d // LANES, LANES), BF16),
            jax.ShapeDtypeStruct((nr, t), F32),
        ],
        compiler_params=_params(("parallel",)),
        name="proj_norm_router",
    )(*lead_args, w.astype(BF16), x, g.reshape(1, d), r_hi, r_lo, r_bias)


def _router_operands(w_group, b_group, w_er, b_er):
    d = w_group.shape[0]
    wt = jnp.concatenate([
        jnp.transpose(w_er.astype(F32), (0, 2, 1)).reshape(MOE_EXPERTS, d),
        jnp.transpose(w_group.astype(F32)),
        jnp.zeros((ROUTER_ROWS - MOE_EXPERTS - MOE_GROUPS, d), F32)], axis=0)
    bias = jnp.concatenate([
        b_er.astype(F32).reshape(-1), b_group.astype(F32),
        jnp.zeros((ROUTER_ROWS - MOE_EXPERTS - MOE_GROUPS,), F32)]).reshape(ROUTER_ROWS, 1)
    hi = wt.astype(BF16)
    lo = (wt - hi.astype(F32)).astype(BF16)
    return hi, lo, bias


def _route_kernel(lt_ref, eid_ref, gate_ref, rank_ref, cnt_ref, run_ref):
    i = pl.program_id(0)
    tm = lt_ref.shape[1]
    ne, npg, ng = MOE_EXPERTS, MOE_EXPERTS_PER_GROUP, MOE_GROUPS

    @pl.when(i == 0)
    def _():
        run_ref[...] = jnp.zeros_like(run_ref)

    gl = lt_ref[ne:ne + ng, :]
    gmax = jnp.max(gl, axis=0, keepdims=True)
    gidx = lax.broadcasted_iota(I32, (ng, tm), 0)
    gsel = jnp.min(jnp.where(gl == gmax, gidx, ng), axis=0, keepdims=True)
    p_g = 1.0 / jnp.sum(jnp.exp(gl - gmax), axis=0, keepdims=True)

    e8 = lt_ref[(ng - 1) * npg:ng * npg, :]
    for g in range(ng - 2, -1, -1):
        e8 = jnp.where(gsel == g, lt_ref[g * npg:(g + 1) * npg, :], e8)
    eidx = lax.broadcasted_iota(I32, (npg, tm), 0)
    m1 = jnp.max(e8, axis=0, keepdims=True)
    i1 = jnp.min(jnp.where(e8 == m1, eidx, npg), axis=0, keepdims=True)
    e8b = jnp.where(eidx == i1, -jnp.inf, e8)
    m2 = jnp.max(e8b, axis=0, keepdims=True)
    i2 = jnp.min(jnp.where(e8b == m2, eidx, npg), axis=0, keepdims=True)
    t2 = jnp.exp(m2 - m1)
    den = 1.0 + t2
    gate_ref[0:1, :] = (1.0 / den) * p_g
    gate_ref[1:2, :] = (t2 / den) * p_g
    id1 = gsel * npg + i1
    id2 = gsel * npg + i2
    eid_ref[0:1, :] = id1
    eid_ref[1:2, :] = id2

    rows = lax.broadcasted_iota(I32, (ne, tm), 0)
    oh1 = rows == id1
    oh2 = rows == id2
    both = jnp.where(oh1, 1.0, 0.0) + jnp.where(oh2, 1.0, 0.0)
    earlier = (lax.broadcasted_iota(I32, (tm, tm), 0) < lax.broadcasted_iota(I32, (tm, tm), 1))
    prefix = jnp.dot(both.astype(BF16), jnp.where(earlier, 1.0, 0.0).astype(BF16),
                     preferred_element_type=F32)
    base = prefix + run_ref[:, 0:1]
    rank_ref[0:1, :] = jnp.sum(jnp.where(oh1, base, 0.0), axis=0, keepdims=True).astype(I32)
    rank_ref[1:2, :] = jnp.sum(jnp.where(oh2, base, 0.0), axis=0, keepdims=True).astype(I32)
    run = run_ref[...] + jnp.sum(both, axis=1, keepdims=True)
    run_ref[...] = run
    cnt_ref[...] = run.astype(I32)


def route(lt, tm=512):
    nr, t = lt.shape
    two = lambda dt: jax.ShapeDtypeStruct((2, t), dt)
    return pl.pallas_call(
        _route_kernel,
        grid=(t // tm,),
        in_specs=[pl.BlockSpec((nr, tm), lambda i: (0, i))],
        out_specs=[
            pl.BlockSpec((2, tm), lambda i: (0, i)),
            pl.BlockSpec((2, tm), lambda i: (0, i)),
            pl.BlockSpec((2, tm), lambda i: (0, i)),
            pl.BlockSpec((MOE_EXPERTS, LANES), lambda i: (0, 0)),
        ],
        out_shape=[two(I32), two(F32), two(I32),
                   jax.ShapeDtypeStruct((MOE_EXPERTS, LANES), I32)],
        scratch_shapes=[pltpu.VMEM((MOE_EXPERTS, LANES), F32)],
        compiler_params=_params(("arbitrary",)),
        name="route",
    )(lt)


def _dispatch_kernel(pad_ref, nu_ref, d0_ref, d1_ref, h_ref, xbuf_ref, zero_ref, sem, zsem):
    i = pl.program_id(0)
    tm = h_ref.shape[0]
    dests = (d0_ref, d1_ref)
    bm = zero_ref.shape[0]
    nblk = xbuf_ref.shape[0] // bm

    def zero_fill(act):
        def per_expert(e, carry):
            pos = pad_ref[0, e]
            length = pad_ref[1, e]
            p = bm // 2
            while p >= 1:
                bit = (length & p) != 0

                @pl.when(bit)
                def _(pos=pos, p=p):
                    act(pltpu.make_async_copy(zero_ref.at[pl.ds(0, p)],
                                              xbuf_ref.at[pl.ds(pos, p)], zsem))

                pos = pos + jnp.where(bit, p, 0)
                p //= 2
            return carry

        lax.fori_loop(0, pad_ref.shape[1], per_expert, 0)

        def per_block(b, carry):
            act(pltpu.make_async_copy(zero_ref, xbuf_ref.at[pl.ds(b * bm, bm)], zsem))
            return carry

        lax.fori_loop(nu_ref[0], nblk, per_block, 0)

    @pl.when(i == 0)
    def _():
        zero_ref[...] = jnp.zeros_like(zero_ref)
        zero_fill(lambda c: c.start())

    def copy(r, k):
        return pltpu.make_async_copy(h_ref.at[r], xbuf_ref.at[dests[k][0, r]], sem)

    def start(r, carry):
        copy(r, 0).start(priority=0)
        copy(r, 1).start(priority=1)
        return carry

    lax.fori_loop(0, tm, start, 0, unroll=8)

    def wait(r, carry):
        copy(r, 0).wait()
        copy(r, 1).wait()
        return carry

    lax.fori_loop(0, tm, wait, 0, unroll=8)

    @pl.when(i == pl.num_programs(0) - 1)
    def _():
        zero_fill(lambda c: c.wait())


def _row_index_spec(tm, ahead=0, last=None):
    def index(i):
        return (i if ahead == 0 else jnp.minimum(i + ahead, last), 0, 0)
    return pl.BlockSpec((None, 1, tm), index, memory_space=pltpu.SMEM)


def dispatch(pad, n_used, dest0, dest1, h3, n_rows):
    t, s, lanes = h3.shape
    nt, _, tm = dest0.shape
    return pl.pallas_call(
        _dispatch_kernel,
        grid=(nt,),
        in_specs=[
            pl.BlockSpec(memory_space=pltpu.SMEM),
            pl.BlockSpec(memory_space=pltpu.SMEM),
            _row_index_spec(tm), _row_index_spec(tm),
            pl.BlockSpec((tm, s, lanes), lambda i: (i, 0, 0)),
        ],
        out_specs=pl.BlockSpec(memory_space=pl.ANY),
        out_shape=jax.ShapeDtypeStruct((n_rows, s, lanes), h3.dtype),
        scratch_shapes=[pltpu.VMEM((MOE_ROWS, s, lanes), h3.dtype),
                        pltpu.SemaphoreType.DMA(()), pltpu.SemaphoreType.DMA(())],
        compiler_params=_params(("arbitrary",)),
        name="moe_dispatch",
    )(pad, n_used, dest0, dest1, h3)


def _experts_kernel(be_ref, nu_ref, nx_ref, x_ref, wg_hbm, wu_hbm, wd_hbm, o_ref,
                    wg_f32, wu_f32, wd_f32, wgb, wub, wdb, slot_ref, sem, *, layer):
    i = pl.program_id(0)
    e = be_ref[i]
    first = i == 0
    changed = jnp.logical_or(first, e != be_ref[jnp.maximum(i - 1, 0)])
    streams = ((wg_hbm, wg_f32), (wu_hbm, wu_f32), (wd_hbm, wd_f32))

    def fetch(expert, slot):
        return [pltpu.make_async_copy(w_hbm.at[layer, expert], w_f32.at[slot], sem.at[slot, j])
                for j, (w_hbm, w_f32) in enumerate(streams)]

    @pl.when(first)
    def _():
        slot_ref[0] = 1
        for c in fetch(e, 0):
            c.start()

    @pl.when(changed)
    def _():
        slot = 1 - slot_ref[0]
        slot_ref[0] = slot
        for c in fetch(e, slot):
            c.wait()
        nxt = nx_ref[i]

        @pl.when(nxt >= 0)
        def _():
            for c in fetch(nxt, 1 - slot):
                c.start()

        wgb[...] = wg_f32[slot].astype(BF16)
        wub[...] = wu_f32[slot].astype(BF16)
        wdb[...] = wd_f32[slot].astype(BF16)

    @pl.when(i < nu_ref[0])
    def _():
        bm, s, lanes = x_ref.shape
        x = x_ref[...].reshape(bm, s * lanes)
        g = jnp.dot(x, wgb[...], preferred_element_type=F32)
        u = jnp.dot(x, wub[...], preferred_element_type=F32)
        a = ((g * jax.nn.sigmoid(g)) * u).astype(BF16)
        y = jnp.dot(a, wdb[...], preferred_element_type=F32)
        o_ref[...] = y.astype(BF16).reshape(o_ref.shape)

    @pl.when(i >= nu_ref[0])
    def _():
        o_ref[...] = jnp.zeros_like(o_ref)


def experts(layer, block_expert, n_used, next_expert, xbuf, w_gate, w_up, w_down):
    n_rows, s, lanes = xbuf.shape
    d = s * lanes
    hid = w_gate.shape[3]
    bm = MOE_ROWS
    nblk = n_rows // bm
    grid_spec = pltpu.PrefetchScalarGridSpec(
        num_scalar_prefetch=3,
        grid=(nblk,),
        in_specs=[
            pl.BlockSpec((bm, s, lanes), lambda i, be, nu, nx: (jnp.minimum(i, nu[0] - 1), 0, 0)),
            pl.BlockSpec(memory_space=pl.ANY),
            pl.BlockSpec(memory_space=pl.ANY),
            pl.BlockSpec(memory_space=pl.ANY),
        ],
        out_specs=pl.BlockSpec((bm, s, lanes), lambda i, be, nu, nx: (i, 0, 0)),
        scratch_shapes=[pltpu.VMEM((2, d, hid), F32), pltpu.VMEM((2, d, hid), F32),
                        pltpu.VMEM((2, hid, d), F32),
                        pltpu.VMEM((d, hid), BF16), pltpu.VMEM((d, hid), BF16),
                        pltpu.VMEM((hid, d), BF16),
                        pltpu.SMEM((1,), I32), pltpu.SemaphoreType.DMA((2, 3))],
    )
    return pl.pallas_call(
        functools.partial(_experts_kernel, layer=layer),
        grid_spec=grid_spec,
        out_shape=jax.ShapeDtypeStruct((n_rows, s, lanes), BF16),
        compiler_params=_params(("arbitrary",)),
        name="moe_experts",
    )(block_expert, n_used, next_expert, xbuf, w_gate, w_up, w_down)


def _combine_kernel(d0_ref, d1_ref, n0_ref, n1_ref, gate_ref, x_ref, g_ref, ybuf_ref, o_ref,
                    buf, sem, *, final_norm):
    i = pl.program_id(0)
    tm, d = x_ref.shape
    slot = i % 2

    last = i + 1 == pl.num_programs(0)
    other = 1 - slot

    def copy(dests, s, r, k):
        return pltpu.make_async_copy(ybuf_ref.at[dests[k][0, r]], buf.at[s, k, r], sem.at[s])

    def start(dests, s, r):
        copy(dests, s, r, 0).start(priority=0)
        copy(dests, s, r, 1).start(priority=1)

    def wait_all(s):
        def wait(r, carry):
            copy((d0_ref, d1_ref), s, r, 0).wait()
            copy((d0_ref, d1_ref), s, r, 1).wait()
            return carry

        lax.fori_loop(0, tm, wait, 0, unroll=8)

    @pl.when(i == 0)
    def _():
        lax.fori_loop(0, tm, lambda r, c: (start((d0_ref, d1_ref), 0, r), c)[1], 0, unroll=8)

    wait_all(slot)

    group = tm // COMBINE_GROUPS
    for p in range(COMBINE_GROUPS):
        rows = slice(p * group, (p + 1) * group)
        for r in range(p * group, (p + 1) * group):
            start((n0_ref, n1_ref), other, r)
        gates = gate_ref[rows, :]
        y = (gates[:, 0:1] * buf[slot, 0, rows].reshape(group, d).astype(F32)
             + gates[:, 1:2] * buf[slot, 1, rows].reshape(group, d).astype(F32))
        out = x_ref[rows, :] + y
        if final_norm:
            out = _rms(out, g_ref[...])
        o_ref[rows, :] = out

    @pl.when(last)
    def _():
        wait_all(other)


def combine(dest0, dest1, gates_t, x, ybuf, g_final, final_norm):
    t, d = x.shape
    nt, _, tm = dest0.shape
    _, s, lanes = ybuf.shape
    return pl.pallas_call(
        functools.partial(_combine_kernel, final_norm=final_norm),
        grid=(nt,),
        in_specs=[
            _row_index_spec(tm), _row_index_spec(tm),
            _row_index_spec(tm, ahead=1, last=nt - 1), _row_index_spec(tm, ahead=1, last=nt - 1),
            pl.BlockSpec((tm, 2), lambda i: (i, 0)),
            pl.BlockSpec((tm, d), lambda i: (i, 0)),
            pl.BlockSpec((1, d), lambda i: (0, 0)),
            pl.BlockSpec(memory_space=pl.ANY),
        ],
        out_specs=pl.BlockSpec((tm, d), lambda i: (i, 0)),
        out_shape=jax.ShapeDtypeStruct((t, d), F32),
        scratch_shapes=[pltpu.VMEM((2, 2, tm, s, lanes), ybuf.dtype),
                        pltpu.SemaphoreType.DMA((2,))],
        compiler_params=_params(("arbitrary",)),
        name="moe_combine",
    )(dest0, dest1, dest0, dest1, gates_t, x, g_final.reshape(1, d), ybuf)


def hier_moe_block(layer, x1, h, lt, w_gate, w_up, w_down, g_final, final_norm, tm=MOE_TOKENS):
    t, d = x1.shape
    bm = MOE_ROWS
    eid, gate, rank, cnt = route(lt)
    counts = cnt[:, 0]
    padded = ((counts + bm - 1) // bm) * bm
    pend = jnp.cumsum(padded)
    pstart = pend - padded
    experts_col = jnp.arange(MOE_EXPERTS, dtype=I32)[:, None, None]
    dest = rank + jnp.sum(jnp.where(eid[None] == experts_col, pstart[:, None, None], 0), axis=0)
    n_rows = 2 * t + MOE_EXPERTS * bm
    nblk = n_rows // bm
    n_used = (pend[-1] // bm).astype(I32)
    first_row = jnp.minimum(jnp.arange(nblk, dtype=I32), n_used - 1) * bm
    block_expert = jnp.sum(pend[None, :] <= first_row[:, None], axis=1).astype(I32)
    block_expert = jnp.minimum(block_expert, MOE_EXPERTS - 1)
    ids = jnp.arange(MOE_EXPERTS, dtype=I32)
    later = jnp.logical_and(ids[None, :] > block_expert[:, None], counts[None, :] > 0)
    next_expert = jnp.min(jnp.where(later, ids[None, :], MOE_EXPERTS), axis=1)
    next_expert = jnp.where(next_expert < MOE_EXPERTS, next_expert, -1).astype(I32)
    dest0 = dest[0].reshape(t // tm, 1, tm)
    dest1 = dest[1].reshape(t // tm, 1, tm)
    pad = jnp.stack([pstart + counts, padded - counts]).astype(I32)
    xbuf = dispatch(pad, n_used.reshape(1), dest0, dest1, h, n_rows)
    ybuf = experts(layer, block_expert, n_used.reshape(1), next_expert, xbuf, w_gate, w_up, w_down)
    return combine(dest0, dest1, jnp.transpose(gate), x1, ybuf, g_final, final_norm)


def _attn_kernel(sink_ref, q_ref, kp_ref, kc_ref, kn_ref, vp_ref, vc_ref, vn_ref, bias_ref, o_ref):
    n = pl.program_id(1)
    nblk = pl.num_programs(1)
    blk = q_ref.shape[0]
    dh = SWA_HEAD_DIM
    nkv = kc_ref.shape[1] // dh
    masked = bias_ref.shape[1] - 1
    part_prev = jnp.where(n > 0, 0, masked)
    part_next = jnp.where(n < nblk - 1, 2, masked)
    lo = lax.broadcasted_iota(I32, (blk, 2 * dh), 1) < dh
    top = lax.broadcasted_iota(I32, (2 * blk, 2 * dh), 0) < blk
    lo2 = lax.broadcasted_iota(I32, (2 * blk, 2 * dh), 1) < dh
    ones_ext = jnp.concatenate([jnp.where(lo, 1.0, 0.0), jnp.where(lo, 0.0, 1.0)], axis=0).astype(BF16)
    qscale = (dh ** -0.5) * LOG2E

    for kv in range(nkv):
        col = slice((kv // 2) * 2 * dh, (kv // 2 + 1) * 2 * dh)

        def extend(ref):
            x = ref[:, col].astype(F32)
            r = pltpu.roll(x, dh, 1)
            x_lo, x_hi = (x, r) if kv % 2 == 0 else (r, x)
            return jnp.concatenate([jnp.where(lo, x_lo, 0.0), jnp.where(lo, 0.0, x_hi)],
                                   axis=0).astype(BF16)

        q2 = jnp.concatenate([q_ref[:, (2 * kv) * 2 * dh:(2 * kv + 1) * 2 * dh],
                              q_ref[:, (2 * kv + 1) * 2 * dh:(2 * kv + 2) * 2 * dh]], axis=0)
        q2 = (q2.astype(F32) * qscale).astype(BF16)

        def scores(k_ref, part):
            return lax.dot_general(q2, extend(k_ref), NT_DIMS,
                                   preferred_element_type=F32) + bias_ref[kv, part]

        s = [scores(kp_ref, part_prev), scores(kc_ref, 1), scores(kn_ref, part_next)]
        mx = jnp.maximum(jnp.maximum(s[0], s[1]), s[2])
        sk = [jnp.where(top, sink_ref[0, 4 * kv + par], sink_ref[0, 4 * kv + 2 + par])
              for par in range(2)]
        m = [jnp.maximum(jnp.broadcast_to(jnp.max(mx[:, par * blk:(par + 1) * blk], axis=-1,
                                                  keepdims=True), (2 * blk, 2 * dh)), sk[par])
             for par in range(2)]
        acc = jnp.zeros((2 * blk, 4 * dh), F32)
        for sp, v_ref in zip(s, (vp_ref, vc_ref, vn_ref)):
            e = jnp.concatenate([jnp.exp2(sp[:, :blk] - m[0]), jnp.exp2(sp[:, blk:] - m[1])],
                                axis=1).astype(BF16)
            rhs = jnp.concatenate([extend(v_ref), ones_ext], axis=1)
            acc = acc + jnp.dot(e, rhs, preferred_element_type=F32)
        den = acc[:, 2 * dh:] + jnp.exp2(jnp.where(lo2, sk[0] - m[0], sk[1] - m[1]))
        o = (acc[:, :2 * dh] / den).astype(BF16)
        o_ref[:, (2 * kv) * 2 * dh:(2 * kv + 1) * 2 * dh] = o[:blk]
        o_ref[:, (2 * kv + 1) * 2 * dh:(2 * kv + 2) * 2 * dh] = o[blk:]


def banded_attention(qkv3, sink, bias):
    b, seq, width = qkv3.shape
    nh = sink.shape[1]
    qd = nh * SWA_HEAD_DIM
    kvd = (width - qd) // 2
    blk = SWA_BLOCK
    nblk = seq // blk
    kcol, vcol = qd // kvd, qd // kvd + 1

    def band(col, off):
        return pl.BlockSpec((None, blk, kvd),
                            lambda bi, n: (bi, jnp.clip(n + off, 0, nblk - 1), col))

    return pl.pallas_call(
        _attn_kernel,
        grid=(b, nblk),
        in_specs=[
            pl.BlockSpec(memory_space=pltpu.SMEM),
            pl.BlockSpec((None, blk, qd), lambda bi, n: (bi, n, 0)),
            band(kcol, -1), band(kcol, 0), band(kcol, 1),
            band(vcol, -1), band(vcol, 0), band(vcol, 1),
            pl.BlockSpec(bias.shape, lambda bi, n: (0, 0, 0, 0)),
        ],
        out_specs=pl.BlockSpec((None, blk, qd), lambda bi, n: (bi, n, 0)),
        out_shape=jax.ShapeDtypeStruct((b, seq, qd), BF16),
        compiler_params=_params(("parallel", "parallel")),
        name="banded_attention",
    )(sink, qkv3, qkv3, qkv3, qkv3, qkv3, qkv3, qkv3, bias)


def _t5_bucket(rel):
    half = REL_BUCKETS // 2
    max_exact = half // 2
    n = jnp.abs(rel)
    large = max_exact + (jnp.log(jnp.maximum(n, 1).astype(F32) / max_exact)
                         / math.log(REL_MAX_DIST / max_exact) * (half - max_exact)).astype(I32)
    large = jnp.minimum(large, half - 1)
    return jnp.where(rel > 0, half, 0) + jnp.where(n < max_exact, n, large)


def _attention_bias(rel_bias):
    blk = SWA_BLOCK
    nh = rel_bias.shape[1]
    nkv = nh // SWA_Q_PER_KV
    rel = jnp.arange(3 * blk)[None, :] - blk - jnp.arange(blk)[:, None]
    onehot = (_t5_bucket(rel)[..., None] == jnp.arange(REL_BUCKETS)).astype(F32)
    bias = jnp.einsum('ijb,bh->hij', onehot, rel_bias.astype(F32), precision=lax.Precision.HIGHEST)
    bias = jnp.where((jnp.abs(rel) <= SWA_WINDOW)[None], bias, NEG_INF) * LOG2E
    tiles = bias.reshape(nkv, 2, 2, blk, 3, blk).transpose(0, 4, 1, 3, 2, 5)
    tiles = tiles.reshape(nkv, 3, 2 * blk, 2 * blk)
    masked = jnp.full((nkv, 1, 2 * blk, 2 * blk), NEG_INF * LOG2E, F32)
    return jnp.concatenate([tiles, masked], axis=1)


def _rotary_tables(seq, dim):
    inv_freq = ROPE_BASE ** (-jnp.arange(0, dim, 2, dtype=F32) / dim)
    ang = jnp.arange(seq, dtype=F32)[:, None] * inv_freq[None, :]
    cos, sin = jnp.cos(ang), jnp.sin(ang)
    return jnp.concatenate([cos, cos], axis=1), jnp.concatenate([-sin, sin], axis=1)


def kernel(x, norm_mix_g, norm_ffn_g, norm_final_g, hyb_w_in, ret_decay_logit, s5_a_re, s5_a_im, s5_log_step, s5_b_re, s5_b_im, s5_c_re, s5_c_im, s5_d, s5_w_glu, s5_b_glu, hyb_w_out, swa_w_qkv, swa_sink, swa_w_o, rel_bias, moe_w_group, moe_b_group, moe_w_expert_router, moe_b_expert_router, moe_w_gate, moe_w_up, moe_w_down):
    nb, seq, d = x.shape
    t = nb * seq
    depth = norm_mix_g.shape[0]
    xt = x.reshape(t, d)
    for layer in range(depth):
        i = layer // 2
        if layer % 2 == 0:
            w = hyb_w_out.shape[1] // 2
            z = norm_matmul(xt, norm_mix_g[layer], hyb_w_in[i])
            z3 = z.reshape(nb, seq, z.shape[1])
            cos, sin = _rotary_tables(seq, w // RET_HEADS)
            log_gamma = jax.nn.log_sigmoid(ret_decay_logit[i].astype(F32))
            y_ret = retention(z3, log_gamma, cos, sin).reshape(t, w)
            lam, bmat, cmat = _s5_discretize(s5_a_re[i], s5_a_im[i], s5_log_step[i], s5_b_re[i],
                                             s5_b_im[i], s5_c_re[i], s5_c_im[i])
            ys5 = s5_scan(z3, lam, bmat, cmat).reshape(2, t, w)
            mix_in = (y_ret, ys5, z, s5_d[i].astype(F32).reshape(-1), s5_w_glu[i],
                      s5_b_glu[i].astype(F32))
            w_mix = hyb_w_out[i]
        else:
            qkv = norm_matmul(xt, norm_mix_g[layer], swa_w_qkv[i])
            mix_in = banded_attention(qkv.reshape(nb, seq, qkv.shape[1]),
                                      swa_sink[i].astype(F32).reshape(1, -1) * LOG2E,
                                      _attention_bias(rel_bias)).reshape(t, -1)
            w_mix = swa_w_o[i]
        r_hi, r_lo, r_bias = _router_operands(moe_w_group[layer], moe_b_group[layer],
                                              moe_w_expert_router[layer], moe_b_expert_router[layer])
        x1, h, lt = proj_norm_router(mix_in, w_mix, xt, norm_ffn_g[layer], r_hi, r_lo, r_bias)
        last = layer == depth - 1
        xt = hier_moe_block(layer, x1, h, lt, moe_w_gate, moe_w_up, moe_w_down,
                            norm_final_g, final_norm=last)
    return xt.reshape(nb, seq, d)
```

```python
import functools
import math

import jax
import jax.numpy as jnp
from jax import lax
from jax.experimental import pallas as pl
from jax.experimental.pallas import tpu as pltpu

F32 = jnp.float32
BF16 = jnp.bfloat16
I32 = jnp.int32

RET_HEADS = 4
RET_CHUNK = 128
S5_GROUP_CH = 16
S5_STATE = 64
SWA_HEAD_DIM = 64
SWA_Q_PER_KV = 4
SWA_WINDOW = 128
SWA_BLOCK = 128
REL_BUCKETS = 32
REL_MAX_DIST = 128
MOE_GROUPS = 4
MOE_EXPERTS_PER_GROUP = 8
MOE_EXPERTS = MOE_GROUPS * MOE_EXPERTS_PER_GROUP
ROPE_BASE = 10000.0
RMS_EPS = 1e-6
GN_EPS = 1e-5
NEG_INF = -1e30
LOG2E = 1.4426950408889634

LANES = 128
SUBLANES = 8
V7X_VMEM_BYTES = 64 * 1024 * 1024
VMEM_LIMIT = V7X_VMEM_BYTES - 8 * 1024 * 1024

DENSE_ROWS = 1024
MOE_ROWS = 512
MOE_TOKENS = 1024
COMBINE_GROUPS = 16
S5_CHUNK = 128
S5_COLS = 512
S5_SCAN_COLS = 512
S5_KBLK = 128
ROUTER_ROWS = 128

NT_DIMS = (((1,), (1,)), ((), ()))
TN_DIMS = (((0,), (0,)), ((), ()))


def _params(semantics):
    return pltpu.CompilerParams(dimension_semantics=semantics, vmem_limit_bytes=VMEM_LIMIT)


def _rms(x, g):
    ms = jnp.mean(x * x, axis=-1, keepdims=True)
    return (x * lax.rsqrt(ms + RMS_EPS)) * g


def _norm_matmul_kernel(x_ref, g_ref, w_ref, o_ref):
    h = _rms(x_ref[...], g_ref[...])
    o_ref[...] = jnp.dot(h.astype(BF16), w_ref[...],
                         preferred_element_type=F32).astype(o_ref.dtype)


def norm_matmul(x, g, w, tm=DENSE_ROWS):
    t, d = x.shape
    tm = min(tm, t)
    n = w.shape[1]
    return pl.pallas_call(
        _norm_matmul_kernel,
        grid=(t // tm,),
        in_specs=[
            pl.BlockSpec((tm, d), lambda i: (i, 0)),
            pl.BlockSpec((1, d), lambda i: (0, 0)),
            pl.BlockSpec((d, n), lambda i: (0, 0)),
        ],
        out_specs=pl.BlockSpec((tm, n), lambda i: (i, 0)),
        out_shape=jax.ShapeDtypeStruct((t, n), BF16),
        compiler_params=_params(("parallel",)),
        name="norm_matmul",
    )(x, g.reshape(1, d), w.astype(BF16))


def _retention_kernel(lg_ref, q_ref, k_ref, v_ref, g_ref, cos_ref, sin_ref, o_ref,
                      qr_ref, kr_ref, inc_ref, st_ref, lhs_ref):
    h = pl.program_id(1)
    lg_f = lg_ref[0, h]
    lg_b = lg_ref[1, h]
    seq, dk = q_ref.shape
    c = RET_CHUNK
    nc = seq // c

    cos = cos_ref[...]
    sin = sin_ref[...]
    swap = (lax.broadcasted_iota(I32, (dk, dk), 0)
            == (lax.broadcasted_iota(I32, (dk, dk), 1) + dk // 2) % dk)
    swap = jnp.where(swap, 1.0, 0.0).astype(BF16)
    q = q_ref[...]
    qr_ref[...] = (q.astype(F32) * cos
                   + jnp.dot(q, swap, preferred_element_type=F32) * sin)
    k = k_ref[...]
    kr_ref[...] = (k.astype(F32) * cos
                   + jnp.dot(k, swap, preferred_element_type=F32) * sin) * (dk ** -0.5)

    pos = lax.broadcasted_iota(I32, (c, dk), 0).astype(F32)
    kf_scale = jnp.exp((c - 1.0 - pos) * lg_f)
    qf_scale = jnp.exp((pos + 1.0) * lg_f)
    kb_scale = jnp.exp(pos * lg_b)
    qb_scale = jnp.exp((c - pos) * lg_b)
    rel = (lax.broadcasted_iota(I32, (c, c), 0) - lax.broadcasted_iota(I32, (c, c), 1)).astype(F32)
    mask = jnp.exp(jnp.abs(rel) * jnp.where(rel >= 0, lg_f, lg_b))
    dec_f = jnp.exp(jnp.full((dk, dk), c * lg_f, F32))
    dec_b = jnp.exp(jnp.full((dk, dk), c * lg_b, F32))

    def increments(n, carry):
        rows = pl.ds(pl.multiple_of(n * c, c), c)
        kc = kr_ref[rows, :]
        kk = jnp.concatenate([kc * kf_scale, kc * kb_scale], axis=1).astype(BF16)
        inc_ref[n] = lax.dot_general(kk, v_ref[rows, :], TN_DIMS,
                                     preferred_element_type=F32)
        return carry

    lax.fori_loop(0, nc, increments, 0, unroll=True)

    def fwd(n, state):
        st_ref[n, :dk, :] = state.astype(BF16)
        return state * dec_f + inc_ref[n, :dk, :]

    lax.fori_loop(0, nc, fwd, jnp.zeros((dk, dk), F32), unroll=True)

    def bwd(i, state):
        n = nc - 1 - i
        st_ref[n, dk:, :] = state.astype(BF16)
        return state * dec_b + inc_ref[n, dk:, :]

    lax.fori_loop(0, nc, bwd, jnp.zeros((dk, dk), F32), unroll=True)

    def operands(n, carry):
        rows = pl.ds(pl.multiple_of(n * c, c), c)
        qc = qr_ref[rows, :]
        s = lax.dot_general(qc.astype(BF16), kr_ref[rows, :].astype(BF16), NT_DIMS,
                            preferred_element_type=F32) * mask
        lhs_ref[n] = jnp.concatenate([s, qc * qf_scale, qc * qb_scale], axis=1).astype(BF16)
        return carry

    lax.fori_loop(0, nc, operands, 0, unroll=True)

    def outputs(n, carry):
        rows = pl.ds(pl.multiple_of(n * c, c), c)
        rhs = jnp.concatenate([v_ref[rows, :], st_ref[n]], axis=0)
        kr_ref[rows, :] = jnp.dot(lhs_ref[n], rhs, preferred_element_type=F32)
        return carry

    lax.fori_loop(0, nc, outputs, 0, unroll=True)

    out = kr_ref[...]
    mu = jnp.mean(out, axis=-1, keepdims=True)
    cen = out - mu
    var = jnp.mean(cen * cen, axis=-1, keepdims=True)
    g = g_ref[...].astype(F32)
    o_ref[...] = ((g * jax.nn.sigmoid(g)) * (cen * lax.rsqrt(var + GN_EPS))).astype(BF16)


def retention(z3, log_gamma, cos, sin):
    b, seq, _ = z3.shape
    nh = RET_HEADS
    dk = cos.shape[1]

    def col(off):
        return pl.BlockSpec((None, seq, dk), lambda bi, hi: (bi, 0, off + hi))

    return pl.pallas_call(
        _retention_kernel,
        grid=(b, nh),
        in_specs=[
            pl.BlockSpec(memory_space=pltpu.SMEM),
            col(0), col(nh), col(2 * nh), col(3 * nh),
            pl.BlockSpec((seq, dk), lambda bi, hi: (0, 0)),
            pl.BlockSpec((seq, dk), lambda bi, hi: (0, 0)),
        ],
        out_specs=pl.BlockSpec((None, seq, dk), lambda bi, hi: (bi, 0, hi)),
        out_shape=jax.ShapeDtypeStruct((b, seq, nh * dk), BF16),
        scratch_shapes=[pltpu.VMEM((seq, dk), F32), pltpu.VMEM((seq, dk), F32),
                        pltpu.VMEM((seq // RET_CHUNK, 2 * dk, dk), F32),
                        pltpu.VMEM((seq // RET_CHUNK, 2 * dk, dk), BF16),
                        pltpu.VMEM((seq // RET_CHUNK, RET_CHUNK, 3 * dk), BF16)],
        compiler_params=_params(("parallel", "parallel")),
        name="retention",
    )(log_gamma, z3, z3, z3, z3, cos, sin)


def _s5_kernel(u_ref, lam_ref, b_ref, c_ref, y_ref, sk_ref, utm_ref, ytm_ref, sre_ref, sim_ref,
               *hist):
    nkb = len(hist) // 2

    def xre(plane):
        return hist[plane // (S5_COLS // LANES)].at[plane % (S5_COLS // LANES)]

    def xim(plane):
        return hist[nkb + plane // (S5_COLS // LANES)].at[plane % (S5_COLS // LANES)]

    d = pl.program_id(0)
    n = pl.program_id(1)
    nb, cn, width = u_ref.shape
    srows = sk_ref.shape[1] // nb
    nk = width // S5_KBLK
    npl = width // LANES

    @pl.when(n == 0)
    def _():
        sre_ref[...] = jnp.zeros_like(sre_ref)
        sim_ref[...] = jnp.zeros_like(sim_ref)

    def skew_rows(b):
        return slice(b * srows + b, b * srows + b + cn)

    for b in range(nb):
        ub = u_ref[b].astype(F32)
        for p in range(npl):
            sk_ref[p, skew_rows(b), :] = ub[:, p * LANES:(p + 1) * LANES]

    def to_time_major(t, carry):
        rows = pl.ds(pl.multiple_of(t * nb, nb), nb)
        for p in range(npl):
            utm_ref[rows, p * LANES:(p + 1) * LANES] = sk_ref[p, pl.ds(t, nb, stride=srows + 1), :]
        return carry

    lax.fori_loop(0, cn, to_time_major, 0, unroll=True)

    u = utm_ref[...].astype(BF16)
    ppc = S5_COLS // LANES
    for kb in range(nk):
        bu = jnp.dot(u[:, kb * S5_KBLK:(kb + 1) * S5_KBLK], b_ref[kb], preferred_element_type=F32)
        for j in range(ppc):
            xre(kb * ppc + j)[...] = bu[:, j * LANES:(j + 1) * LANES]
            xim(kb * ppc + j)[...] = bu[:, S5_COLS + j * LANES:S5_COLS + (j + 1) * LANES]

    pps = S5_SCAN_COLS // LANES
    for cb in range(sre_ref.shape[1] // S5_SCAN_COLS):
        cols = slice(cb * S5_SCAN_COLS, (cb + 1) * S5_SCAN_COLS)
        lr = jnp.broadcast_to(lam_ref[0:1, cols], (nb, S5_SCAN_COLS))
        li = jnp.broadcast_to(lam_ref[1:2, cols], (nb, S5_SCAN_COLS))

        def step(i, carry):
            xr, xi = carry
            t = i + d * (cn - 1 - 2 * i)
            rows = pl.ds(pl.multiple_of(t * nb, nb), nb)
            bur = jnp.concatenate([xre(cb * pps + j)[rows, :] for j in range(pps)], axis=1)
            bui = jnp.concatenate([xim(cb * pps + j)[rows, :] for j in range(pps)], axis=1)
            nxr = lr * xr - li * xi + bur
            nxi = lr * xi + li * xr + bui
            for j in range(pps):
                xre(cb * pps + j)[rows, :] = nxr[:, j * LANES:(j + 1) * LANES]
                xim(cb * pps + j)[rows, :] = nxi[:, j * LANES:(j + 1) * LANES]
            return nxr, nxi

        xr, xi = lax.fori_loop(0, cn, step, (sre_ref[:, cols], sim_ref[:, cols]), unroll=True)
        sre_ref[:, cols] = xr
        sim_ref[:, cols] = xi

    for kb in range(nk):
        xr = jnp.concatenate([xre(kb * ppc + j)[...] for j in range(ppc)], axis=1).astype(BF16)
        xi = jnp.concatenate([xim(kb * ppc + j)[...] for j in range(ppc)], axis=1).astype(BF16)
        y = jnp.dot(xr, c_ref[kb, :S5_COLS, :], preferred_element_type=F32)
        ytm_ref[kb] = y + jnp.dot(xi, c_ref[kb, S5_COLS:, :], preferred_element_type=F32)

    def to_batch_major(t, carry):
        rows = pl.ds(pl.multiple_of(t * nb, nb), nb)
        for p in range(npl):
            sk_ref[p, pl.ds(t, nb, stride=srows + 1), :] = ytm_ref[p, rows, :]
        return carry

    lax.fori_loop(0, cn, to_batch_major, 0, unroll=True)

    for b in range(nb):
        for p in range(npl):
            y_ref[b, :, p * LANES:(p + 1) * LANES] = sk_ref[p, skew_rows(b), :]


def s5_scan(z3, lam, bmat, cmat):
    nb, seq, zw = z3.shape
    width = bmat.shape[1] * bmat.shape[2]
    nstate = lam.shape[2]
    cn = min(S5_CHUNK, seq)
    nch = seq // cn
    srows = cn + SUBLANES
    ucol = zw // width - 1

    def chunk(d, n):
        return n + d * (nch - 1 - 2 * n)

    return pl.pallas_call(
        _s5_kernel,
        grid=(2, nch),
        in_specs=[
            pl.BlockSpec((nb, cn, width), lambda d, n: (0, chunk(d, n), ucol)),
            pl.BlockSpec((None, 2, nstate), lambda d, n: (d, 0, 0)),
            pl.BlockSpec((None,) + bmat.shape[1:], lambda d, n: (d, 0, 0, 0)),
            pl.BlockSpec((None,) + cmat.shape[1:], lambda d, n: (d, 0, 0, 0)),
        ],
        out_specs=pl.BlockSpec((None, nb, cn, width), lambda d, n: (d, 0, chunk(d, n), 0)),
        out_shape=jax.ShapeDtypeStruct((2, nb, seq, width), F32),
        scratch_shapes=[
            pltpu.VMEM((width // LANES, nb * srows, LANES), F32),
            pltpu.VMEM((cn * nb, width), F32),
            pltpu.VMEM((width // LANES, cn * nb, LANES), F32),
            pltpu.VMEM((nb, nstate), F32),
            pltpu.VMEM((nb, nstate), F32),
        ] + [pltpu.VMEM((S5_COLS // LANES, cn * nb, LANES), F32)] * (2 * (nstate // S5_COLS)),
        compiler_params=_params(("arbitrary", "arbitrary")),
        name="s5_scan",
    )(z3, lam, bmat, cmat)


def _s5_discretize(a_re, a_im, log_step, b_re, b_im, c_re, c_im):
    ng, npst = a_re.shape[1], a_re.shape[2]
    gpb = S5_KBLK // S5_GROUP_CH
    nk = ng // gpb
    eye = jnp.eye(gpb, dtype=F32)
    bre, bim = b_re.astype(F32), b_im.astype(F32)
    lams, bmats, cmats = [], [], []
    for direction in range(2):
        ar = a_re[direction].astype(F32)
        ai = a_im[direction].astype(F32)
        dt = jnp.exp(log_step[direction].astype(F32))[:, None]
        mag = jnp.exp(ar * dt)
        lam_re, lam_im = mag * jnp.cos(ai * dt), mag * jnp.sin(ai * dt)
        nr, ni = lam_re - 1.0, lam_im
        den = ar * ar + ai * ai
        coef_re = (nr * ar + ni * ai) / den
        coef_im = (ni * ar - nr * ai) / den
        bbar_re = coef_re[..., None] * bre - coef_im[..., None] * bim
        bbar_im = coef_re[..., None] * bim + coef_im[..., None] * bre

        def in_blocks(m):
            m4 = m.reshape(nk, gpb, npst, S5_GROUP_CH)
            return jnp.einsum('kgpc,gh->kgchp', m4, eye).reshape(nk, S5_KBLK, gpb * npst)

        def out_blocks(m):
            m4 = m.reshape(nk, gpb, S5_GROUP_CH, npst)
            return jnp.einsum('kgcp,gh->kgphc', m4, eye).reshape(nk, gpb * npst, S5_KBLK)

        lams.append(jnp.stack([lam_re.reshape(-1), lam_im.reshape(-1)]))
        bmats.append(jnp.concatenate([in_blocks(bbar_re), in_blocks(bbar_im)], axis=2))
        cmats.append(jnp.concatenate([out_blocks(c_re[direction].astype(F32)),
                                      -out_blocks(c_im[direction].astype(F32))], axis=1))
    return jnp.stack(lams), jnp.stack(bmats).astype(BF16), jnp.stack(cmats).astype(BF16)


def _s5_glu(yf_ref, yb_ref, u_ref, d_ref, w_ref, b_ref):
    y = u_ref[...].astype(F32) * d_ref[...] + yf_ref[...] + yb_ref[...]
    y = jax.nn.gelu(y)
    gate = jax.nn.sigmoid(jnp.dot(y.astype(BF16), w_ref[...], preferred_element_type=F32) + b_ref[...])
    return (y * gate).astype(BF16)


def _proj_kernel(*refs, s5_glu):
    if s5_glu:
        yr_ref, yf_ref, yb_ref, u_ref, d_ref, wg_ref, bg_ref = refs[:7]
        w_ref, x_ref, g_ref, rh_ref, rl_ref, rb_ref, x1_ref, h_ref, lt_ref = refs[7:]
        half = yr_ref.shape[1]
        mix = jnp.dot(yr_ref[...], w_ref[:half, :], preferred_element_type=F32)
        mix = mix + jnp.dot(_s5_glu(yf_ref, yb_ref, u_ref, d_ref, wg_ref, bg_ref), w_ref[half:, :],
                            preferred_element_type=F32)
    else:
        a_ref, w_ref, x_ref, g_ref, rh_ref, rl_ref, rb_ref, x1_ref, h_ref, lt_ref = refs
        mix = jnp.dot(a_ref[...], w_ref[...], preferred_element_type=F32)
    x1 = x_ref[...] + mix
    x1_ref[...] = x1
    h = _rms(x1, g_ref[...])
    h_hi = h.astype(BF16)
    h_ref[...] = h_hi.reshape(h_ref.shape)
    h_lo = (h - h_hi.astype(F32)).astype(BF16)
    lt = lax.dot_general(rh_ref[...], h_hi, NT_DIMS, preferred_element_type=F32)
    lt = lt + lax.dot_general(rh_ref[...], h_lo, NT_DIMS, preferred_element_type=F32)
    lt = lt + lax.dot_general(rl_ref[...], h_hi, NT_DIMS, preferred_element_type=F32)
    lt_ref[...] = lt + rb_ref[...]


def proj_norm_router(mix_in, w, x, g, r_hi, r_lo, r_bias, tm=DENSE_ROWS):
    t, d = x.shape
    tm = min(tm, t)
    k = w.shape[0]
    nr = r_hi.shape[0]
    s5_glu = isinstance(mix_in, tuple)
    if s5_glu:
        y_ret, ys5, z, d_skip, w_glu, b_glu = mix_in
        half = y_ret.shape[1]
        ucol = z.shape[1] // half - 1
        lead_specs = [
            pl.BlockSpec((tm, half), lambda i: (i, 0)),
            pl.BlockSpec((None, tm, half), lambda i: (0, i, 0)),
            pl.BlockSpec((None, tm, half), lambda i: (1, i, 0)),
            pl.BlockSpec((tm, half), lambda i: (i, ucol)),
            pl.BlockSpec((1, half), lambda i: (0, 0)),
            pl.BlockSpec((half, half), lambda i: (0, 0)),
            pl.BlockSpec((1, half), lambda i: (0, 0)),
        ]
        lead_args = (y_ret, ys5, ys5, z, d_skip.reshape(1, half), w_glu.astype(BF16),
                     b_glu.reshape(1, half))
    else:
        lead_specs = [pl.BlockSpec((tm, k), lambda i: (i, 0))]
        lead_args = (mix_in,)
    return pl.pallas_call(
        functools.partial(_proj_kernel, s5_glu=s5_glu),
        grid=(t // tm,),
        in_specs=lead_specs + [
            pl.BlockSpec((k, d), lambda i: (0, 0)),
            pl.BlockSpec((tm, d), lambda i: (i, 0)),
            pl.BlockSpec((1, d), lambda i: (0, 0)),
            pl.BlockSpec((nr, d), lambda i: (0, 0)),
            pl.BlockSpec((nr, d), lambda i: (0, 0)),
            pl.BlockSpec((nr, 1), lambda i: (0, 0)),
        ],
        out_specs=[
            pl.BlockSpec((tm, d), lambda i: (i, 0)),
            pl.BlockSpec((tm, d // LANES, LANES), lambda i: (i, 0, 0)),
            pl.BlockSpec((nr, tm), lambda i: (0, i)),
        ],
        out_shape=[
            jax.ShapeDtypeStruct((t, d), F32),
            jax.ShapeDtypeStruct((t, d // LANES, LANES), BF16),
            jax.ShapeDtypeStruct((nr, t), F32),
        ],
        compiler_params=_params(("parallel",)),
        name="proj_norm_router",
    )(*lead_args, w.astype(BF16), x, g.reshape(1, d), r_hi, r_lo, r_bias)


def _router_operands(w_group, b_group, w_er, b_er):
    d = w_group.shape[0]
    wt = jnp.concatenate([
        jnp.transpose(w_er.astype(F32), (0, 2, 1)).reshape(MOE_EXPERTS, d),
        jnp.transpose(w_group.astype(F32)),
        jnp.zeros((ROUTER_ROWS - MOE_EXPERTS - MOE_GROUPS, d), F32)], axis=0)
    bias = jnp.concatenate([
        b_er.astype(F32).reshape(-1), b_group.astype(F32),
        jnp.zeros((ROUTER_ROWS - MOE_EXPERTS - MOE_GROUPS,), F32)]).reshape(ROUTER_ROWS, 1)
    hi = wt.astype(BF16)
    lo = (wt - hi.astype(F32)).astype(BF16)
    return hi, lo, bias


def _route_kernel(lt_ref, tri_ref, eid_ref, gate_ref, rank_ref, cnt_ref, run_ref):
    i = pl.program_id(0)
    tm = lt_ref.shape[1]
    ne, npg, ng = MOE_EXPERTS, MOE_EXPERTS_PER_GROUP, MOE_GROUPS

    @pl.when(i == 0)
    def _():
        run_ref[...] = jnp.zeros_like(run_ref)

    gl = lt_ref[ne:ne + ng, :]
    gmax = jnp.max(gl, axis=0, keepdims=True)
    gidx = lax.broadcasted_iota(I32, (ng, tm), 0)
    gsel = jnp.min(jnp.where(gl == gmax, gidx, ng), axis=0, keepdims=True)
    p_g = 1.0 / jnp.sum(jnp.exp(gl - gmax), axis=0, keepdims=True)

    e8 = lt_ref[(ng - 1) * npg:ng * npg, :]
    for g in range(ng - 2, -1, -1):
        e8 = jnp.where(gsel == g, lt_ref[g * npg:(g + 1) * npg, :], e8)
    eidx = lax.broadcasted_iota(I32, (npg, tm), 0)
    m1 = jnp.max(e8, axis=0, keepdims=True)
    i1 = jnp.min(jnp.where(e8 == m1, eidx, npg), axis=0, keepdims=True)
    e8b = jnp.where(eidx == i1, -jnp.inf, e8)
    m2 = jnp.max(e8b, axis=0, keepdims=True)
    i2 = jnp.min(jnp.where(e8b == m2, eidx, npg), axis=0, keepdims=True)
    t2 = jnp.exp(m2 - m1)
    den = 1.0 + t2
    gate_ref[0:1, :] = (1.0 / den) * p_g
    gate_ref[1:2, :] = (t2 / den) * p_g
    id1 = gsel * npg + i1
    id2 = gsel * npg + i2
    eid_ref[0:1, :] = id1
    eid_ref[1:2, :] = id2

    rows = lax.broadcasted_iota(I32, (ne, tm), 0)
    oh1 = rows == id1
    oh2 = rows == id2
    both = jnp.where(oh1, 1.0, 0.0) + jnp.where(oh2, 1.0, 0.0)
    prefix = jnp.dot(both.astype(BF16), tri_ref[...], preferred_element_type=F32)
    base = prefix + run_ref[:, 0:1]
    rank_ref[0:1, :] = jnp.sum(jnp.where(oh1, base, 0.0), axis=0, keepdims=True).astype(I32)
    rank_ref[1:2, :] = jnp.sum(jnp.where(oh2, base, 0.0), axis=0, keepdims=True).astype(I32)
    run = run_ref[...] + jnp.sum(both, axis=1, keepdims=True)
    run_ref[...] = run
    cnt_ref[...] = run.astype(I32)


def route(lt, tm=512):
    nr, t = lt.shape
    two = lambda dt: jax.ShapeDtypeStruct((2, t), dt)
    return pl.pallas_call(
        _route_kernel,
        grid=(t // tm,),
        in_specs=[pl.BlockSpec((nr, tm), lambda i: (0, i)),
                  pl.BlockSpec((tm, tm), lambda i: (0, 0))],
        out_specs=[
            pl.BlockSpec((2, tm), lambda i: (0, i)),
            pl.BlockSpec((2, tm), lambda i: (0, i)),
            pl.BlockSpec((2, tm), lambda i: (0, i)),
            pl.BlockSpec((MOE_EXPERTS, LANES), lambda i: (0, 0)),
        ],
        out_shape=[two(I32), two(F32), two(I32),
                   jax.ShapeDtypeStruct((MOE_EXPERTS, LANES), I32)],
        scratch_shapes=[pltpu.VMEM((MOE_EXPERTS, LANES), F32)],
        compiler_params=_params(("arbitrary",)),
        name="route",
    )(lt, (jnp.arange(tm)[:, None] < jnp.arange(tm)[None, :]).astype(BF16))


def _dispatch_kernel(pad_ref, nu_ref, d0_ref, d1_ref, h_ref, xbuf_ref, zero_ref, sem, zsem):
    i = pl.program_id(0)
    tm = h_ref.shape[0]
    dests = (d0_ref, d1_ref)
    bm = zero_ref.shape[0]
    nblk = xbuf_ref.shape[0] // bm

    def zero_fill(act):
        def per_expert(e, carry):
            pos = pad_ref[0, e]
            length = pad_ref[1, e]
            p = bm // 2
            while p >= 1:
                bit = (length & p) != 0

                @pl.when(bit)
                def _(pos=pos, p=p):
                    act(pltpu.make_async_copy(zero_ref.at[pl.ds(0, p)],
                                              xbuf_ref.at[pl.ds(pos, p)], zsem))

                pos = pos + jnp.where(bit, p, 0)
                p //= 2
            return carry

        lax.fori_loop(0, pad_ref.shape[1], per_expert, 0)

        def per_block(b, carry):
            act(pltpu.make_async_copy(zero_ref, xbuf_ref.at[pl.ds(b * bm, bm)], zsem))
            return carry

        lax.fori_loop(nu_ref[0], nblk, per_block, 0)

    @pl.when(i == 0)
    def _():
        zero_ref[...] = jnp.zeros_like(zero_ref)
        zero_fill(lambda c: c.start())

    def copy(r, k):
        return pltpu.make_async_copy(h_ref.at[r], xbuf_ref.at[dests[k][0, r]], sem)

    def start(r, carry):
        copy(r, 0).start(priority=0)
        copy(r, 1).start(priority=1)
        return carry

    lax.fori_loop(0, tm, start, 0, unroll=8)

    def wait(r, carry):
        copy(r, 0).wait()
        copy(r, 1).wait()
        return carry

    lax.fori_loop(0, tm, wait, 0, unroll=8)

    @pl.when(i == pl.num_programs(0) - 1)
    def _():
        zero_fill(lambda c: c.wait())


def _row_index_spec(tm, ahead=0, last=None):
    def index(i):
        return (i if ahead == 0 else jnp.minimum(i + ahead, last), 0, 0)
    return pl.BlockSpec((None, 1, tm), index, memory_space=pltpu.SMEM)


def dispatch(pad, n_used, dest0, dest1, h3, n_rows):
    t, s, lanes = h3.shape
    nt, _, tm = dest0.shape
    return pl.pallas_call(
        _dispatch_kernel,
        grid=(nt,),
        in_specs=[
            pl.BlockSpec(memory_space=pltpu.SMEM),
            pl.BlockSpec(memory_space=pltpu.SMEM),
            _row_index_spec(tm), _row_index_spec(tm),
            pl.BlockSpec((tm, s, lanes), lambda i: (i, 0, 0)),
        ],
        out_specs=pl.BlockSpec(memory_space=pl.ANY),
        out_shape=jax.ShapeDtypeStruct((n_rows, s, lanes), h3.dtype),
        scratch_shapes=[pltpu.VMEM((MOE_ROWS, s, lanes), h3.dtype),
                        pltpu.SemaphoreType.DMA(()), pltpu.SemaphoreType.DMA(())],
        compiler_params=_params(("arbitrary",)),
        name="moe_dispatch",
    )(pad, n_used, dest0, dest1, h3)


def _experts_kernel(be_ref, nu_ref, nx_ref, x_ref, wg_hbm, wu_hbm, wd_hbm, o_ref,
                    wg_f32, wu_f32, wd_f32, wgb, wub, wdb, slot_ref, sem, *, layer):
    i = pl.program_id(0)
    e = be_ref[i]
    first = i == 0
    changed = jnp.logical_or(first, e != be_ref[jnp.maximum(i - 1, 0)])
    streams = ((wg_hbm, wg_f32), (wu_hbm, wu_f32), (wd_hbm, wd_f32))

    def fetch(expert, slot):
        return [pltpu.make_async_copy(w_hbm.at[layer, expert], w_f32.at[slot], sem.at[slot, j])
                for j, (w_hbm, w_f32) in enumerate(streams)]

    @pl.when(first)
    def _():
        slot_ref[0] = 1
        for c in fetch(e, 0):
            c.start()

    @pl.when(changed)
    def _():
        slot = 1 - slot_ref[0]
        slot_ref[0] = slot
        for c in fetch(e, slot):
            c.wait()
        nxt = nx_ref[i]

        @pl.when(nxt >= 0)
        def _():
            for c in fetch(nxt, 1 - slot):
                c.start()

        wgb[...] = wg_f32[slot].astype(BF16)
        wub[...] = wu_f32[slot].astype(BF16)
        wdb[...] = wd_f32[slot].astype(BF16)

    @pl.when(i < nu_ref[0])
    def _():
        bm, s, lanes = x_ref.shape
        x = x_ref[...].reshape(bm, s * lanes)
        g = jnp.dot(x, wgb[...], preferred_element_type=F32)
        u = jnp.dot(x, wub[...], preferred_element_type=F32)
        a = ((g * jax.nn.sigmoid(g)) * u).astype(BF16)
        y = jnp.dot(a, wdb[...], preferred_element_type=F32)
        o_ref[...] = y.astype(BF16).reshape(o_ref.shape)

    @pl.when(i >= nu_ref[0])
    def _():
        o_ref[...] = jnp.zeros_like(o_ref)


def experts(layer, block_expert, n_used, next_expert, xbuf, w_gate, w_up, w_down):
    n_rows, s, lanes = xbuf.shape
    d = s * lanes
    hid = w_gate.shape[3]
    bm = MOE_ROWS
    nblk = n_rows // bm
    grid_spec = pltpu.PrefetchScalarGridSpec(
        num_scalar_prefetch=3,
        grid=(nblk,),
        in_specs=[
            pl.BlockSpec((bm, s, lanes), lambda i, be, nu, nx: (jnp.minimum(i, nu[0] - 1), 0, 0)),
            pl.BlockSpec(memory_space=pl.ANY),
            pl.BlockSpec(memory_space=pl.ANY),
            pl.BlockSpec(memory_space=pl.ANY),
        ],
        out_specs=pl.BlockSpec((bm, s, lanes), lambda i, be, nu, nx: (i, 0, 0)),
        scratch_shapes=[pltpu.VMEM((2, d, hid), F32), pltpu.VMEM((2, d, hid), F32),
                        pltpu.VMEM((2, hid, d), F32),
                        pltpu.VMEM((d, hid), BF16), pltpu.VMEM((d, hid), BF16),
                        pltpu.VMEM((hid, d), BF16),
                        pltpu.SMEM((1,), I32), pltpu.SemaphoreType.DMA((2, 3))],
    )
    return pl.pallas_call(
        functools.partial(_experts_kernel, layer=layer),
        grid_spec=grid_spec,
        out_shape=jax.ShapeDtypeStruct((n_rows, s, lanes), BF16),
        compiler_params=_params(("arbitrary",)),
        name="moe_experts",
    )(block_expert, n_used, next_expert, xbuf, w_gate, w_up, w_down)


def _combine_kernel(d0_ref, d1_ref, n0_ref, n1_ref, gate_ref, x_ref, g_ref, ybuf_ref, o_ref,
                    buf, sem, *, final_norm):
    i = pl.program_id(0)
    tm, d = x_ref.shape
    slot = i % 2

    last = i + 1 == pl.num_programs(0)
    other = 1 - slot

    def copy(dests, s, r, k):
        return pltpu.make_async_copy(ybuf_ref.at[dests[k][0, r]], buf.at[s, k, r], sem.at[s])

    def start(dests, s, r):
        copy(dests, s, r, 0).start(priority=0)
        copy(dests, s, r, 1).start(priority=1)

    def wait_all(s):
        def wait(r, carry):
            copy((d0_ref, d1_ref), s, r, 0).wait()
            copy((d0_ref, d1_ref), s, r, 1).wait()
            return carry

        lax.fori_loop(0, tm, wait, 0, unroll=8)

    @pl.when(i == 0)
    def _():
        lax.fori_loop(0, tm, lambda r, c: (start((d0_ref, d1_ref), 0, r), c)[1], 0, unroll=8)

    wait_all(slot)

    group = tm // COMBINE_GROUPS
    for p in range(COMBINE_GROUPS):
        rows = slice(p * group, (p + 1) * group)
        for r in range(p * group, (p + 1) * group):
            start((n0_ref, n1_ref), other, r)
        gates = gate_ref[rows, :]
        y = (gates[:, 0:1] * buf[slot, 0, rows].reshape(group, d).astype(F32)
             + gates[:, 1:2] * buf[slot, 1, rows].reshape(group, d).astype(F32))
        out = x_ref[rows, :] + y
        if final_norm:
            out = _rms(out, g_ref[...])
        o_ref[rows, :] = out

    @pl.when(last)
    def _():
        wait_all(other)


def combine(dest0, dest1, gates_t, x, ybuf, g_final, final_norm):
    t, d = x.shape
    nt, _, tm = dest0.shape
    _, s, lanes = ybuf.shape
    return pl.pallas_call(
        functools.partial(_combine_kernel, final_norm=final_norm),
        grid=(nt,),
        in_specs=[
            _row_index_spec(tm), _row_index_spec(tm),
            _row_index_spec(tm, ahead=1, last=nt - 1), _row_index_spec(tm, ahead=1, last=nt - 1),
            pl.BlockSpec((tm, 2), lambda i: (i, 0)),
            pl.BlockSpec((tm, d), lambda i: (i, 0)),
            pl.BlockSpec((1, d), lambda i: (0, 0)),
            pl.BlockSpec(memory_space=pl.ANY),
        ],
        out_specs=pl.BlockSpec((tm, d), lambda i: (i, 0)),
        out_shape=jax.ShapeDtypeStruct((t, d), F32),
        scratch_shapes=[pltpu.VMEM((2, 2, tm, s, lanes), ybuf.dtype),
                        pltpu.SemaphoreType.DMA((2,))],
        compiler_params=_params(("arbitrary",)),
        name="moe_combine",
    )(dest0, dest1, dest0, dest1, gates_t, x, g_final.reshape(1, d), ybuf)


def hier_moe_block(layer, x1, h, lt, w_gate, w_up, w_down, g_final, final_norm, tm=MOE_TOKENS):
    t, d = x1.shape
    bm = MOE_ROWS
    eid, gate, rank, cnt = route(lt)
    counts = cnt[:, 0]
    padded = ((counts + bm - 1) // bm) * bm
    pend = jnp.cumsum(padded)
    pstart = pend - padded
    experts_col = jnp.arange(MOE_EXPERTS, dtype=I32)[:, None, None]
    dest = rank + jnp.sum(jnp.where(eid[None] == experts_col, pstart[:, None, None], 0), axis=0)
    n_rows = 2 * t + MOE_EXPERTS * bm
    nblk = n_rows // bm
    n_used = (pend[-1] // bm).astype(I32)
    first_row = jnp.minimum(jnp.arange(nblk, dtype=I32), n_used - 1) * bm
    block_expert = jnp.sum(pend[None, :] <= first_row[:, None], axis=1).astype(I32)
    block_expert = jnp.minimum(block_expert, MOE_EXPERTS - 1)
    ids = jnp.arange(MOE_EXPERTS, dtype=I32)
    later = jnp.logical_and(ids[None, :] > block_expert[:, None], counts[None, :] > 0)
    next_expert = jnp.min(jnp.where(later, ids[None, :], MOE_EXPERTS), axis=1)
    next_expert = jnp.where(next_expert < MOE_EXPERTS, next_expert, -1).astype(I32)
    dest0 = dest[0].reshape(t // tm, 1, tm)
    dest1 = dest[1].reshape(t // tm, 1, tm)
    pad = jnp.stack([pstart + counts, padded - counts]).astype(I32)
    xbuf = dispatch(pad, n_used.reshape(1), dest0, dest1, h, n_rows)
    ybuf = experts(layer, block_expert, n_used.reshape(1), next_expert, xbuf, w_gate, w_up, w_down)
    return combine(dest0, dest1, jnp.transpose(gate), x1, ybuf, g_final, final_norm)


def _attn_kernel(sink_ref, q_ref, kp_ref, kc_ref, kn_ref, vp_ref, vc_ref, vn_ref, bias_ref, o_ref):
    n = pl.program_id(1)
    nblk = pl.num_programs(1)
    blk = q_ref.shape[0]
    dh = SWA_HEAD_DIM
    nkv = kc_ref.shape[1] // dh
    masked = bias_ref.shape[1] - 1
    part_prev = jnp.where(n > 0, 0, masked)
    part_next = jnp.where(n < nblk - 1, 2, masked)
    lo = lax.broadcasted_iota(I32, (blk, 2 * dh), 1) < dh
    top = lax.broadcasted_iota(I32, (2 * blk, 2 * dh), 0) < blk
    lo2 = lax.broadcasted_iota(I32, (2 * blk, 2 * dh), 1) < dh
    ones_ext = jnp.concatenate([jnp.where(lo, 1.0, 0.0), jnp.where(lo, 0.0, 1.0)], axis=0).astype(BF16)
    qscale = (dh ** -0.5) * LOG2E

    for kv in range(nkv):
        col = slice((kv // 2) * 2 * dh, (kv // 2 + 1) * 2 * dh)

        def extend(ref):
            x = ref[:, col].astype(F32)
            r = pltpu.roll(x, dh, 1)
            x_lo, x_hi = (x, r) if kv % 2 == 0 else (r, x)
            return jnp.concatenate([jnp.where(lo, x_lo, 0.0), jnp.where(lo, 0.0, x_hi)],
                                   axis=0).astype(BF16)

        q2 = jnp.concatenate([q_ref[:, (2 * kv) * 2 * dh:(2 * kv + 1) * 2 * dh],
                              q_ref[:, (2 * kv + 1) * 2 * dh:(2 * kv + 2) * 2 * dh]], axis=0)
        q2 = (q2.astype(F32) * qscale).astype(BF16)

        def scores(k_ref, part):
            return lax.dot_general(q2, extend(k_ref), NT_DIMS,
                                   preferred_element_type=F32) + bias_ref[kv, part]

        s = [scores(kp_ref, part_prev), scores(kc_ref, 1), scores(kn_ref, part_next)]
        mx = jnp.maximum(jnp.maximum(s[0], s[1]), s[2])
        sk = [jnp.where(top, sink_ref[0, 4 * kv + par], sink_ref[0, 4 * kv + 2 + par])
              for par in range(2)]
        m = [jnp.maximum(jnp.broadcast_to(jnp.max(mx[:, par * blk:(par + 1) * blk], axis=-1,
                                                  keepdims=True), (2 * blk, 2 * dh)), sk[par])
             for par in range(2)]
        acc = jnp.zeros((2 * blk, 4 * dh), F32)
        for sp, v_ref in zip(s, (vp_ref, vc_ref, vn_ref)):
            e = jnp.concatenate([jnp.exp2(sp[:, :blk] - m[0]), jnp.exp2(sp[:, blk:] - m[1])],
                                axis=1).astype(BF16)
            rhs = jnp.concatenate([extend(v_ref), ones_ext], axis=1)
            acc = acc + jnp.dot(e, rhs, preferred_element_type=F32)
        den = acc[:, 2 * dh:] + jnp.exp2(jnp.where(lo2, sk[0] - m[0], sk[1] - m[1]))
        o = (acc[:, :2 * dh] / den).astype(BF16)
        o_ref[:, (2 * kv) * 2 * dh:(2 * kv + 1) * 2 * dh] = o[:blk]
        o_ref[:, (2 * kv + 1) * 2 * dh:(2 * kv + 2) * 2 * dh] = o[blk:]


def banded_attention(qkv3, sink, bias):
    b, seq, width = qkv3.shape
    nh = sink.shape[1]
    qd = nh * SWA_HEAD_DIM
    kvd = (width - qd) // 2
    blk = SWA_BLOCK
    nblk = seq // blk
    kcol, vcol = qd // kvd, qd // kvd + 1

    def band(col, off):
        return pl.BlockSpec((None, blk, kvd),
                            lambda bi, n: (bi, jnp.clip(n + off, 0, nblk - 1), col))

    return pl.pallas_call(
        _attn_kernel,
        grid=(b, nblk),
        in_specs=[
            pl.BlockSpec(memory_space=pltpu.SMEM),
            pl.BlockSpec((None, blk, qd), lambda bi, n: (bi, n, 0)),
            band(kcol, -1), band(kcol, 0), band(kcol, 1),
            band(vcol, -1), band(vcol, 0), band(vcol, 1),
            pl.BlockSpec(bias.shape, lambda bi, n: (0, 0, 0, 0)),
        ],
        out_specs=pl.BlockSpec((None, blk, qd), lambda bi, n: (bi, n, 0)),
        out_shape=jax.ShapeDtypeStruct((b, seq, qd), BF16),
        compiler_params=_params(("parallel", "parallel")),
        name="banded_attention",
    )(sink, qkv3, qkv3, qkv3, qkv3, qkv3, qkv3, qkv3, bias)


def _t5_bucket(rel):
    half = REL_BUCKETS // 2
    max_exact = half // 2
    n = jnp.abs(rel)
    large = max_exact + (jnp.log(jnp.maximum(n, 1).astype(F32) / max_exact)
                         / math.log(REL_MAX_DIST / max_exact) * (half - max_exact)).astype(I32)
    large = jnp.minimum(large, half - 1)
    return jnp.where(rel > 0, half, 0) + jnp.where(n < max_exact, n, large)


def _attention_bias(rel_bias):
    blk = SWA_BLOCK
    nh = rel_bias.shape[1]
    nkv = nh // SWA_Q_PER_KV
    rel = jnp.arange(3 * blk)[None, :] - blk - jnp.arange(blk)[:, None]
    onehot = (_t5_bucket(rel)[..., None] == jnp.arange(REL_BUCKETS)).astype(F32)
    bias = jnp.einsum('ijb,bh->hij', onehot, rel_bias.astype(F32), precision=lax.Precision.HIGHEST)
    bias = jnp.where((jnp.abs(rel) <= SWA_WINDOW)[None], bias, NEG_INF) * LOG2E
    tiles = bias.reshape(nkv, 2, 2, blk, 3, blk).transpose(0, 4, 1, 3, 2, 5)
    tiles = tiles.reshape(nkv, 3, 2 * blk, 2 * blk)
    masked = jnp.full((nkv, 1, 2 * blk, 2 * blk), NEG_INF * LOG2E, F32)
    return jnp.concatenate([tiles, masked], axis=1)


def _rotary_tables(seq, dim):
    inv_freq = ROPE_BASE ** (-jnp.arange(0, dim, 2, dtype=F32) / dim)
    ang = jnp.arange(seq, dtype=F32)[:, None] * inv_freq[None, :]
    cos, sin = jnp.cos(ang), jnp.sin(ang)
    return jnp.concatenate([cos, cos], axis=1), jnp.concatenate([-sin, sin], axis=1)


def kernel(x, norm_mix_g, norm_ffn_g, norm_final_g, hyb_w_in, ret_decay_logit, s5_a_re, s5_a_im, s5_log_step, s5_b_re, s5_b_im, s5_c_re, s5_c_im, s5_d, s5_w_glu, s5_b_glu, hyb_w_out, swa_w_qkv, swa_sink, swa_w_o, rel_bias, moe_w_group, moe_b_group, moe_w_expert_router, moe_b_expert_router, moe_w_gate, moe_w_up, moe_w_down):
    nb, seq, d = x.shape
    t = nb * seq
    depth = norm_mix_g.shape[0]
    xt = x.reshape(t, d)
    for layer in range(depth):
        i = layer // 2
        if layer % 2 == 0:
            w = hyb_w_out.shape[1] // 2
            z = norm_matmul(xt, norm_mix_g[layer], hyb_w_in[i])
            z3 = z.reshape(nb, seq, z.shape[1])
            cos, sin = _rotary_tables(seq, w // RET_HEADS)
            log_gamma = jax.nn.log_sigmoid(ret_decay_logit[i].astype(F32))
            y_ret = retention(z3, log_gamma, cos, sin).reshape(t, w)
            lam, bmat, cmat = _s5_discretize(s5_a_re[i], s5_a_im[i], s5_log_step[i], s5_b_re[i],
                                             s5_b_im[i], s5_c_re[i], s5_c_im[i])
            ys5 = s5_scan(z3, lam, bmat, cmat).reshape(2, t, w)
            mix_in = (y_ret, ys5, z, s5_d[i].astype(F32).reshape(-1), s5_w_glu[i],
                      s5_b_glu[i].astype(F32))
            w_mix = hyb_w_out[i]
        else:
            qkv = norm_matmul(xt, norm_mix_g[layer], swa_w_qkv[i])
            mix_in = banded_attention(qkv.reshape(nb, seq, qkv.shape[1]),
                                      swa_sink[i].astype(F32).reshape(1, -1) * LOG2E,
                                      _attention_bias(rel_bias)).reshape(t, -1)
            w_mix = swa_w_o[i]
        r_hi, r_lo, r_bias = _router_operands(moe_w_group[layer], moe_b_group[layer],
                                              moe_w_expert_router[layer], moe_b_expert_router[layer])
        x1, h, lt = proj_norm_router(mix_in, w_mix, xt, norm_ffn_g[layer], r_hi, r_lo, r_bias)
        last = layer == depth - 1
        xt = hier_moe_block(layer, x1, h, lt, moe_w_gate, moe_w_up, moe_w_down,
                            norm_final_g, final_norm=last)
    return xt.reshape(nb, seq, d)
```
